```python
import jax, jax.numpy as jnp
from jax import lax
import numpy as np

D_MODEL = 1024
BATCH = 8
SEQ = 4096
DEPTH = 4

MEM_LEN = 256
MIX_W = D_MODEL // 2
N_BRANCH = 4
CONV_W = MIX_W
CONV_K = 31
DN_HEADS = 4
DN_HEAD_DIM = MIX_W // DN_HEADS
DN_CONV_K = 4
DN_CHUNK = 64
GM_W = MIX_W
GM_GROUPS = 4
GM_CHUNK = 128
POOL_W = MIX_W
POOL_WINDOWS = (2, 4, 8, 16)
POOL_GROUPS = len(POOL_WINDOWS)
XA_HEADS = 4
XA_HEAD_DIM = D_MODEL // XA_HEADS
FFN_W = 4 * D_MODEL
IN_W = 2 * CONV_W + 4 * MIX_W + 2 * DN_HEADS + 2 * GM_W + POOL_W + N_BRANCH * D_MODEL

kernel_name = 'hybrid_gated_conv_deltanet_gmlp_pool_trunk'


def rmsnorm(x, g, eps=1e-6):
    xf = x.astype(jnp.float32)
    y = xf * lax.rsqrt(jnp.mean(xf * xf, axis=-1, keepdims=True) + eps)
    return (y * g.astype(jnp.float32)).astype(x.dtype)


def layernorm(x, g, b, eps=1e-5):
    xf = x.astype(jnp.float32)
    mu = jnp.mean(xf, axis=-1, keepdims=True)
    var = jnp.mean(jnp.square(xf - mu), axis=-1, keepdims=True)
    y = (xf - mu) * lax.rsqrt(var + eps)
    return (y * g.astype(jnp.float32) + b.astype(jnp.float32)).astype(x.dtype)


def causal_dwconv(x, w):
    K, C = w.shape
    return lax.conv_general_dilated(x, w[:, None, :].astype(x.dtype), window_strides=(1,),
                                    padding=[(K - 1, 0)],
                                    dimension_numbers=('NWC', 'WIO', 'NWC'),
                                    feature_group_count=C)


def gated_delta_rule(q, k, v, g, beta):
    B, S, H, Dk = q.shape
    Dv = v.shape[-1]
    C = DN_CHUNK
    N = S // C
    f32 = jnp.float32
    q = q.astype(f32)
    k = k.astype(f32)
    q = q * lax.rsqrt(jnp.sum(q * q, -1, keepdims=True) + 1e-6) * (Dk ** -0.5)
    k = k * lax.rsqrt(jnp.sum(k * k, -1, keepdims=True) + 1e-6)

    def chunks(t):
        t = t.astype(f32).reshape((B, N, C, H) + t.shape[3:])
        return jnp.moveaxis(t, 3, 2)

    q, k, v, g, beta = chunks(q), chunks(k), chunks(v), chunks(g), chunks(beta)
    gam = jnp.cumsum(g, axis=-1)
    causal = jnp.tril(jnp.ones((C, C), bool))
    strict = jnp.tril(jnp.ones((C, C), bool), -1)
    decay = jnp.exp(jnp.where(causal, gam[..., :, None] - gam[..., None, :], -jnp.inf))
    kb = k * beta[..., None]
    a = jnp.where(strict, jnp.einsum('bnhid,bnhjd->bnhij', kb, k) * decay, 0.0)
    rhs = jnp.concatenate([v * beta[..., None], kb * jnp.exp(gam)[..., None]], axis=-1)
    sol = lax.linalg.triangular_solve(a + jnp.eye(C, dtype=f32), rhs, left_side=True,
                                      lower=True, unit_diagonal=True)
    u, w = sol[..., :Dv], sol[..., Dv:]
    attn = jnp.einsum('bnhid,bnhjd->bnhij', q, k) * decay
    q_dec = q * jnp.exp(gam)[..., None]
    g_last = gam[..., -1]
    k_dec = k * jnp.exp(g_last[..., None] - gam)[..., None]

    def step(state, xs):
        u_n, w_n, attn_n, qd_n, kd_n, gl_n = xs
        v_new = u_n - jnp.einsum('bhck,bhkv->bhcv', w_n, state)
        o_n = (jnp.einsum('bhck,bhkv->bhcv', qd_n, state)
               + jnp.einsum('bhij,bhjv->bhiv', attn_n, v_new))
        state = (state * jnp.exp(gl_n)[..., None, None]
                 + jnp.einsum('bhck,bhcv->bhkv', kd_n, v_new))
        return state, o_n

    xs = tuple(jnp.moveaxis(t, 1, 0) for t in (u, w, attn, q_dec, k_dec, g_last))
    state0 = jnp.zeros((B, H, Dk, Dv), f32)
    _, o = lax.scan(step, state0, xs)
    return jnp.transpose(o, (1, 0, 3, 2, 4)).reshape(B, S, H, Dv)


def multiscale_pool(x):
    B, S, C = x.shape
    Cg = C // POOL_GROUPS
    xf = x.astype(jnp.float32).reshape(B, S, POOL_GROUPS, Cg)
    cs = jnp.cumsum(xf, axis=1)
    cs_pad = jnp.concatenate([jnp.zeros((B, 1, POOL_GROUPS, Cg), jnp.float32), cs], axis=1)
    t = jnp.arange(S)
    outs = []
    for gi, win in enumerate(POOL_WINDOWS):
        upper = cs[:, :, gi]
        lower = jnp.concatenate([jnp.zeros((B, win - 1, Cg), jnp.float32),
                                 cs_pad[:, :S - win + 1, gi]], axis=1)
        count = jnp.minimum(t + 1, win).astype(jnp.float32)[None, :, None]
        outs.append((upper - lower) / count)
    return (jnp.stack(outs, axis=2) - xf).astype(x.dtype)


def hybrid_mixer(h, w_in, conv_a_w, conv_a_b, ln_a_g, ln_a_b, dn_conv_w, dn_a_log, dn_dt_bias,
                 dn_norm_g, gm_ln_g, gm_ln_b, gm_ws, gm_bs, pool_w, pool_scale, w_branch, w_out):
    B, S, _ = h.shape
    f32 = jnp.float32
    z = h @ w_in
    sizes = [2 * CONV_W, MIX_W, MIX_W, MIX_W, MIX_W, DN_HEADS, DN_HEADS, 2 * GM_W, POOL_W,
             N_BRANCH * D_MODEL]
    idx = np.cumsum(sizes)[:-1].tolist()
    a_in, dq, dk, dv, dgate, dbeta, da, gm_in, pool_in, gate_in = jnp.split(z, idx, axis=-1)

    a1, a2 = jnp.split(a_in, 2, axis=-1)
    a = a1 * jax.nn.sigmoid(a2)
    a = causal_dwconv(a, conv_a_w) + conv_a_b
    a = jax.nn.silu(layernorm(a, ln_a_g, ln_a_b))

    qkv = jax.nn.silu(causal_dwconv(jnp.concatenate([dq, dk, dv], axis=-1), dn_conv_w))
    q, k, v = jnp.split(qkv, 3, axis=-1)
    heads = lambda t: t.reshape(B, S, DN_HEADS, DN_HEAD_DIM)
    beta = jax.nn.sigmoid(dbeta.astype(f32))
    g = -jnp.exp(dn_a_log.astype(f32)) * jax.nn.softplus(da.astype(f32) + dn_dt_bias.astype(f32))
    o = gated_delta_rule(heads(q), heads(k), heads(v), g, beta)
    o = rmsnorm(o, dn_norm_g) * jax.nn.silu(heads(dgate).astype(f32))
    o = o.reshape(B, S, MIX_W).astype(h.dtype)

    u, vg = jnp.split(jax.nn.gelu(gm_in, approximate=False), 2, axis=-1)
    vg = layernorm(vg, gm_ln_g, gm_ln_b)
    vc = vg.reshape(B, S // GM_CHUNK, GM_CHUNK, GM_GROUPS, GM_W // GM_GROUPS)
    ws = gm_ws * jnp.tril(jnp.ones((GM_CHUNK, GM_CHUNK), gm_ws.dtype))
    mixed = jnp.einsum('gts,bnsgc->bntgc', ws, vc) + gm_bs.T[:, :, None]
    c = u * mixed.reshape(B, S, GM_W)

    p = multiscale_pool(pool_in)
    p = jnp.einsum('bsgc,gcd->bsgd', p, pool_w).reshape(B, S, POOL_W) * pool_scale

    gates = jax.nn.sigmoid(gate_in).reshape(B, S, N_BRANCH, D_MODEL)
    br = jnp.stack([a, o, c, p], axis=2)
    proj = jnp.einsum('bsnc,ncd->bsnd', br, w_branch)
    merged = jnp.sum(gates * proj, axis=2)
    return merged @ w_out


def memory_cross_attention(h, mem, norm_mem, wq, wkv, wo):
    B, S, _ = h.shape
    M = mem.shape[1]
    m = rmsnorm(mem, norm_mem)
    q = (h @ wq).reshape(B, S, XA_HEADS, XA_HEAD_DIM)
    k, v = jnp.split(m @ wkv, 2, axis=-1)
    k = k.reshape(B, M, XA_HEADS, XA_HEAD_DIM)
    v = v.reshape(B, M, XA_HEADS, XA_HEAD_DIM)
    s = jnp.einsum('bshd,bmhd->bhsm', q, k).astype(jnp.float32) * (XA_HEAD_DIM ** -0.5)
    pr = jax.nn.softmax(s, axis=-1).astype(v.dtype)
    o = jnp.einsum('bhsm,bmhd->bshd', pr, v).reshape(B, S, D_MODEL)
    return o @ wo


def squared_relu_mlp(h, w1, w2):
    return jnp.square(jax.nn.relu(h @ w1)) @ w2


def _fwd_setup_inputs(seed: int = 0) -> dict:
    key = jax.random.key(seed)
    ks = jax.random.split(key, 40)
    f32 = jnp.float32
    L, D = DEPTH, D_MODEL
    nrm = lambda i, shape, scale: jax.random.normal(ks[i], shape, f32) * scale
    gain = lambda i, shape: 1.0 + 0.1 * jax.random.normal(ks[i], shape, f32)
    dt = jnp.exp(jax.random.uniform(ks[10], (L, DN_HEADS), f32) * (np.log(0.1) - np.log(0.001))
                 + np.log(0.001))
    return {
        'x': nrm(0, (BATCH, SEQ, D), 1.0),
        'mem': nrm(1, (BATCH, MEM_LEN, D), 1.0),
        'norm_mix': gain(2, (L, D)),
        'w_in': nrm(3, (L, D, IN_W), D ** -0.5),
        'conv_a_w': nrm(4, (L, CONV_K, CONV_W), CONV_K ** -0.5),
        'conv_a_b': nrm(5, (L, CONV_W), 0.02),
        'ln_a_g': gain(6, (L, CONV_W)),
        'ln_a_b': nrm(7, (L, CONV_W), 0.02),
        'dn_conv_w': nrm(8, (L, DN_CONV_K, 3 * MIX_W), DN_CONV_K ** -0.5),
        'dn_a_log': jnp.log(jax.random.uniform(ks[9], (L, DN_HEADS), f32, 1.0, 16.0)),
        'dn_dt_bias': dt + jnp.log(-jnp.expm1(-dt)),
        'dn_norm_g': gain(11, (L, DN_HEAD_DIM)),
        'gm_ln_g': gain(12, (L, GM_W)),
        'gm_ln_b': nrm(13, (L, GM_W), 0.02),
        'gm_ws': nrm(14, (L, GM_GROUPS, GM_CHUNK, GM_CHUNK), GM_CHUNK ** -0.5),
        'gm_bs': gain(15, (L, GM_GROUPS, GM_CHUNK)),
        'pool_w': nrm(16, (L, POOL_GROUPS, POOL_W // POOL_GROUPS, POOL_W // POOL_GROUPS),
                      (POOL_W // POOL_GROUPS) ** -0.5),
        'pool_scale': gain(17, (L, POOL_W)),
        'w_branch': nrm(18, (L, N_BRANCH, MIX_W, D), MIX_W ** -0.5),
        'w_out': nrm(19, (L, D, D), D ** -0.5),
        'norm_xa': gain(20, (L, D)),
        'norm_mem': gain(21, (L, D)),
        'xa_wq': nrm(22, (L, D, D), D ** -0.5),
        'xa_wkv': nrm(23, (L, D, 2 * D), D ** -0.5),
        'xa_wo': nrm(24, (L, D, D), D ** -0.5),
        'norm_mlp': gain(25, (L, D)),
        'mlp_w1': nrm(26, (L, D, FFN_W), D ** -0.5),
        'mlp_w2': nrm(27, (L, FFN_W, D), FFN_W ** -0.5),
        'norm_f': gain(28, (D,)),
    }


def _fwd_reference(x, mem, norm_mix, w_in, conv_a_w, conv_a_b, ln_a_g, ln_a_b, dn_conv_w, dn_a_log,
              dn_dt_bias, dn_norm_g, gm_ln_g, gm_ln_b, gm_ws, gm_bs, pool_w, pool_scale, w_branch,
              w_out, norm_xa, norm_mem, xa_wq, xa_wkv, xa_wo, norm_mlp, mlp_w1, mlp_w2, norm_f):
    for l in range(DEPTH):
        x = x + hybrid_mixer(rmsnorm(x, norm_mix[l]), w_in[l], conv_a_w[l], conv_a_b[l], ln_a_g[l],
                             ln_a_b[l], dn_conv_w[l], dn_a_log[l], dn_dt_bias[l], dn_norm_g[l],
                             gm_ln_g[l], gm_ln_b[l], gm_ws[l], gm_bs[l], pool_w[l], pool_scale[l],
                             w_branch[l], w_out[l])
        x = x + memory_cross_attention(rmsnorm(x, norm_xa[l]), mem, norm_mem[l], xa_wq[l],
                                       xa_wkv[l], xa_wo[l])
        x = x + squared_relu_mlp(rmsnorm(x, norm_mlp[l]), mlp_w1[l], mlp_w2[l])
    return rmsnorm(x, norm_f)


import jax as _jax
import jax.numpy as _jnp

TWIN_FORMAT = 'train_step'
FWD_PARAMS = ['x', 'mem', 'norm_mix', 'w_in', 'conv_a_w', 'conv_a_b', 'ln_a_g', 'ln_a_b', 'dn_conv_w', 'dn_a_log', 'dn_dt_bias', 'dn_norm_g', 'gm_ln_g', 'gm_ln_b', 'gm_ws', 'gm_bs', 'pool_w', 'pool_scale', 'w_branch', 'w_out', 'norm_xa', 'norm_mem', 'xa_wq', 'xa_wkv', 'xa_wo', 'norm_mlp', 'mlp_w1', 'mlp_w2', 'norm_f']
TWIN_WEIGHTS = ['norm_mix', 'w_in', 'conv_a_w', 'conv_a_b', 'ln_a_g', 'ln_a_b', 'dn_conv_w', 'dn_a_log', 'dn_dt_bias', 'dn_norm_g', 'gm_ln_g', 'gm_ln_b', 'gm_ws', 'gm_bs', 'pool_w', 'pool_scale', 'w_branch', 'w_out', 'norm_xa', 'norm_mem', 'xa_wq', 'xa_wkv', 'xa_wo', 'norm_mlp', 'mlp_w1', 'mlp_w2', 'norm_f']
TWIN_DIFF_INPUT = 'x'
TWIN_INPUTS = ['x', 'mem', 'norm_mix', 'w_in', 'conv_a_w', 'conv_a_b', 'ln_a_g', 'ln_a_b', 'dn_conv_w', 'dn_a_log', 'dn_dt_bias', 'dn_norm_g', 'gm_ln_g', 'gm_ln_b', 'gm_ws', 'gm_bs', 'pool_w', 'pool_scale', 'w_branch', 'w_out', 'norm_xa', 'norm_mem', 'xa_wq', 'xa_wkv', 'xa_wo', 'norm_mlp', 'mlp_w1', 'mlp_w2', 'norm_f', 'loss_target', 'm_norm_mix', 'm_w_in', 'm_conv_a_w', 'm_conv_a_b', 'm_ln_a_g', 'm_ln_a_b', 'm_dn_conv_w', 'm_dn_a_log', 'm_dn_dt_bias', 'm_dn_norm_g', 'm_gm_ln_g', 'm_gm_ln_b', 'm_gm_ws', 'm_gm_bs', 'm_pool_w', 'm_pool_scale', 'm_w_branch', 'm_w_out', 'm_norm_xa', 'm_norm_mem', 'm_xa_wq', 'm_xa_wkv', 'm_xa_wo', 'm_norm_mlp', 'm_mlp_w1', 'm_mlp_w2', 'm_norm_f', 'v_norm_mix', 'v_w_in', 'v_conv_a_w', 'v_conv_a_b', 'v_ln_a_g', 'v_ln_a_b', 'v_dn_conv_w', 'v_dn_a_log', 'v_dn_dt_bias', 'v_dn_norm_g', 'v_gm_ln_g', 'v_gm_ln_b', 'v_gm_ws', 'v_gm_bs', 'v_pool_w', 'v_pool_scale', 'v_w_branch', 'v_w_out', 'v_norm_xa', 'v_norm_mem', 'v_xa_wq', 'v_xa_wkv', 'v_xa_wo', 'v_norm_mlp', 'v_mlp_w1', 'v_mlp_w2', 'v_norm_f']
TWIN_OUTPUTS = ['loss', 'grad_x', 'grad_norm_mix', 'grad_w_in', 'grad_conv_a_w', 'grad_conv_a_b', 'grad_ln_a_g', 'grad_ln_a_b', 'grad_dn_conv_w', 'grad_dn_a_log', 'grad_dn_dt_bias', 'grad_dn_norm_g', 'grad_gm_ln_g', 'grad_gm_ln_b', 'grad_gm_ws', 'grad_gm_bs', 'grad_pool_w', 'grad_pool_scale', 'grad_w_branch', 'grad_w_out', 'grad_norm_xa', 'grad_norm_mem', 'grad_xa_wq', 'grad_xa_wkv', 'grad_xa_wo', 'grad_norm_mlp', 'grad_mlp_w1', 'grad_mlp_w2', 'grad_norm_f', 'delta_norm_mix', 'delta_w_in', 'delta_conv_a_w', 'delta_conv_a_b', 'delta_ln_a_g', 'delta_ln_a_b', 'delta_dn_conv_w', 'delta_dn_a_log', 'delta_dn_dt_bias', 'delta_dn_norm_g', 'delta_gm_ln_g', 'delta_gm_ln_b', 'delta_gm_ws', 'delta_gm_bs', 'delta_pool_w', 'delta_pool_scale', 'delta_w_branch', 'delta_w_out', 'delta_norm_xa', 'delta_norm_mem', 'delta_xa_wq', 'delta_xa_wkv', 'delta_xa_wo', 'delta_norm_mlp', 'delta_mlp_w1', 'delta_mlp_w2', 'delta_norm_f', 'new_m_norm_mix', 'new_m_w_in', 'new_m_conv_a_w', 'new_m_conv_a_b', 'new_m_ln_a_g', 'new_m_ln_a_b', 'new_m_dn_conv_w', 'new_m_dn_a_log', 'new_m_dn_dt_bias', 'new_m_dn_norm_g', 'new_m_gm_ln_g', 'new_m_gm_ln_b', 'new_m_gm_ws', 'new_m_gm_bs', 'new_m_pool_w', 'new_m_pool_scale', 'new_m_w_branch', 'new_m_w_out', 'new_m_norm_xa', 'new_m_norm_mem', 'new_m_xa_wq', 'new_m_xa_wkv', 'new_m_xa_wo', 'new_m_norm_mlp', 'new_m_mlp_w1', 'new_m_mlp_w2', 'new_m_norm_f', 'new_v_norm_mix', 'new_v_w_in', 'new_v_conv_a_w', 'new_v_conv_a_b', 'new_v_ln_a_g', 'new_v_ln_a_b', 'new_v_dn_conv_w', 'new_v_dn_a_log', 'new_v_dn_dt_bias', 'new_v_dn_norm_g', 'new_v_gm_ln_g', 'new_v_gm_ln_b', 'new_v_gm_ws', 'new_v_gm_bs', 'new_v_pool_w', 'new_v_pool_scale', 'new_v_w_branch', 'new_v_w_out', 'new_v_norm_xa', 'new_v_norm_mem', 'new_v_xa_wq', 'new_v_xa_wkv', 'new_v_xa_wo', 'new_v_norm_mlp', 'new_v_mlp_w1', 'new_v_mlp_w2', 'new_v_norm_f']
TWIN_LEAF_KINDS = {'loss': 'loss', 'grad_x': 'grad_x', 'grad_norm_mix': 'grad_w', 'grad_w_in': 'grad_w', 'grad_conv_a_w': 'grad_w', 'grad_conv_a_b': 'grad_w', 'grad_ln_a_g': 'grad_w', 'grad_ln_a_b': 'grad_w', 'grad_dn_conv_w': 'grad_w', 'grad_dn_a_log': 'grad_w', 'grad_dn_dt_bias': 'grad_w', 'grad_dn_norm_g': 'grad_w', 'grad_gm_ln_g': 'grad_w', 'grad_gm_ln_b': 'grad_w', 'grad_gm_ws': 'grad_w', 'grad_gm_bs': 'grad_w', 'grad_pool_w': 'grad_w', 'grad_pool_scale': 'grad_w', 'grad_w_branch': 'grad_w', 'grad_w_out': 'grad_w', 'grad_norm_xa': 'grad_w', 'grad_norm_mem': 'grad_w', 'grad_xa_wq': 'grad_w', 'grad_xa_wkv': 'grad_w', 'grad_xa_wo': 'grad_w', 'grad_norm_mlp': 'grad_w', 'grad_mlp_w1': 'grad_w', 'grad_mlp_w2': 'grad_w', 'grad_norm_f': 'grad_w', 'delta_norm_mix': 'delta_w', 'delta_w_in': 'delta_w', 'delta_conv_a_w': 'delta_w', 'delta_conv_a_b': 'delta_w', 'delta_ln_a_g': 'delta_w', 'delta_ln_a_b': 'delta_w', 'delta_dn_conv_w': 'delta_w', 'delta_dn_a_log': 'delta_w', 'delta_dn_dt_bias': 'delta_w', 'delta_dn_norm_g': 'delta_w', 'delta_gm_ln_g': 'delta_w', 'delta_gm_ln_b': 'delta_w', 'delta_gm_ws': 'delta_w', 'delta_gm_bs': 'delta_w', 'delta_pool_w': 'delta_w', 'delta_pool_scale': 'delta_w', 'delta_w_branch': 'delta_w', 'delta_w_out': 'delta_w', 'delta_norm_xa': 'delta_w', 'delta_norm_mem': 'delta_w', 'delta_xa_wq': 'delta_w', 'delta_xa_wkv': 'delta_w', 'delta_xa_wo': 'delta_w', 'delta_norm_mlp': 'delta_w', 'delta_mlp_w1': 'delta_w', 'delta_mlp_w2': 'delta_w', 'delta_norm_f': 'delta_w', 'new_m_norm_mix': 'new_m', 'new_m_w_in': 'new_m', 'new_m_conv_a_w': 'new_m', 'new_m_conv_a_b': 'new_m', 'new_m_ln_a_g': 'new_m', 'new_m_ln_a_b': 'new_m', 'new_m_dn_conv_w': 'new_m', 'new_m_dn_a_log': 'new_m', 'new_m_dn_dt_bias': 'new_m', 'new_m_dn_norm_g': 'new_m', 'new_m_gm_ln_g': 'new_m', 'new_m_gm_ln_b': 'new_m', 'new_m_gm_ws': 'new_m', 'new_m_gm_bs': 'new_m', 'new_m_pool_w': 'new_m', 'new_m_pool_scale': 'new_m', 'new_m_w_branch': 'new_m', 'new_m_w_out': 'new_m', 'new_m_norm_xa': 'new_m', 'new_m_norm_mem': 'new_m', 'new_m_xa_wq': 'new_m', 'new_m_xa_wkv': 'new_m', 'new_m_xa_wo': 'new_m', 'new_m_norm_mlp': 'new_m', 'new_m_mlp_w1': 'new_m', 'new_m_mlp_w2': 'new_m', 'new_m_norm_f': 'new_m', 'new_v_norm_mix': 'new_v', 'new_v_w_in': 'new_v', 'new_v_conv_a_w': 'new_v', 'new_v_conv_a_b': 'new_v', 'new_v_ln_a_g': 'new_v', 'new_v_ln_a_b': 'new_v', 'new_v_dn_conv_w': 'new_v', 'new_v_dn_a_log': 'new_v', 'new_v_dn_dt_bias': 'new_v', 'new_v_dn_norm_g': 'new_v', 'new_v_gm_ln_g': 'new_v', 'new_v_gm_ln_b': 'new_v', 'new_v_gm_ws': 'new_v', 'new_v_gm_bs': 'new_v', 'new_v_pool_w': 'new_v', 'new_v_pool_scale': 'new_v', 'new_v_w_branch': 'new_v', 'new_v_w_out': 'new_v', 'new_v_norm_xa': 'new_v', 'new_v_norm_mem': 'new_v', 'new_v_xa_wq': 'new_v', 'new_v_xa_wkv': 'new_v', 'new_v_xa_wo': 'new_v', 'new_v_norm_mlp': 'new_v', 'new_v_mlp_w1': 'new_v', 'new_v_mlp_w2': 'new_v', 'new_v_norm_f': 'new_v'}


def _forward(args):
    return _fwd_reference(*[args[k] for k in FWD_PARAMS])


def _output_shape():
    def fwd():
        inp = _fwd_setup_inputs(0)
        return _fwd_reference(*[inp[k] for k in FWD_PARAMS])
    out = _jax.eval_shape(fwd)
    return out.shape, out.dtype

N_MICROBATCH = 1
ADAM_LR = 0.001
ADAM_B1 = 0.9
ADAM_B2 = 0.999
ADAM_EPS = 1e-08
ADAM_WD = 0.01
ADAM_STEP = 10
PER_EXAMPLE_BATCH_AXIS = {'x': 0, 'mem': 0, 'loss_target': 0}
SHARED_INPUTS = []
_WEIGHT_DTYPES = {'norm_mix': _jnp.float32, 'w_in': _jnp.float32, 'conv_a_w': _jnp.float32, 'conv_a_b': _jnp.float32, 'ln_a_g': _jnp.float32, 'ln_a_b': _jnp.float32, 'dn_conv_w': _jnp.float32, 'dn_a_log': _jnp.float32, 'dn_dt_bias': _jnp.float32, 'dn_norm_g': _jnp.float32, 'gm_ln_g': _jnp.float32, 'gm_ln_b': _jnp.float32, 'gm_ws': _jnp.float32, 'gm_bs': _jnp.float32, 'pool_w': _jnp.float32, 'pool_scale': _jnp.float32, 'w_branch': _jnp.float32, 'w_out': _jnp.float32, 'norm_xa': _jnp.float32, 'norm_mem': _jnp.float32, 'xa_wq': _jnp.float32, 'xa_wkv': _jnp.float32, 'xa_wo': _jnp.float32, 'norm_mlp': _jnp.float32, 'mlp_w1': _jnp.float32, 'mlp_w2': _jnp.float32, 'norm_f': _jnp.float32}
MOMENT_SCALE = {'norm_mix': 1.575612e-01, 'w_in': 5.320533e-02, 'conv_a_w': 1.044386e-01, 'conv_a_b': 6.468140e-01, 'ln_a_g': 2.617549e-01, 'ln_a_b': 3.736927e-01, 'dn_conv_w': 5.991054e-02, 'dn_a_log': 2.095825e-01, 'dn_dt_bias': 2.021600e-01, 'dn_norm_g': 2.056780e-01, 'gm_ln_g': 4.333959e-02, 'gm_ln_b': 4.641020e-02, 'gm_ws': 4.445192e-02, 'gm_bs': 6.293380e-02, 'pool_w': 8.710602e-02, 'pool_scale': 8.601457e-02, 'w_branch': 1.041510e-01, 'w_out': 2.066445e-01, 'norm_xa': 1.552611e-02, 'norm_mem': 2.907524e-02, 'xa_wq': 1.570582e-02, 'xa_wkv': 1.983653e-02, 'xa_wo': 2.341792e-02, 'norm_mlp': 2.264418e-01, 'mlp_w1': 1.111202e-01, 'mlp_w2': 4.800023e-01, 'norm_f': 3.283519e+01}


def _to_microbatches(a, axis):
    t = _jnp.moveaxis(a, axis, 0)
    t = t.reshape((N_MICROBATCH, t.shape[0] // N_MICROBATCH) + t.shape[1:])
    return _jnp.moveaxis(t, 1, axis + 1)


def setup_inputs(seed: int = 0) -> dict:
    inp = _fwd_setup_inputs(seed)
    key = _jax.random.fold_in(_jax.random.key(seed), 7919)
    shape, _ = _output_shape()
    out = dict(inp)
    out["loss_target"] = _jax.random.normal(_jax.random.fold_in(key, 0), shape, _jnp.float32)
    for i, name in enumerate(TWIN_WEIGHTS):
        w = inp[name].astype(_jnp.float32)
        if MOMENT_SCALE is None:
            s = _jnp.sqrt(_jnp.mean(_jnp.square(w)) + 1e-30)
        else:
            s = MOMENT_SCALE[name]
        km, kv = _jax.random.split(_jax.random.fold_in(key, i + 1))
        out[name] = w
        out["m_" + name] = s * _jax.random.normal(km, w.shape, _jnp.float32)
        out["v_" + name] = (s * s) * _jax.random.uniform(kv, w.shape, _jnp.float32, 0.5, 1.5)
    if N_MICROBATCH > 1:
        for name, axis in PER_EXAMPLE_BATCH_AXIS.items():
            out[name] = _to_microbatches(out[name], axis)
    return {'x': out['x'], 'mem': out['mem'], 'norm_mix': out['norm_mix'], 'w_in': out['w_in'], 'conv_a_w': out['conv_a_w'], 'conv_a_b': out['conv_a_b'], 'ln_a_g': out['ln_a_g'], 'ln_a_b': out['ln_a_b'], 'dn_conv_w': out['dn_conv_w'], 'dn_a_log': out['dn_a_log'], 'dn_dt_bias': out['dn_dt_bias'], 'dn_norm_g': out['dn_norm_g'], 'gm_ln_g': out['gm_ln_g'], 'gm_ln_b': out['gm_ln_b'], 'gm_ws': out['gm_ws'], 'gm_bs': out['gm_bs'], 'pool_w': out['pool_w'], 'pool_scale': out['pool_scale'], 'w_branch': out['w_branch'], 'w_out': out['w_out'], 'norm_xa': out['norm_xa'], 'norm_mem': out['norm_mem'], 'xa_wq': out['xa_wq'], 'xa_wkv': out['xa_wkv'], 'xa_wo': out['xa_wo'], 'norm_mlp': out['norm_mlp'], 'mlp_w1': out['mlp_w1'], 'mlp_w2': out['mlp_w2'], 'norm_f': out['norm_f'], 'loss_target': out['loss_target'], 'm_norm_mix': out['m_norm_mix'], 'm_w_in': out['m_w_in'], 'm_conv_a_w': out['m_conv_a_w'], 'm_conv_a_b': out['m_conv_a_b'], 'm_ln_a_g': out['m_ln_a_g'], 'm_ln_a_b': out['m_ln_a_b'], 'm_dn_conv_w': out['m_dn_conv_w'], 'm_dn_a_log': out['m_dn_a_log'], 'm_dn_dt_bias': out['m_dn_dt_bias'], 'm_dn_norm_g': out['m_dn_norm_g'], 'm_gm_ln_g': out['m_gm_ln_g'], 'm_gm_ln_b': out['m_gm_ln_b'], 'm_gm_ws': out['m_gm_ws'], 'm_gm_bs': out['m_gm_bs'], 'm_pool_w': out['m_pool_w'], 'm_pool_scale': out['m_pool_scale'], 'm_w_branch': out['m_w_branch'], 'm_w_out': out['m_w_out'], 'm_norm_xa': out['m_norm_xa'], 'm_norm_mem': out['m_norm_mem'], 'm_xa_wq': out['m_xa_wq'], 'm_xa_wkv': out['m_xa_wkv'], 'm_xa_wo': out['m_xa_wo'], 'm_norm_mlp': out['m_norm_mlp'], 'm_mlp_w1': out['m_mlp_w1'], 'm_mlp_w2': out['m_mlp_w2'], 'm_norm_f': out['m_norm_f'], 'v_norm_mix': out['v_norm_mix'], 'v_w_in': out['v_w_in'], 'v_conv_a_w': out['v_conv_a_w'], 'v_conv_a_b': out['v_conv_a_b'], 'v_ln_a_g': out['v_ln_a_g'], 'v_ln_a_b': out['v_ln_a_b'], 'v_dn_conv_w': out['v_dn_conv_w'], 'v_dn_a_log': out['v_dn_a_log'], 'v_dn_dt_bias': out['v_dn_dt_bias'], 'v_dn_norm_g': out['v_dn_norm_g'], 'v_gm_ln_g': out['v_gm_ln_g'], 'v_gm_ln_b': out['v_gm_ln_b'], 'v_gm_ws': out['v_gm_ws'], 'v_gm_bs': out['v_gm_bs'], 'v_pool_w': out['v_pool_w'], 'v_pool_scale': out['v_pool_scale'], 'v_w_branch': out['v_w_branch'], 'v_w_out': out['v_w_out'], 'v_norm_xa': out['v_norm_xa'], 'v_norm_mem': out['v_norm_mem'], 'v_xa_wq': out['v_xa_wq'], 'v_xa_wkv': out['v_xa_wkv'], 'v_xa_wo': out['v_xa_wo'], 'v_norm_mlp': out['v_norm_mlp'], 'v_mlp_w1': out['v_mlp_w1'], 'v_mlp_w2': out['v_mlp_w2'], 'v_norm_f': out['v_norm_f']}


def _loss(weights, diff, rest, loss_target):
    with _jax.named_scope("forward"):
        args = {**rest, TWIN_DIFF_INPUT: diff, **{k: w.astype(_WEIGHT_DTYPES[k]) for k, w in weights.items()}}
        y = _forward(args)
    with _jax.named_scope("loss_head"):
        err = _jnp.square(y.astype(_jnp.float32) - loss_target)
        return 0.5 * _jnp.sum(_jnp.mean(err, axis=-1)) if err.ndim else 0.5 * err


def _adamw(w, g, m, v):
    m = ADAM_B1 * m + (1.0 - ADAM_B1) * g
    v = ADAM_B2 * v + (1.0 - ADAM_B2) * _jnp.square(g)
    m_hat = m / (1.0 - ADAM_B1 ** ADAM_STEP)
    v_hat = v / (1.0 - ADAM_B2 ** ADAM_STEP)
    delta = -ADAM_LR * (m_hat / (_jnp.sqrt(v_hat) + ADAM_EPS) + ADAM_WD * w)
    return delta, m, v


def reference(x, mem, norm_mix, w_in, conv_a_w, conv_a_b, ln_a_g, ln_a_b, dn_conv_w, dn_a_log, dn_dt_bias, dn_norm_g, gm_ln_g, gm_ln_b, gm_ws, gm_bs, pool_w, pool_scale, w_branch, w_out, norm_xa, norm_mem, xa_wq, xa_wkv, xa_wo, norm_mlp, mlp_w1, mlp_w2, norm_f, loss_target, m_norm_mix, m_w_in, m_conv_a_w, m_conv_a_b, m_ln_a_g, m_ln_a_b, m_dn_conv_w, m_dn_a_log, m_dn_dt_bias, m_dn_norm_g, m_gm_ln_g, m_gm_ln_b, m_gm_ws, m_gm_bs, m_pool_w, m_pool_scale, m_w_branch, m_w_out, m_norm_xa, m_norm_mem, m_xa_wq, m_xa_wkv, m_xa_wo, m_norm_mlp, m_mlp_w1, m_mlp_w2, m_norm_f, v_norm_mix, v_w_in, v_conv_a_w, v_conv_a_b, v_ln_a_g, v_ln_a_b, v_dn_conv_w, v_dn_a_log, v_dn_dt_bias, v_dn_norm_g, v_gm_ln_g, v_gm_ln_b, v_gm_ws, v_gm_bs, v_pool_w, v_pool_scale, v_w_branch, v_w_out, v_norm_xa, v_norm_mem, v_xa_wq, v_xa_wkv, v_xa_wo, v_norm_mlp, v_mlp_w1, v_mlp_w2, v_norm_f):
    given = dict(x=x, mem=mem, norm_mix=norm_mix, w_in=w_in, conv_a_w=conv_a_w, conv_a_b=conv_a_b, ln_a_g=ln_a_g, ln_a_b=ln_a_b, dn_conv_w=dn_conv_w, dn_a_log=dn_a_log, dn_dt_bias=dn_dt_bias, dn_norm_g=dn_norm_g, gm_ln_g=gm_ln_g, gm_ln_b=gm_ln_b, gm_ws=gm_ws, gm_bs=gm_bs, pool_w=pool_w, pool_scale=pool_scale, w_branch=w_branch, w_out=w_out, norm_xa=norm_xa, norm_mem=norm_mem, xa_wq=xa_wq, xa_wkv=xa_wkv, xa_wo=xa_wo, norm_mlp=norm_mlp, mlp_w1=mlp_w1, mlp_w2=mlp_w2, norm_f=norm_f, loss_target=loss_target, m_norm_mix=m_norm_mix, m_w_in=m_w_in, m_conv_a_w=m_conv_a_w, m_conv_a_b=m_conv_a_b, m_ln_a_g=m_ln_a_g, m_ln_a_b=m_ln_a_b, m_dn_conv_w=m_dn_conv_w, m_dn_a_log=m_dn_a_log, m_dn_dt_bias=m_dn_dt_bias, m_dn_norm_g=m_dn_norm_g, m_gm_ln_g=m_gm_ln_g, m_gm_ln_b=m_gm_ln_b, m_gm_ws=m_gm_ws, m_gm_bs=m_gm_bs, m_pool_w=m_pool_w, m_pool_scale=m_pool_scale, m_w_branch=m_w_branch, m_w_out=m_w_out, m_norm_xa=m_norm_xa, m_norm_mem=m_norm_mem, m_xa_wq=m_xa_wq, m_xa_wkv=m_xa_wkv, m_xa_wo=m_xa_wo, m_norm_mlp=m_norm_mlp, m_mlp_w1=m_mlp_w1, m_mlp_w2=m_mlp_w2, m_norm_f=m_norm_f, v_norm_mix=v_norm_mix, v_w_in=v_w_in, v_conv_a_w=v_conv_a_w, v_conv_a_b=v_conv_a_b, v_ln_a_g=v_ln_a_g, v_ln_a_b=v_ln_a_b, v_dn_conv_w=v_dn_conv_w, v_dn_a_log=v_dn_a_log, v_dn_dt_bias=v_dn_dt_bias, v_dn_norm_g=v_dn_norm_g, v_gm_ln_g=v_gm_ln_g, v_gm_ln_b=v_gm_ln_b, v_gm_ws=v_gm_ws, v_gm_bs=v_gm_bs, v_pool_w=v_pool_w, v_pool_scale=v_pool_scale, v_w_branch=v_w_branch, v_w_out=v_w_out, v_norm_xa=v_norm_xa, v_norm_mem=v_norm_mem, v_xa_wq=v_xa_wq, v_xa_wkv=v_xa_wkv, v_xa_wo=v_xa_wo, v_norm_mlp=v_norm_mlp, v_mlp_w1=v_mlp_w1, v_mlp_w2=v_mlp_w2, v_norm_f=v_norm_f)
    weights = {n: given[n] for n in TWIN_WEIGHTS}
    shared = {n: given[n] for n in SHARED_INPUTS}
    per_example = {n: given[n] for n in ['x', 'mem']}
    grad_fn = _jax.value_and_grad(_loss, argnums=(0, 1))

    def one_microbatch(ex, loss_target):
        ex = dict(ex)
        diff = ex.pop(TWIN_DIFF_INPUT)
        return grad_fn(weights, diff, {**shared, **ex}, loss_target)

    if N_MICROBATCH == 1:
        loss, (grad_w, grad_x) = one_microbatch(per_example, given["loss_target"])
    else:
        def body(carry, xs):
            loss_sum, grad_sum = carry
            l_k, (gw_k, gx_k) = one_microbatch(xs[0], xs[1])
            with _jax.named_scope("update"):
                return (loss_sum + l_k, _jax.tree.map(_jnp.add, grad_sum, gw_k)), gx_k

        init = (_jnp.zeros((), _jnp.float32), _jax.tree.map(_jnp.zeros_like, weights))
        (loss, grad_w), grad_x = _jax.lax.scan(body, init, (per_example, given["loss_target"]))
    with _jax.named_scope("update"):
        delta_w, new_m, new_v = {}, {}, {}
        for n in TWIN_WEIGHTS:
            delta_w[n], new_m[n], new_v[n] = _adamw(weights[n], grad_w[n], given["m_" + n], given["v_" + n])
    return (loss, grad_x, *[grad_w[n] for n in TWIN_WEIGHTS], *[delta_w[n] for n in TWIN_WEIGHTS],
            *[new_m[n] for n in TWIN_WEIGHTS], *[new_v[n] for n in TWIN_WEIGHTS])
```

```python
import functools

import jax
import jax.numpy as jnp
from jax import lax
from jax.experimental import pallas as pl
from jax.experimental.pallas import tpu as pltpu

F32 = jnp.float32
BF16 = jnp.bfloat16
HI = lax.Precision.HIGHEST

D_MODEL = 1024
MIX_W = 512
N_BRANCH = 4
HEADS = 4
HEAD_DIM = 128
CONV_K = 31
DN_CONV_K = 4
DN_CHUNK = 64
GM_CHUNK = 128
POOL_K = 16
XA_HEADS = 4
XA_HEAD_DIM = 256
FFN_W = 4096
IN_W = 8712
Z_W = 9216
LANES = 128
VMEM_LIMIT = 56 * 1024 * 1024

ADAM_LR, ADAM_B1, ADAM_B2, ADAM_EPS, ADAM_WD, ADAM_STEP = 0.001, 0.9, 0.999, 1e-08, 0.01, 10

Z_GATE, Z_A, Z_GM, Z_QKV, Z_DG, Z_POOL, Z_BD = 0, 4096, 5120, 6144, 7680, 8192, 8704
_Z_SEGS = ((4616, 8712), (0, 1024), (3080, 4104), (1024, 2560), (2560, 3072), (4104, 4616), (3072, 3080))

BIG = ("w_in", "w_branch", "w_out", "xa_wq", "xa_wkv", "xa_wo", "mlp_w1", "mlp_w2")
BIG_AXIS = {"w_in": 1, "w_branch": 2, "w_out": 0, "xa_wq": 0, "xa_wkv": 1, "xa_wo": 0, "mlp_w1": 1, "mlp_w2": 0}
CONVS = ("conv_a_w", "dn_conv_w")
WEIGHTS = ("norm_mix", "w_in", "conv_a_w", "conv_a_b", "ln_a_g", "ln_a_b", "dn_conv_w", "dn_a_log", "dn_dt_bias",
           "dn_norm_g", "gm_ln_g", "gm_ln_b", "gm_ws", "gm_bs", "pool_w", "pool_scale", "w_branch", "w_out",
           "norm_xa", "norm_mem", "xa_wq", "xa_wkv", "xa_wo", "norm_mlp", "mlp_w1", "mlp_w2", "norm_f")
SMALL = tuple(n for n in WEIGHTS if n not in BIG)


def _params(sem=None):
    return pltpu.CompilerParams(vmem_limit_bytes=VMEM_LIMIT, dimension_semantics=sem)


def _pick(n, cands):
    for c in cands:
        if n % c == 0:
            return c
    return n


def _make_dots(prec):
    def raw(a, b, ca, cb):
        if prec is None:
            a, b = a.astype(BF16), b.astype(BF16)
        return lax.dot_general(a, b, (((ca,), (cb,)), ((), ())), precision=prec, preferred_element_type=F32)

    nn = jax.custom_vjp(lambda a, b: raw(a, b, 1, 0))
    nt = jax.custom_vjp(lambda a, b: raw(a, b, 1, 1))
    tn = jax.custom_vjp(lambda a, b: raw(a, b, 0, 0))
    nn.defvjp(lambda a, b: (raw(a, b, 1, 0), (a, b)), lambda r, g: (raw(g, r[1], 1, 1), raw(r[0], g, 0, 0)))
    nt.defvjp(lambda a, b: (raw(a, b, 1, 1), (a, b)), lambda r, g: (raw(g, r[1], 1, 0), raw(g, r[0], 0, 0)))
    tn.defvjp(lambda a, b: (raw(a, b, 0, 0), (a, b)), lambda r, g: (raw(r[1], g, 1, 1), raw(r[0], g, 1, 0)))
    return nn, nt, tn


_bnn, _bnt, _btn = _make_dots(None)
_hnn, _hnt, _htn = _make_dots(HI)


def _col(x, j):
    lane = lax.broadcasted_iota(jnp.int32, x.shape, 1)
    return jnp.sum(jnp.where(lane == j, x, 0.0), axis=-1, keepdims=True)


def _rms(x, g, eps=1e-6):
    return x * lax.rsqrt(jnp.mean(x * x, axis=-1, keepdims=True) + eps) * g


def _ln(x, g, b, eps=1e-5):
    xc = x - jnp.mean(x, axis=-1, keepdims=True)
    var = jnp.mean(xc * xc, axis=-1, keepdims=True)
    return xc * lax.rsqrt(var + eps) * g + b


_sigmoid = jax.nn.sigmoid


def _silu(x):
    return x * _sigmoid(x)


def _softplus(x):
    return jnp.maximum(x, 0.0) + jnp.log1p(jnp.exp(-jnp.abs(x)))


def _gelu(x):
    return 0.5 * x * (1.0 + lax.erf(x * 0.7071067811865476))


def f_norm(x, g):
    return (_rms(x, g),)


def f_glu(a_in):
    return (a_in[:, :MIX_W] * _sigmoid(a_in[:, MIX_W:]),)


def f_lnsilu(ac, cb, g, b):
    return (_silu(_ln(ac + cb, g, b)),)


def f_dnprep(qc, bd, a_log, dt_bias):
    t = qc.shape[0]
    qkv = _silu(qc)
    qs, ks, bs, gs = [], [], [], []
    for h in range(HEADS):
        q = qkv[:, h * HEAD_DIM:(h + 1) * HEAD_DIM]
        k = qkv[:, MIX_W + h * HEAD_DIM:MIX_W + (h + 1) * HEAD_DIM]
        qs.append(q * lax.rsqrt(jnp.sum(q * q, axis=-1, keepdims=True) + 1e-6) * (HEAD_DIM ** -0.5))
        ks.append(k * lax.rsqrt(jnp.sum(k * k, axis=-1, keepdims=True) + 1e-6))
        beta = _sigmoid(_col(bd, h))
        g = -jnp.exp(_col(a_log, h)) * _softplus(_col(bd, HEADS + h) + _col(dt_bias, h))
        bs.append(jnp.broadcast_to(beta, (t, HEAD_DIM)))
        gs.append(jnp.broadcast_to(g, (t, HEAD_DIM)))
    cat = lambda xs: jnp.concatenate(xs, axis=1)
    return cat(qs), cat(ks), qkv[:, 2 * MIX_W:], cat(bs), cat(gs)


def f_dnout(o, dgate, g):
    outs = []
    for h in range(HEADS):
        sl = slice(h * HEAD_DIM, (h + 1) * HEAD_DIM)
        outs.append(_rms(o[:, sl], g) * _silu(dgate[:, sl]))
    return (jnp.concatenate(outs, axis=1),)


def f_gmlp(gm_in, lg, lb, ws, bst):
    t = gm_in.shape[0]
    ge = _gelu(gm_in)
    u, vg = ge[:, :MIX_W], _ln(ge[:, MIX_W:], lg, lb)
    ri = lax.broadcasted_iota(jnp.int32, (GM_CHUNK, GM_CHUNK), 0)
    ci = lax.broadcasted_iota(jnp.int32, (GM_CHUNK, GM_CHUNK), 1)
    chunks = []
    for n in range(t // GM_CHUNK):
        vc = vg[n * GM_CHUNK:(n + 1) * GM_CHUNK]
        cols = []
        for g in range(4):
            w = jnp.where(ri >= ci, ws[g * GM_CHUNK:(g + 1) * GM_CHUNK], 0.0)
            cols.append(_bnn(w, vc[:, g * LANES:(g + 1) * LANES]) + _col(bst, g))
        chunks.append(jnp.concatenate(cols, axis=1))
    mixed = chunks[0] if len(chunks) == 1 else jnp.concatenate(chunks, axis=0)
    return (u * mixed,)


def f_poolpost(pc, pw, ps):
    cols = [_bnn(pc[:, g * LANES:(g + 1) * LANES], pw[g * LANES:(g + 1) * LANES]) for g in range(4)]
    return (jnp.concatenate(cols, axis=1) * ps,)


def f_merge(gate, p0, p1, p2, p3):
    m = None
    for n, p in enumerate((p0, p1, p2, p3)):
        t = _sigmoid(gate[:, n * D_MODEL:(n + 1) * D_MODEL]) * p
        m = t if m is None else m + t
    return (m,)


def f_attn(q, kv):
    outs = []
    for h in range(XA_HEADS):
        sl = slice(h * XA_HEAD_DIM, (h + 1) * XA_HEAD_DIM)
        s = _bnt(q[:, sl], kv[:, sl]) * (XA_HEAD_DIM ** -0.5)
        s = s - lax.stop_gradient(jnp.max(s, axis=-1, keepdims=True))
        p = jnp.exp(s)
        p = p / jnp.sum(p, axis=-1, keepdims=True)
        outs.append(_bnn(p, kv[:, D_MODEL + h * XA_HEAD_DIM:D_MODEL + (h + 1) * XA_HEAD_DIM]))
    return (jnp.concatenate(outs, axis=1),)


def f_relu2(u):
    r = jnp.maximum(u, 0.0)
    return (r * r,)


def f_loss(x, t, g):
    e = _rms(x, g) - t
    return (0.5 * jnp.mean(e * e, axis=-1, keepdims=True),)


def _row_spec(tr, width, cb):
    return pl.BlockSpec((tr, width), functools.partial(lambda i, cb: (i, cb), cb=cb))


def _full_spec(shape):
    return pl.BlockSpec(shape, lambda i: (0,) * len(shape))


def _rw(f, rows, params, outs, *, name, tr=256):
    s = rows[0][0].shape[0]
    tr = min(tr, s)
    nr, npar = len(rows), len(params)

    def body(*refs):
        rv = [r[...].astype(F32) for r in refs[:nr]]
        pv = [p[...] for p in refs[nr:nr + npar]]
        res = f(*rv, *pv)
        for o_ref, r in zip(refs[nr + npar:], res):
            o_ref[...] = r.astype(o_ref.dtype)

    return pl.pallas_call(
        body, name=name, grid=(s // tr,),
        in_specs=[_row_spec(tr, w, cb) for _, w, cb in rows] + [_full_spec(p.shape) for p in params],
        out_specs=[_row_spec(tr, w, 0) for w, _ in outs],
        out_shape=[jax.ShapeDtypeStruct((s, w), dt) for w, dt in outs],
        compiler_params=_params(("arbitrary",)),
    )(*[a for a, _, _ in rows], *params)


def _rw_bwd(f, rows, params, cts, *, row_grads, param_grads=(), add=None, primal=(), name, tr=256):
    s = rows[0][0].shape[0]
    tr = min(tr, s)
    nr, npar, nct = len(rows), len(params), len(cts)
    add = add or {}
    add_keys = list(add)

    def body(*refs):
        it = iter(refs)
        row_refs = [next(it) for _ in range(nr)]
        par_refs = [next(it) for _ in range(npar)]
        ct_refs = [next(it) for _ in range(nct)]
        add_refs = {k: next(it) for k in add_keys}
        rg_refs = [next(it) for _ in row_grads]
        pg_refs = [next(it) for _ in param_grads]
        pr_refs = [next(it) for _ in primal]
        rv = [r[...].astype(F32) for r in row_refs]
        pv = [p[...] for p in par_refs]
        out, vjp = jax.vjp(f, *rv, *pv)
        grads = vjp(tuple(c[...].astype(F32) for c in ct_refs))
        for (idx, _), ref in zip(row_grads, rg_refs):
            g = grads[idx]
            if idx in add_refs:
                g = g + add_refs[idx][...].astype(F32)
            ref[...] = g.astype(ref.dtype)

        @pl.when(pl.program_id(0) == 0)
        def _():
            for ref in pg_refs:
                ref[...] = jnp.zeros_like(ref)

        for idx, ref in zip(param_grads, pg_refs):
            ref[...] += grads[nr + idx]
        for (idx, _, _), ref in zip(primal, pr_refs):
            ref[...] = out[idx].astype(ref.dtype)

    in_arrays = [a for a, _, _ in rows] + list(params) + [a for a, _, _ in cts] + [add[k][0] for k in add_keys]
    in_specs = ([_row_spec(tr, w, cb) for _, w, cb in rows] + [_full_spec(p.shape) for p in params]
                + [_row_spec(tr, w, cb) for _, w, cb in cts] + [_row_spec(tr, add[k][1], add[k][2]) for k in add_keys])
    out_specs = ([_row_spec(tr, rows[idx][1], 0) for idx, _ in row_grads]
                 + [_full_spec(params[idx].shape) for idx in param_grads]
                 + [_row_spec(tr, w, 0) for _, w, _ in primal])
    out_shape = ([jax.ShapeDtypeStruct((s, rows[idx][1]), dt) for idx, dt in row_grads]
                 + [jax.ShapeDtypeStruct(params[idx].shape, F32) for idx in param_grads]
                 + [jax.ShapeDtypeStruct((s, w), dt) for _, w, dt in primal])
    return pl.pallas_call(
        body, name=name, grid=(s // tr,), in_specs=in_specs, out_specs=out_specs, out_shape=out_shape,
        compiler_params=_params(("arbitrary",)),
    )(*in_arrays)


def _mm(a, b, *, ta=False, tb=False, out_dtype=F32, add=None, name):
    m, k = (a.shape[1], a.shape[0]) if ta else a.shape
    n = b.shape[0] if tb else b.shape[1]
    tm = _pick(m, (512, 256, 128))
    tn = _pick(n, (1024, 512, 256, 128))
    tk = _pick(k, (512, 256, 128))
    nk = k // tk
    dims = (((0 if ta else 1,), (1 if tb else 0,)), ((), ()))

    def body(*refs):
        if add is None:
            a_ref, b_ref, o_ref, acc = refs
        else:
            a_ref, b_ref, add_ref, o_ref, acc = refs
        kk = pl.program_id(2)

        @pl.when(kk == 0)
        def _():
            acc[...] = jnp.zeros_like(acc)

        acc[...] += lax.dot_general(a_ref[...].astype(BF16), b_ref[...].astype(BF16), dims,
                                    preferred_element_type=F32)

        @pl.when(kk == nk - 1)
        def _():
            r = acc[...]
            if add is not None:
                r = r + add_ref[...]
            o_ref[...] = r.astype(o_ref.dtype)

    a_spec = (pl.BlockSpec((tk, tm), lambda i, j, kk: (kk, i)) if ta else pl.BlockSpec((tm, tk), lambda i, j, kk: (i, kk)))
    b_spec = (pl.BlockSpec((tn, tk), lambda i, j, kk: (j, kk)) if tb else pl.BlockSpec((tk, tn), lambda i, j, kk: (kk, j)))
    o_spec = pl.BlockSpec((tm, tn), lambda i, j, kk: (i, j))
    in_specs = [a_spec, b_spec] + ([o_spec] if add is not None else [])
    args = (a, b) + ((add,) if add is not None else ())
    return pl.pallas_call(
        body, name=name, grid=(m // tm, n // tn, nk), in_specs=in_specs, out_specs=o_spec,
        out_shape=jax.ShapeDtypeStruct((m, n), out_dtype),
        scratch_shapes=[pltpu.VMEM((tm, tn), F32)],
        compiler_params=_params(("arbitrary", "arbitrary", "arbitrary")),
    )(*args)


CONV_TT = 256


def _halo(k):
    return -(-(k - 1) // 8) * 8


def _pool_count(i, c, tt):
    win = lax.shift_left(jnp.int32(2), c)
    t = i * tt + lax.broadcasted_iota(jnp.int32, (tt, LANES), 0)
    return win, jnp.minimum(t + 1, win).astype(F32)


def _conv_fwd(x, xcb0, channels, w, k, *, pool=False, name):
    s = x.shape[0]
    tt = min(CONV_TT, s)
    nt = s // tt
    halo = _halo(k)

    def body(*refs):
        if pool:
            xc_ref, xp_ref, o_ref, xs = refs
        else:
            xc_ref, xp_ref, w_ref, o_ref, xs = refs
        c, i = pl.program_id(0), pl.program_id(1)
        xs[0:halo, :] = jnp.where(i > 0, xp_ref[tt - halo:tt, :], 0.0)
        xs[halo:halo + tt, :] = xc_ref[...]
        acc = jnp.zeros((tt, LANES), F32)
        if pool:
            win, cnt = _pool_count(i, c, tt)
            for j in range(k):
                acc = acc + jnp.where(j >= k - win, xs[pl.ds(halo - (k - 1) + j, tt), :], 0.0)
            o_ref[...] = acc / cnt - xc_ref[...]
        else:
            for j in range(k):
                acc = acc + w_ref[j:j + 1, :] * xs[pl.ds(halo - (k - 1) + j, tt), :]
            o_ref[...] = acc

    in_specs = [pl.BlockSpec((tt, LANES), lambda c, i: (i, xcb0 + c)),
                pl.BlockSpec((tt, LANES), lambda c, i: (jnp.maximum(i - 1, 0), xcb0 + c))]
    args = [x, x]
    if not pool:
        in_specs.append(pl.BlockSpec((k, LANES), lambda c, i: (0, c)))
        args.append(w)
    return pl.pallas_call(
        body, name=name, grid=(channels // LANES, nt), in_specs=in_specs,
        out_specs=pl.BlockSpec((tt, LANES), lambda c, i: (i, c)),
        out_shape=jax.ShapeDtypeStruct((s, channels), F32),
        scratch_shapes=[pltpu.VMEM((halo + tt, LANES), F32)],
        compiler_params=_params(("arbitrary", "arbitrary")),
    )(*args)


def _conv_bwd(x, xcb0, channels, w, k, dy, *, pool=False, dx_dtype=BF16, name):
    s = dy.shape[0]
    tt = min(CONV_TT, s)
    nt = s // tt
    halo = _halo(k)

    def body(*refs):
        if pool:
            dyc_ref, dyn_ref, dx_ref, ys = refs
        else:
            xc_ref, xp_ref, dyc_ref, dyn_ref, w_ref, dx_ref, dw_ref, xs, ys = refs
        c, i = pl.program_id(0), pl.program_id(1)
        dyc = dyc_ref[...]
        dyn = jnp.where(i < nt - 1, dyn_ref[0:halo, :], 0.0)
        acc = jnp.zeros((tt, LANES), F32)
        if pool:
            win, cnt = _pool_count(i, c, tt)
            ys[0:tt, :] = dyc / cnt
            ys[tt:tt + halo, :] = dyn / win.astype(F32)
            for j in range(k):
                acc = acc + jnp.where(j >= k - win, ys[pl.ds((k - 1) - j, tt), :], 0.0)
            dx_ref[...] = (acc - dyc).astype(dx_ref.dtype)
            return
        ys[0:tt, :] = dyc
        ys[tt:tt + halo, :] = dyn
        for j in range(k):
            acc = acc + w_ref[j:j + 1, :] * ys[pl.ds((k - 1) - j, tt), :]
        dx_ref[...] = acc.astype(dx_ref.dtype)

        xs[0:halo, :] = jnp.where(i > 0, xp_ref[tt - halo:tt, :], 0.0)
        xs[halo:halo + tt, :] = xc_ref[...]

        @pl.when(i == 0)
        def _():
            dw_ref[...] = jnp.zeros_like(dw_ref)

        for j in range(k):
            dw_ref[j:j + 1, :] += jnp.sum(dyc * xs[pl.ds(halo - (k - 1) + j, tt), :], axis=0, keepdims=True)

    cur = lambda cb0: pl.BlockSpec((tt, LANES), lambda c, i: (i, cb0 + c))
    dy_specs = [cur(0), pl.BlockSpec((tt, LANES), lambda c, i: (jnp.minimum(i + 1, nt - 1), c))]
    dx_spec = pl.BlockSpec((tt, LANES), lambda c, i: (i, c))
    dx_shape = jax.ShapeDtypeStruct((s, channels), dx_dtype)
    if pool:
        return pl.pallas_call(
            body, name=name, grid=(channels // LANES, nt), in_specs=dy_specs, out_specs=dx_spec, out_shape=dx_shape,
            scratch_shapes=[pltpu.VMEM((tt + halo, LANES), F32)],
            compiler_params=_params(("arbitrary", "arbitrary")),
        )(dy, dy)
    in_specs = [cur(xcb0), pl.BlockSpec((tt, LANES), lambda c, i: (jnp.maximum(i - 1, 0), xcb0 + c))] + dy_specs
    in_specs.append(pl.BlockSpec((k, LANES), lambda c, i: (0, c)))
    return pl.pallas_call(
        body, name=name, grid=(channels // LANES, nt), in_specs=in_specs,
        out_specs=[dx_spec, pl.BlockSpec((k, LANES), lambda c, i: (0, c))],
        out_shape=[dx_shape, jax.ShapeDtypeStruct((k, channels), F32)],
        scratch_shapes=[pltpu.VMEM((halo + tt, LANES), F32), pltpu.VMEM((tt + halo, LANES), F32)],
        compiler_params=_params(("arbitrary", "arbitrary")),
    )(x, x, dy, dy, w)


DN_CPS = 4


def _dn_chunk(q, k, v, bb, gb, state):
    c = DN_CHUNK
    ri = lax.broadcasted_iota(jnp.int32, (c, c), 0)
    ci = lax.broadcasted_iota(jnp.int32, (c, c), 1)
    causal, strict = ri >= ci, ri > ci
    eye = jnp.where(ri == ci, 1.0, 0.0)
    gam = _hnn(jnp.where(causal, 1.0, 0.0), gb)
    gam_i = gam[:, :c]
    gam_j = _hnn(jnp.ones((c, c), F32), eye * gam_i)
    decay = jnp.where(causal, jnp.exp(jnp.where(causal, gam_i - gam_j, 0.0)), 0.0)
    kb = k * bb
    b = -jnp.where(strict, _hnt(kb, k) * decay, 0.0)
    inv = eye + b
    bp = b
    for _ in range(5):
        bp = _hnn(bp, bp)
        inv = inv + _hnn(inv, bp)
    eg = jnp.exp(gam)
    u = _hnn(inv, v * bb)
    w = _hnn(inv, kb * eg)
    attn = _hnt(q, k) * decay
    row = lax.broadcasted_iota(jnp.int32, gam.shape, 0)
    g_last = jnp.sum(jnp.where(row == c - 1, gam, 0.0), axis=0, keepdims=True)
    v_new = u - _hnn(w, state)
    o = _hnn(q * eg, state) + _hnn(attn, v_new)
    new_state = state * jnp.exp(g_last) + _htn(k * jnp.exp(g_last - gam), v_new)
    return o, new_state


def _dn_specs(steps, reverse):
    rows = DN_CPS * DN_CHUNK
    if reverse:
        tile = pl.BlockSpec((rows, HEAD_DIM), lambda h, i: (steps - 1 - i, h))
        st = pl.BlockSpec((DN_CPS, 1, HEAD_DIM, HEAD_DIM), lambda h, i: (steps - 1 - i, h, 0, 0))
    else:
        tile = pl.BlockSpec((rows, HEAD_DIM), lambda h, i: (i, h))
        st = pl.BlockSpec((DN_CPS, 1, HEAD_DIM, HEAD_DIM), lambda h, i: (i, h, 0, 0))
    return tile, st


def _dn_fwd(q, k, v, bb, gb, *, name):
    s = q.shape[0]
    steps = s // (DN_CPS * DN_CHUNK)

    def body(q_ref, k_ref, v_ref, b_ref, g_ref, o_ref, st_ref, state):
        @pl.when(pl.program_id(1) == 0)
        def _():
            state[...] = jnp.zeros_like(state)

        st = state[...]
        for n in range(DN_CPS):
            r = slice(n * DN_CHUNK, (n + 1) * DN_CHUNK)
            st_ref[n, 0] = st
            o, st = _dn_chunk(q_ref[r, :], k_ref[r, :], v_ref[r, :], b_ref[r, :], g_ref[r, :], st)
            o_ref[r, :] = o
        state[...] = st

    tile, stspec = _dn_specs(steps, False)
    return pl.pallas_call(
        body, name=name, grid=(HEADS, steps), in_specs=[tile] * 5, out_specs=[tile, stspec],
        out_shape=[jax.ShapeDtypeStruct((s, MIX_W), F32),
                   jax.ShapeDtypeStruct((s // DN_CHUNK, HEADS, HEAD_DIM, HEAD_DIM), F32)],
        scratch_shapes=[pltpu.VMEM((HEAD_DIM, HEAD_DIM), F32)],
        compiler_params=_params(("arbitrary", "arbitrary")),
    )(q, k, v, bb, gb)


def _dn_bwd(q, k, v, bb, gb, states, do, *, name):
    s = q.shape[0]
    steps = s // (DN_CPS * DN_CHUNK)

    def body(q_ref, k_ref, v_ref, b_ref, g_ref, st_ref, do_ref, dq_ref, dk_ref, dv_ref, db_ref, dg_ref, dstate):
        @pl.when(pl.program_id(1) == 0)
        def _():
            dstate[...] = jnp.zeros_like(dstate)

        dst = dstate[...]
        for n in reversed(range(DN_CPS)):
            r = slice(n * DN_CHUNK, (n + 1) * DN_CHUNK)
            _, vjp = jax.vjp(_dn_chunk, q_ref[r, :], k_ref[r, :], v_ref[r, :], b_ref[r, :], g_ref[r, :], st_ref[n, 0])
            dq, dk, dv, db, dg, dst = vjp((do_ref[r, :], dst))
            dq_ref[r, :] = dq
            dk_ref[r, :] = dk
            dv_ref[r, :] = dv
            db_ref[r, :] = db
            dg_ref[r, :] = dg
        dstate[...] = dst

    tile, stspec = _dn_specs(steps, True)
    return pl.pallas_call(
        body, name=name, grid=(HEADS, steps), in_specs=[tile] * 5 + [stspec, tile], out_specs=[tile] * 5,
        out_shape=[jax.ShapeDtypeStruct((s, MIX_W), F32)] * 5,
        scratch_shapes=[pltpu.VMEM((HEAD_DIM, HEAD_DIM), F32)],
        compiler_params=_params(("arbitrary", "arbitrary")),
    )(q, k, v, bb, gb, states, do)


EW_TILE_BYTES = 1 << 20


def _ew(f, ins, out_dtypes, *, name):
    r, c = ins[0][0].shape[1:] if isinstance(ins[0], tuple) else ins[0].shape
    row_bytes = -(-c // LANES) * LANES * 4
    tr = r
    if r * row_bytes > EW_TILE_BYTES:
        tr = next((t for t in (4096, 2048, 1024, 512, 256, 128, 64, 32, 16, 8)
                   if r % t == 0 and t * row_bytes <= EW_TILE_BYTES), r)
    n_in = len(ins)

    def body(*refs):
        res = f(*[x[...] for x in refs[:n_in]])
        for o_ref, v in zip(refs[n_in:], res):
            o_ref[...] = v.astype(o_ref.dtype)

    in_specs, args = [], []
    for x in ins:
        if isinstance(x, tuple):
            in_specs.append(pl.BlockSpec((None, tr, c), functools.partial(lambda i, j: (j, i, 0), j=x[1])))
            args.append(x[0])
        else:
            in_specs.append(pl.BlockSpec((tr, c), lambda i: (i, 0)))
            args.append(x)
    return pl.pallas_call(
        body, name=name, grid=(r // tr,), in_specs=in_specs,
        out_specs=[pl.BlockSpec((tr, c), lambda i: (i, 0)) for _ in out_dtypes],
        out_shape=[jax.ShapeDtypeStruct((r, c), dt) for dt in out_dtypes],
        compiler_params=_params(("arbitrary",)),
    )(*args)


def f_adamw(w, g, m, v):
    m = ADAM_B1 * m + (1.0 - ADAM_B1) * g
    v = ADAM_B2 * v + (1.0 - ADAM_B2) * (g * g)
    m_hat = m / (1.0 - ADAM_B1 ** ADAM_STEP)
    v_hat = v / (1.0 - ADAM_B2 ** ADAM_STEP)
    return -ADAM_LR * (m_hat / (jnp.sqrt(v_hat) + ADAM_EPS) + ADAM_WD * w), m, v


def f_sum4(a, b, c, d):
    return (((a.astype(F32) + b.astype(F32)) + c.astype(F32)) + d.astype(F32),)


def f_add2(a, b):
    return (a + b,)


def _adamw(w, g, m, v, *, name):
    shape = w.shape
    two = (1, shape[0]) if w.ndim == 1 else (-1, shape[-1])
    outs = _ew(f_adamw, [t.reshape(two) for t in (w, g, m, v)], [F32, F32, F32], name=name)
    return [o.reshape(shape) for o in outs]


_ANY = pl.BlockSpec(memory_space=pl.ANY)


def _xy_exchange(src, *, broadcast, name):
    r = src.shape[-2]

    def body(src_ref, out_ref, send_sems, recv_sems, local_sem):
        x, y, c = lax.axis_index("x"), lax.axis_index("y"), lax.axis_index("c")
        me = 2 * x + y
        peers = [(1 - x, y), (x, 1 - y), (1 - x, 1 - y)]

        def block(j):
            return src_ref if broadcast else src_ref.at[j]

        def copy(n, px, py, src_blk, dst_blk):
            return pltpu.make_async_remote_copy(
                src_ref=src_blk, dst_ref=dst_blk, send_sem=send_sems.at[n], recv_sem=recv_sems.at[n],
                device_id=(px, py, c), device_id_type=pl.DeviceIdType.MESH)

        mine = pltpu.make_async_copy(block(me), out_ref.at[me], local_sem)
        mine.start()
        sends = [copy(n, px, py, block(2 * px + py), out_ref.at[me]) for n, (px, py) in enumerate(peers)]
        for cp in sends:
            cp.start()
        for cp in sends:
            cp.wait_send()
        for n, (px, py) in enumerate(peers):
            copy(n, px, py, block(me), out_ref.at[2 * px + py]).wait_recv()
        mine.wait()

    return pl.pallas_call(
        body, name=name, in_specs=[_ANY], out_specs=_ANY,
        out_shape=jax.ShapeDtypeStruct((4, r, LANES), src.dtype),
        scratch_shapes=[pltpu.SemaphoreType.DMA((3,)), pltpu.SemaphoreType.DMA((3,)), pltpu.SemaphoreType.DMA(())],
        compiler_params=pltpu.CompilerParams(has_side_effects=True),
    )(src)


def _c_swap(src, *, name):
    def body(src_ref, out_ref, send_sem, recv_sem):
        sibling = (lax.axis_index("x"), lax.axis_index("y"), 1 - lax.axis_index("c"))
        cp = pltpu.make_async_remote_copy(src_ref=src_ref, dst_ref=out_ref, send_sem=send_sem, recv_sem=recv_sem,
                                          device_id=sibling, device_id_type=pl.DeviceIdType.MESH)
        cp.start()
        cp.wait()

    return pl.pallas_call(
        body, name=name, in_specs=[_ANY], out_specs=_ANY, out_shape=jax.ShapeDtypeStruct(src.shape, src.dtype),
        scratch_shapes=[pltpu.SemaphoreType.DMA(()), pltpu.SemaphoreType.DMA(())],
        compiler_params=pltpu.CompilerParams(has_side_effects=True),
    )(src)


def _all_sum(src, *, broadcast, name):
    got = _xy_exchange(src, broadcast=broadcast, name=name + "_xy")
    part = _ew(f_sum4, [(got, j) for j in range(4)], [F32], name=name + "_sum4")[0]
    other = _c_swap(part, name=name + "_c")
    return _ew(f_add2, [part, other], [F32], name=name + "_add2")[0]


def _flat_rows(parts, dtype):
    flat = jnp.concatenate([p.reshape(-1).astype(dtype) for p in parts])
    pad = (-flat.shape[0]) % (8 * LANES)
    if pad:
        flat = jnp.concatenate([flat, jnp.zeros((pad,), dtype)])
    return flat.reshape(-1, LANES)


def _unflat(buf, shapes):
    flat = buf.reshape(-1)
    out, off = [], 0
    for shp in shapes:
        n = 1
        for d in shp:
            n *= d
        out.append(flat[off:off + n].reshape(shp))
        off += n
    return out


def _regroup_w_in(w):
    cols = [w[:, a:b] for a, b in _Z_SEGS] + [jnp.zeros((w.shape[0], Z_W - IN_W), w.dtype)]
    return jnp.concatenate(cols, axis=1)


def _ungroup_w_in(wz):
    starts, off = {}, 0
    for a, b in _Z_SEGS:
        starts[a] = (off, off + b - a)
        off += b - a
    return jnp.concatenate([wz[:, starts[a][0]:starts[a][1]] for a in sorted(starts)], axis=1)


def _row(vec):
    return vec.reshape(1, -1)


def _lane_pad(vec):
    return jnp.zeros((1, LANES), F32).at[0, :vec.shape[0]].set(vec)


def _layer_fwd(x, mem, w, l):
    nm = lambda s: f"{s}_l{l}"
    sv = {"x0": x}
    h = _rw(f_norm, [(x, D_MODEL, 0)], [_row(w["norm_mix"])], [(D_MODEL, BF16)], name=nm("norm_mix"))[0]
    z = _mm(h, w["w_in_z"], name=nm("in_proj"))
    a_glu = _rw(f_glu, [(z, 2 * MIX_W, Z_A // (2 * MIX_W))], [], [(MIX_W, F32)], name=nm("glu"))[0]
    ac = _conv_fwd(a_glu, 0, MIX_W, w["conv_a_w"], CONV_K, name=nm("conv_a"))
    a_par = [_row(w["conv_a_b"]), _row(w["ln_a_g"]), _row(w["ln_a_b"])]
    a_out = _rw(f_lnsilu, [(ac, MIX_W, 0)], a_par, [(MIX_W, BF16)], name=nm("ln_a"))[0]
    qc = _conv_fwd(z, Z_QKV // LANES, 3 * MIX_W, w["dn_conv_w"], DN_CONV_K, name=nm("conv_dn"))
    dn_par = [_lane_pad(w["dn_a_log"]), _lane_pad(w["dn_dt_bias"])]
    qn, kn, v, bb, gb = _rw(f_dnprep, [(qc, 3 * MIX_W, 0), (z, LANES, Z_BD // LANES)], dn_par,
                            [(MIX_W, F32)] * 5, name=nm("dn_prep"))
    o, states = _dn_fwd(qn, kn, v, bb, gb, name=nm("dn_scan"))
    o_out = _rw(f_dnout, [(o, MIX_W, 0), (z, MIX_W, Z_DG // MIX_W)], [_row(w["dn_norm_g"])], [(MIX_W, BF16)],
                name=nm("dn_out"))[0]
    gm_par = [_row(w["gm_ln_g"]), _row(w["gm_ln_b"]), w["gm_ws"].reshape(4 * GM_CHUNK, GM_CHUNK), w["gm_bs"].T]
    c_out = _rw(f_gmlp, [(z, 2 * MIX_W, Z_GM // (2 * MIX_W))], gm_par, [(MIX_W, BF16)], name=nm("gmlp"))[0]
    pc = _conv_fwd(z, Z_POOL // LANES, MIX_W, None, POOL_K, pool=True, name=nm("pool"))
    p_par = [w["pool_w"].reshape(4 * LANES, LANES), _row(w["pool_scale"])]
    p_out = _rw(f_poolpost, [(pc, MIX_W, 0)], p_par, [(MIX_W, BF16)], name=nm("pool_post"))[0]
    branches = [a_out, o_out, c_out, p_out]
    proj = [_mm(br, w["w_branch"][n], name=nm(f"branch{n}")) for n, br in enumerate(branches)]
    merged = _rw(f_merge, [(z, N_BRANCH * D_MODEL, 0)] + [(p, D_MODEL, 0) for p in proj], [], [(D_MODEL, BF16)],
                 name=nm("merge"), tr=128)[0]
    x1 = _mm(merged, w["w_out"], add=x, name=nm("out_proj"))
    sv.update(h=h, z=z, a_glu=a_glu, ac=ac, qc=qc, qn=qn, kn=kn, v=v, bb=bb, gb=gb, o=o, states=states, pc=pc,
              branches=branches, proj=proj, merged=merged, x1=x1)
    h2 = _rw(f_norm, [(x1, D_MODEL, 0)], [_row(w["norm_xa"])], [(D_MODEL, BF16)], name=nm("norm_xa"))[0]
    mn = _rw(f_norm, [(mem, D_MODEL, 0)], [_row(w["norm_mem"])], [(D_MODEL, BF16)], name=nm("norm_mem"))[0]
    kv = _mm(mn, w["xa_wkv"], name=nm("xa_kv"))
    q = _mm(h2, w["xa_wq"], name=nm("xa_q"))
    att = _rw(f_attn, [(q, D_MODEL, 0)], [kv], [(D_MODEL, BF16)], name=nm("xa_attn"))[0]
    x2 = _mm(att, w["xa_wo"], add=x1, name=nm("xa_o"))
    sv.update(h2=h2, mn=mn, kv=kv, q=q, att=att, x2=x2)
    h3 = _rw(f_norm, [(x2, D_MODEL, 0)], [_row(w["norm_mlp"])], [(D_MODEL, BF16)], name=nm("norm_mlp"))[0]
    u = _mm(h3, w["mlp_w1"], name=nm("mlp_up"))
    act = _rw(f_relu2, [(u, FFN_W, 0)], [], [(FFN_W, BF16)], name=nm("relu2"))[0]
    x3 = _mm(act, w["mlp_w2"], add=x2, name=nm("mlp_down"))
    sv.update(h3=h3, u=u, act=act)
    return x3, sv


def _layer_bwd(dx, mem, w, sv, l):
    nm = lambda s: f"{s}_bwd_l{l}"
    s = dx.shape[0]
    g = {}
    dact = _mm(dx, w["mlp_w2"], tb=True, name=nm("mlp_down_dx"))
    g["mlp_w2"] = _mm(sv["act"], dx, ta=True, name=nm("mlp_down_dw"))
    du = _rw_bwd(f_relu2, [(sv["u"], FFN_W, 0)], [], [(dact, FFN_W, 0)], row_grads=[(0, BF16)], name=nm("relu2"))[0]
    g["mlp_w1"] = _mm(sv["h3"], du, ta=True, name=nm("mlp_up_dw"))
    dh3 = _mm(du, w["mlp_w1"], tb=True, name=nm("mlp_up_dx"))
    dx2, g["norm_mlp"] = _rw_bwd(f_norm, [(sv["x2"], D_MODEL, 0)], [_row(w["norm_mlp"])], [(dh3, D_MODEL, 0)],
                                 row_grads=[(0, F32)], param_grads=[0], add={0: (dx, D_MODEL, 0)}, name=nm("norm_mlp"))
    datt = _mm(dx2, w["xa_wo"], tb=True, name=nm("xa_o_dx"))
    g["xa_wo"] = _mm(sv["att"], dx2, ta=True, name=nm("xa_o_dw"))
    dq, dkv = _rw_bwd(f_attn, [(sv["q"], D_MODEL, 0)], [sv["kv"]], [(datt, D_MODEL, 0)], row_grads=[(0, BF16)],
                      param_grads=[0], name=nm("xa_attn"))
    g["xa_wq"] = _mm(sv["h2"], dq, ta=True, name=nm("xa_q_dw"))
    dh2 = _mm(dq, w["xa_wq"], tb=True, name=nm("xa_q_dx"))
    g["xa_wkv"] = _mm(sv["mn"], dkv, ta=True, name=nm("xa_kv_dw"))
    dmn = _mm(dkv, w["xa_wkv"], tb=True, name=nm("xa_kv_dx"))
    g["norm_mem"] = _rw_bwd(f_norm, [(mem, D_MODEL, 0)], [_row(w["norm_mem"])], [(dmn, D_MODEL, 0)], row_grads=[],
                            param_grads=[0], name=nm("norm_mem"))[0]
    dx1, g["norm_xa"] = _rw_bwd(f_norm, [(sv["x1"], D_MODEL, 0)], [_row(w["norm_xa"])], [(dh2, D_MODEL, 0)],
                                row_grads=[(0, F32)], param_grads=[0], add={0: (dx2, D_MODEL, 0)}, name=nm("norm_xa"))
    z = sv["z"]
    dmerged = _mm(dx1, w["w_out"], tb=True, name=nm("out_proj_dx"))
    g["w_out"] = _mm(sv["merged"], dx1, ta=True, name=nm("out_proj_dw"))
    mg = _rw_bwd(f_merge, [(z, N_BRANCH * D_MODEL, 0)] + [(p, D_MODEL, 0) for p in sv["proj"]], [],
                 [(dmerged, D_MODEL, 0)], row_grads=[(i, BF16) for i in range(5)], name=nm("merge"), tr=128)
    dgate, dproj = mg[0], mg[1:]
    g["w_branch"] = jnp.stack([_mm(br, dp, ta=True, name=nm(f"branch{n}_dw"))
                               for n, (br, dp) in enumerate(zip(sv["branches"], dproj))])
    dbr = [_mm(dp, w["w_branch"][n], tb=True, name=nm(f"branch{n}_dx")) for n, dp in enumerate(dproj)]
    p_par = [w["pool_w"].reshape(4 * LANES, LANES), _row(w["pool_scale"])]
    dpc, dpw, g["pool_scale"] = _rw_bwd(f_poolpost, [(sv["pc"], MIX_W, 0)], p_par, [(dbr[3], MIX_W, 0)],
                                        row_grads=[(0, F32)], param_grads=[0, 1], name=nm("pool_post"))
    g["pool_w"] = dpw.reshape(4, LANES, LANES)
    dpool = _conv_bwd(None, 0, MIX_W, None, POOL_K, dpc, pool=True, name=nm("pool"))
    gm_par = [_row(w["gm_ln_g"]), _row(w["gm_ln_b"]), w["gm_ws"].reshape(4 * GM_CHUNK, GM_CHUNK), w["gm_bs"].T]
    dgm, g["gm_ln_g"], g["gm_ln_b"], dws, dbst = _rw_bwd(
        f_gmlp, [(z, 2 * MIX_W, Z_GM // (2 * MIX_W))], gm_par, [(dbr[2], MIX_W, 0)], row_grads=[(0, BF16)],
        param_grads=[0, 1, 2, 3], name=nm("gmlp"))
    g["gm_ws"] = dws.reshape(4, GM_CHUNK, GM_CHUNK)
    g["gm_bs"] = dbst.T
    do, ddg, g["dn_norm_g"] = _rw_bwd(f_dnout, [(sv["o"], MIX_W, 0), (z, MIX_W, Z_DG // MIX_W)], [_row(w["dn_norm_g"])],
                                      [(dbr[1], MIX_W, 0)], row_grads=[(0, F32), (1, BF16)], param_grads=[0],
                                      name=nm("dn_out"))
    dqn, dkn, dv, dbb, dgb = _dn_bwd(sv["qn"], sv["kn"], sv["v"], sv["bb"], sv["gb"], sv["states"], do, name=nm("dn_scan"))
    dn_par = [_lane_pad(w["dn_a_log"]), _lane_pad(w["dn_dt_bias"])]
    dqc, dbd, dal, ddt = _rw_bwd(
        f_dnprep, [(sv["qc"], 3 * MIX_W, 0), (z, LANES, Z_BD // LANES)], dn_par,
        [(t, MIX_W, 0) for t in (dqn, dkn, dv, dbb, dgb)], row_grads=[(0, F32), (1, BF16)], param_grads=[0, 1],
        name=nm("dn_prep"))
    g["dn_a_log"], g["dn_dt_bias"] = dal[0, :HEADS], ddt[0, :HEADS]
    dqkv, g["dn_conv_w"] = _conv_bwd(z, Z_QKV // LANES, 3 * MIX_W, w["dn_conv_w"], DN_CONV_K, dqc, name=nm("conv_dn"))
    a_par = [_row(w["conv_a_b"]), _row(w["ln_a_g"]), _row(w["ln_a_b"])]
    dac, g["conv_a_b"], g["ln_a_g"], g["ln_a_b"] = _rw_bwd(
        f_lnsilu, [(sv["ac"], MIX_W, 0)], a_par, [(dbr[0], MIX_W, 0)], row_grads=[(0, F32)], param_grads=[0, 1, 2],
        name=nm("ln_a"))
    dglu, g["conv_a_w"] = _conv_bwd(sv["a_glu"], 0, MIX_W, w["conv_a_w"], CONV_K, dac, dx_dtype=F32, name=nm("conv_a"))
    da_in = _rw_bwd(f_glu, [(z, 2 * MIX_W, Z_A // (2 * MIX_W))], [], [(dglu, MIX_W, 0)], row_grads=[(0, BF16)],
                    name=nm("glu"))[0]
    dz = jnp.concatenate([dgate, da_in, dgm, dqkv, ddg, dpool, dbd, jnp.zeros((s, Z_W - Z_BD - LANES), BF16)], axis=1)
    g["w_in"] = _ungroup_w_in(_mm(sv["h"], dz, ta=True, name=nm("in_proj_dw")))
    dh = _mm(dz, w["w_in_z"], tb=True, name=nm("in_proj_dx"))
    dx0, g["norm_mix"] = _rw_bwd(f_norm, [(sv["x0"], D_MODEL, 0)], [_row(w["norm_mix"])], [(dh, D_MODEL, 0)],
                                 row_grads=[(0, F32)], param_grads=[0], add={0: (dx1, D_MODEL, 0)}, name=nm("norm_mix"))
    for n in ("norm_mlp", "norm_xa", "norm_mem", "norm_mix", "pool_scale", "gm_ln_g", "gm_ln_b", "dn_norm_g",
              "conv_a_b", "ln_a_g", "ln_a_b"):
        g[n] = g[n].reshape(-1)
    return dx0, g


def _shard_slice(a, axis, j):
    n = a.shape[axis] // 4
    return lax.slice_in_dim(a, j * n, (j + 1) * n, axis=axis)


def kernel(x, mem, norm_mix, w_in, conv_a_w, conv_a_b, ln_a_g, ln_a_b, dn_conv_w, dn_a_log, dn_dt_bias, dn_norm_g, gm_ln_g, gm_ln_b, gm_ws, gm_bs, pool_w, pool_scale, w_branch, w_out, norm_xa, norm_mem, xa_wq, xa_wkv, xa_wo, norm_mlp, mlp_w1, mlp_w2, norm_f, loss_target, m_norm_mix, m_w_in, m_conv_a_w, m_conv_a_b, m_ln_a_g, m_ln_a_b, m_dn_conv_w, m_dn_a_log, m_dn_dt_bias, m_dn_norm_g, m_gm_ln_g, m_gm_ln_b, m_gm_ws, m_gm_bs, m_pool_w, m_pool_scale, m_w_branch, m_w_out, m_norm_xa, m_norm_mem, m_xa_wq, m_xa_wkv, m_xa_wo, m_norm_mlp, m_mlp_w1, m_mlp_w2, m_norm_f, v_norm_mix, v_w_in, v_conv_a_w, v_conv_a_b, v_ln_a_g, v_ln_a_b, v_dn_conv_w, v_dn_a_log, v_dn_dt_bias, v_dn_norm_g, v_gm_ln_g, v_gm_ln_b, v_gm_ws, v_gm_bs, v_pool_w, v_pool_scale, v_w_branch, v_w_out, v_norm_xa, v_norm_mem, v_xa_wq, v_xa_wkv, v_xa_wo, v_norm_mlp, v_mlp_w1, v_mlp_w2, v_norm_f):
    given = dict(locals())
    wts = {n: given[n] for n in WEIGHTS}
    depth = norm_mix.shape[0]
    x = x[0]
    mem = mem[0]
    tgt = loss_target[0]

    big_shapes = [wts[n].shape[1:] for n in BIG]
    mine = jnp.stack([_flat_rows([wts[n][l] for n in BIG], BF16) for l in range(depth)])
    rows_per_layer = mine.shape[1]
    gathered = _xy_exchange(mine.reshape(-1, LANES), broadcast=True, name="gather_weights")
    gathered = gathered.reshape(4, depth, rows_per_layer, LANES)
    conv_shapes = [wts[n].shape for n in CONVS]
    conv_gathered = _xy_exchange(_flat_rows([wts[n] for n in CONVS], F32), broadcast=True, name="gather_filters")
    conv_parts = [_unflat(conv_gathered[j], conv_shapes) for j in range(4)]
    conv_full = {n: jnp.concatenate([conv_parts[j][i] for j in range(4)], axis=2) for i, n in enumerate(CONVS)}

    def layer_weights(l):
        parts = [_unflat(gathered[j, l], big_shapes) for j in range(4)]
        w = {n: jnp.concatenate([parts[j][i] for j in range(4)], axis=BIG_AXIS[n]) for i, n in enumerate(BIG)}
        w["w_in_z"] = _regroup_w_in(w.pop("w_in"))
        for n in SMALL:
            if n != "norm_f":
                w[n] = conv_full[n][l] if n in CONVS else wts[n][l]
        return w

    saved, layer_w = [], []
    for l in range(depth):
        w = layer_weights(l)
        x, sv = _layer_fwd(x, mem, w, l)
        saved.append(sv)
        layer_w.append(w)
    dx, g_norm_f, loss_rows = _rw_bwd(
        f_loss, [(x, D_MODEL, 0), (tgt, D_MODEL, 0)], [_row(norm_f)], [(jnp.ones((x.shape[0], 1), F32), 1, 0)],
        row_grads=[(0, F32)], param_grads=[0], primal=[(0, 1, F32)], name="loss_head")
    loss = lax.psum(jnp.sum(loss_rows), ("x", "y", "c"))

    grads = [None] * depth
    for l in reversed(range(depth)):
        dx, grads[l] = _layer_bwd(dx, mem, layer_w[l], saved[l], l)
        saved[l] = None
    grad_x = dx[None]

    blocks = jnp.stack([
        jnp.concatenate([_flat_rows([_shard_slice(grads[l][n], BIG_AXIS[n], j) for n in BIG], BF16)
                         for l in range(depth)])
        for j in range(4)])
    big_sum = _all_sum(blocks, broadcast=False, name="reduce_big").reshape(depth, rows_per_layer, LANES)
    big_parts = [_unflat(big_sum[l], big_shapes) for l in range(depth)]
    gw = {n: jnp.stack([big_parts[l][i] for l in range(depth)]) for i, n in enumerate(BIG)}

    small_names = [n for n in SMALL if n != "norm_f"]
    small_shapes = [(depth,) + (conv_full[n].shape[1:] if n in CONVS else wts[n].shape[1:]) for n in small_names]
    small_buf = _flat_rows([jnp.stack([grads[l][n] for l in range(depth)]) for n in small_names] + [g_norm_f], F32)
    small_sum = _all_sum(small_buf, broadcast=True, name="reduce_small")
    small_parts = _unflat(small_sum, small_shapes + [norm_f.shape])
    me = 2 * lax.axis_index("x") + lax.axis_index("y")
    for n, t in zip(small_names + ["norm_f"], small_parts):
        if n in CONVS:
            width = wts[n].shape[2]
            t = lax.dynamic_slice_in_dim(t, me * width, width, axis=2)
        gw[n] = t

    deltas, new_m, new_v = {}, {}, {}
    for n in WEIGHTS:
        deltas[n], new_m[n], new_v[n] = _adamw(wts[n], gw[n], given["m_" + n], given["v_" + n], name=f"adamw_{n}")
    return (loss, grad_x, *[gw[n] for n in WEIGHTS], *[deltas[n] for n in WEIGHTS], *[new_m[n] for n in WEIGHTS],
            *[new_v[n] for n in WEIGHTS])
```

```python
import functools

import jax
import jax.numpy as jnp
from jax import lax
from jax.experimental import pallas as pl
from jax.experimental.pallas import tpu as pltpu

F32 = jnp.float32
BF16 = jnp.bfloat16
HI = lax.Precision.HIGHEST

D_MODEL = 1024
MIX_W = 512
N_BRANCH = 4
HEADS = 4
HEAD_DIM = 128
CONV_K = 31
DN_CONV_K = 4
DN_CHUNK = 64
GM_CHUNK = 128
POOL_K = 16
XA_HEADS = 4
XA_HEAD_DIM = 256
FFN_W = 4096
IN_W = 8712
Z_W = 9216
LANES = 128
VMEM_LIMIT = 56 * 1024 * 1024

ADAM_LR, ADAM_B1, ADAM_B2, ADAM_EPS, ADAM_WD, ADAM_STEP = 0.001, 0.9, 0.999, 1e-08, 0.01, 10

Z_GATE, Z_A, Z_GM, Z_QKV, Z_DG, Z_POOL, Z_BD = 0, 4096, 5120, 6144, 7680, 8192, 8704
_Z_SEGS = ((4616, 8712), (0, 1024), (3080, 4104), (1024, 2560), (2560, 3072), (4104, 4616), (3072, 3080))

BIG = ("w_in", "w_branch", "w_out", "xa_wq", "xa_wkv", "xa_wo", "mlp_w1", "mlp_w2")
BIG_AXIS = {"w_in": 1, "w_branch": 2, "w_out": 0, "xa_wq": 0, "xa_wkv": 1, "xa_wo": 0, "mlp_w1": 1, "mlp_w2": 0}
CONVS = ("conv_a_w", "dn_conv_w")
WEIGHTS = ("norm_mix", "w_in", "conv_a_w", "conv_a_b", "ln_a_g", "ln_a_b", "dn_conv_w", "dn_a_log", "dn_dt_bias",
           "dn_norm_g", "gm_ln_g", "gm_ln_b", "gm_ws", "gm_bs", "pool_w", "pool_scale", "w_branch", "w_out",
           "norm_xa", "norm_mem", "xa_wq", "xa_wkv", "xa_wo", "norm_mlp", "mlp_w1", "mlp_w2", "norm_f")
SMALL = tuple(n for n in WEIGHTS if n not in BIG)


def _params(sem=None):
    return pltpu.CompilerParams(vmem_limit_bytes=VMEM_LIMIT, dimension_semantics=sem)


def _pick(n, cands):
    for c in cands:
        if n % c == 0:
            return c
    return n


def _make_dots(prec):
    def raw(a, b, ca, cb):
        if prec is None:
            a, b = a.astype(BF16), b.astype(BF16)
        return lax.dot_general(a, b, (((ca,), (cb,)), ((), ())), precision=prec, preferred_element_type=F32)

    nn = jax.custom_vjp(lambda a, b: raw(a, b, 1, 0))
    nt = jax.custom_vjp(lambda a, b: raw(a, b, 1, 1))
    tn = jax.custom_vjp(lambda a, b: raw(a, b, 0, 0))
    nn.defvjp(lambda a, b: (raw(a, b, 1, 0), (a, b)), lambda r, g: (raw(g, r[1], 1, 1), raw(r[0], g, 0, 0)))
    nt.defvjp(lambda a, b: (raw(a, b, 1, 1), (a, b)), lambda r, g: (raw(g, r[1], 1, 0), raw(g, r[0], 0, 0)))
    tn.defvjp(lambda a, b: (raw(a, b, 0, 0), (a, b)), lambda r, g: (raw(r[1], g, 1, 1), raw(r[0], g, 1, 0)))
    return nn, nt, tn


_bnn, _bnt, _btn = _make_dots(None)
_hnn, _hnt, _htn = _make_dots(HI)
_mnn, _mnt, _mtn = _make_dots(lax.Precision.HIGH)


def _col(x, j):
    lane = lax.broadcasted_iota(jnp.int32, x.shape, 1)
    return jnp.sum(jnp.where(lane == j, x, 0.0), axis=-1, keepdims=True)


def _rms(x, g, eps=1e-6):
    return x * lax.rsqrt(jnp.mean(x * x, axis=-1, keepdims=True) + eps) * g


def _ln(x, g, b, eps=1e-5):
    xc = x - jnp.mean(x, axis=-1, keepdims=True)
    var = jnp.mean(xc * xc, axis=-1, keepdims=True)
    return xc * lax.rsqrt(var + eps) * g + b


_sigmoid = jax.nn.sigmoid


def _silu(x):
    return x * _sigmoid(x)


def _softplus(x):
    return jnp.maximum(x, 0.0) + jnp.log1p(jnp.exp(-jnp.abs(x)))


def _gelu(x):
    return 0.5 * x * (1.0 + lax.erf(x * 0.7071067811865476))


def f_norm(x, g):
    return (_rms(x, g),)


def f_glu(a_in):
    return (a_in[:, :MIX_W] * _sigmoid(a_in[:, MIX_W:]),)


def f_lnsilu(ac, cb, g, b):
    return (_silu(_ln(ac + cb, g, b)),)


def f_dnprep(qc, bd, a_log, dt_bias):
    t = qc.shape[0]
    qkv = _silu(qc)
    qs, ks, bs, gs = [], [], [], []
    for h in range(HEADS):
        q = qkv[:, h * HEAD_DIM:(h + 1) * HEAD_DIM]
        k = qkv[:, MIX_W + h * HEAD_DIM:MIX_W + (h + 1) * HEAD_DIM]
        qs.append(q * lax.rsqrt(jnp.sum(q * q, axis=-1, keepdims=True) + 1e-6) * (HEAD_DIM ** -0.5))
        ks.append(k * lax.rsqrt(jnp.sum(k * k, axis=-1, keepdims=True) + 1e-6))
        beta = _sigmoid(_col(bd, h))
        g = -jnp.exp(_col(a_log, h)) * _softplus(_col(bd, HEADS + h) + _col(dt_bias, h))
        bs.append(jnp.broadcast_to(beta, (t, HEAD_DIM)))
        gs.append(jnp.broadcast_to(g, (t, HEAD_DIM)))
    cat = lambda xs: jnp.concatenate(xs, axis=1)
    return cat(qs), cat(ks), qkv[:, 2 * MIX_W:], cat(bs), cat(gs)


def f_dnout(o, dgate, g):
    outs = []
    for h in range(HEADS):
        sl = slice(h * HEAD_DIM, (h + 1) * HEAD_DIM)
        outs.append(_rms(o[:, sl], g) * _silu(dgate[:, sl]))
    return (jnp.concatenate(outs, axis=1),)


def f_gmlp(gm_in, lg, lb, ws, bst):
    t = gm_in.shape[0]
    ge = _gelu(gm_in)
    u, vg = ge[:, :MIX_W], _ln(ge[:, MIX_W:], lg, lb)
    ri = lax.broadcasted_iota(jnp.int32, (GM_CHUNK, GM_CHUNK), 0)
    ci = lax.broadcasted_iota(jnp.int32, (GM_CHUNK, GM_CHUNK), 1)
    chunks = []
    for n in range(t // GM_CHUNK):
        vc = vg[n * GM_CHUNK:(n + 1) * GM_CHUNK]
        cols = []
        for g in range(4):
            w = jnp.where(ri >= ci, ws[g * GM_CHUNK:(g + 1) * GM_CHUNK], 0.0)
            cols.append(_bnn(w, vc[:, g * LANES:(g + 1) * LANES]) + _col(bst, g))
        chunks.append(jnp.concatenate(cols, axis=1))
    mixed = chunks[0] if len(chunks) == 1 else jnp.concatenate(chunks, axis=0)
    return (u * mixed,)


def f_poolpost(pc, pw, ps):
    cols = [_bnn(pc[:, g * LANES:(g + 1) * LANES], pw[g * LANES:(g + 1) * LANES]) for g in range(4)]
    return (jnp.concatenate(cols, axis=1) * ps,)


def f_merge(gate, p0, p1, p2, p3):
    m = None
    for n, p in enumerate((p0, p1, p2, p3)):
        t = _sigmoid(gate[:, n * D_MODEL:(n + 1) * D_MODEL]) * p
        m = t if m is None else m + t
    return (m,)


def f_attn(q, kv):
    outs = []
    for h in range(XA_HEADS):
        sl = slice(h * XA_HEAD_DIM, (h + 1) * XA_HEAD_DIM)
        s = _bnt(q[:, sl], kv[:, sl]) * (XA_HEAD_DIM ** -0.5)
        s = s - lax.stop_gradient(jnp.max(s, axis=-1, keepdims=True))
        p = jnp.exp(s)
        p = p / jnp.sum(p, axis=-1, keepdims=True)
        outs.append(_bnn(p, kv[:, D_MODEL + h * XA_HEAD_DIM:D_MODEL + (h + 1) * XA_HEAD_DIM]))
    return (jnp.concatenate(outs, axis=1),)


def f_relu2(u):
    r = jnp.maximum(u, 0.0)
    return (r * r,)


def f_loss(x, t, g):
    e = _rms(x, g) - t
    return (0.5 * jnp.mean(e * e, axis=-1, keepdims=True),)


def _row_spec(tr, width, cb):
    return pl.BlockSpec((tr, width), functools.partial(lambda i, cb: (i, cb), cb=cb))


def _full_spec(shape):
    return pl.BlockSpec(shape, lambda i: (0,) * len(shape))


def _rw(f, rows, params, outs, *, name, tr=256):
    s = rows[0][0].shape[0]
    tr = min(tr, s)
    nr, npar = len(rows), len(params)

    def body(*refs):
        rv = [r[...].astype(F32) for r in refs[:nr]]
        pv = [p[...] for p in refs[nr:nr + npar]]
        res = f(*rv, *pv)
        for o_ref, r in zip(refs[nr + npar:], res):
            o_ref[...] = r.astype(o_ref.dtype)

    return pl.pallas_call(
        body, name=name, grid=(s // tr,),
        in_specs=[_row_spec(tr, w, cb) for _, w, cb in rows] + [_full_spec(p.shape) for p in params],
        out_specs=[_row_spec(tr, w, 0) for w, _ in outs],
        out_shape=[jax.ShapeDtypeStruct((s, w), dt) for w, dt in outs],
        compiler_params=_params(("arbitrary",)),
    )(*[a for a, _, _ in rows], *params)


def _rw_bwd(f, rows, params, cts, *, row_grads, param_grads=(), add=None, primal=(), name, tr=256):
    s = rows[0][0].shape[0]
    tr = min(tr, s)
    nr, npar, nct = len(rows), len(params), len(cts)
    add = add or {}
    add_keys = list(add)

    def body(*refs):
        it = iter(refs)
        row_refs = [next(it) for _ in range(nr)]
        par_refs = [next(it) for _ in range(npar)]
        ct_refs = [next(it) for _ in range(nct)]
        add_refs = {k: next(it) for k in add_keys}
        rg_refs = [next(it) for _ in row_grads]
        pg_refs = [next(it) for _ in param_grads]
        pr_refs = [next(it) for _ in primal]
        rv = [r[...].astype(F32) for r in row_refs]
        pv = [p[...] for p in par_refs]
        out, vjp = jax.vjp(f, *rv, *pv)
        grads = vjp(tuple(c[...].astype(F32) for c in ct_refs))
        for (idx, _), ref in zip(row_grads, rg_refs):
            g = grads[idx]
            if idx in add_refs:
                g = g + add_refs[idx][...].astype(F32)
            ref[...] = g.astype(ref.dtype)

        @pl.when(pl.program_id(0) == 0)
        def _():
            for ref in pg_refs:
                ref[...] = jnp.zeros_like(ref)

        for idx, ref in zip(param_grads, pg_refs):
            ref[...] += grads[nr + idx]
        for (idx, _, _), ref in zip(primal, pr_refs):
            ref[...] = out[idx].astype(ref.dtype)

    in_arrays = [a for a, _, _ in rows] + list(params) + [a for a, _, _ in cts] + [add[k][0] for k in add_keys]
    in_specs = ([_row_spec(tr, w, cb) for _, w, cb in rows] + [_full_spec(p.shape) for p in params]
                + [_row_spec(tr, w, cb) for _, w, cb in cts] + [_row_spec(tr, add[k][1], add[k][2]) for k in add_keys])
    out_specs = ([_row_spec(tr, rows[idx][1], 0) for idx, _ in row_grads]
                 + [_full_spec(params[idx].shape) for idx in param_grads]
                 + [_row_spec(tr, w, 0) for _, w, _ in primal])
    out_shape = ([jax.ShapeDtypeStruct((s, rows[idx][1]), dt) for idx, dt in row_grads]
                 + [jax.ShapeDtypeStruct(params[idx].shape, F32) for idx in param_grads]
                 + [jax.ShapeDtypeStruct((s, w), dt) for _, w, dt in primal])
    return pl.pallas_call(
        body, name=name, grid=(s // tr,), in_specs=in_specs, out_specs=out_specs, out_shape=out_shape,
        compiler_params=_params(("arbitrary",)),
    )(*in_arrays)


def _mm(a, b, *, ta=False, tb=False, out_dtype=F32, add=None, name):
    m, k = (a.shape[1], a.shape[0]) if ta else a.shape
    n = b.shape[0] if tb else b.shape[1]
    tm = _pick(m, (1024, 512, 256, 128))
    tn = _pick(n, (1024, 512, 256, 128))
    tk = _pick(k, (512, 256, 128))
    nk = k // tk
    dims = (((0 if ta else 1,), (1 if tb else 0,)), ((), ()))

    def body(*refs):
        if add is None:
            a_ref, b_ref, o_ref, acc = refs
        else:
            a_ref, b_ref, add_ref, o_ref, acc = refs
        kk = pl.program_id(2)

        @pl.when(kk == 0)
        def _():
            acc[...] = jnp.zeros_like(acc)

        acc[...] += lax.dot_general(a_ref[...].astype(BF16), b_ref[...].astype(BF16), dims,
                                    preferred_element_type=F32)

        @pl.when(kk == nk - 1)
        def _():
            r = acc[...]
            if add is not None:
                r = r + add_ref[...]
            o_ref[...] = r.astype(o_ref.dtype)

    a_spec = (pl.BlockSpec((tk, tm), lambda i, j, kk: (kk, i)) if ta else pl.BlockSpec((tm, tk), lambda i, j, kk: (i, kk)))
    b_spec = (pl.BlockSpec((tn, tk), lambda i, j, kk: (j, kk)) if tb else pl.BlockSpec((tk, tn), lambda i, j, kk: (kk, j)))
    o_spec = pl.BlockSpec((tm, tn), lambda i, j, kk: (i, j))
    in_specs = [a_spec, b_spec] + ([o_spec] if add is not None else [])
    args = (a, b) + ((add,) if add is not None else ())
    return pl.pallas_call(
        body, name=name, grid=(m // tm, n // tn, nk), in_specs=in_specs, out_specs=o_spec,
        out_shape=jax.ShapeDtypeStruct((m, n), out_dtype),
        scratch_shapes=[pltpu.VMEM((tm, tn), F32)],
        compiler_params=_params(("arbitrary", "arbitrary", "arbitrary")),
    )(*args)


CONV_TT = 256


def _halo(k):
    return -(-(k - 1) // 8) * 8


def _pool_count(i, c, tt):
    win = lax.shift_left(jnp.int32(2), c)
    t = i * tt + lax.broadcasted_iota(jnp.int32, (tt, LANES), 0)
    return win, jnp.minimum(t + 1, win).astype(F32)


def _conv_fwd(x, xcb0, channels, w, k, *, pool=False, name):
    s = x.shape[0]
    tt = min(CONV_TT, s)
    nt = s // tt
    halo = _halo(k)

    def body(*refs):
        if pool:
            xc_ref, xp_ref, o_ref, xs = refs
        else:
            xc_ref, xp_ref, w_ref, o_ref, xs = refs
        c, i = pl.program_id(0), pl.program_id(1)
        xs[0:halo, :] = jnp.where(i > 0, xp_ref[tt - halo:tt, :], 0.0)
        xs[halo:halo + tt, :] = xc_ref[...]
        acc = jnp.zeros((tt, LANES), F32)
        if pool:
            win, cnt = _pool_count(i, c, tt)
            for j in range(k):
                acc = acc + jnp.where(j >= k - win, xs[pl.ds(halo - (k - 1) + j, tt), :], 0.0)
            o_ref[...] = acc / cnt - xc_ref[...]
        else:
            for j in range(k):
                acc = acc + w_ref[j:j + 1, :] * xs[pl.ds(halo - (k - 1) + j, tt), :]
            o_ref[...] = acc

    in_specs = [pl.BlockSpec((tt, LANES), lambda c, i: (i, xcb0 + c)),
                pl.BlockSpec((tt, LANES), lambda c, i: (jnp.maximum(i - 1, 0), xcb0 + c))]
    args = [x, x]
    if not pool:
        in_specs.append(pl.BlockSpec((k, LANES), lambda c, i: (0, c)))
        args.append(w)
    return pl.pallas_call(
        body, name=name, grid=(channels // LANES, nt), in_specs=in_specs,
        out_specs=pl.BlockSpec((tt, LANES), lambda c, i: (i, c)),
        out_shape=jax.ShapeDtypeStruct((s, channels), F32),
        scratch_shapes=[pltpu.VMEM((halo + tt, LANES), F32)],
        compiler_params=_params(("arbitrary", "arbitrary")),
    )(*args)


def _conv_bwd(x, xcb0, channels, w, k, dy, *, pool=False, dx_dtype=BF16, name):
    s = dy.shape[0]
    tt = min(CONV_TT, s)
    nt = s // tt
    halo = _halo(k)

    def body(*refs):
        if pool:
            dyc_ref, dyn_ref, dx_ref, ys = refs
        else:
            xc_ref, xp_ref, dyc_ref, dyn_ref, w_ref, dx_ref, dw_ref, xs, ys = refs
        c, i = pl.program_id(0), pl.program_id(1)
        dyc = dyc_ref[...]
        dyn = jnp.where(i < nt - 1, dyn_ref[0:halo, :], 0.0)
        acc = jnp.zeros((tt, LANES), F32)
        if pool:
            win, cnt = _pool_count(i, c, tt)
            ys[0:tt, :] = dyc / cnt
            ys[tt:tt + halo, :] = dyn / win.astype(F32)
            for j in range(k):
                acc = acc + jnp.where(j >= k - win, ys[pl.ds((k - 1) - j, tt), :], 0.0)
            dx_ref[...] = (acc - dyc).astype(dx_ref.dtype)
            return
        ys[0:tt, :] = dyc
        ys[tt:tt + halo, :] = dyn
        for j in range(k):
            acc = acc + w_ref[j:j + 1, :] * ys[pl.ds((k - 1) - j, tt), :]
        dx_ref[...] = acc.astype(dx_ref.dtype)

        xs[0:halo, :] = jnp.where(i > 0, xp_ref[tt - halo:tt, :], 0.0)
        xs[halo:halo + tt, :] = xc_ref[...]

        @pl.when(i == 0)
        def _():
            dw_ref[...] = jnp.zeros_like(dw_ref)

        for j in range(k):
            dw_ref[j:j + 1, :] += jnp.sum(dyc * xs[pl.ds(halo - (k - 1) + j, tt), :], axis=0, keepdims=True)

    cur = lambda cb0: pl.BlockSpec((tt, LANES), lambda c, i: (i, cb0 + c))
    dy_specs = [cur(0), pl.BlockSpec((tt, LANES), lambda c, i: (jnp.minimum(i + 1, nt - 1), c))]
    dx_spec = pl.BlockSpec((tt, LANES), lambda c, i: (i, c))
    dx_shape = jax.ShapeDtypeStruct((s, channels), dx_dtype)
    if pool:
        return pl.pallas_call(
            body, name=name, grid=(channels // LANES, nt), in_specs=dy_specs, out_specs=dx_spec, out_shape=dx_shape,
            scratch_shapes=[pltpu.VMEM((tt + halo, LANES), F32)],
            compiler_params=_params(("arbitrary", "arbitrary")),
        )(dy, dy)
    in_specs = [cur(xcb0), pl.BlockSpec((tt, LANES), lambda c, i: (jnp.maximum(i - 1, 0), xcb0 + c))] + dy_specs
    in_specs.append(pl.BlockSpec((k, LANES), lambda c, i: (0, c)))
    return pl.pallas_call(
        body, name=name, grid=(channels // LANES, nt), in_specs=in_specs,
        out_specs=[dx_spec, pl.BlockSpec((k, LANES), lambda c, i: (0, c))],
        out_shape=[dx_shape, jax.ShapeDtypeStruct((k, channels), F32)],
        scratch_shapes=[pltpu.VMEM((halo + tt, LANES), F32), pltpu.VMEM((tt + halo, LANES), F32)],
        compiler_params=_params(("arbitrary", "arbitrary")),
    )(x, x, dy, dy, w)


DN_CPS = 4


def _dn_chunk(q, k, v, bb, gb, state):
    c = DN_CHUNK
    ri = lax.broadcasted_iota(jnp.int32, (c, c), 0)
    ci = lax.broadcasted_iota(jnp.int32, (c, c), 1)
    causal, strict = ri >= ci, ri > ci
    eye = jnp.where(ri == ci, 1.0, 0.0)
    gam = _hnn(jnp.where(causal, 1.0, 0.0), gb)
    gam_i = gam[:, :c]
    gam_j = _hnn(jnp.ones((c, c), F32), eye * gam_i)
    decay = jnp.where(causal, jnp.exp(jnp.where(causal, gam_i - gam_j, 0.0)), 0.0)
    kb = k * bb
    b = -jnp.where(strict, _mnt(kb, k) * decay, 0.0)
    inv = eye + b
    bp = b
    for _ in range(5):
        bp = _mnn(bp, bp)
        inv = inv + _mnn(inv, bp)
    eg = jnp.exp(gam)
    u = _mnn(inv, v * bb)
    w = _mnn(inv, kb * eg)
    attn = _mnt(q, k) * decay
    row = lax.broadcasted_iota(jnp.int32, gam.shape, 0)
    g_last = jnp.sum(jnp.where(row == c - 1, gam, 0.0), axis=0, keepdims=True)
    v_new = u - _mnn(w, state)
    o = _mnn(q * eg, state) + _mnn(attn, v_new)
    new_state = state * jnp.exp(g_last) + _mtn(k * jnp.exp(g_last - gam), v_new)
    return o, new_state


def _dn_specs(steps, reverse):
    rows = DN_CPS * DN_CHUNK
    if reverse:
        tile = pl.BlockSpec((rows, HEAD_DIM), lambda h, i: (steps - 1 - i, h))
        st = pl.BlockSpec((DN_CPS, 1, HEAD_DIM, HEAD_DIM), lambda h, i: (steps - 1 - i, h, 0, 0))
    else:
        tile = pl.BlockSpec((rows, HEAD_DIM), lambda h, i: (i, h))
        st = pl.BlockSpec((DN_CPS, 1, HEAD_DIM, HEAD_DIM), lambda h, i: (i, h, 0, 0))
    return tile, st


def _dn_fwd(q, k, v, bb, gb, *, name):
    s = q.shape[0]
    steps = s // (DN_CPS * DN_CHUNK)

    def body(q_ref, k_ref, v_ref, b_ref, g_ref, o_ref, st_ref, state):
        @pl.when(pl.program_id(1) == 0)
        def _():
            state[...] = jnp.zeros_like(state)

        st = state[...]
        for n in range(DN_CPS):
            r = slice(n * DN_CHUNK, (n + 1) * DN_CHUNK)
            st_ref[n, 0] = st
            o, st = _dn_chunk(q_ref[r, :], k_ref[r, :], v_ref[r, :], b_ref[r, :], g_ref[r, :], st)
            o_ref[r, :] = o
        state[...] = st

    tile, stspec = _dn_specs(steps, False)
    return pl.pallas_call(
        body, name=name, grid=(HEADS, steps), in_specs=[tile] * 5, out_specs=[tile, stspec],
        out_shape=[jax.ShapeDtypeStruct((s, MIX_W), F32),
                   jax.ShapeDtypeStruct((s // DN_CHUNK, HEADS, HEAD_DIM, HEAD_DIM), F32)],
        scratch_shapes=[pltpu.VMEM((HEAD_DIM, HEAD_DIM), F32)],
        compiler_params=_params(("arbitrary", "arbitrary")),
    )(q, k, v, bb, gb)


def _dn_bwd(q, k, v, bb, gb, states, do, *, name):
    s = q.shape[0]
    steps = s // (DN_CPS * DN_CHUNK)

    def body(q_ref, k_ref, v_ref, b_ref, g_ref, st_ref, do_ref, dq_ref, dk_ref, dv_ref, db_ref, dg_ref, dstate):
        @pl.when(pl.program_id(1) == 0)
        def _():
            dstate[...] = jnp.zeros_like(dstate)

        dst = dstate[...]
        for n in reversed(range(DN_CPS)):
            r = slice(n * DN_CHUNK, (n + 1) * DN_CHUNK)
            _, vjp = jax.vjp(_dn_chunk, q_ref[r, :], k_ref[r, :], v_ref[r, :], b_ref[r, :], g_ref[r, :], st_ref[n, 0])
            dq, dk, dv, db, dg, dst = vjp((do_ref[r, :], dst))
            dq_ref[r, :] = dq
            dk_ref[r, :] = dk
            dv_ref[r, :] = dv
            db_ref[r, :] = db
            dg_ref[r, :] = dg
        dstate[...] = dst

    tile, stspec = _dn_specs(steps, True)
    return pl.pallas_call(
        body, name=name, grid=(HEADS, steps), in_specs=[tile] * 5 + [stspec, tile], out_specs=[tile] * 5,
        out_shape=[jax.ShapeDtypeStruct((s, MIX_W), F32)] * 5,
        scratch_shapes=[pltpu.VMEM((HEAD_DIM, HEAD_DIM), F32)],
        compiler_params=_params(("arbitrary", "arbitrary")),
    )(q, k, v, bb, gb, states, do)


EW_TILE_BYTES = 2 << 20


def _ew(f, ins, out_dtypes, *, name):
    r, c = ins[0][0].shape[1:] if isinstance(ins[0], tuple) else ins[0].shape
    row_bytes = -(-c // LANES) * LANES * 4
    tr = r
    if r * row_bytes > EW_TILE_BYTES:
        tr = next((t for t in (4096, 2048, 1024, 512, 256, 128, 64, 32, 16, 8)
                   if r % t == 0 and t * row_bytes <= EW_TILE_BYTES), r)
    n_in = len(ins)

    def body(*refs):
        res = f(*[x[...] for x in refs[:n_in]])
        for o_ref, v in zip(refs[n_in:], res):
            o_ref[...] = v.astype(o_ref.dtype)

    in_specs, args = [], []
    for x in ins:
        if isinstance(x, tuple):
            in_specs.append(pl.BlockSpec((None, tr, c), functools.partial(lambda i, j: (j, i, 0), j=x[1])))
            args.append(x[0])
        else:
            in_specs.append(pl.BlockSpec((tr, c), lambda i: (i, 0)))
            args.append(x)
    return pl.pallas_call(
        body, name=name, grid=(r // tr,), in_specs=in_specs,
        out_specs=[pl.BlockSpec((tr, c), lambda i: (i, 0)) for _ in out_dtypes],
        out_shape=[jax.ShapeDtypeStruct((r, c), dt) for dt in out_dtypes],
        compiler_params=_params(("arbitrary",)),
    )(*args)


def f_adamw(w, g, m, v):
    m = ADAM_B1 * m + (1.0 - ADAM_B1) * g
    v = ADAM_B2 * v + (1.0 - ADAM_B2) * (g * g)
    m_hat = m / (1.0 - ADAM_B1 ** ADAM_STEP)
    v_hat = v / (1.0 - ADAM_B2 ** ADAM_STEP)
    return -ADAM_LR * (m_hat / (jnp.sqrt(v_hat) + ADAM_EPS) + ADAM_WD * w), m, v


def f_sum4(a, b, c, d):
    return (((a.astype(F32) + b.astype(F32)) + c.astype(F32)) + d.astype(F32),)


def f_add2(a, b):
    return (a + b,)


def _adamw(w, g, m, v, *, name):
    shape = w.shape
    two = (1, shape[0]) if w.ndim == 1 else (-1, shape[-1])
    outs = _ew(f_adamw, [t.reshape(two) for t in (w, g, m, v)], [F32, F32, F32], name=name)
    return [o.reshape(shape) for o in outs]


_ANY = pl.BlockSpec(memory_space=pl.ANY)


def _xy_exchange(srcs, *, broadcast, name):
    nt = len(srcs)

    def body(*refs):
        src_refs, out_refs = refs[:nt], refs[nt:2 * nt]
        send_sems, recv_sems, local_sems = refs[2 * nt:]
        x, y, c = lax.axis_index("x"), lax.axis_index("y"), lax.axis_index("c")
        me = 2 * x + y
        peers = [(1 - x, y), (x, 1 - y), (1 - x, 1 - y)]

        def block(t, j):
            return src_refs[t] if broadcast else src_refs[t].at[j]

        def copy(t, n, px, py, src_blk, dst_blk):
            return pltpu.make_async_remote_copy(
                src_ref=src_blk, dst_ref=dst_blk, send_sem=send_sems.at[3 * t + n], recv_sem=recv_sems.at[3 * t + n],
                device_id=(px, py, c), device_id_type=pl.DeviceIdType.MESH)

        mine = [pltpu.make_async_copy(block(t, me), out_refs[t].at[me], local_sems.at[t]) for t in range(nt)]
        sends = [copy(t, n, px, py, block(t, 2 * px + py), out_refs[t].at[me])
                 for t in range(nt) for n, (px, py) in enumerate(peers)]
        for cp in mine + sends:
            cp.start()
        for cp in sends:
            cp.wait_send()
        for t in range(nt):
            for n, (px, py) in enumerate(peers):
                copy(t, n, px, py, block(t, me), out_refs[t].at[2 * px + py]).wait_recv()
        for cp in mine:
            cp.wait()

    shapes = [(4,) + tuple(s.shape) if broadcast else tuple(s.shape) for s in srcs]
    return pl.pallas_call(
        body, name=name, in_specs=[_ANY] * nt, out_specs=[_ANY] * nt,
        out_shape=[jax.ShapeDtypeStruct(shp, s.dtype) for shp, s in zip(shapes, srcs)],
        scratch_shapes=[pltpu.SemaphoreType.DMA((3 * nt,)), pltpu.SemaphoreType.DMA((3 * nt,)),
                        pltpu.SemaphoreType.DMA((nt,))],
        compiler_params=pltpu.CompilerParams(has_side_effects=True),
    )(*srcs)


def _c_swap(srcs, *, name):
    nt = len(srcs)

    def body(*refs):
        src_refs, out_refs = refs[:nt], refs[nt:2 * nt]
        send_sems, recv_sems = refs[2 * nt:]
        sibling = (lax.axis_index("x"), lax.axis_index("y"), 1 - lax.axis_index("c"))
        cps = [pltpu.make_async_remote_copy(src_ref=src_refs[t], dst_ref=out_refs[t], send_sem=send_sems.at[t],
                                            recv_sem=recv_sems.at[t], device_id=sibling,
                                            device_id_type=pl.DeviceIdType.MESH) for t in range(nt)]
        for cp in cps:
            cp.start()
        for cp in cps:
            cp.wait()

    return pl.pallas_call(
        body, name=name, in_specs=[_ANY] * nt, out_specs=[_ANY] * nt,
        out_shape=[jax.ShapeDtypeStruct(s.shape, s.dtype) for s in srcs],
        scratch_shapes=[pltpu.SemaphoreType.DMA((nt,)), pltpu.SemaphoreType.DMA((nt,))],
        compiler_params=pltpu.CompilerParams(has_side_effects=True),
    )(*srcs)


def _all_sum(srcs, *, broadcast, name):
    got = _xy_exchange(srcs, broadcast=broadcast, name=name + "_xy")
    parts = []
    for t, gt in enumerate(got):
        g3 = gt.reshape(4, -1, gt.shape[-1])
        parts.append(_ew(f_sum4, [(g3, j) for j in range(4)], [F32], name=f"{name}_sum4_{t}")[0])
    others = _c_swap(parts, name=name + "_c")
    return [_ew(f_add2, [p, o], [F32], name=f"{name}_add2_{t}")[0].reshape(gt.shape[1:])
            for t, (p, o, gt) in enumerate(zip(parts, others, got))]


def _flat_rows(parts, dtype, row_multiple=8):
    flat = jnp.concatenate([p.reshape(-1).astype(dtype) for p in parts])
    pad = (-flat.shape[0]) % (row_multiple * LANES)
    if pad:
        flat = jnp.concatenate([flat, jnp.zeros((pad,), dtype)])
    return flat.reshape(-1, LANES)


def _unflat(buf, shapes):
    flat = buf.reshape(-1)
    out, off = [], 0
    for shp in shapes:
        n = 1
        for d in shp:
            n *= d
        out.append(flat[off:off + n].reshape(shp))
        off += n
    return out


def _regroup_w_in(w):
    cols = [w[:, a:b] for a, b in _Z_SEGS] + [jnp.zeros((w.shape[0], Z_W - IN_W), w.dtype)]
    return jnp.concatenate(cols, axis=1)


def _ungroup_w_in(wz):
    starts, off = {}, 0
    for a, b in _Z_SEGS:
        starts[a] = (off, off + b - a)
        off += b - a
    return jnp.concatenate([wz[:, starts[a][0]:starts[a][1]] for a in sorted(starts)], axis=1)


def _row(vec):
    return vec.reshape(1, -1)


def _lane_pad(vec):
    return jnp.zeros((1, LANES), F32).at[0, :vec.shape[0]].set(vec)


def _layer_fwd(x, mem, w, l):
    nm = lambda s: f"{s}_l{l}"
    sv = {"x0": x}
    h = _rw(f_norm, [(x, D_MODEL, 0)], [_row(w["norm_mix"])], [(D_MODEL, BF16)], name=nm("norm_mix"))[0]
    z = _mm(h, w["w_in_z"], name=nm("in_proj"))
    a_glu = _rw(f_glu, [(z, 2 * MIX_W, Z_A // (2 * MIX_W))], [], [(MIX_W, F32)], name=nm("glu"))[0]
    ac = _conv_fwd(a_glu, 0, MIX_W, w["conv_a_w"], CONV_K, name=nm("conv_a"))
    a_par = [_row(w["conv_a_b"]), _row(w["ln_a_g"]), _row(w["ln_a_b"])]
    a_out = _rw(f_lnsilu, [(ac, MIX_W, 0)], a_par, [(MIX_W, BF16)], name=nm("ln_a"))[0]
    qc = _conv_fwd(z, Z_QKV // LANES, 3 * MIX_W, w["dn_conv_w"], DN_CONV_K, name=nm("conv_dn"))
    dn_par = [_lane_pad(w["dn_a_log"]), _lane_pad(w["dn_dt_bias"])]
    qn, kn, v, bb, gb = _rw(f_dnprep, [(qc, 3 * MIX_W, 0), (z, LANES, Z_BD // LANES)], dn_par,
                            [(MIX_W, F32)] * 5, name=nm("dn_prep"))
    o, states = _dn_fwd(qn, kn, v, bb, gb, name=nm("dn_scan"))
    o_out = _rw(f_dnout, [(o, MIX_W, 0), (z, MIX_W, Z_DG // MIX_W)], [_row(w["dn_norm_g"])], [(MIX_W, BF16)],
                name=nm("dn_out"))[0]
    gm_par = [_row(w["gm_ln_g"]), _row(w["gm_ln_b"]), w["gm_ws"].reshape(4 * GM_CHUNK, GM_CHUNK), w["gm_bs"].T]
    c_out = _rw(f_gmlp, [(z, 2 * MIX_W, Z_GM // (2 * MIX_W))], gm_par, [(MIX_W, BF16)], name=nm("gmlp"))[0]
    pc = _conv_fwd(z, Z_POOL // LANES, MIX_W, None, POOL_K, pool=True, name=nm("pool"))
    p_par = [w["pool_w"].reshape(4 * LANES, LANES), _row(w["pool_scale"])]
    p_out = _rw(f_poolpost, [(pc, MIX_W, 0)], p_par, [(MIX_W, BF16)], name=nm("pool_post"))[0]
    branches = [a_out, o_out, c_out, p_out]
    proj = [_mm(br, w["w_branch"][n], name=nm(f"branch{n}")) for n, br in enumerate(branches)]
    merged = _rw(f_merge, [(z, N_BRANCH * D_MODEL, 0)] + [(p, D_MODEL, 0) for p in proj], [], [(D_MODEL, BF16)],
                 name=nm("merge"), tr=128)[0]
    x1 = _mm(merged, w["w_out"], add=x, name=nm("out_proj"))
    sv.update(h=h, z=z, a_glu=a_glu, ac=ac, qc=qc, qn=qn, kn=kn, v=v, bb=bb, gb=gb, o=o, states=states, pc=pc,
              branches=branches, proj=proj, merged=merged, x1=x1)
    h2 = _rw(f_norm, [(x1, D_MODEL, 0)], [_row(w["norm_xa"])], [(D_MODEL, BF16)], name=nm("norm_xa"))[0]
    mn = _rw(f_norm, [(mem, D_MODEL, 0)], [_row(w["norm_mem"])], [(D_MODEL, BF16)], name=nm("norm_mem"))[0]
    kv = _mm(mn, w["xa_wkv"], name=nm("xa_kv"))
    q = _mm(h2, w["xa_wq"], name=nm("xa_q"))
    att = _rw(f_attn, [(q, D_MODEL, 0)], [kv], [(D_MODEL, BF16)], name=nm("xa_attn"))[0]
    x2 = _mm(att, w["xa_wo"], add=x1, name=nm("xa_o"))
    sv.update(h2=h2, mn=mn, kv=kv, q=q, att=att, x2=x2)
    h3 = _rw(f_norm, [(x2, D_MODEL, 0)], [_row(w["norm_mlp"])], [(D_MODEL, BF16)], name=nm("norm_mlp"))[0]
    u = _mm(h3, w["mlp_w1"], name=nm("mlp_up"))
    act = _rw(f_relu2, [(u, FFN_W, 0)], [], [(FFN_W, BF16)], name=nm("relu2"))[0]
    x3 = _mm(act, w["mlp_w2"], add=x2, name=nm("mlp_down"))
    sv.update(h3=h3, u=u, act=act)
    return x3, sv


def _layer_bwd(dx, mem, w, sv, l):
    nm = lambda s: f"{s}_bwd_l{l}"
    s = dx.shape[0]
    g = {}
    dact = _mm(dx, w["mlp_w2"], tb=True, name=nm("mlp_down_dx"))
    g["mlp_w2"] = _mm(sv["act"], dx, ta=True, out_dtype=BF16, name=nm("mlp_down_dw"))
    du = _rw_bwd(f_relu2, [(sv["u"], FFN_W, 0)], [], [(dact, FFN_W, 0)], row_grads=[(0, BF16)], name=nm("relu2"))[0]
    g["mlp_w1"] = _mm(sv["h3"], du, ta=True, out_dtype=BF16, name=nm("mlp_up_dw"))
    dh3 = _mm(du, w["mlp_w1"], tb=True, name=nm("mlp_up_dx"))
    dx2, g["norm_mlp"] = _rw_bwd(f_norm, [(sv["x2"], D_MODEL, 0)], [_row(w["norm_mlp"])], [(dh3, D_MODEL, 0)],
                                 row_grads=[(0, F32)], param_grads=[0], add={0: (dx, D_MODEL, 0)}, name=nm("norm_mlp"))
    datt = _mm(dx2, w["xa_wo"], tb=True, name=nm("xa_o_dx"))
    g["xa_wo"] = _mm(sv["att"], dx2, ta=True, out_dtype=BF16, name=nm("xa_o_dw"))
    dq, dkv = _rw_bwd(f_attn, [(sv["q"], D_MODEL, 0)], [sv["kv"]], [(datt, D_MODEL, 0)], row_grads=[(0, BF16)],
                      param_grads=[0], name=nm("xa_attn"))
    g["xa_wq"] = _mm(sv["h2"], dq, ta=True, out_dtype=BF16, name=nm("xa_q_dw"))
    dh2 = _mm(dq, w["xa_wq"], tb=True, name=nm("xa_q_dx"))
    g["xa_wkv"] = _mm(sv["mn"], dkv, ta=True, out_dtype=BF16, name=nm("xa_kv_dw"))
    dmn = _mm(dkv, w["xa_wkv"], tb=True, name=nm("xa_kv_dx"))
    g["norm_mem"] = _rw_bwd(f_norm, [(mem, D_MODEL, 0)], [_row(w["norm_mem"])], [(dmn, D_MODEL, 0)], row_grads=[],
                            param_grads=[0], name=nm("norm_mem"))[0]
    dx1, g["norm_xa"] = _rw_bwd(f_norm, [(sv["x1"], D_MODEL, 0)], [_row(w["norm_xa"])], [(dh2, D_MODEL, 0)],
                                row_grads=[(0, F32)], param_grads=[0], add={0: (dx2, D_MODEL, 0)}, name=nm("norm_xa"))
    z = sv["z"]
    dmerged = _mm(dx1, w["w_out"], tb=True, name=nm("out_proj_dx"))
    g["w_out"] = _mm(sv["merged"], dx1, ta=True, out_dtype=BF16, name=nm("out_proj_dw"))
    mg = _rw_bwd(f_merge, [(z, N_BRANCH * D_MODEL, 0)] + [(p, D_MODEL, 0) for p in sv["proj"]], [],
                 [(dmerged, D_MODEL, 0)], row_grads=[(i, BF16) for i in range(5)], name=nm("merge"), tr=128)
    dgate, dproj = mg[0], mg[1:]
    g["w_branch"] = jnp.stack([_mm(br, dp, ta=True, out_dtype=BF16, name=nm(f"branch{n}_dw"))
                               for n, (br, dp) in enumerate(zip(sv["branches"], dproj))])
    dbr = [_mm(dp, w["w_branch"][n], tb=True, name=nm(f"branch{n}_dx")) for n, dp in enumerate(dproj)]
    p_par = [w["pool_w"].reshape(4 * LANES, LANES), _row(w["pool_scale"])]
    dpc, dpw, g["pool_scale"] = _rw_bwd(f_poolpost, [(sv["pc"], MIX_W, 0)], p_par, [(dbr[3], MIX_W, 0)],
                                        row_grads=[(0, F32)], param_grads=[0, 1], name=nm("pool_post"))
    g["pool_w"] = dpw.reshape(4, LANES, LANES)
    dpool = _conv_bwd(None, 0, MIX_W, None, POOL_K, dpc, pool=True, name=nm("pool"))
    gm_par = [_row(w["gm_ln_g"]), _row(w["gm_ln_b"]), w["gm_ws"].reshape(4 * GM_CHUNK, GM_CHUNK), w["gm_bs"].T]
    dgm, g["gm_ln_g"], g["gm_ln_b"], dws, dbst = _rw_bwd(
        f_gmlp, [(z, 2 * MIX_W, Z_GM // (2 * MIX_W))], gm_par, [(dbr[2], MIX_W, 0)], row_grads=[(0, BF16)],
        param_grads=[0, 1, 2, 3], name=nm("gmlp"))
    g["gm_ws"] = dws.reshape(4, GM_CHUNK, GM_CHUNK)
    g["gm_bs"] = dbst.T
    do, ddg, g["dn_norm_g"] = _rw_bwd(f_dnout, [(sv["o"], MIX_W, 0), (z, MIX_W, Z_DG // MIX_W)], [_row(w["dn_norm_g"])],
                                      [(dbr[1], MIX_W, 0)], row_grads=[(0, F32), (1, BF16)], param_grads=[0],
                                      name=nm("dn_out"))
    dqn, dkn, dv, dbb, dgb = _dn_bwd(sv["qn"], sv["kn"], sv["v"], sv["bb"], sv["gb"], sv["states"], do, name=nm("dn_scan"))
    dn_par = [_lane_pad(w["dn_a_log"]), _lane_pad(w["dn_dt_bias"])]
    dqc, dbd, dal, ddt = _rw_bwd(
        f_dnprep, [(sv["qc"], 3 * MIX_W, 0), (z, LANES, Z_BD // LANES)], dn_par,
        [(t, MIX_W, 0) for t in (dqn, dkn, dv, dbb, dgb)], row_grads=[(0, F32), (1, BF16)], param_grads=[0, 1],
        name=nm("dn_prep"))
    g["dn_a_log"], g["dn_dt_bias"] = dal[0, :HEADS], ddt[0, :HEADS]
    dqkv, g["dn_conv_w"] = _conv_bwd(z, Z_QKV // LANES, 3 * MIX_W, w["dn_conv_w"], DN_CONV_K, dqc, name=nm("conv_dn"))
    a_par = [_row(w["conv_a_b"]), _row(w["ln_a_g"]), _row(w["ln_a_b"])]
    dac, g["conv_a_b"], g["ln_a_g"], g["ln_a_b"] = _rw_bwd(
        f_lnsilu, [(sv["ac"], MIX_W, 0)], a_par, [(dbr[0], MIX_W, 0)], row_grads=[(0, F32)], param_grads=[0, 1, 2],
        name=nm("ln_a"))
    dglu, g["conv_a_w"] = _conv_bwd(sv["a_glu"], 0, MIX_W, w["conv_a_w"], CONV_K, dac, dx_dtype=F32, name=nm("conv_a"))
    da_in = _rw_bwd(f_glu, [(z, 2 * MIX_W, Z_A // (2 * MIX_W))], [], [(dglu, MIX_W, 0)], row_grads=[(0, BF16)],
                    name=nm("glu"))[0]
    dz = jnp.concatenate([dgate, da_in, dgm, dqkv, ddg, dpool, dbd, jnp.zeros((s, Z_W - Z_BD - LANES), BF16)], axis=1)
    g["w_in"] = _ungroup_w_in(_mm(sv["h"], dz, ta=True, out_dtype=BF16, name=nm("in_proj_dw")))
    dh = _mm(dz, w["w_in_z"], tb=True, name=nm("in_proj_dx"))
    dx0, g["norm_mix"] = _rw_bwd(f_norm, [(sv["x0"], D_MODEL, 0)], [_row(w["norm_mix"])], [(dh, D_MODEL, 0)],
                                 row_grads=[(0, F32)], param_grads=[0], add={0: (dx1, D_MODEL, 0)}, name=nm("norm_mix"))
    for n in ("norm_mlp", "norm_xa", "norm_mem", "norm_mix", "pool_scale", "gm_ln_g", "gm_ln_b", "dn_norm_g",
              "conv_a_b", "ln_a_g", "ln_a_b"):
        g[n] = g[n].reshape(-1)
    return dx0, g


def _shard_slice(a, axis, j):
    n = a.shape[axis] // 4
    return lax.slice_in_dim(a, j * n, (j + 1) * n, axis=axis)


def kernel(x, mem, norm_mix, w_in, conv_a_w, conv_a_b, ln_a_g, ln_a_b, dn_conv_w, dn_a_log, dn_dt_bias, dn_norm_g, gm_ln_g, gm_ln_b, gm_ws, gm_bs, pool_w, pool_scale, w_branch, w_out, norm_xa, norm_mem, xa_wq, xa_wkv, xa_wo, norm_mlp, mlp_w1, mlp_w2, norm_f, loss_target, m_norm_mix, m_w_in, m_conv_a_w, m_conv_a_b, m_ln_a_g, m_ln_a_b, m_dn_conv_w, m_dn_a_log, m_dn_dt_bias, m_dn_norm_g, m_gm_ln_g, m_gm_ln_b, m_gm_ws, m_gm_bs, m_pool_w, m_pool_scale, m_w_branch, m_w_out, m_norm_xa, m_norm_mem, m_xa_wq, m_xa_wkv, m_xa_wo, m_norm_mlp, m_mlp_w1, m_mlp_w2, m_norm_f, v_norm_mix, v_w_in, v_conv_a_w, v_conv_a_b, v_ln_a_g, v_ln_a_b, v_dn_conv_w, v_dn_a_log, v_dn_dt_bias, v_dn_norm_g, v_gm_ln_g, v_gm_ln_b, v_gm_ws, v_gm_bs, v_pool_w, v_pool_scale, v_w_branch, v_w_out, v_norm_xa, v_norm_mem, v_xa_wq, v_xa_wkv, v_xa_wo, v_norm_mlp, v_mlp_w1, v_mlp_w2, v_norm_f):
    given = dict(locals())
    wts = {n: given[n] for n in WEIGHTS}
    depth = norm_mix.shape[0]
    x = x[0]
    mem = mem[0]
    tgt = loss_target[0]

    gathered = dict(zip(BIG, _xy_exchange([wts[n].astype(BF16) for n in BIG], broadcast=True, name="gather_weights")))
    conv_shapes = [wts[n].shape for n in CONVS]
    conv_gathered = _xy_exchange([_flat_rows([wts[n] for n in CONVS], F32)], broadcast=True, name="gather_filters")[0]
    conv_parts = [_unflat(conv_gathered[j], conv_shapes) for j in range(4)]
    conv_full = {n: jnp.concatenate([conv_parts[j][i] for j in range(4)], axis=2) for i, n in enumerate(CONVS)}

    def layer_weights(l):
        w = {n: jnp.concatenate([gathered[n][j, l] for j in range(4)], axis=BIG_AXIS[n]) for n in BIG}
        w["w_in_z"] = _regroup_w_in(w.pop("w_in"))
        for n in SMALL:
            if n != "norm_f":
                w[n] = conv_full[n][l] if n in CONVS else wts[n][l]
        return w

    saved, layer_w = [], []
    for l in range(depth):
        w = layer_weights(l)
        x, sv = _layer_fwd(x, mem, w, l)
        saved.append(sv)
        layer_w.append(w)
    dx, g_norm_f, loss_rows = _rw_bwd(
        f_loss, [(x, D_MODEL, 0), (tgt, D_MODEL, 0)], [_row(norm_f)], [(jnp.ones((x.shape[0], 1), F32), 1, 0)],
        row_grads=[(0, F32)], param_grads=[0], primal=[(0, 1, F32)], name="loss_head")
    loss = lax.psum(jnp.sum(loss_rows), ("x", "y", "c"))

    grads = [None] * depth
    for l in reversed(range(depth)):
        dx, grads[l] = _layer_bwd(dx, mem, layer_w[l], saved[l], l)
        saved[l] = None
    grad_x = dx[None]

    blocks = [jnp.stack([jnp.stack([_shard_slice(grads[l][n], BIG_AXIS[n], j) for l in range(depth)])
                         for j in range(4)]).astype(BF16) for n in BIG]
    gw = dict(zip(BIG, _all_sum(blocks, broadcast=False, name="reduce_big")))

    small_names = [n for n in SMALL if n != "norm_f"]
    small_shapes = [(depth,) + (conv_full[n].shape[1:] if n in CONVS else wts[n].shape[1:]) for n in small_names]
    small_buf = _flat_rows([jnp.stack([grads[l][n] for l in range(depth)]) for n in small_names] + [g_norm_f], F32,
                           row_multiple=1024)
    small_sum = _all_sum([small_buf], broadcast=True, name="reduce_small")[0]
    small_parts = _unflat(small_sum, small_shapes + [norm_f.shape])
    me = 2 * lax.axis_index("x") + lax.axis_index("y")
    for n, t in zip(small_names + ["norm_f"], small_parts):
        if n in CONVS:
            width = wts[n].shape[2]
            t = lax.dynamic_slice_in_dim(t, me * width, width, axis=2)
        gw[n] = t

    deltas, new_m, new_v = {}, {}, {}
    for n in WEIGHTS:
        deltas[n], new_m[n], new_v[n] = _adamw(wts[n], gw[n], given["m_" + n], given["v_" + n], name=f"adamw_{n}")
    return (loss, grad_x, *[gw[n] for n in WEIGHTS], *[deltas[n] for n in WEIGHTS], *[new_m[n] for n in WEIGHTS],
            *[new_v[n] for n in WEIGHTS])
```

```python
import functools

import jax
import jax.numpy as jnp
from jax import lax
from jax.experimental import pallas as pl
from jax.experimental.pallas import tpu as pltpu

F32 = jnp.float32
BF16 = jnp.bfloat16
HI = lax.Precision.HIGHEST

D_MODEL = 1024
MIX_W = 512
N_BRANCH = 4
HEADS = 4
HEAD_DIM = 128
CONV_K = 31
DN_CONV_K = 4
DN_CHUNK = 64
GM_CHUNK = 128
POOL_K = 16
XA_HEADS = 4
XA_HEAD_DIM = 256
FFN_W = 4096
IN_W = 8712
Z_W = 9216
LANES = 128
VMEM_LIMIT = 56 * 1024 * 1024

ADAM_LR, ADAM_B1, ADAM_B2, ADAM_EPS, ADAM_WD, ADAM_STEP = 0.001, 0.9, 0.999, 1e-08, 0.01, 10

Z_GATE, Z_A, Z_GM, Z_QKV, Z_DG, Z_POOL, Z_BD = 0, 4096, 5120, 6144, 7680, 8192, 8704
_Z_SEGS = ((4616, 8712), (0, 1024), (3080, 4104), (1024, 2560), (2560, 3072), (4104, 4616), (3072, 3080))

BIG = ("w_in", "w_branch", "w_out", "xa_wq", "xa_wkv", "xa_wo", "mlp_w1", "mlp_w2")
BIG_AXIS = {"w_in": 1, "w_branch": 2, "w_out": 0, "xa_wq": 0, "xa_wkv": 1, "xa_wo": 0, "mlp_w1": 1, "mlp_w2": 0}
CONVS = ("conv_a_w", "dn_conv_w")
WEIGHTS = ("norm_mix", "w_in", "conv_a_w", "conv_a_b", "ln_a_g", "ln_a_b", "dn_conv_w", "dn_a_log", "dn_dt_bias",
           "dn_norm_g", "gm_ln_g", "gm_ln_b", "gm_ws", "gm_bs", "pool_w", "pool_scale", "w_branch", "w_out",
           "norm_xa", "norm_mem", "xa_wq", "xa_wkv", "xa_wo", "norm_mlp", "mlp_w1", "mlp_w2", "norm_f")
SMALL = tuple(n for n in WEIGHTS if n not in BIG)


def _params(sem=None):
    return pltpu.CompilerParams(vmem_limit_bytes=VMEM_LIMIT, dimension_semantics=sem)


def _pick(n, cands):
    for c in cands:
        if n % c == 0:
            return c
    return n


def _make_dots(prec):
    def raw(a, b, ca, cb):
        if prec is None:
            a, b = a.astype(BF16), b.astype(BF16)
        return lax.dot_general(a, b, (((ca,), (cb,)), ((), ())), precision=prec, preferred_element_type=F32)

    nn = jax.custom_vjp(lambda a, b: raw(a, b, 1, 0))
    nt = jax.custom_vjp(lambda a, b: raw(a, b, 1, 1))
    tn = jax.custom_vjp(lambda a, b: raw(a, b, 0, 0))
    nn.defvjp(lambda a, b: (raw(a, b, 1, 0), (a, b)), lambda r, g: (raw(g, r[1], 1, 1), raw(r[0], g, 0, 0)))
    nt.defvjp(lambda a, b: (raw(a, b, 1, 1), (a, b)), lambda r, g: (raw(g, r[1], 1, 0), raw(g, r[0], 0, 0)))
    tn.defvjp(lambda a, b: (raw(a, b, 0, 0), (a, b)), lambda r, g: (raw(r[1], g, 1, 1), raw(r[0], g, 1, 0)))
    return nn, nt, tn


_bnn, _bnt, _btn = _make_dots(None)
_hnn, _hnt, _htn = _make_dots(HI)
_mnn, _mnt, _mtn = _make_dots(lax.Precision.HIGH)


def _col(x, j):
    lane = lax.broadcasted_iota(jnp.int32, x.shape, 1)
    return jnp.sum(jnp.where(lane == j, x, 0.0), axis=-1, keepdims=True)


def _rms(x, g, eps=1e-6):
    return x * lax.rsqrt(jnp.mean(x * x, axis=-1, keepdims=True) + eps) * g


def _ln(x, g, b, eps=1e-5):
    xc = x - jnp.mean(x, axis=-1, keepdims=True)
    var = jnp.mean(xc * xc, axis=-1, keepdims=True)
    return xc * lax.rsqrt(var + eps) * g + b


_sigmoid = jax.nn.sigmoid


def _silu(x):
    return x * _sigmoid(x)


def _softplus(x):
    return jnp.maximum(x, 0.0) + jnp.log1p(jnp.exp(-jnp.abs(x)))


def _gelu(x):
    return 0.5 * x * (1.0 + lax.erf(x * 0.7071067811865476))


def f_norm(x, g):
    return (_rms(x, g),)


def f_glu(a_in):
    return (a_in[:, :MIX_W] * _sigmoid(a_in[:, MIX_W:]),)


def f_lnsilu(ac, cb, g, b):
    return (_silu(_ln(ac + cb, g, b)),)


def f_dnprep(qc, bd, a_log, dt_bias):
    t = qc.shape[0]
    qkv = _silu(qc)
    qs, ks, bs, gs = [], [], [], []
    for h in range(HEADS):
        q = qkv[:, h * HEAD_DIM:(h + 1) * HEAD_DIM]
        k = qkv[:, MIX_W + h * HEAD_DIM:MIX_W + (h + 1) * HEAD_DIM]
        qs.append(q * lax.rsqrt(jnp.sum(q * q, axis=-1, keepdims=True) + 1e-6) * (HEAD_DIM ** -0.5))
        ks.append(k * lax.rsqrt(jnp.sum(k * k, axis=-1, keepdims=True) + 1e-6))
        beta = _sigmoid(_col(bd, h))
        g = -jnp.exp(_col(a_log, h)) * _softplus(_col(bd, HEADS + h) + _col(dt_bias, h))
        bs.append(jnp.broadcast_to(beta, (t, HEAD_DIM)))
        gs.append(jnp.broadcast_to(g, (t, HEAD_DIM)))
    cat = lambda xs: jnp.concatenate(xs, axis=1)
    return cat(qs), cat(ks), qkv[:, 2 * MIX_W:], cat(bs), cat(gs)


def f_dnout(o, dgate, g):
    outs = []
    for h in range(HEADS):
        sl = slice(h * HEAD_DIM, (h + 1) * HEAD_DIM)
        outs.append(_rms(o[:, sl], g) * _silu(dgate[:, sl]))
    return (jnp.concatenate(outs, axis=1),)


def f_gmlp(gm_in, lg, lb, ws, bst):
    t = gm_in.shape[0]
    ge = _gelu(gm_in)
    u, vg = ge[:, :MIX_W], _ln(ge[:, MIX_W:], lg, lb)
    ri = lax.broadcasted_iota(jnp.int32, (GM_CHUNK, GM_CHUNK), 0)
    ci = lax.broadcasted_iota(jnp.int32, (GM_CHUNK, GM_CHUNK), 1)
    chunks = []
    for n in range(t // GM_CHUNK):
        vc = vg[n * GM_CHUNK:(n + 1) * GM_CHUNK]
        cols = []
        for g in range(4):
            w = jnp.where(ri >= ci, ws[g * GM_CHUNK:(g + 1) * GM_CHUNK], 0.0)
            cols.append(_bnn(w, vc[:, g * LANES:(g + 1) * LANES]) + _col(bst, g))
        chunks.append(jnp.concatenate(cols, axis=1))
    mixed = chunks[0] if len(chunks) == 1 else jnp.concatenate(chunks, axis=0)
    return (u * mixed,)


def f_poolpost(pc, pw, ps):
    cols = [_bnn(pc[:, g * LANES:(g + 1) * LANES], pw[g * LANES:(g + 1) * LANES]) for g in range(4)]
    return (jnp.concatenate(cols, axis=1) * ps,)


def f_merge(gate, p0, p1, p2, p3):
    m = None
    for n, p in enumerate((p0, p1, p2, p3)):
        t = _sigmoid(gate[:, n * D_MODEL:(n + 1) * D_MODEL]) * p
        m = t if m is None else m + t
    return (m,)


def f_attn(q, kv):
    outs = []
    for h in range(XA_HEADS):
        sl = slice(h * XA_HEAD_DIM, (h + 1) * XA_HEAD_DIM)
        s = _bnt(q[:, sl], kv[:, sl]) * (XA_HEAD_DIM ** -0.5)
        s = s - lax.stop_gradient(jnp.max(s, axis=-1, keepdims=True))
        p = jnp.exp(s)
        p = p / jnp.sum(p, axis=-1, keepdims=True)
        outs.append(_bnn(p, kv[:, D_MODEL + h * XA_HEAD_DIM:D_MODEL + (h + 1) * XA_HEAD_DIM]))
    return (jnp.concatenate(outs, axis=1),)


def f_relu2(u):
    r = jnp.maximum(u, 0.0)
    return (r * r,)


def f_loss(x, t, g):
    e = _rms(x, g) - t
    return (0.5 * jnp.mean(e * e, axis=-1, keepdims=True),)


def _row_spec(tr, width, cb):
    return pl.BlockSpec((tr, width), functools.partial(lambda i, cb: (i, cb), cb=cb))


def _full_spec(shape):
    return pl.BlockSpec(shape, lambda i: (0,) * len(shape))


def _rw(f, rows, params, outs, *, name, tr=256):
    s = rows[0][0].shape[0]
    tr = min(tr, s)
    nr, npar = len(rows), len(params)

    def body(*refs):
        rv = [r[...].astype(F32) for r in refs[:nr]]
        pv = [p[...] for p in refs[nr:nr + npar]]
        res = f(*rv, *pv)
        for o_ref, r in zip(refs[nr + npar:], res):
            o_ref[...] = r.astype(o_ref.dtype)

    return pl.pallas_call(
        body, name=name, grid=(s // tr,),
        in_specs=[_row_spec(tr, w, cb) for _, w, cb in rows] + [_full_spec(p.shape) for p in params],
        out_specs=[_row_spec(tr, w, 0) for w, _ in outs],
        out_shape=[jax.ShapeDtypeStruct((s, w), dt) for w, dt in outs],
        compiler_params=_params(("arbitrary",)),
    )(*[a for a, _, _ in rows], *params)


def _rw_bwd(f, rows, params, cts, *, row_grads, param_grads=(), add=None, primal=(), name, tr=256):
    s = rows[0][0].shape[0]
    tr = min(tr, s)
    nr, npar, nct = len(rows), len(params), len(cts)
    add = add or {}
    add_keys = list(add)

    def body(*refs):
        it = iter(refs)
        row_refs = [next(it) for _ in range(nr)]
        par_refs = [next(it) for _ in range(npar)]
        ct_refs = [next(it) for _ in range(nct)]
        add_refs = {k: next(it) for k in add_keys}
        rg_refs = [next(it) for _ in row_grads]
        pg_refs = [next(it) for _ in param_grads]
        pr_refs = [next(it) for _ in primal]
        rv = [r[...].astype(F32) for r in row_refs]
        pv = [p[...] for p in par_refs]
        out, vjp = jax.vjp(f, *rv, *pv)
        grads = vjp(tuple(c[...].astype(F32) for c in ct_refs))
        for (idx, _), ref in zip(row_grads, rg_refs):
            g = grads[idx]
            if idx in add_refs:
                g = g + add_refs[idx][...].astype(F32)
            ref[...] = g.astype(ref.dtype)

        @pl.when(pl.program_id(0) == 0)
        def _():
            for ref in pg_refs:
                ref[...] = jnp.zeros_like(ref)

        for idx, ref in zip(param_grads, pg_refs):
            ref[...] += grads[nr + idx]
        for (idx, _, _), ref in zip(primal, pr_refs):
            ref[...] = out[idx].astype(ref.dtype)

    in_arrays = [a for a, _, _ in rows] + list(params) + [a for a, _, _ in cts] + [add[k][0] for k in add_keys]
    in_specs = ([_row_spec(tr, w, cb) for _, w, cb in rows] + [_full_spec(p.shape) for p in params]
                + [_row_spec(tr, w, cb) for _, w, cb in cts] + [_row_spec(tr, add[k][1], add[k][2]) for k in add_keys])
    out_specs = ([_row_spec(tr, rows[idx][1], 0) for idx, _ in row_grads]
                 + [_full_spec(params[idx].shape) for idx in param_grads]
                 + [_row_spec(tr, w, 0) for _, w, _ in primal])
    out_shape = ([jax.ShapeDtypeStruct((s, rows[idx][1]), dt) for idx, dt in row_grads]
                 + [jax.ShapeDtypeStruct(params[idx].shape, F32) for idx in param_grads]
                 + [jax.ShapeDtypeStruct((s, w), dt) for _, w, dt in primal])
    return pl.pallas_call(
        body, name=name, grid=(s // tr,), in_specs=in_specs, out_specs=out_specs, out_shape=out_shape,
        compiler_params=_params(("arbitrary",)),
    )(*in_arrays)


def _mm(a, b, *, ta=False, tb=False, out_dtype=F32, add=None, name):
    m, k = (a.shape[1], a.shape[0]) if ta else a.shape
    n = b.shape[0] if tb else b.shape[1]
    tm = _pick(m, (1024, 512, 256, 128))
    tn = _pick(n, (1024, 512, 256, 128))
    tk = _pick(k, (512, 256, 128))
    nk = k // tk
    dims = (((0 if ta else 1,), (1 if tb else 0,)), ((), ()))

    def body(*refs):
        if add is None:
            a_ref, b_ref, o_ref, acc = refs
        else:
            a_ref, b_ref, add_ref, o_ref, acc = refs
        kk = pl.program_id(2)

        @pl.when(kk == 0)
        def _():
            acc[...] = jnp.zeros_like(acc)

        acc[...] += lax.dot_general(a_ref[...].astype(BF16), b_ref[...].astype(BF16), dims,
                                    preferred_element_type=F32)

        @pl.when(kk == nk - 1)
        def _():
            r = acc[...]
            if add is not None:
                r = r + add_ref[...]
            o_ref[...] = r.astype(o_ref.dtype)

    a_spec = (pl.BlockSpec((tk, tm), lambda i, j, kk: (kk, i)) if ta else pl.BlockSpec((tm, tk), lambda i, j, kk: (i, kk)))
    b_spec = (pl.BlockSpec((tn, tk), lambda i, j, kk: (j, kk)) if tb else pl.BlockSpec((tk, tn), lambda i, j, kk: (kk, j)))
    o_spec = pl.BlockSpec((tm, tn), lambda i, j, kk: (i, j))
    in_specs = [a_spec, b_spec] + ([o_spec] if add is not None else [])
    args = (a, b) + ((add,) if add is not None else ())
    return pl.pallas_call(
        body, name=name, grid=(m // tm, n // tn, nk), in_specs=in_specs, out_specs=o_spec,
        out_shape=jax.ShapeDtypeStruct((m, n), out_dtype),
        scratch_shapes=[pltpu.VMEM((tm, tn), F32)],
        compiler_params=_params(("arbitrary", "arbitrary", "arbitrary")),
    )(*args)


CONV_TT = 1024
CONV_SUB = 256


def _halo(k):
    return -(-(k - 1) // 8) * 8


def _pool_count(row0, c, rows):
    win = lax.shift_left(jnp.int32(2), c)
    t = row0 + lax.broadcasted_iota(jnp.int32, (rows, LANES), 0)
    return win, jnp.minimum(t + 1, win).astype(F32)


def _conv_fwd(x, xcb0, channels, w, k, *, pool=False, name):
    s = x.shape[0]
    tt = min(CONV_TT, s)
    sub = min(CONV_SUB, tt)
    nt = s // tt
    halo = _halo(k)

    def body(*refs):
        if pool:
            xc_ref, xp_ref, o_ref, xs = refs
        else:
            xc_ref, xp_ref, w_ref, o_ref, xs = refs
        c, i = pl.program_id(0), pl.program_id(1)
        xs[0:halo, :] = jnp.where(i > 0, xp_ref[...], 0.0)
        xs[halo:halo + tt, :] = xc_ref[...]
        for r0 in range(0, tt, sub):
            acc = jnp.zeros((sub, LANES), F32)
            if pool:
                win, cnt = _pool_count(i * tt + r0, c, sub)
                for j in range(k):
                    acc = acc + jnp.where(j >= k - win, xs[pl.ds(r0 + halo - (k - 1) + j, sub), :], 0.0)
                o_ref[r0:r0 + sub, :] = acc / cnt - xc_ref[r0:r0 + sub, :]
            else:
                for j in range(k):
                    acc = acc + w_ref[j:j + 1, :] * xs[pl.ds(r0 + halo - (k - 1) + j, sub), :]
                o_ref[r0:r0 + sub, :] = acc

    per = tt // halo
    in_specs = [pl.BlockSpec((tt, LANES), lambda c, i: (i, xcb0 + c)),
                pl.BlockSpec((halo, LANES), lambda c, i: (jnp.maximum(i * per - 1, 0), xcb0 + c))]
    args = [x, x]
    if not pool:
        in_specs.append(pl.BlockSpec((k, LANES), lambda c, i: (0, c)))
        args.append(w)
    return pl.pallas_call(
        body, name=name, grid=(channels // LANES, nt), in_specs=in_specs,
        out_specs=pl.BlockSpec((tt, LANES), lambda c, i: (i, c)),
        out_shape=jax.ShapeDtypeStruct((s, channels), F32),
        scratch_shapes=[pltpu.VMEM((halo + tt, LANES), F32)],
        compiler_params=_params(("arbitrary", "arbitrary")),
    )(*args)


def _conv_bwd(x, xcb0, channels, w, k, dy, *, pool=False, dx_dtype=BF16, name):
    s = dy.shape[0]
    tt = min(CONV_TT, s)
    sub = min(CONV_SUB, tt)
    nt = s // tt
    halo = _halo(k)

    def body(*refs):
        if pool:
            dyc_ref, dyn_ref, dx_ref, ys = refs
        else:
            xc_ref, xp_ref, dyc_ref, dyn_ref, w_ref, dx_ref, dw_ref, xs, ys = refs
        c, i = pl.program_id(0), pl.program_id(1)
        dyn = jnp.where(i < nt - 1, dyn_ref[...], 0.0)
        if pool:
            win, cnt = _pool_count(i * tt, c, tt)
            ys[0:tt, :] = dyc_ref[...] / cnt
            ys[tt:tt + halo, :] = dyn / win.astype(F32)
            for r0 in range(0, tt, sub):
                acc = jnp.zeros((sub, LANES), F32)
                for j in range(k):
                    acc = acc + jnp.where(j >= k - win, ys[pl.ds(r0 + (k - 1) - j, sub), :], 0.0)
                dx_ref[r0:r0 + sub, :] = (acc - dyc_ref[r0:r0 + sub, :]).astype(dx_ref.dtype)
            return
        ys[0:tt, :] = dyc_ref[...]
        ys[tt:tt + halo, :] = dyn
        xs[0:halo, :] = jnp.where(i > 0, xp_ref[...], 0.0)
        xs[halo:halo + tt, :] = xc_ref[...]

        @pl.when(i == 0)
        def _():
            dw_ref[...] = jnp.zeros_like(dw_ref)

        for r0 in range(0, tt, sub):
            acc = jnp.zeros((sub, LANES), F32)
            for j in range(k):
                acc = acc + w_ref[j:j + 1, :] * ys[pl.ds(r0 + (k - 1) - j, sub), :]
            dx_ref[r0:r0 + sub, :] = acc.astype(dx_ref.dtype)
            dyc = dyc_ref[r0:r0 + sub, :]
            for j in range(k):
                dw_ref[j:j + 1, :] += jnp.sum(dyc * xs[pl.ds(r0 + halo - (k - 1) + j, sub), :], axis=0, keepdims=True)

    per = tt // halo
    cur = lambda cb0: pl.BlockSpec((tt, LANES), lambda c, i: (i, cb0 + c))
    dy_specs = [cur(0), pl.BlockSpec((halo, LANES), lambda c, i: (jnp.minimum((i + 1) * per, s // halo - 1), c))]
    dx_spec = pl.BlockSpec((tt, LANES), lambda c, i: (i, c))
    dx_shape = jax.ShapeDtypeStruct((s, channels), dx_dtype)
    if pool:
        return pl.pallas_call(
            body, name=name, grid=(channels // LANES, nt), in_specs=dy_specs, out_specs=dx_spec, out_shape=dx_shape,
            scratch_shapes=[pltpu.VMEM((tt + halo, LANES), F32)],
            compiler_params=_params(("arbitrary", "arbitrary")),
        )(dy, dy)
    in_specs = [cur(xcb0), pl.BlockSpec((halo, LANES), lambda c, i: (jnp.maximum(i * per - 1, 0), xcb0 + c))] + dy_specs
    in_specs.append(pl.BlockSpec((k, LANES), lambda c, i: (0, c)))
    return pl.pallas_call(
        body, name=name, grid=(channels // LANES, nt), in_specs=in_specs,
        out_specs=[dx_spec, pl.BlockSpec((k, LANES), lambda c, i: (0, c))],
        out_shape=[dx_shape, jax.ShapeDtypeStruct((k, channels), F32)],
        scratch_shapes=[pltpu.VMEM((halo + tt, LANES), F32), pltpu.VMEM((tt + halo, LANES), F32)],
        compiler_params=_params(("arbitrary", "arbitrary")),
    )(x, x, dy, dy, w)


DN_STEP_ROWS = 256


def _dn_step(q, k, v, bb, gb, state):
    c = DN_CHUNK
    r = q.shape[0]
    ri = lax.broadcasted_iota(jnp.int32, (r, r), 0)
    ci = lax.broadcasted_iota(jnp.int32, (r, r), 1)
    same = jnp.bitwise_or(ri, c - 1) == jnp.bitwise_or(ci, c - 1)
    causal = same & (ri >= ci)
    strict = same & (ri > ci)
    eye = jnp.where(ri == ci, 1.0, 0.0)
    gam = _hnn(jnp.where(causal, 1.0, 0.0), gb)
    gam_i = jnp.concatenate([gam] * (r // LANES), axis=1)
    gam_j = _hnn(jnp.ones((r, r), F32), eye * gam_i)
    decay = jnp.where(causal, jnp.exp(jnp.where(causal, gam_i - gam_j, 0.0)), 0.0)
    row = lax.broadcasted_iota(jnp.int32, gam.shape, 0)
    lasts = [jnp.sum(jnp.where(row == n * c + c - 1, gam, 0.0), axis=0, keepdims=True) for n in range(r // c)]
    g_last = jnp.zeros_like(gam)
    for n, gl in enumerate(lasts):
        g_last = jnp.where(jnp.bitwise_or(row, c - 1) == n * c + c - 1, gl, g_last)
    kb = k * bb
    b = -jnp.where(strict, _mnt(kb, k) * decay, 0.0)
    inv = eye + b
    bp = b
    for _ in range(5):
        bp = _mnn(bp, bp)
        inv = inv + _mnn(inv, bp)
    eg = jnp.exp(gam)
    u = _mnn(inv, v * bb)
    w = _mnn(inv, kb * eg)
    attn = _mnt(q, k) * decay
    q_dec = q * eg
    k_dec = k * jnp.exp(g_last - gam)
    v_news, o_inter = [], []
    for n in range(r // c):
        s = slice(n * c, (n + 1) * c)
        v_new = u[s] - _mnn(w[s], state)
        o_inter.append(_mnn(q_dec[s], state))
        state = state * jnp.exp(lasts[n]) + _mtn(k_dec[s], v_new)
        v_news.append(v_new)
    o = jnp.concatenate(o_inter, axis=0) + _mnn(attn, jnp.concatenate(v_news, axis=0))
    return o, state


def _dn_specs(steps, reverse):
    if reverse:
        tile = pl.BlockSpec((DN_STEP_ROWS, HEAD_DIM), lambda h, i: (steps - 1 - i, h))
        st = pl.BlockSpec((1, 1, HEAD_DIM, HEAD_DIM), lambda h, i: (steps - 1 - i, h, 0, 0))
    else:
        tile = pl.BlockSpec((DN_STEP_ROWS, HEAD_DIM), lambda h, i: (i, h))
        st = pl.BlockSpec((1, 1, HEAD_DIM, HEAD_DIM), lambda h, i: (i, h, 0, 0))
    return tile, st


def _dn_fwd(q, k, v, bb, gb, *, name):
    s = q.shape[0]
    steps = s // DN_STEP_ROWS

    def body(q_ref, k_ref, v_ref, b_ref, g_ref, o_ref, st_ref, state):
        @pl.when(pl.program_id(1) == 0)
        def _():
            state[...] = jnp.zeros_like(state)

        st = state[...]
        st_ref[0, 0] = st
        o, st = _dn_step(q_ref[...], k_ref[...], v_ref[...], b_ref[...], g_ref[...], st)
        o_ref[...] = o
        state[...] = st

    tile, stspec = _dn_specs(steps, False)
    return pl.pallas_call(
        body, name=name, grid=(HEADS, steps), in_specs=[tile] * 5, out_specs=[tile, stspec],
        out_shape=[jax.ShapeDtypeStruct((s, MIX_W), F32),
                   jax.ShapeDtypeStruct((steps, HEADS, HEAD_DIM, HEAD_DIM), F32)],
        scratch_shapes=[pltpu.VMEM((HEAD_DIM, HEAD_DIM), F32)],
        compiler_params=_params(("arbitrary", "arbitrary")),
    )(q, k, v, bb, gb)


def _dn_bwd(q, k, v, bb, gb, states, do, *, name):
    s = q.shape[0]
    steps = s // DN_STEP_ROWS

    def body(q_ref, k_ref, v_ref, b_ref, g_ref, st_ref, do_ref, dq_ref, dk_ref, dv_ref, db_ref, dg_ref, dstate):
        @pl.when(pl.program_id(1) == 0)
        def _():
            dstate[...] = jnp.zeros_like(dstate)

        _, vjp = jax.vjp(_dn_step, q_ref[...], k_ref[...], v_ref[...], b_ref[...], g_ref[...], st_ref[0, 0])
        dq, dk, dv, db, dg, dst = vjp((do_ref[...], dstate[...]))
        dq_ref[...] = dq
        dk_ref[...] = dk
        dv_ref[...] = dv
        db_ref[...] = db
        dg_ref[...] = dg
        dstate[...] = dst

    tile, stspec = _dn_specs(steps, True)
    return pl.pallas_call(
        body, name=name, grid=(HEADS, steps), in_specs=[tile] * 5 + [stspec, tile], out_specs=[tile] * 5,
        out_shape=[jax.ShapeDtypeStruct((s, MIX_W), F32)] * 5,
        scratch_shapes=[pltpu.VMEM((HEAD_DIM, HEAD_DIM), F32)],
        compiler_params=_params(("arbitrary", "arbitrary")),
    )(q, k, v, bb, gb, states, do)


EW_TILE_BYTES = 2 << 20


def _ew(f, ins, out_dtypes, *, name):
    r, c = ins[0][0].shape[1:] if isinstance(ins[0], tuple) else ins[0].shape
    row_bytes = -(-c // LANES) * LANES * 4
    tr = r
    if r * row_bytes > EW_TILE_BYTES:
        tr = next((t for t in (4096, 2048, 1024, 512, 256, 128, 64, 32, 16, 8)
                   if r % t == 0 and t * row_bytes <= EW_TILE_BYTES), r)
    n_in = len(ins)

    def body(*refs):
        res = f(*[x[...] for x in refs[:n_in]])
        for o_ref, v in zip(refs[n_in:], res):
            o_ref[...] = v.astype(o_ref.dtype)

    in_specs, args = [], []
    for x in ins:
        if isinstance(x, tuple):
            in_specs.append(pl.BlockSpec((None, tr, c), functools.partial(lambda i, j: (j, i, 0), j=x[1])))
            args.append(x[0])
        else:
            in_specs.append(pl.BlockSpec((tr, c), lambda i: (i, 0)))
            args.append(x)
    return pl.pallas_call(
        body, name=name, grid=(r // tr,), in_specs=in_specs,
        out_specs=[pl.BlockSpec((tr, c), lambda i: (i, 0)) for _ in out_dtypes],
        out_shape=[jax.ShapeDtypeStruct((r, c), dt) for dt in out_dtypes],
        compiler_params=_params(("arbitrary",)),
    )(*args)


def f_adamw(w, g, m, v):
    m = ADAM_B1 * m + (1.0 - ADAM_B1) * g
    v = ADAM_B2 * v + (1.0 - ADAM_B2) * (g * g)
    m_hat = m / (1.0 - ADAM_B1 ** ADAM_STEP)
    v_hat = v / (1.0 - ADAM_B2 ** ADAM_STEP)
    return -ADAM_LR * (m_hat / (jnp.sqrt(v_hat) + ADAM_EPS) + ADAM_WD * w), m, v


def f_sum4(a, b, c, d):
    return (((a.astype(F32) + b.astype(F32)) + c.astype(F32)) + d.astype(F32),)


def f_add2(a, b):
    return (a + b,)


def _adamw(w, g, m, v, *, name):
    shape = w.shape
    two = (1, shape[0]) if w.ndim == 1 else (-1, shape[-1])
    outs = _ew(f_adamw, [t.reshape(two) for t in (w, g, m, v)], [F32, F32, F32], name=name)
    return [o.reshape(shape) for o in outs]


_ANY = pl.BlockSpec(memory_space=pl.ANY)


def _xy_exchange(srcs, *, broadcast, name):
    nt = len(srcs)

    def body(*refs):
        src_refs, out_refs = refs[:nt], refs[nt:2 * nt]
        send_sems, recv_sems, local_sems = refs[2 * nt:]
        x, y, c = lax.axis_index("x"), lax.axis_index("y"), lax.axis_index("c")
        me = 2 * x + y
        peers = [(1 - x, y), (x, 1 - y), (1 - x, 1 - y)]

        def block(t, j):
            return src_refs[t] if broadcast else src_refs[t].at[j]

        def copy(t, n, px, py, src_blk, dst_blk):
            return pltpu.make_async_remote_copy(
                src_ref=src_blk, dst_ref=dst_blk, send_sem=send_sems.at[3 * t + n], recv_sem=recv_sems.at[3 * t + n],
                device_id=(px, py, c), device_id_type=pl.DeviceIdType.MESH)

        mine = [pltpu.make_async_copy(block(t, me), out_refs[t].at[me], local_sems.at[t]) for t in range(nt)]
        sends = [copy(t, n, px, py, block(t, 2 * px + py), out_refs[t].at[me])
                 for t in range(nt) for n, (px, py) in enumerate(peers)]
        for cp in mine + sends:
            cp.start()
        for cp in sends:
            cp.wait_send()
        for t in range(nt):
            for n, (px, py) in enumerate(peers):
                copy(t, n, px, py, block(t, me), out_refs[t].at[2 * px + py]).wait_recv()
        for cp in mine:
            cp.wait()

    shapes = [(4,) + tuple(s.shape) if broadcast else tuple(s.shape) for s in srcs]
    return pl.pallas_call(
        body, name=name, in_specs=[_ANY] * nt, out_specs=[_ANY] * nt,
        out_shape=[jax.ShapeDtypeStruct(shp, s.dtype) for shp, s in zip(shapes, srcs)],
        scratch_shapes=[pltpu.SemaphoreType.DMA((3 * nt,)), pltpu.SemaphoreType.DMA((3 * nt,)),
                        pltpu.SemaphoreType.DMA((nt,))],
        compiler_params=pltpu.CompilerParams(has_side_effects=True),
    )(*srcs)


def _c_swap(srcs, *, name):
    nt = len(srcs)

    def body(*refs):
        src_refs, out_refs = refs[:nt], refs[nt:2 * nt]
        send_sems, recv_sems = refs[2 * nt:]
        sibling = (lax.axis_index("x"), lax.axis_index("y"), 1 - lax.axis_index("c"))
        cps = [pltpu.make_async_remote_copy(src_ref=src_refs[t], dst_ref=out_refs[t], send_sem=send_sems.at[t],
                                            recv_sem=recv_sems.at[t], device_id=sibling,
                                            device_id_type=pl.DeviceIdType.MESH) for t in range(nt)]
        for cp in cps:
            cp.start()
        for cp in cps:
            cp.wait()

    return pl.pallas_call(
        body, name=name, in_specs=[_ANY] * nt, out_specs=[_ANY] * nt,
        out_shape=[jax.ShapeDtypeStruct(s.shape, s.dtype) for s in srcs],
        scratch_shapes=[pltpu.SemaphoreType.DMA((nt,)), pltpu.SemaphoreType.DMA((nt,))],
        compiler_params=pltpu.CompilerParams(has_side_effects=True),
    )(*srcs)


def _all_sum(srcs, *, broadcast, name):
    got = _xy_exchange(srcs, broadcast=broadcast, name=name + "_xy")
    parts = []
    for t, gt in enumerate(got):
        g3 = gt.reshape(4, -1, gt.shape[-1])
        parts.append(_ew(f_sum4, [(g3, j) for j in range(4)], [F32], name=f"{name}_sum4_{t}")[0])
    others = _c_swap(parts, name=name + "_c")
    return [_ew(f_add2, [p, o], [F32], name=f"{name}_add2_{t}")[0].reshape(gt.shape[1:])
            for t, (p, o, gt) in enumerate(zip(parts, others, got))]


def _flat_rows(parts, dtype, row_multiple=8):
    flat = jnp.concatenate([p.reshape(-1).astype(dtype) for p in parts])
    pad = (-flat.shape[0]) % (row_multiple * LANES)
    if pad:
        flat = jnp.concatenate([flat, jnp.zeros((pad,), dtype)])
    return flat.reshape(-1, LANES)


def _unflat(buf, shapes):
    flat = buf.reshape(-1)
    out, off = [], 0
    for shp in shapes:
        n = 1
        for d in shp:
            n *= d
        out.append(flat[off:off + n].reshape(shp))
        off += n
    return out


def _regroup_w_in(w):
    cols = [w[:, a:b] for a, b in _Z_SEGS] + [jnp.zeros((w.shape[0], Z_W - IN_W), w.dtype)]
    return jnp.concatenate(cols, axis=1)


def _ungroup_w_in(wz):
    starts, off = {}, 0
    for a, b in _Z_SEGS:
        starts[a] = (off, off + b - a)
        off += b - a
    return jnp.concatenate([wz[:, starts[a][0]:starts[a][1]] for a in sorted(starts)], axis=1)


def _row(vec):
    return vec.reshape(1, -1)


def _lane_pad(vec):
    return jnp.zeros((1, LANES), F32).at[0, :vec.shape[0]].set(vec)


def _layer_fwd(x, mem, w, l):
    nm = lambda s: f"{s}_l{l}"
    sv = {"x0": x}
    h = _rw(f_norm, [(x, D_MODEL, 0)], [_row(w["norm_mix"])], [(D_MODEL, BF16)], name=nm("norm_mix"))[0]
    z = _mm(h, w["w_in_z"], name=nm("in_proj"))
    a_glu = _rw(f_glu, [(z, 2 * MIX_W, Z_A // (2 * MIX_W))], [], [(MIX_W, F32)], name=nm("glu"))[0]
    ac = _conv_fwd(a_glu, 0, MIX_W, w["conv_a_w"], CONV_K, name=nm("conv_a"))
    a_par = [_row(w["conv_a_b"]), _row(w["ln_a_g"]), _row(w["ln_a_b"])]
    a_out = _rw(f_lnsilu, [(ac, MIX_W, 0)], a_par, [(MIX_W, BF16)], name=nm("ln_a"))[0]
    qc = _conv_fwd(z, Z_QKV // LANES, 3 * MIX_W, w["dn_conv_w"], DN_CONV_K, name=nm("conv_dn"))
    dn_par = [_lane_pad(w["dn_a_log"]), _lane_pad(w["dn_dt_bias"])]
    qn, kn, v, bb, gb = _rw(f_dnprep, [(qc, 3 * MIX_W, 0), (z, LANES, Z_BD // LANES)], dn_par,
                            [(MIX_W, F32)] * 5, name=nm("dn_prep"))
    o, states = _dn_fwd(qn, kn, v, bb, gb, name=nm("dn_scan"))
    o_out = _rw(f_dnout, [(o, MIX_W, 0), (z, MIX_W, Z_DG // MIX_W)], [_row(w["dn_norm_g"])], [(MIX_W, BF16)],
                name=nm("dn_out"))[0]
    gm_par = [_row(w["gm_ln_g"]), _row(w["gm_ln_b"]), w["gm_ws"].reshape(4 * GM_CHUNK, GM_CHUNK), w["gm_bs"].T]
    c_out = _rw(f_gmlp, [(z, 2 * MIX_W, Z_GM // (2 * MIX_W))], gm_par, [(MIX_W, BF16)], name=nm("gmlp"))[0]
    pc = _conv_fwd(z, Z_POOL // LANES, MIX_W, None, POOL_K, pool=True, name=nm("pool"))
    p_par = [w["pool_w"].reshape(4 * LANES, LANES), _row(w["pool_scale"])]
    p_out = _rw(f_poolpost, [(pc, MIX_W, 0)], p_par, [(MIX_W, BF16)], name=nm("pool_post"))[0]
    branches = [a_out, o_out, c_out, p_out]
    proj = [_mm(br, w["w_branch"][n], name=nm(f"branch{n}")) for n, br in enumerate(branches)]
    merged = _rw(f_merge, [(z, N_BRANCH * D_MODEL, 0)] + [(p, D_MODEL, 0) for p in proj], [], [(D_MODEL, BF16)],
                 name=nm("merge"), tr=128)[0]
    x1 = _mm(merged, w["w_out"], add=x, name=nm("out_proj"))
    sv.update(h=h, z=z, a_glu=a_glu, ac=ac, qc=qc, qn=qn, kn=kn, v=v, bb=bb, gb=gb, o=o, states=states, pc=pc,
              branches=branches, proj=proj, merged=merged, x1=x1)
    h2 = _rw(f_norm, [(x1, D_MODEL, 0)], [_row(w["norm_xa"])], [(D_MODEL, BF16)], name=nm("norm_xa"))[0]
    mn = _rw(f_norm, [(mem, D_MODEL, 0)], [_row(w["norm_mem"])], [(D_MODEL, BF16)], name=nm("norm_mem"))[0]
    kv = _mm(mn, w["xa_wkv"], name=nm("xa_kv"))
    q = _mm(h2, w["xa_wq"], name=nm("xa_q"))
    att = _rw(f_attn, [(q, D_MODEL, 0)], [kv], [(D_MODEL, BF16)], name=nm("xa_attn"))[0]
    x2 = _mm(att, w["xa_wo"], add=x1, name=nm("xa_o"))
    sv.update(h2=h2, mn=mn, kv=kv, q=q, att=att, x2=x2)
    h3 = _rw(f_norm, [(x2, D_MODEL, 0)], [_row(w["norm_mlp"])], [(D_MODEL, BF16)], name=nm("norm_mlp"))[0]
    u = _mm(h3, w["mlp_w1"], name=nm("mlp_up"))
    act = _rw(f_relu2, [(u, FFN_W, 0)], [], [(FFN_W, BF16)], name=nm("relu2"))[0]
    x3 = _mm(act, w["mlp_w2"], add=x2, name=nm("mlp_down"))
    sv.update(h3=h3, u=u, act=act)
    return x3, sv


def _layer_bwd(dx, mem, w, sv, l):
    nm = lambda s: f"{s}_bwd_l{l}"
    s = dx.shape[0]
    g = {}
    dact = _mm(dx, w["mlp_w2"], tb=True, name=nm("mlp_down_dx"))
    g["mlp_w2"] = _mm(sv["act"], dx, ta=True, out_dtype=BF16, name=nm("mlp_down_dw"))
    du = _rw_bwd(f_relu2, [(sv["u"], FFN_W, 0)], [], [(dact, FFN_W, 0)], row_grads=[(0, BF16)], name=nm("relu2"))[0]
    g["mlp_w1"] = _mm(sv["h3"], du, ta=True, out_dtype=BF16, name=nm("mlp_up_dw"))
    dh3 = _mm(du, w["mlp_w1"], tb=True, name=nm("mlp_up_dx"))
    dx2, g["norm_mlp"] = _rw_bwd(f_norm, [(sv["x2"], D_MODEL, 0)], [_row(w["norm_mlp"])], [(dh3, D_MODEL, 0)],
                                 row_grads=[(0, F32)], param_grads=[0], add={0: (dx, D_MODEL, 0)}, name=nm("norm_mlp"))
    datt = _mm(dx2, w["xa_wo"], tb=True, name=nm("xa_o_dx"))
    g["xa_wo"] = _mm(sv["att"], dx2, ta=True, out_dtype=BF16, name=nm("xa_o_dw"))
    dq, dkv = _rw_bwd(f_attn, [(sv["q"], D_MODEL, 0)], [sv["kv"]], [(datt, D_MODEL, 0)], row_grads=[(0, BF16)],
                      param_grads=[0], name=nm("xa_attn"))
    g["xa_wq"] = _mm(sv["h2"], dq, ta=True, out_dtype=BF16, name=nm("xa_q_dw"))
    dh2 = _mm(dq, w["xa_wq"], tb=True, name=nm("xa_q_dx"))
    g["xa_wkv"] = _mm(sv["mn"], dkv, ta=True, out_dtype=BF16, name=nm("xa_kv_dw"))
    dmn = _mm(dkv, w["xa_wkv"], tb=True, name=nm("xa_kv_dx"))
    g["norm_mem"] = _rw_bwd(f_norm, [(mem, D_MODEL, 0)], [_row(w["norm_mem"])], [(dmn, D_MODEL, 0)], row_grads=[],
                            param_grads=[0], name=nm("norm_mem"))[0]
    dx1, g["norm_xa"] = _rw_bwd(f_norm, [(sv["x1"], D_MODEL, 0)], [_row(w["norm_xa"])], [(dh2, D_MODEL, 0)],
                                row_grads=[(0, F32)], param_grads=[0], add={0: (dx2, D_MODEL, 0)}, name=nm("norm_xa"))
    z = sv["z"]
    dmerged = _mm(dx1, w["w_out"], tb=True, name=nm("out_proj_dx"))
    g["w_out"] = _mm(sv["merged"], dx1, ta=True, out_dtype=BF16, name=nm("out_proj_dw"))
    mg = _rw_bwd(f_merge, [(z, N_BRANCH * D_MODEL, 0)] + [(p, D_MODEL, 0) for p in sv["proj"]], [],
                 [(dmerged, D_MODEL, 0)], row_grads=[(i, BF16) for i in range(5)], name=nm("merge"), tr=128)
    dgate, dproj = mg[0], mg[1:]
    g["w_branch"] = jnp.stack([_mm(br, dp, ta=True, out_dtype=BF16, name=nm(f"branch{n}_dw"))
                               for n, (br, dp) in enumerate(zip(sv["branches"], dproj))])
    dbr = [_mm(dp, w["w_branch"][n], tb=True, name=nm(f"branch{n}_dx")) for n, dp in enumerate(dproj)]
    p_par = [w["pool_w"].reshape(4 * LANES, LANES), _row(w["pool_scale"])]
    dpc, dpw, g["pool_scale"] = _rw_bwd(f_poolpost, [(sv["pc"], MIX_W, 0)], p_par, [(dbr[3], MIX_W, 0)],
                                        row_grads=[(0, F32)], param_grads=[0, 1], name=nm("pool_post"))
    g["pool_w"] = dpw.reshape(4, LANES, LANES)
    dpool = _conv_bwd(None, 0, MIX_W, None, POOL_K, dpc, pool=True, name=nm("pool"))
    gm_par = [_row(w["gm_ln_g"]), _row(w["gm_ln_b"]), w["gm_ws"].reshape(4 * GM_CHUNK, GM_CHUNK), w["gm_bs"].T]
    dgm, g["gm_ln_g"], g["gm_ln_b"], dws, dbst = _rw_bwd(
        f_gmlp, [(z, 2 * MIX_W, Z_GM // (2 * MIX_W))], gm_par, [(dbr[2], MIX_W, 0)], row_grads=[(0, BF16)],
        param_grads=[0, 1, 2, 3], name=nm("gmlp"))
    g["gm_ws"] = dws.reshape(4, GM_CHUNK, GM_CHUNK)
    g["gm_bs"] = dbst.T
    do, ddg, g["dn_norm_g"] = _rw_bwd(f_dnout, [(sv["o"], MIX_W, 0), (z, MIX_W, Z_DG // MIX_W)], [_row(w["dn_norm_g"])],
                                      [(dbr[1], MIX_W, 0)], row_grads=[(0, F32), (1, BF16)], param_grads=[0],
                                      name=nm("dn_out"))
    dqn, dkn, dv, dbb, dgb = _dn_bwd(sv["qn"], sv["kn"], sv["v"], sv["bb"], sv["gb"], sv["states"], do, name=nm("dn_scan"))
    dn_par = [_lane_pad(w["dn_a_log"]), _lane_pad(w["dn_dt_bias"])]
    dqc, dbd, dal, ddt = _rw_bwd(
        f_dnprep, [(sv["qc"], 3 * MIX_W, 0), (z, LANES, Z_BD // LANES)], dn_par,
        [(t, MIX_W, 0) for t in (dqn, dkn, dv, dbb, dgb)], row_grads=[(0, F32), (1, BF16)], param_grads=[0, 1],
        name=nm("dn_prep"))
    g["dn_a_log"], g["dn_dt_bias"] = dal[0, :HEADS], ddt[0, :HEADS]
    dqkv, g["dn_conv_w"] = _conv_bwd(z, Z_QKV // LANES, 3 * MIX_W, w["dn_conv_w"], DN_CONV_K, dqc, name=nm("conv_dn"))
    a_par = [_row(w["conv_a_b"]), _row(w["ln_a_g"]), _row(w["ln_a_b"])]
    dac, g["conv_a_b"], g["ln_a_g"], g["ln_a_b"] = _rw_bwd(
        f_lnsilu, [(sv["ac"], MIX_W, 0)], a_par, [(dbr[0], MIX_W, 0)], row_grads=[(0, F32)], param_grads=[0, 1, 2],
        name=nm("ln_a"))
    dglu, g["conv_a_w"] = _conv_bwd(sv["a_glu"], 0, MIX_W, w["conv_a_w"], CONV_K, dac, dx_dtype=F32, name=nm("conv_a"))
    da_in = _rw_bwd(f_glu, [(z, 2 * MIX_W, Z_A // (2 * MIX_W))], [], [(dglu, MIX_W, 0)], row_grads=[(0, BF16)],
                    name=nm("glu"))[0]
    dz = jnp.concatenate([dgate, da_in, dgm, dqkv, ddg, dpool, dbd, jnp.zeros((s, Z_W - Z_BD - LANES), BF16)], axis=1)
    g["w_in"] = _ungroup_w_in(_mm(sv["h"], dz, ta=True, out_dtype=BF16, name=nm("in_proj_dw")))
    dh = _mm(dz, w["w_in_z"], tb=True, name=nm("in_proj_dx"))
    dx0, g["norm_mix"] = _rw_bwd(f_norm, [(sv["x0"], D_MODEL, 0)], [_row(w["norm_mix"])], [(dh, D_MODEL, 0)],
                                 row_grads=[(0, F32)], param_grads=[0], add={0: (dx1, D_MODEL, 0)}, name=nm("norm_mix"))
    for n in ("norm_mlp", "norm_xa", "norm_mem", "norm_mix", "pool_scale", "gm_ln_g", "gm_ln_b", "dn_norm_g",
              "conv_a_b", "ln_a_g", "ln_a_b"):
        g[n] = g[n].reshape(-1)
    return dx0, g


def _shard_slice(a, axis, j):
    n = a.shape[axis] // 4
    return lax.slice_in_dim(a, j * n, (j + 1) * n, axis=axis)


def kernel(x, mem, norm_mix, w_in, conv_a_w, conv_a_b, ln_a_g, ln_a_b, dn_conv_w, dn_a_log, dn_dt_bias, dn_norm_g, gm_ln_g, gm_ln_b, gm_ws, gm_bs, pool_w, pool_scale, w_branch, w_out, norm_xa, norm_mem, xa_wq, xa_wkv, xa_wo, norm_mlp, mlp_w1, mlp_w2, norm_f, loss_target, m_norm_mix, m_w_in, m_conv_a_w, m_conv_a_b, m_ln_a_g, m_ln_a_b, m_dn_conv_w, m_dn_a_log, m_dn_dt_bias, m_dn_norm_g, m_gm_ln_g, m_gm_ln_b, m_gm_ws, m_gm_bs, m_pool_w, m_pool_scale, m_w_branch, m_w_out, m_norm_xa, m_norm_mem, m_xa_wq, m_xa_wkv, m_xa_wo, m_norm_mlp, m_mlp_w1, m_mlp_w2, m_norm_f, v_norm_mix, v_w_in, v_conv_a_w, v_conv_a_b, v_ln_a_g, v_ln_a_b, v_dn_conv_w, v_dn_a_log, v_dn_dt_bias, v_dn_norm_g, v_gm_ln_g, v_gm_ln_b, v_gm_ws, v_gm_bs, v_pool_w, v_pool_scale, v_w_branch, v_w_out, v_norm_xa, v_norm_mem, v_xa_wq, v_xa_wkv, v_xa_wo, v_norm_mlp, v_mlp_w1, v_mlp_w2, v_norm_f):
    given = dict(locals())
    wts = {n: given[n] for n in WEIGHTS}
    depth = norm_mix.shape[0]
    x = x[0]
    mem = mem[0]
    tgt = loss_target[0]

    gathered = dict(zip(BIG, _xy_exchange([wts[n].astype(BF16) for n in BIG], broadcast=True, name="gather_weights")))
    conv_shapes = [wts[n].shape for n in CONVS]
    conv_gathered = _xy_exchange([_flat_rows([wts[n] for n in CONVS], F32)], broadcast=True, name="gather_filters")[0]
    conv_parts = [_unflat(conv_gathered[j], conv_shapes) for j in range(4)]
    conv_full = {n: jnp.concatenate([conv_parts[j][i] for j in range(4)], axis=2) for i, n in enumerate(CONVS)}

    def layer_weights(l):
        w = {n: jnp.concatenate([gathered[n][j, l] for j in range(4)], axis=BIG_AXIS[n]) for n in BIG}
        w["w_in_z"] = _regroup_w_in(w.pop("w_in"))
        for n in SMALL:
            if n != "norm_f":
                w[n] = conv_full[n][l] if n in CONVS else wts[n][l]
        return w

    saved, layer_w = [], []
    for l in range(depth):
        w = layer_weights(l)
        x, sv = _layer_fwd(x, mem, w, l)
        saved.append(sv)
        layer_w.append(w)
    dx, g_norm_f, loss_rows = _rw_bwd(
        f_loss, [(x, D_MODEL, 0), (tgt, D_MODEL, 0)], [_row(norm_f)], [(jnp.ones((x.shape[0], 1), F32), 1, 0)],
        row_grads=[(0, F32)], param_grads=[0], primal=[(0, 1, F32)], name="loss_head")
    loss = lax.psum(jnp.sum(loss_rows), ("x", "y", "c"))

    grads = [None] * depth
    for l in reversed(range(depth)):
        dx, grads[l] = _layer_bwd(dx, mem, layer_w[l], saved[l], l)
        saved[l] = None
    grad_x = dx[None]

    blocks = [jnp.stack([jnp.stack([_shard_slice(grads[l][n], BIG_AXIS[n], j) for l in range(depth)])
                         for j in range(4)]).astype(BF16) for n in BIG]
    gw = dict(zip(BIG, _all_sum(blocks, broadcast=False, name="reduce_big")))

    small_names = [n for n in SMALL if n != "norm_f"]
    small_shapes = [(depth,) + (conv_full[n].shape[1:] if n in CONVS else wts[n].shape[1:]) for n in small_names]
    small_buf = _flat_rows([jnp.stack([grads[l][n] for l in range(depth)]) for n in small_names] + [g_norm_f], F32,
                           row_multiple=1024)
    small_sum = _all_sum([small_buf], broadcast=True, name="reduce_small")[0]
    small_parts = _unflat(small_sum, small_shapes + [norm_f.shape])
    me = 2 * lax.axis_index("x") + lax.axis_index("y")
    for n, t in zip(small_names + ["norm_f"], small_parts):
        if n in CONVS:
            width = wts[n].shape[2]
            t = lax.dynamic_slice_in_dim(t, me * width, width, axis=2)
        gw[n] = t

    deltas, new_m, new_v = {}, {}, {}
    for n in WEIGHTS:
        deltas[n], new_m[n], new_v[n] = _adamw(wts[n], gw[n], given["m_" + n], given["v_" + n], name=f"adamw_{n}")
    return (loss, grad_x, *[gw[n] for n in WEIGHTS], *[deltas[n] for n in WEIGHTS], *[new_m[n] for n in WEIGHTS],
            *[new_v[n] for n in WEIGHTS])
```

```python
import functools

import jax
import jax.numpy as jnp
from jax import lax
from jax.experimental import pallas as pl
from jax.experimental.pallas import tpu as pltpu

F32 = jnp.float32
BF16 = jnp.bfloat16
HI = lax.Precision.HIGHEST

D_MODEL = 1024
MIX_W = 512
N_BRANCH = 4
HEADS = 4
HEAD_DIM = 128
CONV_K = 31
DN_CONV_K = 4
DN_CHUNK = 64
GM_CHUNK = 128
POOL_K = 16
XA_HEADS = 4
XA_HEAD_DIM = 256
FFN_W = 4096
IN_W = 8712
Z_W = 9216
LANES = 128
VMEM_LIMIT = 56 * 1024 * 1024

ADAM_LR, ADAM_B1, ADAM_B2, ADAM_EPS, ADAM_WD, ADAM_STEP = 0.001, 0.9, 0.999, 1e-08, 0.01, 10

Z_GATE, Z_A, Z_GM, Z_QKV, Z_DG, Z_POOL, Z_BD = 0, 4096, 5120, 6144, 7680, 8192, 8704
_Z_SEGS = ((4616, 8712), (0, 1024), (3080, 4104), (1024, 2560), (2560, 3072), (4104, 4616), (3072, 3080))

BIG = ("w_in", "w_branch", "w_out", "xa_wq", "xa_wkv", "xa_wo", "mlp_w1", "mlp_w2")
BIG_AXIS = {"w_in": 1, "w_branch": 2, "w_out": 0, "xa_wq": 0, "xa_wkv": 1, "xa_wo": 0, "mlp_w1": 1, "mlp_w2": 0}
CONVS = ("conv_a_w", "dn_conv_w")
WEIGHTS = ("norm_mix", "w_in", "conv_a_w", "conv_a_b", "ln_a_g", "ln_a_b", "dn_conv_w", "dn_a_log", "dn_dt_bias",
           "dn_norm_g", "gm_ln_g", "gm_ln_b", "gm_ws", "gm_bs", "pool_w", "pool_scale", "w_branch", "w_out",
           "norm_xa", "norm_mem", "xa_wq", "xa_wkv", "xa_wo", "norm_mlp", "mlp_w1", "mlp_w2", "norm_f")
SMALL = tuple(n for n in WEIGHTS if n not in BIG)


def _params(sem=None):
    return pltpu.CompilerParams(vmem_limit_bytes=VMEM_LIMIT, dimension_semantics=sem)


def _pick(n, cands):
    for c in cands:
        if n % c == 0:
            return c
    return n


def _make_dots(prec):
    def raw(a, b, ca, cb):
        if prec is None:
            a, b = a.astype(BF16), b.astype(BF16)
        return lax.dot_general(a, b, (((ca,), (cb,)), ((), ())), precision=prec, preferred_element_type=F32)

    nn = jax.custom_vjp(lambda a, b: raw(a, b, 1, 0))
    nt = jax.custom_vjp(lambda a, b: raw(a, b, 1, 1))
    tn = jax.custom_vjp(lambda a, b: raw(a, b, 0, 0))
    nn.defvjp(lambda a, b: (raw(a, b, 1, 0), (a, b)), lambda r, g: (raw(g, r[1], 1, 1), raw(r[0], g, 0, 0)))
    nt.defvjp(lambda a, b: (raw(a, b, 1, 1), (a, b)), lambda r, g: (raw(g, r[1], 1, 0), raw(g, r[0], 0, 0)))
    tn.defvjp(lambda a, b: (raw(a, b, 0, 0), (a, b)), lambda r, g: (raw(r[1], g, 1, 1), raw(r[0], g, 1, 0)))
    return nn, nt, tn


_bnn, _bnt, _btn = _make_dots(None)
_hnn, _hnt, _htn = _make_dots(HI)
_mnn, _mnt, _mtn = _make_dots(lax.Precision.HIGH)


def _col(x, j):
    lane = lax.broadcasted_iota(jnp.int32, x.shape, 1)
    return jnp.sum(jnp.where(lane == j, x, 0.0), axis=-1, keepdims=True)


def _rms(x, g, eps=1e-6):
    return x * lax.rsqrt(jnp.mean(x * x, axis=-1, keepdims=True) + eps) * g


def _ln(x, g, b, eps=1e-5):
    xc = x - jnp.mean(x, axis=-1, keepdims=True)
    var = jnp.mean(xc * xc, axis=-1, keepdims=True)
    return xc * lax.rsqrt(var + eps) * g + b


_sigmoid = jax.nn.sigmoid


def _silu(x):
    return x * _sigmoid(x)


def _softplus(x):
    return jnp.maximum(x, 0.0) + jnp.log1p(jnp.exp(-jnp.abs(x)))


def _gelu(x):
    return 0.5 * x * (1.0 + lax.erf(x * 0.7071067811865476))


def f_norm(x, g):
    return (_rms(x, g),)


def f_glu(a_in):
    return (a_in[:, :MIX_W] * _sigmoid(a_in[:, MIX_W:]),)


def f_lnsilu(ac, cb, g, b):
    return (_silu(_ln(ac + cb, g, b)),)


def f_dnprep(qc, bd, a_log, dt_bias):
    t = qc.shape[0]
    qkv = _silu(qc)
    qs, ks, bs, gs = [], [], [], []
    for h in range(HEADS):
        q = qkv[:, h * HEAD_DIM:(h + 1) * HEAD_DIM]
        k = qkv[:, MIX_W + h * HEAD_DIM:MIX_W + (h + 1) * HEAD_DIM]
        qs.append(q * lax.rsqrt(jnp.sum(q * q, axis=-1, keepdims=True) + 1e-6) * (HEAD_DIM ** -0.5))
        ks.append(k * lax.rsqrt(jnp.sum(k * k, axis=-1, keepdims=True) + 1e-6))
        beta = _sigmoid(_col(bd, h))
        g = -jnp.exp(_col(a_log, h)) * _softplus(_col(bd, HEADS + h) + _col(dt_bias, h))
        bs.append(jnp.broadcast_to(beta, (t, HEAD_DIM)))
        gs.append(jnp.broadcast_to(g, (t, HEAD_DIM)))
    cat = lambda xs: jnp.concatenate(xs, axis=1)
    return cat(qs), cat(ks), qkv[:, 2 * MIX_W:], cat(bs), cat(gs)


def f_dnout(o, dgate, g):
    outs = []
    for h in range(HEADS):
        sl = slice(h * HEAD_DIM, (h + 1) * HEAD_DIM)
        outs.append(_rms(o[:, sl], g) * _silu(dgate[:, sl]))
    return (jnp.concatenate(outs, axis=1),)


def f_gmlp(gm_in, lg, lb, ws, bst):
    t = gm_in.shape[0]
    ge = _gelu(gm_in)
    u, vg = ge[:, :MIX_W], _ln(ge[:, MIX_W:], lg, lb)
    ri = lax.broadcasted_iota(jnp.int32, (GM_CHUNK, GM_CHUNK), 0)
    ci = lax.broadcasted_iota(jnp.int32, (GM_CHUNK, GM_CHUNK), 1)
    chunks = []
    for n in range(t // GM_CHUNK):
        vc = vg[n * GM_CHUNK:(n + 1) * GM_CHUNK]
        cols = []
        for g in range(4):
            w = jnp.where(ri >= ci, ws[g * GM_CHUNK:(g + 1) * GM_CHUNK], 0.0)
            cols.append(_bnn(w, vc[:, g * LANES:(g + 1) * LANES]) + _col(bst, g))
        chunks.append(jnp.concatenate(cols, axis=1))
    mixed = chunks[0] if len(chunks) == 1 else jnp.concatenate(chunks, axis=0)
    return (u * mixed,)


def f_poolpost(pc, pw, ps):
    cols = [_bnn(pc[:, g * LANES:(g + 1) * LANES], pw[g * LANES:(g + 1) * LANES]) for g in range(4)]
    return (jnp.concatenate(cols, axis=1) * ps,)


def f_merge(gate, p0, p1, p2, p3):
    m = None
    for n, p in enumerate((p0, p1, p2, p3)):
        t = _sigmoid(gate[:, n * D_MODEL:(n + 1) * D_MODEL]) * p
        m = t if m is None else m + t
    return (m,)


def f_attn(q, kv):
    outs = []
    for h in range(XA_HEADS):
        sl = slice(h * XA_HEAD_DIM, (h + 1) * XA_HEAD_DIM)
        s = _bnt(q[:, sl], kv[:, sl]) * (XA_HEAD_DIM ** -0.5)
        s = s - lax.stop_gradient(jnp.max(s, axis=-1, keepdims=True))
        p = jnp.exp(s)
        p = p / jnp.sum(p, axis=-1, keepdims=True)
        outs.append(_bnn(p, kv[:, D_MODEL + h * XA_HEAD_DIM:D_MODEL + (h + 1) * XA_HEAD_DIM]))
    return (jnp.concatenate(outs, axis=1),)


def f_relu2(u):
    r = jnp.maximum(u, 0.0)
    return (r * r,)


def f_loss(x, t, g):
    e = _rms(x, g) - t
    return (0.5 * jnp.mean(e * e, axis=-1, keepdims=True),)


def _row_spec(tr, width, cb):
    return pl.BlockSpec((tr, width), functools.partial(lambda i, cb: (i, cb), cb=cb))


def _full_spec(shape):
    return pl.BlockSpec(shape, lambda i: (0,) * len(shape))


def _rw(f, rows, params, outs, *, name, tr=256):
    s = rows[0][0].shape[0]
    tr = min(tr, s)
    nr, npar = len(rows), len(params)

    def body(*refs):
        rv = [r[...].astype(F32) for r in refs[:nr]]
        pv = [p[...] for p in refs[nr:nr + npar]]
        res = f(*rv, *pv)
        for o_ref, r in zip(refs[nr + npar:], res):
            o_ref[...] = r.astype(o_ref.dtype)

    return pl.pallas_call(
        body, name=name, grid=(s // tr,),
        in_specs=[_row_spec(tr, w, cb) for _, w, cb in rows] + [_full_spec(p.shape) for p in params],
        out_specs=[_row_spec(tr, w, 0) for w, _ in outs],
        out_shape=[jax.ShapeDtypeStruct((s, w), dt) for w, dt in outs],
        compiler_params=_params(("arbitrary",)),
    )(*[a for a, _, _ in rows], *params)


def _rw_bwd(f, rows, params, cts, *, row_grads, param_grads=(), add=None, primal=(), name, tr=256):
    s = rows[0][0].shape[0]
    tr = min(tr, s)
    nr, npar, nct = len(rows), len(params), len(cts)
    add = add or {}
    add_keys = list(add)

    def body(*refs):
        it = iter(refs)
        row_refs = [next(it) for _ in range(nr)]
        par_refs = [next(it) for _ in range(npar)]
        ct_refs = [next(it) for _ in range(nct)]
        add_refs = {k: next(it) for k in add_keys}
        rg_refs = [next(it) for _ in row_grads]
        pg_refs = [next(it) for _ in param_grads]
        pr_refs = [next(it) for _ in primal]
        rv = [r[...].astype(F32) for r in row_refs]
        pv = [p[...] for p in par_refs]
        out, vjp = jax.vjp(f, *rv, *pv)
        grads = vjp(tuple(c[...].astype(F32) for c in ct_refs))
        for (idx, _), ref in zip(row_grads, rg_refs):
            g = grads[idx]
            if idx in add_refs:
                g = g + add_refs[idx][...].astype(F32)
            ref[...] = g.astype(ref.dtype)

        @pl.when(pl.program_id(0) == 0)
        def _():
            for ref in pg_refs:
                ref[...] = jnp.zeros_like(ref)

        for idx, ref in zip(param_grads, pg_refs):
            ref[...] += grads[nr + idx]
        for (idx, _, _), ref in zip(primal, pr_refs):
            ref[...] = out[idx].astype(ref.dtype)

    in_arrays = [a for a, _, _ in rows] + list(params) + [a for a, _, _ in cts] + [add[k][0] for k in add_keys]
    in_specs = ([_row_spec(tr, w, cb) for _, w, cb in rows] + [_full_spec(p.shape) for p in params]
                + [_row_spec(tr, w, cb) for _, w, cb in cts] + [_row_spec(tr, add[k][1], add[k][2]) for k in add_keys])
    out_specs = ([_row_spec(tr, rows[idx][1], 0) for idx, _ in row_grads]
                 + [_full_spec(params[idx].shape) for idx in param_grads]
                 + [_row_spec(tr, w, 0) for _, w, _ in primal])
    out_shape = ([jax.ShapeDtypeStruct((s, rows[idx][1]), dt) for idx, dt in row_grads]
                 + [jax.ShapeDtypeStruct(params[idx].shape, F32) for idx in param_grads]
                 + [jax.ShapeDtypeStruct((s, w), dt) for _, w, dt in primal])
    return pl.pallas_call(
        body, name=name, grid=(s // tr,), in_specs=in_specs, out_specs=out_specs, out_shape=out_shape,
        compiler_params=_params(("arbitrary",)),
    )(*in_arrays)


def _mm(a, b, *, ta=False, tb=False, out_dtype=F32, add=None, name):
    m, k = (a.shape[1], a.shape[0]) if ta else a.shape
    n = b.shape[0] if tb else b.shape[1]
    tm = _pick(m, (1024, 512, 256, 128))
    tn = _pick(n, (1024, 512, 256, 128))
    tk = _pick(k, (512, 256, 128))
    nk = k // tk
    dims = (((0 if ta else 1,), (1 if tb else 0,)), ((), ()))

    def body(*refs):
        if add is None:
            a_ref, b_ref, o_ref, acc = refs
        else:
            a_ref, b_ref, add_ref, o_ref, acc = refs
        kk = pl.program_id(2)

        @pl.when(kk == 0)
        def _():
            acc[...] = jnp.zeros_like(acc)

        acc[...] += lax.dot_general(a_ref[...].astype(BF16), b_ref[...].astype(BF16), dims,
                                    preferred_element_type=F32)

        @pl.when(kk == nk - 1)
        def _():
            r = acc[...]
            if add is not None:
                r = r + add_ref[...]
            o_ref[...] = r.astype(o_ref.dtype)

    a_spec = (pl.BlockSpec((tk, tm), lambda i, j, kk: (kk, i)) if ta else pl.BlockSpec((tm, tk), lambda i, j, kk: (i, kk)))
    b_spec = (pl.BlockSpec((tn, tk), lambda i, j, kk: (j, kk)) if tb else pl.BlockSpec((tk, tn), lambda i, j, kk: (kk, j)))
    o_spec = pl.BlockSpec((tm, tn), lambda i, j, kk: (i, j))
    in_specs = [a_spec, b_spec] + ([o_spec] if add is not None else [])
    args = (a, b) + ((add,) if add is not None else ())
    return pl.pallas_call(
        body, name=name, grid=(m // tm, n // tn, nk), in_specs=in_specs, out_specs=o_spec,
        out_shape=jax.ShapeDtypeStruct((m, n), out_dtype),
        scratch_shapes=[pltpu.VMEM((tm, tn), F32)],
        compiler_params=_params(("arbitrary", "arbitrary", "arbitrary")),
    )(*args)


CONV_TT = 1024
CONV_SUB = 256


def _halo(k):
    return -(-(k - 1) // 8) * 8


def _pool_count(row0, c, rows):
    win = lax.shift_left(jnp.int32(2), c)
    t = row0 + lax.broadcasted_iota(jnp.int32, (rows, LANES), 0)
    return win, jnp.minimum(t + 1, win).astype(F32)


def _conv_fwd(x, xcb0, channels, w, k, *, pool=False, name):
    s = x.shape[0]
    tt = min(CONV_TT, s)
    sub = min(CONV_SUB, tt)
    nt = s // tt
    halo = _halo(k)

    def body(*refs):
        if pool:
            xc_ref, xp_ref, o_ref, xs = refs
        else:
            xc_ref, xp_ref, w_ref, o_ref, xs = refs
        c, i = pl.program_id(0), pl.program_id(1)
        xs[0:halo, :] = jnp.where(i > 0, xp_ref[...], 0.0)
        xs[halo:halo + tt, :] = xc_ref[...]
        for r0 in range(0, tt, sub):
            acc = jnp.zeros((sub, LANES), F32)
            if pool:
                win, cnt = _pool_count(i * tt + r0, c, sub)
                for j in range(k):
                    acc = acc + jnp.where(j >= k - win, xs[pl.ds(r0 + halo - (k - 1) + j, sub), :], 0.0)
                o_ref[r0:r0 + sub, :] = acc / cnt - xc_ref[r0:r0 + sub, :]
            else:
                for j in range(k):
                    acc = acc + w_ref[j:j + 1, :] * xs[pl.ds(r0 + halo - (k - 1) + j, sub), :]
                o_ref[r0:r0 + sub, :] = acc

    per = tt // halo
    in_specs = [pl.BlockSpec((tt, LANES), lambda c, i: (i, xcb0 + c)),
                pl.BlockSpec((halo, LANES), lambda c, i: (jnp.maximum(i * per - 1, 0), xcb0 + c))]
    args = [x, x]
    if not pool:
        in_specs.append(pl.BlockSpec((k, LANES), lambda c, i: (0, c)))
        args.append(w)
    return pl.pallas_call(
        body, name=name, grid=(channels // LANES, nt), in_specs=in_specs,
        out_specs=pl.BlockSpec((tt, LANES), lambda c, i: (i, c)),
        out_shape=jax.ShapeDtypeStruct((s, channels), F32),
        scratch_shapes=[pltpu.VMEM((halo + tt, LANES), F32)],
        compiler_params=_params(("arbitrary", "arbitrary")),
    )(*args)


def _conv_bwd(x, xcb0, channels, w, k, dy, *, pool=False, dx_dtype=BF16, name):
    s = dy.shape[0]
    tt = min(CONV_TT, s)
    sub = min(CONV_SUB, tt)
    nt = s // tt
    halo = _halo(k)

    def body(*refs):
        if pool:
            dyc_ref, dyn_ref, dx_ref, ys = refs
        else:
            xc_ref, xp_ref, dyc_ref, dyn_ref, w_ref, dx_ref, dw_ref, xs, ys = refs
        c, i = pl.program_id(0), pl.program_id(1)
        dyn = jnp.where(i < nt - 1, dyn_ref[...], 0.0)
        if pool:
            win, cnt = _pool_count(i * tt, c, tt)
            ys[0:tt, :] = dyc_ref[...] / cnt
            ys[tt:tt + halo, :] = dyn / win.astype(F32)
            for r0 in range(0, tt, sub):
                acc = jnp.zeros((sub, LANES), F32)
                for j in range(k):
                    acc = acc + jnp.where(j >= k - win, ys[pl.ds(r0 + (k - 1) - j, sub), :], 0.0)
                dx_ref[r0:r0 + sub, :] = (acc - dyc_ref[r0:r0 + sub, :]).astype(dx_ref.dtype)
            return
        ys[0:tt, :] = dyc_ref[...]
        ys[tt:tt + halo, :] = dyn
        xs[0:halo, :] = jnp.where(i > 0, xp_ref[...], 0.0)
        xs[halo:halo + tt, :] = xc_ref[...]

        @pl.when(i == 0)
        def _():
            dw_ref[...] = jnp.zeros_like(dw_ref)

        for r0 in range(0, tt, sub):
            acc = jnp.zeros((sub, LANES), F32)
            for j in range(k):
                acc = acc + w_ref[j:j + 1, :] * ys[pl.ds(r0 + (k - 1) - j, sub), :]
            dx_ref[r0:r0 + sub, :] = acc.astype(dx_ref.dtype)
            dyc = dyc_ref[r0:r0 + sub, :]
            for j in range(k):
                dw_ref[j:j + 1, :] += jnp.sum(dyc * xs[pl.ds(r0 + halo - (k - 1) + j, sub), :], axis=0, keepdims=True)

    per = tt // halo
    cur = lambda cb0: pl.BlockSpec((tt, LANES), lambda c, i: (i, cb0 + c))
    dy_specs = [cur(0), pl.BlockSpec((halo, LANES), lambda c, i: (jnp.minimum((i + 1) * per, s // halo - 1), c))]
    dx_spec = pl.BlockSpec((tt, LANES), lambda c, i: (i, c))
    dx_shape = jax.ShapeDtypeStruct((s, channels), dx_dtype)
    if pool:
        return pl.pallas_call(
            body, name=name, grid=(channels // LANES, nt), in_specs=dy_specs, out_specs=dx_spec, out_shape=dx_shape,
            scratch_shapes=[pltpu.VMEM((tt + halo, LANES), F32)],
            compiler_params=_params(("arbitrary", "arbitrary")),
        )(dy, dy)
    in_specs = [cur(xcb0), pl.BlockSpec((halo, LANES), lambda c, i: (jnp.maximum(i * per - 1, 0), xcb0 + c))] + dy_specs
    in_specs.append(pl.BlockSpec((k, LANES), lambda c, i: (0, c)))
    return pl.pallas_call(
        body, name=name, grid=(channels // LANES, nt), in_specs=in_specs,
        out_specs=[dx_spec, pl.BlockSpec((k, LANES), lambda c, i: (0, c))],
        out_shape=[dx_shape, jax.ShapeDtypeStruct((k, channels), F32)],
        scratch_shapes=[pltpu.VMEM((halo + tt, LANES), F32), pltpu.VMEM((tt + halo, LANES), F32)],
        compiler_params=_params(("arbitrary", "arbitrary")),
    )(x, x, dy, dy, w)


DN_STEP_ROWS = 256


def _dn_step(q, k, v, bb, gb, state):
    c = DN_CHUNK
    r = q.shape[0]
    ri = lax.broadcasted_iota(jnp.int32, (r, r), 0)
    ci = lax.broadcasted_iota(jnp.int32, (r, r), 1)
    same = jnp.bitwise_or(ri, c - 1) == jnp.bitwise_or(ci, c - 1)
    causal = same & (ri >= ci)
    strict = same & (ri > ci)
    eye = jnp.where(ri == ci, 1.0, 0.0)
    gam = _hnn(jnp.where(causal, 1.0, 0.0), gb)
    gam_i = jnp.concatenate([gam] * (r // LANES), axis=1)
    gam_j = _hnn(jnp.ones((r, r), F32), eye * gam_i)
    decay = jnp.where(causal, jnp.exp(jnp.where(causal, gam_i - gam_j, 0.0)), 0.0)
    row = lax.broadcasted_iota(jnp.int32, gam.shape, 0)
    lasts = [jnp.sum(jnp.where(row == n * c + c - 1, gam, 0.0), axis=0, keepdims=True) for n in range(r // c)]
    g_last = jnp.zeros_like(gam)
    for n, gl in enumerate(lasts):
        g_last = jnp.where(jnp.bitwise_or(row, c - 1) == n * c + c - 1, gl, g_last)
    kb = k * bb
    b = -jnp.where(strict, _mnt(kb, k) * decay, 0.0)
    inv = eye + b
    bp = b
    for _ in range(5):
        bp = _mnn(bp, bp)
        inv = inv + _mnn(inv, bp)
    eg = jnp.exp(gam)
    u = _mnn(inv, v * bb)
    w = _mnn(inv, kb * eg)
    attn = _mnt(q, k) * decay
    q_dec = q * eg
    k_dec = k * jnp.exp(g_last - gam)
    v_news, o_inter = [], []
    for n in range(r // c):
        s = slice(n * c, (n + 1) * c)
        v_new = u[s] - _mnn(w[s], state)
        o_inter.append(_mnn(q_dec[s], state))
        state = state * jnp.exp(lasts[n]) + _mtn(k_dec[s], v_new)
        v_news.append(v_new)
    o = jnp.concatenate(o_inter, axis=0) + _mnn(attn, jnp.concatenate(v_news, axis=0))
    return o, state


def _dn_specs(steps, reverse):
    if reverse:
        tile = pl.BlockSpec((DN_STEP_ROWS, HEAD_DIM), lambda h, i: (steps - 1 - i, h))
        st = pl.BlockSpec((1, 1, HEAD_DIM, HEAD_DIM), lambda h, i: (steps - 1 - i, h, 0, 0))
    else:
        tile = pl.BlockSpec((DN_STEP_ROWS, HEAD_DIM), lambda h, i: (i, h))
        st = pl.BlockSpec((1, 1, HEAD_DIM, HEAD_DIM), lambda h, i: (i, h, 0, 0))
    return tile, st


def _dn_fwd(q, k, v, bb, gb, *, name):
    s = q.shape[0]
    steps = s // DN_STEP_ROWS

    def body(q_ref, k_ref, v_ref, b_ref, g_ref, o_ref, st_ref, state):
        @pl.when(pl.program_id(1) == 0)
        def _():
            state[...] = jnp.zeros_like(state)

        st = state[...]
        st_ref[0, 0] = st
        o, st = _dn_step(q_ref[...], k_ref[...], v_ref[...], b_ref[...], g_ref[...], st)
        o_ref[...] = o
        state[...] = st

    tile, stspec = _dn_specs(steps, False)
    return pl.pallas_call(
        body, name=name, grid=(HEADS, steps), in_specs=[tile] * 5, out_specs=[tile, stspec],
        out_shape=[jax.ShapeDtypeStruct((s, MIX_W), F32),
                   jax.ShapeDtypeStruct((steps, HEADS, HEAD_DIM, HEAD_DIM), F32)],
        scratch_shapes=[pltpu.VMEM((HEAD_DIM, HEAD_DIM), F32)],
        compiler_params=_params(("arbitrary", "arbitrary")),
    )(q, k, v, bb, gb)


def _dn_bwd(q, k, v, bb, gb, states, do, *, name):
    s = q.shape[0]
    steps = s // DN_STEP_ROWS

    def body(q_ref, k_ref, v_ref, b_ref, g_ref, st_ref, do_ref, dq_ref, dk_ref, dv_ref, db_ref, dg_ref, dstate):
        @pl.when(pl.program_id(1) == 0)
        def _():
            dstate[...] = jnp.zeros_like(dstate)

        _, vjp = jax.vjp(_dn_step, q_ref[...], k_ref[...], v_ref[...], b_ref[...], g_ref[...], st_ref[0, 0])
        dq, dk, dv, db, dg, dst = vjp((do_ref[...], dstate[...]))
        dq_ref[...] = dq
        dk_ref[...] = dk
        dv_ref[...] = dv
        db_ref[...] = db
        dg_ref[...] = dg
        dstate[...] = dst

    tile, stspec = _dn_specs(steps, True)
    return pl.pallas_call(
        body, name=name, grid=(HEADS, steps), in_specs=[tile] * 5 + [stspec, tile], out_specs=[tile] * 5,
        out_shape=[jax.ShapeDtypeStruct((s, MIX_W), F32)] * 5,
        scratch_shapes=[pltpu.VMEM((HEAD_DIM, HEAD_DIM), F32)],
        compiler_params=_params(("arbitrary", "arbitrary")),
    )(q, k, v, bb, gb, states, do)


EW_TILE_BYTES = 2 << 20


def _ew(f, ins, out_dtypes, *, name):
    r, c = ins[0][0].shape[1:] if isinstance(ins[0], tuple) else ins[0].shape
    row_bytes = -(-c // LANES) * LANES * 4
    tr = r
    if r * row_bytes > EW_TILE_BYTES:
        tr = next((t for t in (4096, 2048, 1024, 512, 256, 128, 64, 32, 16, 8)
                   if r % t == 0 and t * row_bytes <= EW_TILE_BYTES), r)
    n_in = len(ins)

    def body(*refs):
        res = f(*[x[...] for x in refs[:n_in]])
        for o_ref, v in zip(refs[n_in:], res):
            o_ref[...] = v.astype(o_ref.dtype)

    in_specs, args = [], []
    for x in ins:
        if isinstance(x, tuple):
            in_specs.append(pl.BlockSpec((None, tr, c), functools.partial(lambda i, j: (j, i, 0), j=x[1])))
            args.append(x[0])
        else:
            in_specs.append(pl.BlockSpec((tr, c), lambda i: (i, 0)))
            args.append(x)
    return pl.pallas_call(
        body, name=name, grid=(r // tr,), in_specs=in_specs,
        out_specs=[pl.BlockSpec((tr, c), lambda i: (i, 0)) for _ in out_dtypes],
        out_shape=[jax.ShapeDtypeStruct((r, c), dt) for dt in out_dtypes],
        compiler_params=_params(("arbitrary",)),
    )(*args)


def f_adamw(w, g, m, v):
    m = ADAM_B1 * m + (1.0 - ADAM_B1) * g
    v = ADAM_B2 * v + (1.0 - ADAM_B2) * (g * g)
    m_hat = m / (1.0 - ADAM_B1 ** ADAM_STEP)
    v_hat = v / (1.0 - ADAM_B2 ** ADAM_STEP)
    return -ADAM_LR * (m_hat / (jnp.sqrt(v_hat) + ADAM_EPS) + ADAM_WD * w), m, v


def f_sum4(a, b, c, d):
    return (((a.astype(F32) + b.astype(F32)) + c.astype(F32)) + d.astype(F32),)


def f_add2(a, b):
    return (a + b,)


def _adamw(w, g, m, v, *, name):
    shape = w.shape
    two = (1, shape[0]) if w.ndim == 1 else (-1, shape[-1])
    outs = _ew(f_adamw, [t.reshape(two) for t in (w, g, m, v)], [F32, F32, F32], name=name)
    return [o.reshape(shape) for o in outs]


_ANY = pl.BlockSpec(memory_space=pl.ANY)


def _xy_exchange(srcs, *, broadcast, name):
    nt = len(srcs)

    def body(*refs):
        src_refs, out_refs = refs[:nt], refs[nt:2 * nt]
        send_sems, recv_sems, local_sems = refs[2 * nt:]
        x, y, c = lax.axis_index("x"), lax.axis_index("y"), lax.axis_index("c")
        me = 2 * x + y
        peers = [(1 - x, y), (x, 1 - y), (1 - x, 1 - y)]

        def block(t, j):
            return src_refs[t] if broadcast else src_refs[t].at[j]

        def copy(t, n, px, py, src_blk, dst_blk):
            return pltpu.make_async_remote_copy(
                src_ref=src_blk, dst_ref=dst_blk, send_sem=send_sems.at[3 * t + n], recv_sem=recv_sems.at[3 * t + n],
                device_id=(px, py, c), device_id_type=pl.DeviceIdType.MESH)

        mine = [pltpu.make_async_copy(block(t, me), out_refs[t].at[me], local_sems.at[t]) for t in range(nt)]
        sends = [copy(t, n, px, py, block(t, 2 * px + py), out_refs[t].at[me])
                 for t in range(nt) for n, (px, py) in enumerate(peers)]
        for cp in mine + sends:
            cp.start()
        for cp in sends:
            cp.wait_send()
        for t in range(nt):
            for n, (px, py) in enumerate(peers):
                copy(t, n, px, py, block(t, me), out_refs[t].at[2 * px + py]).wait_recv()
        for cp in mine:
            cp.wait()

    shapes = [(4,) + tuple(s.shape) if broadcast else tuple(s.shape) for s in srcs]
    return pl.pallas_call(
        body, name=name, in_specs=[_ANY] * nt, out_specs=[_ANY] * nt,
        out_shape=[jax.ShapeDtypeStruct(shp, s.dtype) for shp, s in zip(shapes, srcs)],
        scratch_shapes=[pltpu.SemaphoreType.DMA((3 * nt,)), pltpu.SemaphoreType.DMA((3 * nt,)),
                        pltpu.SemaphoreType.DMA((nt,))],
        compiler_params=pltpu.CompilerParams(has_side_effects=True),
    )(*srcs)


def _c_swap(srcs, *, name):
    nt = len(srcs)

    def body(*refs):
        src_refs, out_refs = refs[:nt], refs[nt:2 * nt]
        send_sems, recv_sems = refs[2 * nt:]
        sibling = (lax.axis_index("x"), lax.axis_index("y"), 1 - lax.axis_index("c"))
        cps = [pltpu.make_async_remote_copy(src_ref=src_refs[t], dst_ref=out_refs[t], send_sem=send_sems.at[t],
                                            recv_sem=recv_sems.at[t], device_id=sibling,
                                            device_id_type=pl.DeviceIdType.MESH) for t in range(nt)]
        for cp in cps:
            cp.start()
        for cp in cps:
            cp.wait()

    return pl.pallas_call(
        body, name=name, in_specs=[_ANY] * nt, out_specs=[_ANY] * nt,
        out_shape=[jax.ShapeDtypeStruct(s.shape, s.dtype) for s in srcs],
        scratch_shapes=[pltpu.SemaphoreType.DMA((nt,)), pltpu.SemaphoreType.DMA((nt,))],
        compiler_params=pltpu.CompilerParams(has_side_effects=True),
    )(*srcs)


_HBM = pl.BlockSpec(memory_space=pltpu.HBM)
_SEM = pl.BlockSpec(memory_space=pltpu.SEMAPHORE)
_DATAFLOW = pltpu.SideEffectType.DATAFLOW_SIDE_EFFECTING


def _xy_peers():
    x, y, c = lax.axis_index("x"), lax.axis_index("y"), lax.axis_index("c")
    return 2 * x + y, c, [(1 - x, y), (x, 1 - y), (1 - x, 1 - y)]


def _xy_start(srcs, *, broadcast, name):
    nt = len(srcs)
    lands = [lax.empty((4,) + tuple(s.shape) if broadcast else tuple(s.shape), s.dtype) for s in srcs]

    def body(*refs):
        src_refs, land_refs = refs[:nt], refs[nt:2 * nt]
        send_sems, recv_sems = refs[2 * nt], refs[2 * nt + 1]
        token = refs[-1]
        me, c, peers = _xy_peers()
        for t in range(nt):
            for n, (px, py) in enumerate(peers):
                src_blk = src_refs[t] if broadcast else src_refs[t].at[2 * px + py]
                pltpu.make_async_remote_copy(
                    src_ref=src_blk, dst_ref=land_refs[t].at[me], send_sem=send_sems.at[3 * t + n],
                    recv_sem=recv_sems.at[3 * t + n], device_id=(px, py, c),
                    device_id_type=pl.DeviceIdType.MESH).start()
        token[...] = jnp.zeros_like(token)

    res = pl.pallas_call(
        body, name=name, in_specs=[_HBM] * (2 * nt),
        out_specs=[_SEM, _SEM] + [_HBM] * (2 * nt) + [pl.BlockSpec(memory_space=pltpu.VMEM)],
        out_shape=[pltpu.SemaphoreType.DMA((3 * nt,)), pltpu.SemaphoreType.DMA((3 * nt,))]
        + [pltpu.HBM(a.shape, a.dtype) for a in list(srcs) + lands] + [jax.ShapeDtypeStruct((8, LANES), F32)],
        input_output_aliases={i: 2 + i for i in range(2 * nt)},
        compiler_params=pltpu.CompilerParams(has_side_effects=_DATAFLOW),
    )(*[pltpu.with_memory_space_constraint(a, pltpu.HBM) for a in list(srcs) + lands])
    return res[0], res[1], res[2:2 + nt], res[2 + nt:2 + 2 * nt], res[-1]


def _xy_wait(started, after, *, broadcast, name):
    send_sems, recv_sems, srcs, lands, _ = started
    nt = len(srcs)

    def body(*refs):
        src_refs, land_refs = refs[:nt], refs[nt:2 * nt]
        send_sems, recv_sems = refs[2 * nt], refs[2 * nt + 1]
        me, c, peers = _xy_peers()
        for t in range(nt):
            for n, (px, py) in enumerate(peers):
                src_blk = src_refs[t] if broadcast else src_refs[t].at[me]
                cp = pltpu.make_async_remote_copy(
                    src_ref=src_blk, dst_ref=land_refs[t].at[2 * px + py], send_sem=send_sems.at[3 * t + n],
                    recv_sem=recv_sems.at[3 * t + n], device_id=(px, py, c), device_id_type=pl.DeviceIdType.MESH)
                cp.wait_send()
                cp.wait_recv()

    res = pl.pallas_call(
        body, name=name, in_specs=[_HBM] * (2 * nt) + [_SEM, _SEM, _ANY], out_specs=[_HBM] * (2 * nt),
        out_shape=[pltpu.HBM(a.shape, a.dtype) for a in list(srcs) + list(lands)],
        input_output_aliases={i: i for i in range(2 * nt)},
        compiler_params=pltpu.CompilerParams(has_side_effects=_DATAFLOW),
    )(*srcs, *lands, send_sems, recv_sems, after)
    return res[:nt], res[nt:]


def _all_sum(srcs, *, broadcast, name):
    got = _xy_exchange(srcs, broadcast=broadcast, name=name + "_xy")
    parts = []
    for t, gt in enumerate(got):
        g3 = gt.reshape(4, -1, gt.shape[-1])
        parts.append(_ew(f_sum4, [(g3, j) for j in range(4)], [F32], name=f"{name}_sum4_{t}")[0])
    others = _c_swap(parts, name=name + "_c")
    return [_ew(f_add2, [p, o], [F32], name=f"{name}_add2_{t}")[0].reshape(gt.shape[1:])
            for t, (p, o, gt) in enumerate(zip(parts, others, got))]


def _flat_rows(parts, dtype, row_multiple=8):
    flat = jnp.concatenate([p.reshape(-1).astype(dtype) for p in parts])
    pad = (-flat.shape[0]) % (row_multiple * LANES)
    if pad:
        flat = jnp.concatenate([flat, jnp.zeros((pad,), dtype)])
    return flat.reshape(-1, LANES)


def _unflat(buf, shapes):
    flat = buf.reshape(-1)
    out, off = [], 0
    for shp in shapes:
        n = 1
        for d in shp:
            n *= d
        out.append(flat[off:off + n].reshape(shp))
        off += n
    return out


def _regroup_w_in(w):
    cols = [w[:, a:b] for a, b in _Z_SEGS] + [jnp.zeros((w.shape[0], Z_W - IN_W), w.dtype)]
    return jnp.concatenate(cols, axis=1)


def _ungroup_w_in(wz):
    starts, off = {}, 0
    for a, b in _Z_SEGS:
        starts[a] = (off, off + b - a)
        off += b - a
    return jnp.concatenate([wz[:, starts[a][0]:starts[a][1]] for a in sorted(starts)], axis=1)


def _row(vec):
    return vec.reshape(1, -1)


def _lane_pad(vec):
    return jnp.zeros((1, LANES), F32).at[0, :vec.shape[0]].set(vec)


def _layer_fwd(x, mem, w, l):
    nm = lambda s: f"{s}_l{l}"
    sv = {"x0": x}
    h = _rw(f_norm, [(x, D_MODEL, 0)], [_row(w["norm_mix"])], [(D_MODEL, BF16)], name=nm("norm_mix"))[0]
    z = _mm(h, w["w_in_z"], name=nm("in_proj"))
    a_glu = _rw(f_glu, [(z, 2 * MIX_W, Z_A // (2 * MIX_W))], [], [(MIX_W, F32)], name=nm("glu"))[0]
    ac = _conv_fwd(a_glu, 0, MIX_W, w["conv_a_w"], CONV_K, name=nm("conv_a"))
    a_par = [_row(w["conv_a_b"]), _row(w["ln_a_g"]), _row(w["ln_a_b"])]
    a_out = _rw(f_lnsilu, [(ac, MIX_W, 0)], a_par, [(MIX_W, BF16)], name=nm("ln_a"))[0]
    qc = _conv_fwd(z, Z_QKV // LANES, 3 * MIX_W, w["dn_conv_w"], DN_CONV_K, name=nm("conv_dn"))
    dn_par = [_lane_pad(w["dn_a_log"]), _lane_pad(w["dn_dt_bias"])]
    qn, kn, v, bb, gb = _rw(f_dnprep, [(qc, 3 * MIX_W, 0), (z, LANES, Z_BD // LANES)], dn_par,
                            [(MIX_W, F32)] * 5, name=nm("dn_prep"))
    o, states = _dn_fwd(qn, kn, v, bb, gb, name=nm("dn_scan"))
    o_out = _rw(f_dnout, [(o, MIX_W, 0), (z, MIX_W, Z_DG // MIX_W)], [_row(w["dn_norm_g"])], [(MIX_W, BF16)],
                name=nm("dn_out"))[0]
    gm_par = [_row(w["gm_ln_g"]), _row(w["gm_ln_b"]), w["gm_ws"].reshape(4 * GM_CHUNK, GM_CHUNK), w["gm_bs"].T]
    c_out = _rw(f_gmlp, [(z, 2 * MIX_W, Z_GM // (2 * MIX_W))], gm_par, [(MIX_W, BF16)], name=nm("gmlp"))[0]
    pc = _conv_fwd(z, Z_POOL // LANES, MIX_W, None, POOL_K, pool=True, name=nm("pool"))
    p_par = [w["pool_w"].reshape(4 * LANES, LANES), _row(w["pool_scale"])]
    p_out = _rw(f_poolpost, [(pc, MIX_W, 0)], p_par, [(MIX_W, BF16)], name=nm("pool_post"))[0]
    branches = [a_out, o_out, c_out, p_out]
    proj = [_mm(br, w["w_branch"][n], name=nm(f"branch{n}")) for n, br in enumerate(branches)]
    merged = _rw(f_merge, [(z, N_BRANCH * D_MODEL, 0)] + [(p, D_MODEL, 0) for p in proj], [], [(D_MODEL, BF16)],
                 name=nm("merge"), tr=128)[0]
    x1 = _mm(merged, w["w_out"], add=x, name=nm("out_proj"))
    sv.update(h=h, z=z, a_glu=a_glu, ac=ac, qc=qc, qn=qn, kn=kn, v=v, bb=bb, gb=gb, o=o, states=states, pc=pc,
              branches=branches, proj=proj, merged=merged, x1=x1)
    h2 = _rw(f_norm, [(x1, D_MODEL, 0)], [_row(w["norm_xa"])], [(D_MODEL, BF16)], name=nm("norm_xa"))[0]
    mn = _rw(f_norm, [(mem, D_MODEL, 0)], [_row(w["norm_mem"])], [(D_MODEL, BF16)], name=nm("norm_mem"))[0]
    kv = _mm(mn, w["xa_wkv"], name=nm("xa_kv"))
    q = _mm(h2, w["xa_wq"], name=nm("xa_q"))
    att = _rw(f_attn, [(q, D_MODEL, 0)], [kv], [(D_MODEL, BF16)], name=nm("xa_attn"))[0]
    x2 = _mm(att, w["xa_wo"], add=x1, name=nm("xa_o"))
    sv.update(h2=h2, mn=mn, kv=kv, q=q, att=att, x2=x2)
    h3 = _rw(f_norm, [(x2, D_MODEL, 0)], [_row(w["norm_mlp"])], [(D_MODEL, BF16)], name=nm("norm_mlp"))[0]
    u = _mm(h3, w["mlp_w1"], name=nm("mlp_up"))
    act = _rw(f_relu2, [(u, FFN_W, 0)], [], [(FFN_W, BF16)], name=nm("relu2"))[0]
    x3 = _mm(act, w["mlp_w2"], add=x2, name=nm("mlp_down"))
    sv.update(h3=h3, u=u, act=act)
    return x3, sv


def _layer_bwd(dx, mem, w, sv, l):
    nm = lambda s: f"{s}_bwd_l{l}"
    s = dx.shape[0]
    g = {}
    dact = _mm(dx, w["mlp_w2"], tb=True, name=nm("mlp_down_dx"))
    g["mlp_w2"] = _mm(sv["act"], dx, ta=True, out_dtype=BF16, name=nm("mlp_down_dw"))
    du = _rw_bwd(f_relu2, [(sv["u"], FFN_W, 0)], [], [(dact, FFN_W, 0)], row_grads=[(0, BF16)], name=nm("relu2"))[0]
    g["mlp_w1"] = _mm(sv["h3"], du, ta=True, out_dtype=BF16, name=nm("mlp_up_dw"))
    dh3 = _mm(du, w["mlp_w1"], tb=True, name=nm("mlp_up_dx"))
    dx2, g["norm_mlp"] = _rw_bwd(f_norm, [(sv["x2"], D_MODEL, 0)], [_row(w["norm_mlp"])], [(dh3, D_MODEL, 0)],
                                 row_grads=[(0, F32)], param_grads=[0], add={0: (dx, D_MODEL, 0)}, name=nm("norm_mlp"))
    datt = _mm(dx2, w["xa_wo"], tb=True, name=nm("xa_o_dx"))
    g["xa_wo"] = _mm(sv["att"], dx2, ta=True, out_dtype=BF16, name=nm("xa_o_dw"))
    dq, dkv = _rw_bwd(f_attn, [(sv["q"], D_MODEL, 0)], [sv["kv"]], [(datt, D_MODEL, 0)], row_grads=[(0, BF16)],
                      param_grads=[0], name=nm("xa_attn"))
    g["xa_wq"] = _mm(sv["h2"], dq, ta=True, out_dtype=BF16, name=nm("xa_q_dw"))
    dh2 = _mm(dq, w["xa_wq"], tb=True, name=nm("xa_q_dx"))
    g["xa_wkv"] = _mm(sv["mn"], dkv, ta=True, out_dtype=BF16, name=nm("xa_kv_dw"))
    dmn = _mm(dkv, w["xa_wkv"], tb=True, name=nm("xa_kv_dx"))
    g["norm_mem"] = _rw_bwd(f_norm, [(mem, D_MODEL, 0)], [_row(w["norm_mem"])], [(dmn, D_MODEL, 0)], row_grads=[],
                            param_grads=[0], name=nm("norm_mem"))[0]
    dx1, g["norm_xa"] = _rw_bwd(f_norm, [(sv["x1"], D_MODEL, 0)], [_row(w["norm_xa"])], [(dh2, D_MODEL, 0)],
                                row_grads=[(0, F32)], param_grads=[0], add={0: (dx2, D_MODEL, 0)}, name=nm("norm_xa"))
    z = sv["z"]
    dmerged = _mm(dx1, w["w_out"], tb=True, name=nm("out_proj_dx"))
    g["w_out"] = _mm(sv["merged"], dx1, ta=True, out_dtype=BF16, name=nm("out_proj_dw"))
    mg = _rw_bwd(f_merge, [(z, N_BRANCH * D_MODEL, 0)] + [(p, D_MODEL, 0) for p in sv["proj"]], [],
                 [(dmerged, D_MODEL, 0)], row_grads=[(i, BF16) for i in range(5)], name=nm("merge"), tr=128)
    dgate, dproj = mg[0], mg[1:]
    g["w_branch"] = jnp.stack([_mm(br, dp, ta=True, out_dtype=BF16, name=nm(f"branch{n}_dw"))
                               for n, (br, dp) in enumerate(zip(sv["branches"], dproj))])
    dbr = [_mm(dp, w["w_branch"][n], tb=True, name=nm(f"branch{n}_dx")) for n, dp in enumerate(dproj)]
    p_par = [w["pool_w"].reshape(4 * LANES, LANES), _row(w["pool_scale"])]
    dpc, dpw, g["pool_scale"] = _rw_bwd(f_poolpost, [(sv["pc"], MIX_W, 0)], p_par, [(dbr[3], MIX_W, 0)],
                                        row_grads=[(0, F32)], param_grads=[0, 1], name=nm("pool_post"))
    g["pool_w"] = dpw.reshape(4, LANES, LANES)
    dpool = _conv_bwd(None, 0, MIX_W, None, POOL_K, dpc, pool=True, name=nm("pool"))
    gm_par = [_row(w["gm_ln_g"]), _row(w["gm_ln_b"]), w["gm_ws"].reshape(4 * GM_CHUNK, GM_CHUNK), w["gm_bs"].T]
    dgm, g["gm_ln_g"], g["gm_ln_b"], dws, dbst = _rw_bwd(
        f_gmlp, [(z, 2 * MIX_W, Z_GM // (2 * MIX_W))], gm_par, [(dbr[2], MIX_W, 0)], row_grads=[(0, BF16)],
        param_grads=[0, 1, 2, 3], name=nm("gmlp"))
    g["gm_ws"] = dws.reshape(4, GM_CHUNK, GM_CHUNK)
    g["gm_bs"] = dbst.T
    do, ddg, g["dn_norm_g"] = _rw_bwd(f_dnout, [(sv["o"], MIX_W, 0), (z, MIX_W, Z_DG // MIX_W)], [_row(w["dn_norm_g"])],
                                      [(dbr[1], MIX_W, 0)], row_grads=[(0, F32), (1, BF16)], param_grads=[0],
                                      name=nm("dn_out"))
    dqn, dkn, dv, dbb, dgb = _dn_bwd(sv["qn"], sv["kn"], sv["v"], sv["bb"], sv["gb"], sv["states"], do, name=nm("dn_scan"))
    dn_par = [_lane_pad(w["dn_a_log"]), _lane_pad(w["dn_dt_bias"])]
    dqc, dbd, dal, ddt = _rw_bwd(
        f_dnprep, [(sv["qc"], 3 * MIX_W, 0), (z, LANES, Z_BD // LANES)], dn_par,
        [(t, MIX_W, 0) for t in (dqn, dkn, dv, dbb, dgb)], row_grads=[(0, F32), (1, BF16)], param_grads=[0, 1],
        name=nm("dn_prep"))
    g["dn_a_log"], g["dn_dt_bias"] = dal[0, :HEADS], ddt[0, :HEADS]
    dqkv, g["dn_conv_w"] = _conv_bwd(z, Z_QKV // LANES, 3 * MIX_W, w["dn_conv_w"], DN_CONV_K, dqc, name=nm("conv_dn"))
    a_par = [_row(w["conv_a_b"]), _row(w["ln_a_g"]), _row(w["ln_a_b"])]
    dac, g["conv_a_b"], g["ln_a_g"], g["ln_a_b"] = _rw_bwd(
        f_lnsilu, [(sv["ac"], MIX_W, 0)], a_par, [(dbr[0], MIX_W, 0)], row_grads=[(0, F32)], param_grads=[0, 1, 2],
        name=nm("ln_a"))
    dglu, g["conv_a_w"] = _conv_bwd(sv["a_glu"], 0, MIX_W, w["conv_a_w"], CONV_K, dac, dx_dtype=F32, name=nm("conv_a"))
    da_in = _rw_bwd(f_glu, [(z, 2 * MIX_W, Z_A // (2 * MIX_W))], [], [(dglu, MIX_W, 0)], row_grads=[(0, BF16)],
                    name=nm("glu"))[0]
    dz = jnp.concatenate([dgate, da_in, dgm, dqkv, ddg, dpool, dbd, jnp.zeros((s, Z_W - Z_BD - LANES), BF16)], axis=1)
    g["w_in"] = _ungroup_w_in(_mm(sv["h"], dz, ta=True, out_dtype=BF16, name=nm("in_proj_dw")))
    dh = _mm(dz, w["w_in_z"], tb=True, name=nm("in_proj_dx"))
    dx0, g["norm_mix"] = _rw_bwd(f_norm, [(sv["x0"], D_MODEL, 0)], [_row(w["norm_mix"])], [(dh, D_MODEL, 0)],
                                 row_grads=[(0, F32)], param_grads=[0], add={0: (dx1, D_MODEL, 0)}, name=nm("norm_mix"))
    for n in ("norm_mlp", "norm_xa", "norm_mem", "norm_mix", "pool_scale", "gm_ln_g", "gm_ln_b", "dn_norm_g",
              "conv_a_b", "ln_a_g", "ln_a_b"):
        g[n] = g[n].reshape(-1)
    return dx0, g


def _shard_slice(a, axis, j):
    n = a.shape[axis] // 4
    return lax.slice_in_dim(a, j * n, (j + 1) * n, axis=axis)


def kernel(x, mem, norm_mix, w_in, conv_a_w, conv_a_b, ln_a_g, ln_a_b, dn_conv_w, dn_a_log, dn_dt_bias, dn_norm_g, gm_ln_g, gm_ln_b, gm_ws, gm_bs, pool_w, pool_scale, w_branch, w_out, norm_xa, norm_mem, xa_wq, xa_wkv, xa_wo, norm_mlp, mlp_w1, mlp_w2, norm_f, loss_target, m_norm_mix, m_w_in, m_conv_a_w, m_conv_a_b, m_ln_a_g, m_ln_a_b, m_dn_conv_w, m_dn_a_log, m_dn_dt_bias, m_dn_norm_g, m_gm_ln_g, m_gm_ln_b, m_gm_ws, m_gm_bs, m_pool_w, m_pool_scale, m_w_branch, m_w_out, m_norm_xa, m_norm_mem, m_xa_wq, m_xa_wkv, m_xa_wo, m_norm_mlp, m_mlp_w1, m_mlp_w2, m_norm_f, v_norm_mix, v_w_in, v_conv_a_w, v_conv_a_b, v_ln_a_g, v_ln_a_b, v_dn_conv_w, v_dn_a_log, v_dn_dt_bias, v_dn_norm_g, v_gm_ln_g, v_gm_ln_b, v_gm_ws, v_gm_bs, v_pool_w, v_pool_scale, v_w_branch, v_w_out, v_norm_xa, v_norm_mem, v_xa_wq, v_xa_wkv, v_xa_wo, v_norm_mlp, v_mlp_w1, v_mlp_w2, v_norm_f):
    given = dict(locals())
    wts = {n: given[n] for n in WEIGHTS}
    depth = norm_mix.shape[0]
    x = x[0]
    mem = mem[0]
    tgt = loss_target[0]

    me = 2 * lax.axis_index("x") + lax.axis_index("y")
    gathers = [_xy_start([wts[n][l].astype(BF16) for n in BIG], broadcast=True, name=f"gather_start_l{l}")
               for l in range(depth)]
    started = gathers[0][4][0, 0]
    for gs in gathers[1:]:
        started = started + gs[4][0, 0]
    conv_shapes = [wts[n].shape for n in CONVS]
    conv_gathered = _xy_exchange([_flat_rows([wts[n] for n in CONVS], F32)], broadcast=True, name="gather_filters")[0]
    conv_parts = [_unflat(conv_gathered[j], conv_shapes) for j in range(4)]
    conv_full = {n: jnp.concatenate([conv_parts[j][i] for j in range(4)], axis=2) for i, n in enumerate(CONVS)}

    def layer_weights(l, after):
        mine, landed = _xy_wait(gathers[l], after, broadcast=True, name=f"gather_wait_l{l}")
        w = {}
        for n, own, got in zip(BIG, mine, landed):
            got = lax.dynamic_update_index_in_dim(got, own[None], me, 0)
            w[n] = jnp.concatenate([got[j] for j in range(4)], axis=BIG_AXIS[n])
        w["w_in_z"] = _regroup_w_in(w.pop("w_in"))
        for n in SMALL:
            if n != "norm_f":
                w[n] = conv_full[n][l] if n in CONVS else wts[n][l]
        return w

    saved, layer_w = [], []
    x = x + started
    for l in range(depth):
        w = layer_weights(l, x)
        x, sv = _layer_fwd(x, mem, w, l)
        saved.append(sv)
        layer_w.append(w)
    dx, g_norm_f, loss_rows = _rw_bwd(
        f_loss, [(x, D_MODEL, 0), (tgt, D_MODEL, 0)], [_row(norm_f)], [(jnp.ones((x.shape[0], 1), F32), 1, 0)],
        row_grads=[(0, F32)], param_grads=[0], primal=[(0, 1, F32)], name="loss_head")
    loss = lax.psum(jnp.sum(loss_rows), ("x", "y", "c"))

    grads, reduces = [None] * depth, [None] * depth
    for l in reversed(range(depth)):
        dx, grads[l] = _layer_bwd(dx, mem, layer_w[l], saved[l], l)
        saved[l] = None
        blocks = [jnp.stack([_shard_slice(grads[l][n], BIG_AXIS[n], j) for j in range(4)]).astype(BF16) for n in BIG]
        reduces[l] = _xy_start(blocks, broadcast=False, name=f"reduce_start_l{l}")
        dx = dx + reduces[l][4][0, 0]
    grad_x = dx[None]

    parts = []
    for l in reversed(range(depth)):
        mine, landed = _xy_wait(reduces[l], dx, broadcast=False, name=f"reduce_wait_l{l}")
        for t, (own, got) in enumerate(zip(mine, landed)):
            got = lax.dynamic_update_index_in_dim(got, lax.dynamic_index_in_dim(own, me, 0), me, 0)
            g3 = got.reshape(4, -1, got.shape[-1])
            parts.append(_ew(f_sum4, [(g3, j) for j in range(4)], [F32], name=f"reduce_sum4_l{l}_{t}")[0])
    others = _c_swap(parts, name="reduce_big_c")
    sums = [_ew(f_add2, [p, o], [F32], name=f"reduce_add2_{i}")[0] for i, (p, o) in enumerate(zip(parts, others))]
    gw = {}
    for t, n in enumerate(BIG):
        per_layer = {l: sums[i * len(BIG) + t] for i, l in enumerate(reversed(range(depth)))}
        gw[n] = jnp.stack([per_layer[l].reshape(wts[n].shape[1:]) for l in range(depth)])

    small_names = [n for n in SMALL if n != "norm_f"]
    small_shapes = [(depth,) + (conv_full[n].shape[1:] if n in CONVS else wts[n].shape[1:]) for n in small_names]
    small_buf = _flat_rows([jnp.stack([grads[l][n] for l in range(depth)]) for n in small_names] + [g_norm_f], F32,
                           row_multiple=1024)
    small_sum = _all_sum([small_buf], broadcast=True, name="reduce_small")[0]
    small_parts = _unflat(small_sum, small_shapes + [norm_f.shape])
    me = 2 * lax.axis_index("x") + lax.axis_index("y")
    for n, t in zip(small_names + ["norm_f"], small_parts):
        if n in CONVS:
            width = wts[n].shape[2]
            t = lax.dynamic_slice_in_dim(t, me * width, width, axis=2)
        gw[n] = t

    deltas, new_m, new_v = {}, {}, {}
    for n in WEIGHTS:
        deltas[n], new_m[n], new_v[n] = _adamw(wts[n], gw[n], given["m_" + n], given["v_" + n], name=f"adamw_{n}")
    return (loss, grad_x, *[gw[n] for n in WEIGHTS], *[deltas[n] for n in WEIGHTS], *[new_m[n] for n in WEIGHTS],
            *[new_v[n] for n in WEIGHTS])
```

```python
import functools

import jax
import jax.numpy as jnp
from jax import lax
from jax.experimental import pallas as pl
from jax.experimental.pallas import tpu as pltpu

F32 = jnp.float32
BF16 = jnp.bfloat16
HI = lax.Precision.HIGHEST

D_MODEL = 1024
MIX_W = 512
N_BRANCH = 4
HEADS = 4
HEAD_DIM = 128
CONV_K = 31
DN_CONV_K = 4
DN_CHUNK = 64
GM_CHUNK = 128
POOL_K = 16
XA_HEADS = 4
XA_HEAD_DIM = 256
FFN_W = 4096
IN_W = 8712
Z_W = 9216
LANES = 128
VMEM_LIMIT = 56 * 1024 * 1024

ADAM_LR, ADAM_B1, ADAM_B2, ADAM_EPS, ADAM_WD, ADAM_STEP = 0.001, 0.9, 0.999, 1e-08, 0.01, 10

Z_GATE, Z_A, Z_GM, Z_QKV, Z_DG, Z_POOL, Z_BD = 0, 4096, 5120, 6144, 7680, 8192, 8704
_Z_SEGS = ((4616, 8712), (0, 1024), (3080, 4104), (1024, 2560), (2560, 3072), (4104, 4616), (3072, 3080))

BIG = ("w_in", "w_branch", "w_out", "xa_wq", "xa_wkv", "xa_wo", "mlp_w1", "mlp_w2")
BIG_AXIS = {"w_in": 1, "w_branch": 2, "w_out": 0, "xa_wq": 0, "xa_wkv": 1, "xa_wo": 0, "mlp_w1": 1, "mlp_w2": 0}
CONVS = ("conv_a_w", "dn_conv_w")
WEIGHTS = ("norm_mix", "w_in", "conv_a_w", "conv_a_b", "ln_a_g", "ln_a_b", "dn_conv_w", "dn_a_log", "dn_dt_bias",
           "dn_norm_g", "gm_ln_g", "gm_ln_b", "gm_ws", "gm_bs", "pool_w", "pool_scale", "w_branch", "w_out",
           "norm_xa", "norm_mem", "xa_wq", "xa_wkv", "xa_wo", "norm_mlp", "mlp_w1", "mlp_w2", "norm_f")
SMALL = tuple(n for n in WEIGHTS if n not in BIG)


def _params(sem=None):
    return pltpu.CompilerParams(vmem_limit_bytes=VMEM_LIMIT, dimension_semantics=sem)


def _pick(n, cands):
    for c in cands:
        if n % c == 0:
            return c
    return n


def _make_dots(prec):
    def raw(a, b, ca, cb):
        if prec is None:
            a, b = a.astype(BF16), b.astype(BF16)
        return lax.dot_general(a, b, (((ca,), (cb,)), ((), ())), precision=prec, preferred_element_type=F32)

    nn = jax.custom_vjp(lambda a, b: raw(a, b, 1, 0))
    nt = jax.custom_vjp(lambda a, b: raw(a, b, 1, 1))
    tn = jax.custom_vjp(lambda a, b: raw(a, b, 0, 0))
    nn.defvjp(lambda a, b: (raw(a, b, 1, 0), (a, b)), lambda r, g: (raw(g, r[1], 1, 1), raw(r[0], g, 0, 0)))
    nt.defvjp(lambda a, b: (raw(a, b, 1, 1), (a, b)), lambda r, g: (raw(g, r[1], 1, 0), raw(g, r[0], 0, 0)))
    tn.defvjp(lambda a, b: (raw(a, b, 0, 0), (a, b)), lambda r, g: (raw(r[1], g, 1, 1), raw(r[0], g, 1, 0)))
    return nn, nt, tn


_bnn, _bnt, _btn = _make_dots(None)
_hnn, _hnt, _htn = _make_dots(HI)
_mnn, _mnt, _mtn = _make_dots(lax.Precision.HIGH)


def _col(x, j):
    lane = lax.broadcasted_iota(jnp.int32, x.shape, 1)
    return jnp.sum(jnp.where(lane == j, x, 0.0), axis=-1, keepdims=True)


def _rms(x, g, eps=1e-6):
    return x * lax.rsqrt(jnp.mean(x * x, axis=-1, keepdims=True) + eps) * g


def _ln(x, g, b, eps=1e-5):
    xc = x - jnp.mean(x, axis=-1, keepdims=True)
    var = jnp.mean(xc * xc, axis=-1, keepdims=True)
    return xc * lax.rsqrt(var + eps) * g + b


_sigmoid = jax.nn.sigmoid


def _silu(x):
    return x * _sigmoid(x)


def _softplus(x):
    return jnp.maximum(x, 0.0) + jnp.log1p(jnp.exp(-jnp.abs(x)))


def _gelu(x):
    return 0.5 * x * (1.0 + lax.erf(x * 0.7071067811865476))


def f_norm(x, g):
    return (_rms(x, g),)


def f_glu(a_in):
    return (a_in[:, :MIX_W] * _sigmoid(a_in[:, MIX_W:]),)


def f_lnsilu(ac, cb, g, b):
    return (_silu(_ln(ac + cb, g, b)),)


def f_dnprep(qc, bd, a_log, dt_bias):
    t = qc.shape[0]
    qkv = _silu(qc)
    qs, ks, bs, gs = [], [], [], []
    for h in range(HEADS):
        q = qkv[:, h * HEAD_DIM:(h + 1) * HEAD_DIM]
        k = qkv[:, MIX_W + h * HEAD_DIM:MIX_W + (h + 1) * HEAD_DIM]
        qs.append(q * lax.rsqrt(jnp.sum(q * q, axis=-1, keepdims=True) + 1e-6) * (HEAD_DIM ** -0.5))
        ks.append(k * lax.rsqrt(jnp.sum(k * k, axis=-1, keepdims=True) + 1e-6))
        beta = _sigmoid(_col(bd, h))
        g = -jnp.exp(_col(a_log, h)) * _softplus(_col(bd, HEADS + h) + _col(dt_bias, h))
        bs.append(jnp.broadcast_to(beta, (t, HEAD_DIM)))
        gs.append(jnp.broadcast_to(g, (t, HEAD_DIM)))
    cat = lambda xs: jnp.concatenate(xs, axis=1)
    return cat(qs), cat(ks), qkv[:, 2 * MIX_W:], cat(bs), cat(gs)


def f_dnout(o, dgate, g):
    outs = []
    for h in range(HEADS):
        sl = slice(h * HEAD_DIM, (h + 1) * HEAD_DIM)
        outs.append(_rms(o[:, sl], g) * _silu(dgate[:, sl]))
    return (jnp.concatenate(outs, axis=1),)


def f_gmlp(gm_in, lg, lb, ws, bst):
    t = gm_in.shape[0]
    ge = _gelu(gm_in)
    u, vg = ge[:, :MIX_W], _ln(ge[:, MIX_W:], lg, lb)
    ri = lax.broadcasted_iota(jnp.int32, (GM_CHUNK, GM_CHUNK), 0)
    ci = lax.broadcasted_iota(jnp.int32, (GM_CHUNK, GM_CHUNK), 1)
    chunks = []
    for n in range(t // GM_CHUNK):
        vc = vg[n * GM_CHUNK:(n + 1) * GM_CHUNK]
        cols = []
        for g in range(4):
            w = jnp.where(ri >= ci, ws[g * GM_CHUNK:(g + 1) * GM_CHUNK], 0.0)
            cols.append(_bnn(w, vc[:, g * LANES:(g + 1) * LANES]) + _col(bst, g))
        chunks.append(jnp.concatenate(cols, axis=1))
    mixed = chunks[0] if len(chunks) == 1 else jnp.concatenate(chunks, axis=0)
    return (u * mixed,)


def f_poolpost(pc, pw, ps):
    cols = [_bnn(pc[:, g * LANES:(g + 1) * LANES], pw[g * LANES:(g + 1) * LANES]) for g in range(4)]
    return (jnp.concatenate(cols, axis=1) * ps,)


def f_merge(gate, p0, p1, p2, p3):
    m = None
    for n, p in enumerate((p0, p1, p2, p3)):
        t = _sigmoid(gate[:, n * D_MODEL:(n + 1) * D_MODEL]) * p
        m = t if m is None else m + t
    return (m,)


def f_attn(q, kv):
    outs = []
    for h in range(XA_HEADS):
        sl = slice(h * XA_HEAD_DIM, (h + 1) * XA_HEAD_DIM)
        s = _bnt(q[:, sl], kv[:, sl]) * (XA_HEAD_DIM ** -0.5)
        s = s - lax.stop_gradient(jnp.max(s, axis=-1, keepdims=True))
        p = jnp.exp(s)
        p = p / jnp.sum(p, axis=-1, keepdims=True)
        outs.append(_bnn(p, kv[:, D_MODEL + h * XA_HEAD_DIM:D_MODEL + (h + 1) * XA_HEAD_DIM]))
    return (jnp.concatenate(outs, axis=1),)


def f_relu2(u):
    r = jnp.maximum(u, 0.0)
    return (r * r,)


def f_loss(x, t, g):
    e = _rms(x, g) - t
    return (0.5 * jnp.mean(e * e, axis=-1, keepdims=True),)


def _row_spec(tr, width, cb):
    return pl.BlockSpec((tr, width), functools.partial(lambda i, cb: (i, cb), cb=cb))


def _full_spec(shape):
    return pl.BlockSpec(shape, lambda i: (0,) * len(shape))


def _rw(f, rows, params, outs, *, name, tr=256):
    s = rows[0][0].shape[0]
    tr = min(tr, s)
    nr, npar = len(rows), len(params)

    def body(*refs):
        rv = [r[...].astype(F32) for r in refs[:nr]]
        pv = [p[...] for p in refs[nr:nr + npar]]
        res = f(*rv, *pv)
        for o_ref, r in zip(refs[nr + npar:], res):
            o_ref[...] = r.astype(o_ref.dtype)

    return pl.pallas_call(
        body, name=name, grid=(s // tr,),
        in_specs=[_row_spec(tr, w, cb) for _, w, cb in rows] + [_full_spec(p.shape) for p in params],
        out_specs=[_row_spec(tr, w, 0) for w, _ in outs],
        out_shape=[jax.ShapeDtypeStruct((s, w), dt) for w, dt in outs],
        compiler_params=_params(("arbitrary",)),
    )(*[a for a, _, _ in rows], *params)


def _rw_bwd(f, rows, params, cts, *, row_grads, param_grads=(), add=None, primal=(), name, tr=256):
    s = rows[0][0].shape[0]
    tr = min(tr, s)
    nr, npar, nct = len(rows), len(params), len(cts)
    add = add or {}
    add_keys = list(add)

    def body(*refs):
        it = iter(refs)
        row_refs = [next(it) for _ in range(nr)]
        par_refs = [next(it) for _ in range(npar)]
        ct_refs = [next(it) for _ in range(nct)]
        add_refs = {k: next(it) for k in add_keys}
        rg_refs = [next(it) for _ in row_grads]
        pg_refs = [next(it) for _ in param_grads]
        pr_refs = [next(it) for _ in primal]
        rv = [r[...].astype(F32) for r in row_refs]
        pv = [p[...] for p in par_refs]
        out, vjp = jax.vjp(f, *rv, *pv)
        grads = vjp(tuple(c[...].astype(F32) for c in ct_refs))
        for (idx, _), ref in zip(row_grads, rg_refs):
            g = grads[idx]
            if idx in add_refs:
                g = g + add_refs[idx][...].astype(F32)
            ref[...] = g.astype(ref.dtype)

        @pl.when(pl.program_id(0) == 0)
        def _():
            for ref in pg_refs:
                ref[...] = jnp.zeros_like(ref)

        for idx, ref in zip(param_grads, pg_refs):
            ref[...] += grads[nr + idx]
        for (idx, _, _), ref in zip(primal, pr_refs):
            ref[...] = out[idx].astype(ref.dtype)

    in_arrays = [a for a, _, _ in rows] + list(params) + [a for a, _, _ in cts] + [add[k][0] for k in add_keys]
    in_specs = ([_row_spec(tr, w, cb) for _, w, cb in rows] + [_full_spec(p.shape) for p in params]
                + [_row_spec(tr, w, cb) for _, w, cb in cts] + [_row_spec(tr, add[k][1], add[k][2]) for k in add_keys])
    out_specs = ([_row_spec(tr, rows[idx][1], 0) for idx, _ in row_grads]
                 + [_full_spec(params[idx].shape) for idx in param_grads]
                 + [_row_spec(tr, w, 0) for _, w, _ in primal])
    out_shape = ([jax.ShapeDtypeStruct((s, rows[idx][1]), dt) for idx, dt in row_grads]
                 + [jax.ShapeDtypeStruct(params[idx].shape, F32) for idx in param_grads]
                 + [jax.ShapeDtypeStruct((s, w), dt) for _, w, dt in primal])
    return pl.pallas_call(
        body, name=name, grid=(s // tr,), in_specs=in_specs, out_specs=out_specs, out_shape=out_shape,
        compiler_params=_params(("arbitrary",)),
    )(*in_arrays)


def _mm(a, b, *, ta=False, tb=False, out_dtype=F32, add=None, name):
    m, k = (a.shape[1], a.shape[0]) if ta else a.shape
    n = b.shape[0] if tb else b.shape[1]
    tm = _pick(m, (1024, 512, 256, 128))
    tn = _pick(n, (1024, 512, 256, 128))
    tk = _pick(k, (1024, 512, 256, 128))
    nk = k // tk
    dims = (((0 if ta else 1,), (1 if tb else 0,)), ((), ()))

    def body(*refs):
        a_ref, b_ref = refs[:2]
        add_ref = refs[2] if add is not None else None
        o_ref = refs[-2] if nk > 1 else refs[-1]
        part = lax.dot_general(a_ref[...].astype(BF16), b_ref[...].astype(BF16), dims, preferred_element_type=F32)

        def finish(r):
            if add is not None:
                r = r + add_ref[...]
            o_ref[...] = r.astype(o_ref.dtype)

        if nk == 1:
            finish(part)
            return
        acc = refs[-1]
        kk = pl.program_id(2)

        @pl.when(kk == 0)
        def _():
            acc[...] = part

        @pl.when((kk > 0) & (kk < nk - 1))
        def _():
            acc[...] += part

        @pl.when(kk == nk - 1)
        def _():
            finish(acc[...] + part)

    a_spec = (pl.BlockSpec((tk, tm), lambda i, j, kk: (kk, i)) if ta else pl.BlockSpec((tm, tk), lambda i, j, kk: (i, kk)))
    b_spec = (pl.BlockSpec((tn, tk), lambda i, j, kk: (j, kk)) if tb else pl.BlockSpec((tk, tn), lambda i, j, kk: (kk, j)))
    o_spec = pl.BlockSpec((tm, tn), lambda i, j, kk: (i, j))
    in_specs = [a_spec, b_spec] + ([o_spec] if add is not None else [])
    args = (a, b) + ((add,) if add is not None else ())
    return pl.pallas_call(
        body, name=name, grid=(m // tm, n // tn, nk), in_specs=in_specs, out_specs=o_spec,
        out_shape=jax.ShapeDtypeStruct((m, n), out_dtype),
        scratch_shapes=[pltpu.VMEM((tm, tn), F32)] if nk > 1 else [],
        compiler_params=_params(("arbitrary", "arbitrary", "arbitrary")),
    )(*args)


CONV_TT = 1024
CONV_SUB = 256


def _halo(k):
    return -(-(k - 1) // 8) * 8


def _pool_count(row0, c, rows):
    win = lax.shift_left(jnp.int32(2), c)
    t = row0 + lax.broadcasted_iota(jnp.int32, (rows, LANES), 0)
    return win, jnp.minimum(t + 1, win).astype(F32)


def _conv_fwd(x, xcb0, channels, w, k, *, pool=False, name):
    s = x.shape[0]
    tt = min(CONV_TT, s)
    sub = min(CONV_SUB, tt)
    nt = s // tt
    halo = _halo(k)

    def body(*refs):
        if pool:
            xc_ref, xp_ref, o_ref, xs = refs
        else:
            xc_ref, xp_ref, w_ref, o_ref, xs = refs
        c, i = pl.program_id(0), pl.program_id(1)
        xs[0:halo, :] = jnp.where(i > 0, xp_ref[...], 0.0)
        xs[halo:halo + tt, :] = xc_ref[...]
        for r0 in range(0, tt, sub):
            acc = jnp.zeros((sub, LANES), F32)
            if pool:
                win, cnt = _pool_count(i * tt + r0, c, sub)
                for j in range(k):
                    acc = acc + jnp.where(j >= k - win, xs[pl.ds(r0 + halo - (k - 1) + j, sub), :], 0.0)
                o_ref[r0:r0 + sub, :] = acc / cnt - xc_ref[r0:r0 + sub, :]
            else:
                for j in range(k):
                    acc = acc + w_ref[j:j + 1, :] * xs[pl.ds(r0 + halo - (k - 1) + j, sub), :]
                o_ref[r0:r0 + sub, :] = acc

    per = tt // halo
    in_specs = [pl.BlockSpec((tt, LANES), lambda c, i: (i, xcb0 + c)),
                pl.BlockSpec((halo, LANES), lambda c, i: (jnp.maximum(i * per - 1, 0), xcb0 + c))]
    args = [x, x]
    if not pool:
        in_specs.append(pl.BlockSpec((k, LANES), lambda c, i: (0, c)))
        args.append(w)
    return pl.pallas_call(
        body, name=name, grid=(channels // LANES, nt), in_specs=in_specs,
        out_specs=pl.BlockSpec((tt, LANES), lambda c, i: (i, c)),
        out_shape=jax.ShapeDtypeStruct((s, channels), F32),
        scratch_shapes=[pltpu.VMEM((halo + tt, LANES), F32)],
        compiler_params=_params(("arbitrary", "arbitrary")),
    )(*args)


def _conv_bwd(x, xcb0, channels, w, k, dy, *, pool=False, dx_dtype=BF16, name):
    s = dy.shape[0]
    tt = min(CONV_TT, s)
    sub = min(CONV_SUB, tt)
    nt = s // tt
    halo = _halo(k)

    def body(*refs):
        if pool:
            dyc_ref, dyn_ref, dx_ref, ys = refs
        else:
            xc_ref, xp_ref, dyc_ref, dyn_ref, w_ref, dx_ref, dw_ref, xs, ys = refs
        c, i = pl.program_id(0), pl.program_id(1)
        dyn = jnp.where(i < nt - 1, dyn_ref[...], 0.0)
        if pool:
            win, cnt = _pool_count(i * tt, c, tt)
            ys[0:tt, :] = dyc_ref[...] / cnt
            ys[tt:tt + halo, :] = dyn / win.astype(F32)
            for r0 in range(0, tt, sub):
                acc = jnp.zeros((sub, LANES), F32)
                for j in range(k):
                    acc = acc + jnp.where(j >= k - win, ys[pl.ds(r0 + (k - 1) - j, sub), :], 0.0)
                dx_ref[r0:r0 + sub, :] = (acc - dyc_ref[r0:r0 + sub, :]).astype(dx_ref.dtype)
            return
        ys[0:tt, :] = dyc_ref[...]
        ys[tt:tt + halo, :] = dyn
        xs[0:halo, :] = jnp.where(i > 0, xp_ref[...], 0.0)
        xs[halo:halo + tt, :] = xc_ref[...]

        @pl.when(i == 0)
        def _():
            dw_ref[...] = jnp.zeros_like(dw_ref)

        for r0 in range(0, tt, sub):
            acc = jnp.zeros((sub, LANES), F32)
            for j in range(k):
                acc = acc + w_ref[j:j + 1, :] * ys[pl.ds(r0 + (k - 1) - j, sub), :]
            dx_ref[r0:r0 + sub, :] = acc.astype(dx_ref.dtype)
            dyc = dyc_ref[r0:r0 + sub, :]
            for j in range(k):
                dw_ref[j:j + 1, :] += jnp.sum(dyc * xs[pl.ds(r0 + halo - (k - 1) + j, sub), :], axis=0, keepdims=True)

    per = tt // halo
    cur = lambda cb0: pl.BlockSpec((tt, LANES), lambda c, i: (i, cb0 + c))
    dy_specs = [cur(0), pl.BlockSpec((halo, LANES), lambda c, i: (jnp.minimum((i + 1) * per, s // halo - 1), c))]
    dx_spec = pl.BlockSpec((tt, LANES), lambda c, i: (i, c))
    dx_shape = jax.ShapeDtypeStruct((s, channels), dx_dtype)
    if pool:
        return pl.pallas_call(
            body, name=name, grid=(channels // LANES, nt), in_specs=dy_specs, out_specs=dx_spec, out_shape=dx_shape,
            scratch_shapes=[pltpu.VMEM((tt + halo, LANES), F32)],
            compiler_params=_params(("arbitrary", "arbitrary")),
        )(dy, dy)
    in_specs = [cur(xcb0), pl.BlockSpec((halo, LANES), lambda c, i: (jnp.maximum(i * per - 1, 0), xcb0 + c))] + dy_specs
    in_specs.append(pl.BlockSpec((k, LANES), lambda c, i: (0, c)))
    return pl.pallas_call(
        body, name=name, grid=(channels // LANES, nt), in_specs=in_specs,
        out_specs=[dx_spec, pl.BlockSpec((k, LANES), lambda c, i: (0, c))],
        out_shape=[dx_shape, jax.ShapeDtypeStruct((k, channels), F32)],
        scratch_shapes=[pltpu.VMEM((halo + tt, LANES), F32), pltpu.VMEM((tt + halo, LANES), F32)],
        compiler_params=_params(("arbitrary", "arbitrary")),
    )(x, x, dy, dy, w)


DN_STEP_ROWS = 256


def _dn_step(q, k, v, bb, gb, state):
    c = DN_CHUNK
    r = q.shape[0]
    ri = lax.broadcasted_iota(jnp.int32, (r, r), 0)
    ci = lax.broadcasted_iota(jnp.int32, (r, r), 1)
    same = jnp.bitwise_or(ri, c - 1) == jnp.bitwise_or(ci, c - 1)
    causal = same & (ri >= ci)
    strict = same & (ri > ci)
    eye = jnp.where(ri == ci, 1.0, 0.0)
    gam = _hnn(jnp.where(causal, 1.0, 0.0), gb)
    gam_i = jnp.concatenate([gam] * (r // LANES), axis=1)
    gam_j = _hnn(jnp.ones((r, r), F32), eye * gam_i)
    decay = jnp.where(causal, jnp.exp(jnp.where(causal, gam_i - gam_j, 0.0)), 0.0)
    row = lax.broadcasted_iota(jnp.int32, gam.shape, 0)
    lasts = [jnp.sum(jnp.where(row == n * c + c - 1, gam, 0.0), axis=0, keepdims=True) for n in range(r // c)]
    g_last = jnp.zeros_like(gam)
    for n, gl in enumerate(lasts):
        g_last = jnp.where(jnp.bitwise_or(row, c - 1) == n * c + c - 1, gl, g_last)
    kb = k * bb
    b = -jnp.where(strict, _mnt(kb, k) * decay, 0.0)
    inv = eye + b
    bp = b
    for _ in range(5):
        bp = _mnn(bp, bp)
        inv = inv + _mnn(inv, bp)
    eg = jnp.exp(gam)
    u = _mnn(inv, v * bb)
    w = _mnn(inv, kb * eg)
    attn = _mnt(q, k) * decay
    q_dec = q * eg
    k_dec = k * jnp.exp(g_last - gam)
    v_news, o_inter = [], []
    for n in range(r // c):
        s = slice(n * c, (n + 1) * c)
        v_new = u[s] - _mnn(w[s], state)
        o_inter.append(_mnn(q_dec[s], state))
        state = state * jnp.exp(lasts[n]) + _mtn(k_dec[s], v_new)
        v_news.append(v_new)
    o = jnp.concatenate(o_inter, axis=0) + _mnn(attn, jnp.concatenate(v_news, axis=0))
    return o, state


def _dn_specs(steps, reverse):
    if reverse:
        tile = pl.BlockSpec((DN_STEP_ROWS, HEAD_DIM), lambda h, i: (steps - 1 - i, h))
        st = pl.BlockSpec((1, 1, HEAD_DIM, HEAD_DIM), lambda h, i: (steps - 1 - i, h, 0, 0))
    else:
        tile = pl.BlockSpec((DN_STEP_ROWS, HEAD_DIM), lambda h, i: (i, h))
        st = pl.BlockSpec((1, 1, HEAD_DIM, HEAD_DIM), lambda h, i: (i, h, 0, 0))
    return tile, st


def _dn_fwd(q, k, v, bb, gb, *, name):
    s = q.shape[0]
    steps = s // DN_STEP_ROWS

    def body(q_ref, k_ref, v_ref, b_ref, g_ref, o_ref, st_ref, state):
        @pl.when(pl.program_id(1) == 0)
        def _():
            state[...] = jnp.zeros_like(state)

        st = state[...]
        st_ref[0, 0] = st
        o, st = _dn_step(q_ref[...], k_ref[...], v_ref[...], b_ref[...], g_ref[...], st)
        o_ref[...] = o
        state[...] = st

    tile, stspec = _dn_specs(steps, False)
    return pl.pallas_call(
        body, name=name, grid=(HEADS, steps), in_specs=[tile] * 5, out_specs=[tile, stspec],
        out_shape=[jax.ShapeDtypeStruct((s, MIX_W), F32),
                   jax.ShapeDtypeStruct((steps, HEADS, HEAD_DIM, HEAD_DIM), F32)],
        scratch_shapes=[pltpu.VMEM((HEAD_DIM, HEAD_DIM), F32)],
        compiler_params=_params(("arbitrary", "arbitrary")),
    )(q, k, v, bb, gb)


def _dn_bwd(q, k, v, bb, gb, states, do, *, name):
    s = q.shape[0]
    steps = s // DN_STEP_ROWS

    def body(q_ref, k_ref, v_ref, b_ref, g_ref, st_ref, do_ref, dq_ref, dk_ref, dv_ref, db_ref, dg_ref, dstate):
        @pl.when(pl.program_id(1) == 0)
        def _():
            dstate[...] = jnp.zeros_like(dstate)

        _, vjp = jax.vjp(_dn_step, q_ref[...], k_ref[...], v_ref[...], b_ref[...], g_ref[...], st_ref[0, 0])
        dq, dk, dv, db, dg, dst = vjp((do_ref[...], dstate[...]))
        dq_ref[...] = dq
        dk_ref[...] = dk
        dv_ref[...] = dv
        db_ref[...] = db
        dg_ref[...] = dg
        dstate[...] = dst

    tile, stspec = _dn_specs(steps, True)
    return pl.pallas_call(
        body, name=name, grid=(HEADS, steps), in_specs=[tile] * 5 + [stspec, tile], out_specs=[tile] * 5,
        out_shape=[jax.ShapeDtypeStruct((s, MIX_W), F32)] * 5,
        scratch_shapes=[pltpu.VMEM((HEAD_DIM, HEAD_DIM), F32)],
        compiler_params=_params(("arbitrary", "arbitrary")),
    )(q, k, v, bb, gb, states, do)


EW_TILE_BYTES = 2 << 20


def _ew(f, ins, out_dtypes, *, name):
    r, c = ins[0][0].shape[1:] if isinstance(ins[0], tuple) else ins[0].shape
    row_bytes = -(-c // LANES) * LANES * 4
    tr = r
    if r * row_bytes > EW_TILE_BYTES:
        tr = next((t for t in (4096, 2048, 1024, 512, 256, 128, 64, 32, 16, 8)
                   if r % t == 0 and t * row_bytes <= EW_TILE_BYTES), r)
    n_in = len(ins)

    def body(*refs):
        res = f(*[x[...] for x in refs[:n_in]])
        for o_ref, v in zip(refs[n_in:], res):
            o_ref[...] = v.astype(o_ref.dtype)

    in_specs, args = [], []
    for x in ins:
        if isinstance(x, tuple):
            in_specs.append(pl.BlockSpec((None, tr, c), functools.partial(lambda i, j: (j, i, 0), j=x[1])))
            args.append(x[0])
        else:
            in_specs.append(pl.BlockSpec((tr, c), lambda i: (i, 0)))
            args.append(x)
    return pl.pallas_call(
        body, name=name, grid=(r // tr,), in_specs=in_specs,
        out_specs=[pl.BlockSpec((tr, c), lambda i: (i, 0)) for _ in out_dtypes],
        out_shape=[jax.ShapeDtypeStruct((r, c), dt) for dt in out_dtypes],
        compiler_params=_params(("arbitrary",)),
    )(*args)


def f_adamw(w, g, m, v):
    m = ADAM_B1 * m + (1.0 - ADAM_B1) * g
    v = ADAM_B2 * v + (1.0 - ADAM_B2) * (g * g)
    m_hat = m / (1.0 - ADAM_B1 ** ADAM_STEP)
    v_hat = v / (1.0 - ADAM_B2 ** ADAM_STEP)
    return -ADAM_LR * (m_hat / (jnp.sqrt(v_hat) + ADAM_EPS) + ADAM_WD * w), m, v


def f_sum4(a, b, c, d):
    return (((a.astype(F32) + b.astype(F32)) + c.astype(F32)) + d.astype(F32),)


def f_add2(a, b):
    return (a + b,)


def _adamw(w, g, m, v, *, name):
    shape = w.shape
    two = (1, shape[0]) if w.ndim == 1 else (-1, shape[-1])
    outs = _ew(f_adamw, [t.reshape(two) for t in (w, g, m, v)], [F32, F32, F32], name=name)
    return [o.reshape(shape) for o in outs]


_ANY = pl.BlockSpec(memory_space=pl.ANY)


def _xy_exchange(srcs, *, broadcast, name):
    nt = len(srcs)

    def body(*refs):
        src_refs, out_refs = refs[:nt], refs[nt:2 * nt]
        send_sems, recv_sems, local_sems = refs[2 * nt:]
        x, y, c = lax.axis_index("x"), lax.axis_index("y"), lax.axis_index("c")
        me = 2 * x + y
        peers = [(1 - x, y), (x, 1 - y), (1 - x, 1 - y)]

        def block(t, j):
            return src_refs[t] if broadcast else src_refs[t].at[j]

        def copy(t, n, px, py, src_blk, dst_blk):
            return pltpu.make_async_remote_copy(
                src_ref=src_blk, dst_ref=dst_blk, send_sem=send_sems.at[3 * t + n], recv_sem=recv_sems.at[3 * t + n],
                device_id=(px, py, c), device_id_type=pl.DeviceIdType.MESH)

        mine = [pltpu.make_async_copy(block(t, me), out_refs[t].at[me], local_sems.at[t]) for t in range(nt)]
        sends = [copy(t, n, px, py, block(t, 2 * px + py), out_refs[t].at[me])
                 for t in range(nt) for n, (px, py) in enumerate(peers)]
        for cp in mine + sends:
            cp.start()
        for cp in sends:
            cp.wait_send()
        for t in range(nt):
            for n, (px, py) in enumerate(peers):
                copy(t, n, px, py, block(t, me), out_refs[t].at[2 * px + py]).wait_recv()
        for cp in mine:
            cp.wait()

    shapes = [(4,) + tuple(s.shape) if broadcast else tuple(s.shape) for s in srcs]
    return pl.pallas_call(
        body, name=name, in_specs=[_ANY] * nt, out_specs=[_ANY] * nt,
        out_shape=[jax.ShapeDtypeStruct(shp, s.dtype) for shp, s in zip(shapes, srcs)],
        scratch_shapes=[pltpu.SemaphoreType.DMA((3 * nt,)), pltpu.SemaphoreType.DMA((3 * nt,)),
                        pltpu.SemaphoreType.DMA((nt,))],
        compiler_params=pltpu.CompilerParams(has_side_effects=True),
    )(*srcs)


def _c_swap(srcs, *, name):
    nt = len(srcs)

    def body(*refs):
        src_refs, out_refs = refs[:nt], refs[nt:2 * nt]
        send_sems, recv_sems = refs[2 * nt:]
        sibling = (lax.axis_index("x"), lax.axis_index("y"), 1 - lax.axis_index("c"))
        cps = [pltpu.make_async_remote_copy(src_ref=src_refs[t], dst_ref=out_refs[t], send_sem=send_sems.at[t],
                                            recv_sem=recv_sems.at[t], device_id=sibling,
                                            device_id_type=pl.DeviceIdType.MESH) for t in range(nt)]
        for cp in cps:
            cp.start()
        for cp in cps:
            cp.wait()

    return pl.pallas_call(
        body, name=name, in_specs=[_ANY] * nt, out_specs=[_ANY] * nt,
        out_shape=[jax.ShapeDtypeStruct(s.shape, s.dtype) for s in srcs],
        scratch_shapes=[pltpu.SemaphoreType.DMA((nt,)), pltpu.SemaphoreType.DMA((nt,))],
        compiler_params=pltpu.CompilerParams(has_side_effects=True),
    )(*srcs)


_HBM = pl.BlockSpec(memory_space=pltpu.HBM)
_SEM = pl.BlockSpec(memory_space=pltpu.SEMAPHORE)
_DATAFLOW = pltpu.SideEffectType.DATAFLOW_SIDE_EFFECTING


def _xy_peers():
    x, y, c = lax.axis_index("x"), lax.axis_index("y"), lax.axis_index("c")
    return 2 * x + y, c, [(1 - x, y), (x, 1 - y), (1 - x, 1 - y)]


def _xy_start(srcs, *, broadcast, name):
    nt = len(srcs)
    lands = [lax.empty((4,) + tuple(s.shape) if broadcast else tuple(s.shape), s.dtype) for s in srcs]

    def body(*refs):
        src_refs, land_refs = refs[:nt], refs[nt:2 * nt]
        send_sems, recv_sems = refs[2 * nt], refs[2 * nt + 1]
        token = refs[-1]
        me, c, peers = _xy_peers()
        for t in range(nt):
            for n, (px, py) in enumerate(peers):
                src_blk = src_refs[t] if broadcast else src_refs[t].at[2 * px + py]
                pltpu.make_async_remote_copy(
                    src_ref=src_blk, dst_ref=land_refs[t].at[me], send_sem=send_sems.at[3 * t + n],
                    recv_sem=recv_sems.at[3 * t + n], device_id=(px, py, c),
                    device_id_type=pl.DeviceIdType.MESH).start()
        token[...] = jnp.zeros_like(token)

    res = pl.pallas_call(
        body, name=name, in_specs=[_HBM] * (2 * nt),
        out_specs=[_SEM, _SEM] + [_HBM] * (2 * nt) + [pl.BlockSpec(memory_space=pltpu.VMEM)],
        out_shape=[pltpu.SemaphoreType.DMA((3 * nt,)), pltpu.SemaphoreType.DMA((3 * nt,))]
        + [pltpu.HBM(a.shape, a.dtype) for a in list(srcs) + lands] + [jax.ShapeDtypeStruct((8, LANES), F32)],
        input_output_aliases={i: 2 + i for i in range(2 * nt)},
        compiler_params=pltpu.CompilerParams(has_side_effects=_DATAFLOW),
    )(*[pltpu.with_memory_space_constraint(a, pltpu.HBM) for a in list(srcs) + lands])
    return res[0], res[1], res[2:2 + nt], res[2 + nt:2 + 2 * nt], res[-1]


def _xy_wait(started, after, *, broadcast, name):
    send_sems, recv_sems, srcs, lands, _ = started
    nt = len(srcs)

    def body(*refs):
        src_refs, land_refs = refs[:nt], refs[nt:2 * nt]
        send_sems, recv_sems = refs[2 * nt], refs[2 * nt + 1]
        me, c, peers = _xy_peers()
        for t in range(nt):
            for n, (px, py) in enumerate(peers):
                src_blk = src_refs[t] if broadcast else src_refs[t].at[me]
                cp = pltpu.make_async_remote_copy(
                    src_ref=src_blk, dst_ref=land_refs[t].at[2 * px + py], send_sem=send_sems.at[3 * t + n],
                    recv_sem=recv_sems.at[3 * t + n], device_id=(px, py, c), device_id_type=pl.DeviceIdType.MESH)
                cp.wait_send()
                cp.wait_recv()

    res = pl.pallas_call(
        body, name=name, in_specs=[_HBM] * (2 * nt) + [_SEM, _SEM, _ANY], out_specs=[_HBM] * (2 * nt),
        out_shape=[pltpu.HBM(a.shape, a.dtype) for a in list(srcs) + list(lands)],
        input_output_aliases={i: i for i in range(2 * nt)},
        compiler_params=pltpu.CompilerParams(has_side_effects=_DATAFLOW),
    )(*srcs, *lands, send_sems, recv_sems, after)
    return res[:nt], res[nt:]


def _all_sum(srcs, *, broadcast, name):
    got = _xy_exchange(srcs, broadcast=broadcast, name=name + "_xy")
    parts = []
    for t, gt in enumerate(got):
        g3 = gt.reshape(4, -1, gt.shape[-1])
        parts.append(_ew(f_sum4, [(g3, j) for j in range(4)], [F32], name=f"{name}_sum4_{t}")[0])
    others = _c_swap(parts, name=name + "_c")
    return [_ew(f_add2, [p, o], [F32], name=f"{name}_add2_{t}")[0].reshape(gt.shape[1:])
            for t, (p, o, gt) in enumerate(zip(parts, others, got))]


def _flat_rows(parts, dtype, row_multiple=8):
    flat = jnp.concatenate([p.reshape(-1).astype(dtype) for p in parts])
    pad = (-flat.shape[0]) % (row_multiple * LANES)
    if pad:
        flat = jnp.concatenate([flat, jnp.zeros((pad,), dtype)])
    return flat.reshape(-1, LANES)


def _unflat(buf, shapes):
    flat = buf.reshape(-1)
    out, off = [], 0
    for shp in shapes:
        n = 1
        for d in shp:
            n *= d
        out.append(flat[off:off + n].reshape(shp))
        off += n
    return out


def _regroup_w_in(w):
    cols = [w[:, a:b] for a, b in _Z_SEGS] + [jnp.zeros((w.shape[0], Z_W - IN_W), w.dtype)]
    return jnp.concatenate(cols, axis=1)


def _ungroup_w_in(wz):
    starts, off = {}, 0
    for a, b in _Z_SEGS:
        starts[a] = (off, off + b - a)
        off += b - a
    return jnp.concatenate([wz[:, starts[a][0]:starts[a][1]] for a in sorted(starts)], axis=1)


def _row(vec):
    return vec.reshape(1, -1)


def _lane_pad(vec):
    return jnp.zeros((1, LANES), F32).at[0, :vec.shape[0]].set(vec)


def _layer_fwd(x, mem, w, l):
    nm = lambda s: f"{s}_l{l}"
    sv = {"x0": x}
    h = _rw(f_norm, [(x, D_MODEL, 0)], [_row(w["norm_mix"])], [(D_MODEL, BF16)], name=nm("norm_mix"))[0]
    z = _mm(h, w["w_in_z"], name=nm("in_proj"))
    a_glu = _rw(f_glu, [(z, 2 * MIX_W, Z_A // (2 * MIX_W))], [], [(MIX_W, F32)], name=nm("glu"))[0]
    ac = _conv_fwd(a_glu, 0, MIX_W, w["conv_a_w"], CONV_K, name=nm("conv_a"))
    a_par = [_row(w["conv_a_b"]), _row(w["ln_a_g"]), _row(w["ln_a_b"])]
    a_out = _rw(f_lnsilu, [(ac, MIX_W, 0)], a_par, [(MIX_W, BF16)], name=nm("ln_a"))[0]
    qc = _conv_fwd(z, Z_QKV // LANES, 3 * MIX_W, w["dn_conv_w"], DN_CONV_K, name=nm("conv_dn"))
    dn_par = [_lane_pad(w["dn_a_log"]), _lane_pad(w["dn_dt_bias"])]
    qn, kn, v, bb, gb = _rw(f_dnprep, [(qc, 3 * MIX_W, 0), (z, LANES, Z_BD // LANES)], dn_par,
                            [(MIX_W, F32)] * 5, name=nm("dn_prep"))
    o, states = _dn_fwd(qn, kn, v, bb, gb, name=nm("dn_scan"))
    o_out = _rw(f_dnout, [(o, MIX_W, 0), (z, MIX_W, Z_DG // MIX_W)], [_row(w["dn_norm_g"])], [(MIX_W, BF16)],
                name=nm("dn_out"))[0]
    gm_par = [_row(w["gm_ln_g"]), _row(w["gm_ln_b"]), w["gm_ws"].reshape(4 * GM_CHUNK, GM_CHUNK), w["gm_bs"].T]
    c_out = _rw(f_gmlp, [(z, 2 * MIX_W, Z_GM // (2 * MIX_W))], gm_par, [(MIX_W, BF16)], name=nm("gmlp"))[0]
    pc = _conv_fwd(z, Z_POOL // LANES, MIX_W, None, POOL_K, pool=True, name=nm("pool"))
    p_par = [w["pool_w"].reshape(4 * LANES, LANES), _row(w["pool_scale"])]
    p_out = _rw(f_poolpost, [(pc, MIX_W, 0)], p_par, [(MIX_W, BF16)], name=nm("pool_post"))[0]
    branches = [a_out, o_out, c_out, p_out]
    proj = [_mm(br, w["w_branch"][n], name=nm(f"branch{n}")) for n, br in enumerate(branches)]
    merged = _rw(f_merge, [(z, N_BRANCH * D_MODEL, 0)] + [(p, D_MODEL, 0) for p in proj], [], [(D_MODEL, BF16)],
                 name=nm("merge"), tr=128)[0]
    x1 = _mm(merged, w["w_out"], add=x, name=nm("out_proj"))
    sv.update(h=h, z=z, a_glu=a_glu, ac=ac, qc=qc, qn=qn, kn=kn, v=v, bb=bb, gb=gb, o=o, states=states, pc=pc,
              branches=branches, proj=proj, merged=merged, x1=x1)
    h2 = _rw(f_norm, [(x1, D_MODEL, 0)], [_row(w["norm_xa"])], [(D_MODEL, BF16)], name=nm("norm_xa"))[0]
    mn = _rw(f_norm, [(mem, D_MODEL, 0)], [_row(w["norm_mem"])], [(D_MODEL, BF16)], name=nm("norm_mem"))[0]
    kv = _mm(mn, w["xa_wkv"], name=nm("xa_kv"))
    q = _mm(h2, w["xa_wq"], name=nm("xa_q"))
    att = _rw(f_attn, [(q, D_MODEL, 0)], [kv], [(D_MODEL, BF16)], name=nm("xa_attn"))[0]
    x2 = _mm(att, w["xa_wo"], add=x1, name=nm("xa_o"))
    sv.update(h2=h2, mn=mn, kv=kv, q=q, att=att, x2=x2)
    h3 = _rw(f_norm, [(x2, D_MODEL, 0)], [_row(w["norm_mlp"])], [(D_MODEL, BF16)], name=nm("norm_mlp"))[0]
    u = _mm(h3, w["mlp_w1"], name=nm("mlp_up"))
    act = _rw(f_relu2, [(u, FFN_W, 0)], [], [(FFN_W, BF16)], name=nm("relu2"))[0]
    x3 = _mm(act, w["mlp_w2"], add=x2, name=nm("mlp_down"))
    sv.update(h3=h3, u=u, act=act)
    return x3, sv


def _layer_bwd(dx, mem, w, sv, l):
    nm = lambda s: f"{s}_bwd_l{l}"
    s = dx.shape[0]
    g = {}
    dact = _mm(dx, w["mlp_w2"], tb=True, name=nm("mlp_down_dx"))
    g["mlp_w2"] = _mm(sv["act"], dx, ta=True, out_dtype=BF16, name=nm("mlp_down_dw"))
    du = _rw_bwd(f_relu2, [(sv["u"], FFN_W, 0)], [], [(dact, FFN_W, 0)], row_grads=[(0, BF16)], name=nm("relu2"))[0]
    g["mlp_w1"] = _mm(sv["h3"], du, ta=True, out_dtype=BF16, name=nm("mlp_up_dw"))
    dh3 = _mm(du, w["mlp_w1"], tb=True, name=nm("mlp_up_dx"))
    dx2, g["norm_mlp"] = _rw_bwd(f_norm, [(sv["x2"], D_MODEL, 0)], [_row(w["norm_mlp"])], [(dh3, D_MODEL, 0)],
                                 row_grads=[(0, F32)], param_grads=[0], add={0: (dx, D_MODEL, 0)}, name=nm("norm_mlp"))
    datt = _mm(dx2, w["xa_wo"], tb=True, name=nm("xa_o_dx"))
    g["xa_wo"] = _mm(sv["att"], dx2, ta=True, out_dtype=BF16, name=nm("xa_o_dw"))
    dq, dkv = _rw_bwd(f_attn, [(sv["q"], D_MODEL, 0)], [sv["kv"]], [(datt, D_MODEL, 0)], row_grads=[(0, BF16)],
                      param_grads=[0], name=nm("xa_attn"))
    g["xa_wq"] = _mm(sv["h2"], dq, ta=True, out_dtype=BF16, name=nm("xa_q_dw"))
    dh2 = _mm(dq, w["xa_wq"], tb=True, name=nm("xa_q_dx"))
    g["xa_wkv"] = _mm(sv["mn"], dkv, ta=True, out_dtype=BF16, name=nm("xa_kv_dw"))
    dmn = _mm(dkv, w["xa_wkv"], tb=True, name=nm("xa_kv_dx"))
    g["norm_mem"] = _rw_bwd(f_norm, [(mem, D_MODEL, 0)], [_row(w["norm_mem"])], [(dmn, D_MODEL, 0)], row_grads=[],
                            param_grads=[0], name=nm("norm_mem"))[0]
    dx1, g["norm_xa"] = _rw_bwd(f_norm, [(sv["x1"], D_MODEL, 0)], [_row(w["norm_xa"])], [(dh2, D_MODEL, 0)],
                                row_grads=[(0, F32)], param_grads=[0], add={0: (dx2, D_MODEL, 0)}, name=nm("norm_xa"))
    z = sv["z"]
    dmerged = _mm(dx1, w["w_out"], tb=True, name=nm("out_proj_dx"))
    g["w_out"] = _mm(sv["merged"], dx1, ta=True, out_dtype=BF16, name=nm("out_proj_dw"))
    mg = _rw_bwd(f_merge, [(z, N_BRANCH * D_MODEL, 0)] + [(p, D_MODEL, 0) for p in sv["proj"]], [],
                 [(dmerged, D_MODEL, 0)], row_grads=[(i, BF16) for i in range(5)], name=nm("merge"), tr=128)
    dgate, dproj = mg[0], mg[1:]
    g["w_branch"] = jnp.stack([_mm(br, dp, ta=True, out_dtype=BF16, name=nm(f"branch{n}_dw"))
                               for n, (br, dp) in enumerate(zip(sv["branches"], dproj))])
    dbr = [_mm(dp, w["w_branch"][n], tb=True, name=nm(f"branch{n}_dx")) for n, dp in enumerate(dproj)]
    p_par = [w["pool_w"].reshape(4 * LANES, LANES), _row(w["pool_scale"])]
    dpc, dpw, g["pool_scale"] = _rw_bwd(f_poolpost, [(sv["pc"], MIX_W, 0)], p_par, [(dbr[3], MIX_W, 0)],
                                        row_grads=[(0, F32)], param_grads=[0, 1], name=nm("pool_post"))
    g["pool_w"] = dpw.reshape(4, LANES, LANES)
    dpool = _conv_bwd(None, 0, MIX_W, None, POOL_K, dpc, pool=True, name=nm("pool"))
    gm_par = [_row(w["gm_ln_g"]), _row(w["gm_ln_b"]), w["gm_ws"].reshape(4 * GM_CHUNK, GM_CHUNK), w["gm_bs"].T]
    dgm, g["gm_ln_g"], g["gm_ln_b"], dws, dbst = _rw_bwd(
        f_gmlp, [(z, 2 * MIX_W, Z_GM // (2 * MIX_W))], gm_par, [(dbr[2], MIX_W, 0)], row_grads=[(0, BF16)],
        param_grads=[0, 1, 2, 3], name=nm("gmlp"))
    g["gm_ws"] = dws.reshape(4, GM_CHUNK, GM_CHUNK)
    g["gm_bs"] = dbst.T
    do, ddg, g["dn_norm_g"] = _rw_bwd(f_dnout, [(sv["o"], MIX_W, 0), (z, MIX_W, Z_DG // MIX_W)], [_row(w["dn_norm_g"])],
                                      [(dbr[1], MIX_W, 0)], row_grads=[(0, F32), (1, BF16)], param_grads=[0],
                                      name=nm("dn_out"))
    dqn, dkn, dv, dbb, dgb = _dn_bwd(sv["qn"], sv["kn"], sv["v"], sv["bb"], sv["gb"], sv["states"], do, name=nm("dn_scan"))
    dn_par = [_lane_pad(w["dn_a_log"]), _lane_pad(w["dn_dt_bias"])]
    dqc, dbd, dal, ddt = _rw_bwd(
        f_dnprep, [(sv["qc"], 3 * MIX_W, 0), (z, LANES, Z_BD // LANES)], dn_par,
        [(t, MIX_W, 0) for t in (dqn, dkn, dv, dbb, dgb)], row_grads=[(0, F32), (1, BF16)], param_grads=[0, 1],
        name=nm("dn_prep"))
    g["dn_a_log"], g["dn_dt_bias"] = dal[0, :HEADS], ddt[0, :HEADS]
    dqkv, g["dn_conv_w"] = _conv_bwd(z, Z_QKV // LANES, 3 * MIX_W, w["dn_conv_w"], DN_CONV_K, dqc, name=nm("conv_dn"))
    a_par = [_row(w["conv_a_b"]), _row(w["ln_a_g"]), _row(w["ln_a_b"])]
    dac, g["conv_a_b"], g["ln_a_g"], g["ln_a_b"] = _rw_bwd(
        f_lnsilu, [(sv["ac"], MIX_W, 0)], a_par, [(dbr[0], MIX_W, 0)], row_grads=[(0, F32)], param_grads=[0, 1, 2],
        name=nm("ln_a"))
    dglu, g["conv_a_w"] = _conv_bwd(sv["a_glu"], 0, MIX_W, w["conv_a_w"], CONV_K, dac, dx_dtype=F32, name=nm("conv_a"))
    da_in = _rw_bwd(f_glu, [(z, 2 * MIX_W, Z_A // (2 * MIX_W))], [], [(dglu, MIX_W, 0)], row_grads=[(0, BF16)],
                    name=nm("glu"))[0]
    dz = jnp.concatenate([dgate, da_in, dgm, dqkv, ddg, dpool, dbd, jnp.zeros((s, Z_W - Z_BD - LANES), BF16)], axis=1)
    g["w_in"] = _ungroup_w_in(_mm(sv["h"], dz, ta=True, out_dtype=BF16, name=nm("in_proj_dw")))
    dh = _mm(dz, w["w_in_z"], tb=True, name=nm("in_proj_dx"))
    dx0, g["norm_mix"] = _rw_bwd(f_norm, [(sv["x0"], D_MODEL, 0)], [_row(w["norm_mix"])], [(dh, D_MODEL, 0)],
                                 row_grads=[(0, F32)], param_grads=[0], add={0: (dx1, D_MODEL, 0)}, name=nm("norm_mix"))
    for n in ("norm_mlp", "norm_xa", "norm_mem", "norm_mix", "pool_scale", "gm_ln_g", "gm_ln_b", "dn_norm_g",
              "conv_a_b", "ln_a_g", "ln_a_b"):
        g[n] = g[n].reshape(-1)
    return dx0, g


def _shard_slice(a, axis, j):
    n = a.shape[axis] // 4
    return lax.slice_in_dim(a, j * n, (j + 1) * n, axis=axis)


def kernel(x, mem, norm_mix, w_in, conv_a_w, conv_a_b, ln_a_g, ln_a_b, dn_conv_w, dn_a_log, dn_dt_bias, dn_norm_g, gm_ln_g, gm_ln_b, gm_ws, gm_bs, pool_w, pool_scale, w_branch, w_out, norm_xa, norm_mem, xa_wq, xa_wkv, xa_wo, norm_mlp, mlp_w1, mlp_w2, norm_f, loss_target, m_norm_mix, m_w_in, m_conv_a_w, m_conv_a_b, m_ln_a_g, m_ln_a_b, m_dn_conv_w, m_dn_a_log, m_dn_dt_bias, m_dn_norm_g, m_gm_ln_g, m_gm_ln_b, m_gm_ws, m_gm_bs, m_pool_w, m_pool_scale, m_w_branch, m_w_out, m_norm_xa, m_norm_mem, m_xa_wq, m_xa_wkv, m_xa_wo, m_norm_mlp, m_mlp_w1, m_mlp_w2, m_norm_f, v_norm_mix, v_w_in, v_conv_a_w, v_conv_a_b, v_ln_a_g, v_ln_a_b, v_dn_conv_w, v_dn_a_log, v_dn_dt_bias, v_dn_norm_g, v_gm_ln_g, v_gm_ln_b, v_gm_ws, v_gm_bs, v_pool_w, v_pool_scale, v_w_branch, v_w_out, v_norm_xa, v_norm_mem, v_xa_wq, v_xa_wkv, v_xa_wo, v_norm_mlp, v_mlp_w1, v_mlp_w2, v_norm_f):
    given = dict(locals())
    wts = {n: given[n] for n in WEIGHTS}
    depth = norm_mix.shape[0]
    x = x[0]
    mem = mem[0]
    tgt = loss_target[0]

    me = 2 * lax.axis_index("x") + lax.axis_index("y")
    sharded = BIG + CONVS
    shard_axis = dict(BIG_AXIS, conv_a_w=1, dn_conv_w=1)

    def shards(l):
        return [wts[n][l].astype(BF16) for n in BIG] + [wts[n][l] for n in CONVS]

    def assemble(l, waited):
        mine, landed = waited
        w = {}
        for n, own, got in zip(sharded, mine, landed):
            got = lax.dynamic_update_index_in_dim(got, own[None], me, 0)
            w[n] = jnp.concatenate([got[j] for j in range(4)], axis=shard_axis[n])
        w["w_in_z"] = _regroup_w_in(w.pop("w_in"))
        for n in SMALL:
            if n != "norm_f" and n not in CONVS:
                w[n] = wts[n][l]
        return w

    first = _xy_start(shards(0), broadcast=True, name="gather_start_l0")
    first = _xy_wait(first, first[4], broadcast=True, name="gather_wait_l0")
    first, later = lax.optimization_barrier((first, [shards(l) for l in range(1, depth)]))
    gathers = [None] + [_xy_start(src, broadcast=True, name=f"gather_start_l{l + 1}") for l, src in enumerate(later)]
    started = jnp.zeros((), F32)
    for gs in gathers[1:]:
        started = started + gs[4][0, 0]

    def layer_weights(l, after):
        if l == 0:
            return assemble(0, first)
        return assemble(l, _xy_wait(gathers[l], after, broadcast=True, name=f"gather_wait_l{l}"))

    saved, layer_w = [], []
    x = x + started
    for l in range(depth):
        w = layer_weights(l, x)
        x, sv = _layer_fwd(x, mem, w, l)
        saved.append(sv)
        layer_w.append(w)
    dx, g_norm_f, loss_rows = _rw_bwd(
        f_loss, [(x, D_MODEL, 0), (tgt, D_MODEL, 0)], [_row(norm_f)], [(jnp.ones((x.shape[0], 1), F32), 1, 0)],
        row_grads=[(0, F32)], param_grads=[0], primal=[(0, 1, F32)], name="loss_head")
    loss = lax.psum(jnp.sum(loss_rows), ("x", "y", "c"))

    grads, reduces = [None] * depth, [None] * depth
    for l in reversed(range(depth)):
        dx, grads[l] = _layer_bwd(dx, mem, layer_w[l], saved[l], l)
        saved[l] = None
        blocks = [jnp.stack([_shard_slice(grads[l][n], BIG_AXIS[n], j) for j in range(4)]).astype(BF16) for n in BIG]
        reduces[l] = _xy_start(blocks, broadcast=False, name=f"reduce_start_l{l}")
        dx = dx + reduces[l][4][0, 0]
    grad_x = dx[None]

    parts = []
    for l in reversed(range(depth)):
        mine, landed = _xy_wait(reduces[l], dx, broadcast=False, name=f"reduce_wait_l{l}")
        for t, (own, got) in enumerate(zip(mine, landed)):
            got = lax.dynamic_update_index_in_dim(got, lax.dynamic_index_in_dim(own, me, 0), me, 0)
            g3 = got.reshape(4, -1, got.shape[-1])
            parts.append(_ew(f_sum4, [(g3, j) for j in range(4)], [F32], name=f"reduce_sum4_l{l}_{t}")[0])
    others = _c_swap(parts, name="reduce_big_c")
    sums = [_ew(f_add2, [p, o], [F32], name=f"reduce_add2_{i}")[0] for i, (p, o) in enumerate(zip(parts, others))]
    gw = {}
    for t, n in enumerate(BIG):
        per_layer = {l: sums[i * len(BIG) + t] for i, l in enumerate(reversed(range(depth)))}
        gw[n] = jnp.stack([per_layer[l].reshape(wts[n].shape[1:]) for l in range(depth)])

    small_names = [n for n in SMALL if n != "norm_f"]
    small_shapes = [(depth,) + (wts[n].shape[1:2] + (4 * wts[n].shape[2],) if n in CONVS else wts[n].shape[1:])
                    for n in small_names]
    small_buf = _flat_rows([jnp.stack([grads[l][n] for l in range(depth)]) for n in small_names] + [g_norm_f], F32,
                           row_multiple=1024)
    small_sum = _all_sum([small_buf], broadcast=True, name="reduce_small")[0]
    small_parts = _unflat(small_sum, small_shapes + [norm_f.shape])
    me = 2 * lax.axis_index("x") + lax.axis_index("y")
    for n, t in zip(small_names + ["norm_f"], small_parts):
        if n in CONVS:
            width = wts[n].shape[2]
            t = lax.dynamic_slice_in_dim(t, me * width, width, axis=2)
        gw[n] = t

    deltas, new_m, new_v = {}, {}, {}
    for n in WEIGHTS:
        deltas[n], new_m[n], new_v[n] = _adamw(wts[n], gw[n], given["m_" + n], given["v_" + n], name=f"adamw_{n}")
    return (loss, grad_x, *[gw[n] for n in WEIGHTS], *[deltas[n] for n in WEIGHTS], *[new_m[n] for n in WEIGHTS],
            *[new_v[n] for n in WEIGHTS])
```

```python
import functools

import jax
import jax.numpy as jnp
from jax import lax
from jax.experimental import pallas as pl
from jax.experimental.pallas import tpu as pltpu

F32 = jnp.float32
BF16 = jnp.bfloat16
HI = lax.Precision.HIGHEST

D_MODEL = 1024
MIX_W = 512
N_BRANCH = 4
HEADS = 4
HEAD_DIM = 128
CONV_K = 31
DN_CONV_K = 4
DN_CHUNK = 64
GM_CHUNK = 128
POOL_K = 16
XA_HEADS = 4
XA_HEAD_DIM = 256
FFN_W = 4096
IN_W = 8712
Z_W = 9216
LANES = 128
VMEM_LIMIT = 56 * 1024 * 1024

ADAM_LR, ADAM_B1, ADAM_B2, ADAM_EPS, ADAM_WD, ADAM_STEP = 0.001, 0.9, 0.999, 1e-08, 0.01, 10

Z_GATE, Z_A, Z_GM, Z_QKV, Z_DG, Z_POOL, Z_BD = 0, 4096, 5120, 6144, 7680, 8192, 8704
_Z_SEGS = ((4616, 8712), (0, 1024), (3080, 4104), (1024, 2560), (2560, 3072), (4104, 4616), (3072, 3080))

BIG = ("w_in", "w_branch", "w_out", "xa_wq", "xa_wkv", "xa_wo", "mlp_w1", "mlp_w2")
BIG_AXIS = {"w_in": 1, "w_branch": 2, "w_out": 0, "xa_wq": 0, "xa_wkv": 1, "xa_wo": 0, "mlp_w1": 1, "mlp_w2": 0}
CONVS = ("conv_a_w", "dn_conv_w")
WEIGHTS = ("norm_mix", "w_in", "conv_a_w", "conv_a_b", "ln_a_g", "ln_a_b", "dn_conv_w", "dn_a_log", "dn_dt_bias",
           "dn_norm_g", "gm_ln_g", "gm_ln_b", "gm_ws", "gm_bs", "pool_w", "pool_scale", "w_branch", "w_out",
           "norm_xa", "norm_mem", "xa_wq", "xa_wkv", "xa_wo", "norm_mlp", "mlp_w1", "mlp_w2", "norm_f")
SMALL = tuple(n for n in WEIGHTS if n not in BIG)


def _params(sem=None):
    return pltpu.CompilerParams(vmem_limit_bytes=VMEM_LIMIT, dimension_semantics=sem)


def _pick(n, cands):
    for c in cands:
        if n % c == 0:
            return c
    return n


def _make_dots(prec):
    def raw(a, b, ca, cb):
        if prec is None:
            a, b = a.astype(BF16), b.astype(BF16)
        return lax.dot_general(a, b, (((ca,), (cb,)), ((), ())), precision=prec, preferred_element_type=F32)

    nn = jax.custom_vjp(lambda a, b: raw(a, b, 1, 0))
    nt = jax.custom_vjp(lambda a, b: raw(a, b, 1, 1))
    tn = jax.custom_vjp(lambda a, b: raw(a, b, 0, 0))
    nn.defvjp(lambda a, b: (raw(a, b, 1, 0), (a, b)), lambda r, g: (raw(g, r[1], 1, 1), raw(r[0], g, 0, 0)))
    nt.defvjp(lambda a, b: (raw(a, b, 1, 1), (a, b)), lambda r, g: (raw(g, r[1], 1, 0), raw(g, r[0], 0, 0)))
    tn.defvjp(lambda a, b: (raw(a, b, 0, 0), (a, b)), lambda r, g: (raw(r[1], g, 1, 1), raw(r[0], g, 1, 0)))
    return nn, nt, tn


_bnn, _bnt, _btn = _make_dots(None)
_hnn, _hnt, _htn = _make_dots(HI)
_mnn, _mnt, _mtn = _make_dots(lax.Precision.HIGH)


def _col(x, j):
    lane = lax.broadcasted_iota(jnp.int32, x.shape, 1)
    return jnp.sum(jnp.where(lane == j, x, 0.0), axis=-1, keepdims=True)


def _rms(x, g, eps=1e-6):
    return x * lax.rsqrt(jnp.mean(x * x, axis=-1, keepdims=True) + eps) * g


def _ln(x, g, b, eps=1e-5):
    xc = x - jnp.mean(x, axis=-1, keepdims=True)
    var = jnp.mean(xc * xc, axis=-1, keepdims=True)
    return xc * lax.rsqrt(var + eps) * g + b


_sigmoid = jax.nn.sigmoid


def _silu(x):
    return x * _sigmoid(x)


def _softplus(x):
    return jnp.maximum(x, 0.0) + jnp.log1p(jnp.exp(-jnp.abs(x)))


def _gelu(x):
    return 0.5 * x * (1.0 + lax.erf(x * 0.7071067811865476))


def f_norm(x, g):
    return (_rms(x, g),)


def f_glu(a_in):
    return (a_in[:, :MIX_W] * _sigmoid(a_in[:, MIX_W:]),)


def f_lnsilu(ac, cb, g, b):
    return (_silu(_ln(ac + cb, g, b)),)


def f_dnprep(qc, bd, a_log, dt_bias):
    t = qc.shape[0]
    qkv = _silu(qc)
    qs, ks, bs, gs = [], [], [], []
    for h in range(HEADS):
        q = qkv[:, h * HEAD_DIM:(h + 1) * HEAD_DIM]
        k = qkv[:, MIX_W + h * HEAD_DIM:MIX_W + (h + 1) * HEAD_DIM]
        qs.append(q * lax.rsqrt(jnp.sum(q * q, axis=-1, keepdims=True) + 1e-6) * (HEAD_DIM ** -0.5))
        ks.append(k * lax.rsqrt(jnp.sum(k * k, axis=-1, keepdims=True) + 1e-6))
        beta = _sigmoid(_col(bd, h))
        g = -jnp.exp(_col(a_log, h)) * _softplus(_col(bd, HEADS + h) + _col(dt_bias, h))
        bs.append(jnp.broadcast_to(beta, (t, HEAD_DIM)))
        gs.append(jnp.broadcast_to(g, (t, HEAD_DIM)))
    cat = lambda xs: jnp.concatenate(xs, axis=1)
    return cat(qs), cat(ks), qkv[:, 2 * MIX_W:], cat(bs), cat(gs)


def f_dnout(o, dgate, g):
    outs = []
    for h in range(HEADS):
        sl = slice(h * HEAD_DIM, (h + 1) * HEAD_DIM)
        outs.append(_rms(o[:, sl], g) * _silu(dgate[:, sl]))
    return (jnp.concatenate(outs, axis=1),)


def f_gmlp(gm_in, lg, lb, ws, bst):
    t = gm_in.shape[0]
    ge = _gelu(gm_in)
    u, vg = ge[:, :MIX_W], _ln(ge[:, MIX_W:], lg, lb)
    ri = lax.broadcasted_iota(jnp.int32, (GM_CHUNK, GM_CHUNK), 0)
    ci = lax.broadcasted_iota(jnp.int32, (GM_CHUNK, GM_CHUNK), 1)
    chunks = []
    for n in range(t // GM_CHUNK):
        vc = vg[n * GM_CHUNK:(n + 1) * GM_CHUNK]
        cols = []
        for g in range(4):
            w = jnp.where(ri >= ci, ws[g * GM_CHUNK:(g + 1) * GM_CHUNK], 0.0)
            cols.append(_bnn(w, vc[:, g * LANES:(g + 1) * LANES]) + _col(bst, g))
        chunks.append(jnp.concatenate(cols, axis=1))
    mixed = chunks[0] if len(chunks) == 1 else jnp.concatenate(chunks, axis=0)
    return (u * mixed,)


def f_poolpost(pc, pw, ps):
    cols = [_bnn(pc[:, g * LANES:(g + 1) * LANES], pw[g * LANES:(g + 1) * LANES]) for g in range(4)]
    return (jnp.concatenate(cols, axis=1) * ps,)


def f_merge(gate, p0, p1, p2, p3):
    m = None
    for n, p in enumerate((p0, p1, p2, p3)):
        t = _sigmoid(gate[:, n * D_MODEL:(n + 1) * D_MODEL]) * p
        m = t if m is None else m + t
    return (m,)


def f_attn(q, kv):
    outs = []
    for h in range(XA_HEADS):
        sl = slice(h * XA_HEAD_DIM, (h + 1) * XA_HEAD_DIM)
        s = _bnt(q[:, sl], kv[:, sl]) * (XA_HEAD_DIM ** -0.5)
        s = s - lax.stop_gradient(jnp.max(s, axis=-1, keepdims=True))
        p = jnp.exp(s)
        p = p / jnp.sum(p, axis=-1, keepdims=True)
        outs.append(_bnn(p, kv[:, D_MODEL + h * XA_HEAD_DIM:D_MODEL + (h + 1) * XA_HEAD_DIM]))
    return (jnp.concatenate(outs, axis=1),)


def f_relu2(u):
    r = jnp.maximum(u, 0.0)
    return (r * r,)


def f_loss(x, t, g):
    e = _rms(x, g) - t
    return (0.5 * jnp.mean(e * e, axis=-1, keepdims=True),)


def _row_spec(tr, width, cb):
    return pl.BlockSpec((tr, width), functools.partial(lambda i, cb: (i, cb), cb=cb))


def _full_spec(shape):
    return pl.BlockSpec(shape, lambda i: (0,) * len(shape))


def _rw(f, rows, params, outs, *, name, tr=256):
    s = rows[0][0].shape[0]
    tr = min(tr, s)
    nr, npar = len(rows), len(params)

    def body(*refs):
        rv = [r[...].astype(F32) for r in refs[:nr]]
        pv = [p[...] for p in refs[nr:nr + npar]]
        res = f(*rv, *pv)
        for o_ref, r in zip(refs[nr + npar:], res):
            o_ref[...] = r.astype(o_ref.dtype)

    return pl.pallas_call(
        body, name=name, grid=(s // tr,),
        in_specs=[_row_spec(tr, w, cb) for _, w, cb in rows] + [_full_spec(p.shape) for p in params],
        out_specs=[_row_spec(tr, w, 0) for w, _ in outs],
        out_shape=[jax.ShapeDtypeStruct((s, w), dt) for w, dt in outs],
        compiler_params=_params(("arbitrary",)),
    )(*[a for a, _, _ in rows], *params)


def _rw_bwd(f, rows, params, cts, *, row_grads, param_grads=(), add=None, primal=(), name, tr=256):
    s = rows[0][0].shape[0]
    tr = min(tr, s)
    nr, npar, nct = len(rows), len(params), len(cts)
    add = add or {}
    add_keys = list(add)

    def body(*refs):
        it = iter(refs)
        row_refs = [next(it) for _ in range(nr)]
        par_refs = [next(it) for _ in range(npar)]
        ct_refs = [next(it) for _ in range(nct)]
        add_refs = {k: next(it) for k in add_keys}
        rg_refs = [next(it) for _ in row_grads]
        pg_refs = [next(it) for _ in param_grads]
        pr_refs = [next(it) for _ in primal]
        rv = [r[...].astype(F32) for r in row_refs]
        pv = [p[...] for p in par_refs]
        out, vjp = jax.vjp(f, *rv, *pv)
        grads = vjp(tuple(c[...].astype(F32) for c in ct_refs))
        for (idx, _), ref in zip(row_grads, rg_refs):
            g = grads[idx]
            if idx in add_refs:
                g = g + add_refs[idx][...].astype(F32)
            ref[...] = g.astype(ref.dtype)

        @pl.when(pl.program_id(0) == 0)
        def _():
            for ref in pg_refs:
                ref[...] = jnp.zeros_like(ref)

        for idx, ref in zip(param_grads, pg_refs):
            ref[...] += grads[nr + idx]
        for (idx, _, _), ref in zip(primal, pr_refs):
            ref[...] = out[idx].astype(ref.dtype)

    in_arrays = [a for a, _, _ in rows] + list(params) + [a for a, _, _ in cts] + [add[k][0] for k in add_keys]
    in_specs = ([_row_spec(tr, w, cb) for _, w, cb in rows] + [_full_spec(p.shape) for p in params]
                + [_row_spec(tr, w, cb) for _, w, cb in cts] + [_row_spec(tr, add[k][1], add[k][2]) for k in add_keys])
    out_specs = ([_row_spec(tr, rows[idx][1], 0) for idx, _ in row_grads]
                 + [_full_spec(params[idx].shape) for idx in param_grads]
                 + [_row_spec(tr, w, 0) for _, w, _ in primal])
    out_shape = ([jax.ShapeDtypeStruct((s, rows[idx][1]), dt) for idx, dt in row_grads]
                 + [jax.ShapeDtypeStruct(params[idx].shape, F32) for idx in param_grads]
                 + [jax.ShapeDtypeStruct((s, w), dt) for _, w, dt in primal])
    return pl.pallas_call(
        body, name=name, grid=(s // tr,), in_specs=in_specs, out_specs=out_specs, out_shape=out_shape,
        compiler_params=_params(("arbitrary",)),
    )(*in_arrays)


def _mm(a, b, *, ta=False, tb=False, out_dtype=F32, add=None, name):
    m, k = (a.shape[1], a.shape[0]) if ta else a.shape
    n = b.shape[0] if tb else b.shape[1]
    tm = _pick(m, (1024, 512, 256, 128))
    tn = _pick(n, (1024, 512, 256, 128))
    tk = _pick(k, (1024, 512, 256, 128))
    nk = k // tk
    dims = (((0 if ta else 1,), (1 if tb else 0,)), ((), ()))

    def body(*refs):
        a_ref, b_ref = refs[:2]
        add_ref = refs[2] if add is not None else None
        o_ref = refs[-2] if nk > 1 else refs[-1]
        part = lax.dot_general(a_ref[...].astype(BF16), b_ref[...].astype(BF16), dims, preferred_element_type=F32)

        def finish(r):
            if add is not None:
                r = r + add_ref[...]
            o_ref[...] = r.astype(o_ref.dtype)

        if nk == 1:
            finish(part)
            return
        acc = refs[-1]
        kk = pl.program_id(2)

        @pl.when(kk == 0)
        def _():
            acc[...] = part

        @pl.when((kk > 0) & (kk < nk - 1))
        def _():
            acc[...] += part

        @pl.when(kk == nk - 1)
        def _():
            finish(acc[...] + part)

    a_spec = (pl.BlockSpec((tk, tm), lambda i, j, kk: (kk, i)) if ta else pl.BlockSpec((tm, tk), lambda i, j, kk: (i, kk)))
    b_spec = (pl.BlockSpec((tn, tk), lambda i, j, kk: (j, kk)) if tb else pl.BlockSpec((tk, tn), lambda i, j, kk: (kk, j)))
    o_spec = pl.BlockSpec((tm, tn), lambda i, j, kk: (i, j))
    in_specs = [a_spec, b_spec] + ([o_spec] if add is not None else [])
    args = (a, b) + ((add,) if add is not None else ())
    return pl.pallas_call(
        body, name=name, grid=(m // tm, n // tn, nk), in_specs=in_specs, out_specs=o_spec,
        out_shape=jax.ShapeDtypeStruct((m, n), out_dtype),
        scratch_shapes=[pltpu.VMEM((tm, tn), F32)] if nk > 1 else [],
        compiler_params=_params(("arbitrary", "arbitrary", "arbitrary")),
    )(*args)


CONV_TT = 1024
CONV_SUB = 256


def _halo(k):
    return -(-(k - 1) // 8) * 8


def _pool_count(row0, c, rows):
    win = lax.shift_left(jnp.int32(2), c)
    t = row0 + lax.broadcasted_iota(jnp.int32, (rows, LANES), 0)
    return win, jnp.minimum(t + 1, win).astype(F32)


def _conv_fwd(x, xcb0, channels, w, k, *, pool=False, name):
    s = x.shape[0]
    tt = min(CONV_TT, s)
    sub = min(CONV_SUB, tt)
    nt = s // tt
    halo = _halo(k)

    def body(*refs):
        if pool:
            xc_ref, xp_ref, o_ref, xs = refs
        else:
            xc_ref, xp_ref, w_ref, o_ref, xs = refs
        c, i = pl.program_id(0), pl.program_id(1)
        xs[0:halo, :] = jnp.where(i > 0, xp_ref[...], 0.0)
        xs[halo:halo + tt, :] = xc_ref[...]
        for r0 in range(0, tt, sub):
            acc = jnp.zeros((sub, LANES), F32)
            if pool:
                win, cnt = _pool_count(i * tt + r0, c, sub)
                for j in range(k):
                    acc = acc + jnp.where(j >= k - win, xs[pl.ds(r0 + halo - (k - 1) + j, sub), :], 0.0)
                o_ref[r0:r0 + sub, :] = acc / cnt - xc_ref[r0:r0 + sub, :]
            else:
                for j in range(k):
                    acc = acc + w_ref[j:j + 1, :] * xs[pl.ds(r0 + halo - (k - 1) + j, sub), :]
                o_ref[r0:r0 + sub, :] = acc

    per = tt // halo
    in_specs = [pl.BlockSpec((tt, LANES), lambda c, i: (i, xcb0 + c)),
                pl.BlockSpec((halo, LANES), lambda c, i: (jnp.maximum(i * per - 1, 0), xcb0 + c))]
    args = [x, x]
    if not pool:
        in_specs.append(pl.BlockSpec((k, LANES), lambda c, i: (0, c)))
        args.append(w)
    return pl.pallas_call(
        body, name=name, grid=(channels // LANES, nt), in_specs=in_specs,
        out_specs=pl.BlockSpec((tt, LANES), lambda c, i: (i, c)),
        out_shape=jax.ShapeDtypeStruct((s, channels), F32),
        scratch_shapes=[pltpu.VMEM((halo + tt, LANES), F32)],
        compiler_params=_params(("arbitrary", "arbitrary")),
    )(*args)


def _conv_bwd(x, xcb0, channels, w, k, dy, *, pool=False, dx_dtype=BF16, name):
    s = dy.shape[0]
    tt = min(CONV_TT, s)
    sub = min(CONV_SUB, tt)
    nt = s // tt
    halo = _halo(k)

    def body(*refs):
        if pool:
            dyc_ref, dyn_ref, dx_ref, ys = refs
        else:
            xc_ref, xp_ref, dyc_ref, dyn_ref, w_ref, dx_ref, dw_ref, xs, ys = refs
        c, i = pl.program_id(0), pl.program_id(1)
        dyn = jnp.where(i < nt - 1, dyn_ref[...], 0.0)
        if pool:
            win, cnt = _pool_count(i * tt, c, tt)
            ys[0:tt, :] = dyc_ref[...] / cnt
            ys[tt:tt + halo, :] = dyn / win.astype(F32)
            for r0 in range(0, tt, sub):
                acc = jnp.zeros((sub, LANES), F32)
                for j in range(k):
                    acc = acc + jnp.where(j >= k - win, ys[pl.ds(r0 + (k - 1) - j, sub), :], 0.0)
                dx_ref[r0:r0 + sub, :] = (acc - dyc_ref[r0:r0 + sub, :]).astype(dx_ref.dtype)
            return
        ys[0:tt, :] = dyc_ref[...]
        ys[tt:tt + halo, :] = dyn
        xs[0:halo, :] = jnp.where(i > 0, xp_ref[...], 0.0)
        xs[halo:halo + tt, :] = xc_ref[...]

        @pl.when(i == 0)
        def _():
            dw_ref[...] = jnp.zeros_like(dw_ref)

        for r0 in range(0, tt, sub):
            acc = jnp.zeros((sub, LANES), F32)
            for j in range(k):
                acc = acc + w_ref[j:j + 1, :] * ys[pl.ds(r0 + (k - 1) - j, sub), :]
            dx_ref[r0:r0 + sub, :] = acc.astype(dx_ref.dtype)
            dyc = dyc_ref[r0:r0 + sub, :]
            for j in range(k):
                dw_ref[j:j + 1, :] += jnp.sum(dyc * xs[pl.ds(r0 + halo - (k - 1) + j, sub), :], axis=0, keepdims=True)

    per = tt // halo
    cur = lambda cb0: pl.BlockSpec((tt, LANES), lambda c, i: (i, cb0 + c))
    dy_specs = [cur(0), pl.BlockSpec((halo, LANES), lambda c, i: (jnp.minimum((i + 1) * per, s // halo - 1), c))]
    dx_spec = pl.BlockSpec((tt, LANES), lambda c, i: (i, c))
    dx_shape = jax.ShapeDtypeStruct((s, channels), dx_dtype)
    if pool:
        return pl.pallas_call(
            body, name=name, grid=(channels // LANES, nt), in_specs=dy_specs, out_specs=dx_spec, out_shape=dx_shape,
            scratch_shapes=[pltpu.VMEM((tt + halo, LANES), F32)],
            compiler_params=_params(("arbitrary", "arbitrary")),
        )(dy, dy)
    in_specs = [cur(xcb0), pl.BlockSpec((halo, LANES), lambda c, i: (jnp.maximum(i * per - 1, 0), xcb0 + c))] + dy_specs
    in_specs.append(pl.BlockSpec((k, LANES), lambda c, i: (0, c)))
    return pl.pallas_call(
        body, name=name, grid=(channels // LANES, nt), in_specs=in_specs,
        out_specs=[dx_spec, pl.BlockSpec((k, LANES), lambda c, i: (0, c))],
        out_shape=[dx_shape, jax.ShapeDtypeStruct((k, channels), F32)],
        scratch_shapes=[pltpu.VMEM((halo + tt, LANES), F32), pltpu.VMEM((tt + halo, LANES), F32)],
        compiler_params=_params(("arbitrary", "arbitrary")),
    )(x, x, dy, dy, w)


DN_STEP_ROWS = 256


def _dn_step(q, k, v, bb, gb, state):
    c = DN_CHUNK
    r = q.shape[0]
    ri = lax.broadcasted_iota(jnp.int32, (r, r), 0)
    ci = lax.broadcasted_iota(jnp.int32, (r, r), 1)
    same = jnp.bitwise_or(ri, c - 1) == jnp.bitwise_or(ci, c - 1)
    causal = same & (ri >= ci)
    strict = same & (ri > ci)
    eye = jnp.where(ri == ci, 1.0, 0.0)
    gam = _hnn(jnp.where(causal, 1.0, 0.0), gb)
    gam_i = jnp.concatenate([gam] * (r // LANES), axis=1)
    gam_j = gam_i.T
    decay = jnp.where(causal, jnp.exp(jnp.where(causal, gam_i - gam_j, 0.0)), 0.0)
    row = lax.broadcasted_iota(jnp.int32, gam.shape, 0)
    lasts = [jnp.sum(jnp.where(row == n * c + c - 1, gam, 0.0), axis=0, keepdims=True) for n in range(r // c)]
    g_last = jnp.zeros_like(gam)
    for n, gl in enumerate(lasts):
        g_last = jnp.where(jnp.bitwise_or(row, c - 1) == n * c + c - 1, gl, g_last)
    kb = k * bb
    b = -jnp.where(strict, _mnt(kb, k) * decay, 0.0)
    inv = eye + b
    bp = b
    for _ in range(5):
        bp = _mnn(bp, bp)
        inv = inv + _mnn(inv, bp)
    eg = jnp.exp(gam)
    u = _mnn(inv, v * bb)
    w = _mnn(inv, kb * eg)
    attn = _mnt(q, k) * decay
    q_dec = q * eg
    k_dec = k * jnp.exp(g_last - gam)
    v_news, o_inter = [], []
    for n in range(r // c):
        s = slice(n * c, (n + 1) * c)
        v_new = u[s] - _mnn(w[s], state)
        o_inter.append(_mnn(q_dec[s], state))
        state = state * jnp.exp(lasts[n]) + _mtn(k_dec[s], v_new)
        v_news.append(v_new)
    o = jnp.concatenate(o_inter, axis=0) + _mnn(attn, jnp.concatenate(v_news, axis=0))
    return o, state


def _dn_specs(steps, reverse):
    if reverse:
        tile = pl.BlockSpec((DN_STEP_ROWS, HEAD_DIM), lambda h, i: (steps - 1 - i, h))
        st = pl.BlockSpec((1, 1, HEAD_DIM, HEAD_DIM), lambda h, i: (steps - 1 - i, h, 0, 0))
    else:
        tile = pl.BlockSpec((DN_STEP_ROWS, HEAD_DIM), lambda h, i: (i, h))
        st = pl.BlockSpec((1, 1, HEAD_DIM, HEAD_DIM), lambda h, i: (i, h, 0, 0))
    return tile, st


def _dn_fwd(q, k, v, bb, gb, *, name):
    s = q.shape[0]
    steps = s // DN_STEP_ROWS

    def body(q_ref, k_ref, v_ref, b_ref, g_ref, o_ref, st_ref, state):
        @pl.when(pl.program_id(1) == 0)
        def _():
            state[...] = jnp.zeros_like(state)

        st = state[...]
        st_ref[0, 0] = st
        o, st = _dn_step(q_ref[...], k_ref[...], v_ref[...], b_ref[...], g_ref[...], st)
        o_ref[...] = o
        state[...] = st

    tile, stspec = _dn_specs(steps, False)
    return pl.pallas_call(
        body, name=name, grid=(HEADS, steps), in_specs=[tile] * 5, out_specs=[tile, stspec],
        out_shape=[jax.ShapeDtypeStruct((s, MIX_W), F32),
                   jax.ShapeDtypeStruct((steps, HEADS, HEAD_DIM, HEAD_DIM), F32)],
        scratch_shapes=[pltpu.VMEM((HEAD_DIM, HEAD_DIM), F32)],
        compiler_params=_params(("arbitrary", "arbitrary")),
    )(q, k, v, bb, gb)


def _dn_bwd(q, k, v, bb, gb, states, do, *, name):
    s = q.shape[0]
    steps = s // DN_STEP_ROWS

    def body(q_ref, k_ref, v_ref, b_ref, g_ref, st_ref, do_ref, dq_ref, dk_ref, dv_ref, db_ref, dg_ref, dstate):
        @pl.when(pl.program_id(1) == 0)
        def _():
            dstate[...] = jnp.zeros_like(dstate)

        _, vjp = jax.vjp(_dn_step, q_ref[...], k_ref[...], v_ref[...], b_ref[...], g_ref[...], st_ref[0, 0])
        dq, dk, dv, db, dg, dst = vjp((do_ref[...], dstate[...]))
        dq_ref[...] = dq
        dk_ref[...] = dk
        dv_ref[...] = dv
        db_ref[...] = db
        dg_ref[...] = dg
        dstate[...] = dst

    tile, stspec = _dn_specs(steps, True)
    return pl.pallas_call(
        body, name=name, grid=(HEADS, steps), in_specs=[tile] * 5 + [stspec, tile], out_specs=[tile] * 5,
        out_shape=[jax.ShapeDtypeStruct((s, MIX_W), F32)] * 5,
        scratch_shapes=[pltpu.VMEM((HEAD_DIM, HEAD_DIM), F32)],
        compiler_params=_params(("arbitrary", "arbitrary")),
    )(q, k, v, bb, gb, states, do)


EW_TILE_BYTES = 2 << 20


def _ew(f, ins, out_dtypes, *, name):
    r, c = ins[0][0].shape[1:] if isinstance(ins[0], tuple) else ins[0].shape
    row_bytes = -(-c // LANES) * LANES * 4
    tr = r
    if r * row_bytes > EW_TILE_BYTES:
        tr = next((t for t in (4096, 2048, 1024, 512, 256, 128, 64, 32, 16, 8)
                   if r % t == 0 and t * row_bytes <= EW_TILE_BYTES), r)
    n_in = len(ins)

    def body(*refs):
        res = f(*[x[...] for x in refs[:n_in]])
        for o_ref, v in zip(refs[n_in:], res):
            o_ref[...] = v.astype(o_ref.dtype)

    in_specs, args = [], []
    for x in ins:
        if isinstance(x, tuple):
            in_specs.append(pl.BlockSpec((None, tr, c), functools.partial(lambda i, j: (j, i, 0), j=x[1])))
            args.append(x[0])
        else:
            in_specs.append(pl.BlockSpec((tr, c), lambda i: (i, 0)))
            args.append(x)
    return pl.pallas_call(
        body, name=name, grid=(r // tr,), in_specs=in_specs,
        out_specs=[pl.BlockSpec((tr, c), lambda i: (i, 0)) for _ in out_dtypes],
        out_shape=[jax.ShapeDtypeStruct((r, c), dt) for dt in out_dtypes],
        compiler_params=_params(("arbitrary",)),
    )(*args)


def f_adamw(w, g, m, v):
    m = ADAM_B1 * m + (1.0 - ADAM_B1) * g
    v = ADAM_B2 * v + (1.0 - ADAM_B2) * (g * g)
    m_hat = m / (1.0 - ADAM_B1 ** ADAM_STEP)
    v_hat = v / (1.0 - ADAM_B2 ** ADAM_STEP)
    return -ADAM_LR * (m_hat / (jnp.sqrt(v_hat) + ADAM_EPS) + ADAM_WD * w), m, v


def f_sum4(a, b, c, d):
    return (((a.astype(F32) + b.astype(F32)) + c.astype(F32)) + d.astype(F32),)


def f_add2(a, b):
    return (a + b,)


def _adamw(w, g, m, v, *, name):
    shape = w.shape
    two = (1, shape[0]) if w.ndim == 1 else (-1, shape[-1])
    outs = _ew(f_adamw, [t.reshape(two) for t in (w, g, m, v)], [F32, F32, F32], name=name)
    return [o.reshape(shape) for o in outs]


_ANY = pl.BlockSpec(memory_space=pl.ANY)


def _xy_exchange(srcs, *, broadcast, name):
    nt = len(srcs)

    def body(*refs):
        src_refs, out_refs = refs[:nt], refs[nt:2 * nt]
        send_sems, recv_sems, local_sems = refs[2 * nt:]
        x, y, c = lax.axis_index("x"), lax.axis_index("y"), lax.axis_index("c")
        me = 2 * x + y
        peers = [(1 - x, y), (x, 1 - y), (1 - x, 1 - y)]

        def block(t, j):
            return src_refs[t] if broadcast else src_refs[t].at[j]

        def copy(t, n, px, py, src_blk, dst_blk):
            return pltpu.make_async_remote_copy(
                src_ref=src_blk, dst_ref=dst_blk, send_sem=send_sems.at[3 * t + n], recv_sem=recv_sems.at[3 * t + n],
                device_id=(px, py, c), device_id_type=pl.DeviceIdType.MESH)

        mine = [pltpu.make_async_copy(block(t, me), out_refs[t].at[me], local_sems.at[t]) for t in range(nt)]
        sends = [copy(t, n, px, py, block(t, 2 * px + py), out_refs[t].at[me])
                 for t in range(nt) for n, (px, py) in enumerate(peers)]
        for cp in mine + sends:
            cp.start()
        for cp in sends:
            cp.wait_send()
        for t in range(nt):
            for n, (px, py) in enumerate(peers):
                copy(t, n, px, py, block(t, me), out_refs[t].at[2 * px + py]).wait_recv()
        for cp in mine:
            cp.wait()

    shapes = [(4,) + tuple(s.shape) if broadcast else tuple(s.shape) for s in srcs]
    return pl.pallas_call(
        body, name=name, in_specs=[_ANY] * nt, out_specs=[_ANY] * nt,
        out_shape=[jax.ShapeDtypeStruct(shp, s.dtype) for shp, s in zip(shapes, srcs)],
        scratch_shapes=[pltpu.SemaphoreType.DMA((3 * nt,)), pltpu.SemaphoreType.DMA((3 * nt,)),
                        pltpu.SemaphoreType.DMA((nt,))],
        compiler_params=pltpu.CompilerParams(has_side_effects=True),
    )(*srcs)


def _c_swap(srcs, *, name):
    nt = len(srcs)

    def body(*refs):
        src_refs, out_refs = refs[:nt], refs[nt:2 * nt]
        send_sems, recv_sems = refs[2 * nt:]
        sibling = (lax.axis_index("x"), lax.axis_index("y"), 1 - lax.axis_index("c"))
        cps = [pltpu.make_async_remote_copy(src_ref=src_refs[t], dst_ref=out_refs[t], send_sem=send_sems.at[t],
                                            recv_sem=recv_sems.at[t], device_id=sibling,
                                            device_id_type=pl.DeviceIdType.MESH) for t in range(nt)]
        for cp in cps:
            cp.start()
        for cp in cps:
            cp.wait()

    return pl.pallas_call(
        body, name=name, in_specs=[_ANY] * nt, out_specs=[_ANY] * nt,
        out_shape=[jax.ShapeDtypeStruct(s.shape, s.dtype) for s in srcs],
        scratch_shapes=[pltpu.SemaphoreType.DMA((nt,)), pltpu.SemaphoreType.DMA((nt,))],
        compiler_params=pltpu.CompilerParams(has_side_effects=True),
    )(*srcs)


_HBM = pl.BlockSpec(memory_space=pltpu.HBM)
_SEM = pl.BlockSpec(memory_space=pltpu.SEMAPHORE)
_DATAFLOW = pltpu.SideEffectType.DATAFLOW_SIDE_EFFECTING


def _xy_peers():
    x, y, c = lax.axis_index("x"), lax.axis_index("y"), lax.axis_index("c")
    return 2 * x + y, c, [(1 - x, y), (x, 1 - y), (1 - x, 1 - y)]


def _xy_start(srcs, *, broadcast, name):
    nt = len(srcs)
    lands = [lax.empty((4,) + tuple(s.shape) if broadcast else tuple(s.shape), s.dtype) for s in srcs]

    def body(*refs):
        src_refs, land_refs = refs[:nt], refs[nt:2 * nt]
        send_sems, recv_sems = refs[2 * nt], refs[2 * nt + 1]
        token = refs[-1]
        me, c, peers = _xy_peers()
        for t in range(nt):
            for n, (px, py) in enumerate(peers):
                src_blk = src_refs[t] if broadcast else src_refs[t].at[2 * px + py]
                pltpu.make_async_remote_copy(
                    src_ref=src_blk, dst_ref=land_refs[t].at[me], send_sem=send_sems.at[3 * t + n],
                    recv_sem=recv_sems.at[3 * t + n], device_id=(px, py, c),
                    device_id_type=pl.DeviceIdType.MESH).start()
        token[...] = jnp.zeros_like(token)

    res = pl.pallas_call(
        body, name=name, in_specs=[_HBM] * (2 * nt),
        out_specs=[_SEM, _SEM] + [_HBM] * (2 * nt) + [pl.BlockSpec(memory_space=pltpu.VMEM)],
        out_shape=[pltpu.SemaphoreType.DMA((3 * nt,)), pltpu.SemaphoreType.DMA((3 * nt,))]
        + [pltpu.HBM(a.shape, a.dtype) for a in list(srcs) + lands] + [jax.ShapeDtypeStruct((8, LANES), F32)],
        input_output_aliases={i: 2 + i for i in range(2 * nt)},
        compiler_params=pltpu.CompilerParams(has_side_effects=_DATAFLOW),
    )(*[pltpu.with_memory_space_constraint(a, pltpu.HBM) for a in list(srcs) + lands])
    return res[0], res[1], res[2:2 + nt], res[2 + nt:2 + 2 * nt], res[-1]


def _xy_wait(started, after, *, broadcast, name):
    send_sems, recv_sems, srcs, lands, _ = started
    nt = len(srcs)

    def body(*refs):
        src_refs, land_refs = refs[:nt], refs[nt:2 * nt]
        send_sems, recv_sems = refs[2 * nt], refs[2 * nt + 1]
        me, c, peers = _xy_peers()
        for t in range(nt):
            for n, (px, py) in enumerate(peers):
                src_blk = src_refs[t] if broadcast else src_refs[t].at[me]
                cp = pltpu.make_async_remote_copy(
                    src_ref=src_blk, dst_ref=land_refs[t].at[2 * px + py], send_sem=send_sems.at[3 * t + n],
                    recv_sem=recv_sems.at[3 * t + n], device_id=(px, py, c), device_id_type=pl.DeviceIdType.MESH)
                cp.wait_send()
                cp.wait_recv()

    res = pl.pallas_call(
        body, name=name, in_specs=[_HBM] * (2 * nt) + [_SEM, _SEM, _ANY], out_specs=[_HBM] * (2 * nt),
        out_shape=[pltpu.HBM(a.shape, a.dtype) for a in list(srcs) + list(lands)],
        input_output_aliases={i: i for i in range(2 * nt)},
        compiler_params=pltpu.CompilerParams(has_side_effects=_DATAFLOW),
    )(*srcs, *lands, send_sems, recv_sems, after)
    return res[:nt], res[nt:]


def _all_sum(srcs, *, broadcast, name):
    got = _xy_exchange(srcs, broadcast=broadcast, name=name + "_xy")
    parts = []
    for t, gt in enumerate(got):
        g3 = gt.reshape(4, -1, gt.shape[-1])
        parts.append(_ew(f_sum4, [(g3, j) for j in range(4)], [F32], name=f"{name}_sum4_{t}")[0])
    others = _c_swap(parts, name=name + "_c")
    return [_ew(f_add2, [p, o], [F32], name=f"{name}_add2_{t}")[0].reshape(gt.shape[1:])
            for t, (p, o, gt) in enumerate(zip(parts, others, got))]


def _flat_rows(parts, dtype, row_multiple=8):
    flat = jnp.concatenate([p.reshape(-1).astype(dtype) for p in parts])
    pad = (-flat.shape[0]) % (row_multiple * LANES)
    if pad:
        flat = jnp.concatenate([flat, jnp.zeros((pad,), dtype)])
    return flat.reshape(-1, LANES)


def _unflat(buf, shapes):
    flat = buf.reshape(-1)
    out, off = [], 0
    for shp in shapes:
        n = 1
        for d in shp:
            n *= d
        out.append(flat[off:off + n].reshape(shp))
        off += n
    return out


def _regroup_w_in(w):
    cols = [w[:, a:b] for a, b in _Z_SEGS] + [jnp.zeros((w.shape[0], Z_W - IN_W), w.dtype)]
    return jnp.concatenate(cols, axis=1)


def _ungroup_w_in(wz):
    starts, off = {}, 0
    for a, b in _Z_SEGS:
        starts[a] = (off, off + b - a)
        off += b - a
    return jnp.concatenate([wz[:, starts[a][0]:starts[a][1]] for a in sorted(starts)], axis=1)


def _row(vec):
    return vec.reshape(1, -1)


def _lane_pad(vec):
    return jnp.zeros((1, LANES), F32).at[0, :vec.shape[0]].set(vec)


def _layer_fwd(x, mem, w, l):
    nm = lambda s: f"{s}_l{l}"
    sv = {"x0": x}
    h = _rw(f_norm, [(x, D_MODEL, 0)], [_row(w["norm_mix"])], [(D_MODEL, BF16)], name=nm("norm_mix"))[0]
    z = _mm(h, w["w_in_z"], name=nm("in_proj"))
    a_glu = _rw(f_glu, [(z, 2 * MIX_W, Z_A // (2 * MIX_W))], [], [(MIX_W, F32)], name=nm("glu"))[0]
    ac = _conv_fwd(a_glu, 0, MIX_W, w["conv_a_w"], CONV_K, name=nm("conv_a"))
    a_par = [_row(w["conv_a_b"]), _row(w["ln_a_g"]), _row(w["ln_a_b"])]
    a_out = _rw(f_lnsilu, [(ac, MIX_W, 0)], a_par, [(MIX_W, BF16)], name=nm("ln_a"))[0]
    qc = _conv_fwd(z, Z_QKV // LANES, 3 * MIX_W, w["dn_conv_w"], DN_CONV_K, name=nm("conv_dn"))
    dn_par = [_lane_pad(w["dn_a_log"]), _lane_pad(w["dn_dt_bias"])]
    qn, kn, v, bb, gb = _rw(f_dnprep, [(qc, 3 * MIX_W, 0), (z, LANES, Z_BD // LANES)], dn_par,
                            [(MIX_W, F32)] * 5, name=nm("dn_prep"))
    o, states = _dn_fwd(qn, kn, v, bb, gb, name=nm("dn_scan"))
    o_out = _rw(f_dnout, [(o, MIX_W, 0), (z, MIX_W, Z_DG // MIX_W)], [_row(w["dn_norm_g"])], [(MIX_W, BF16)],
                name=nm("dn_out"))[0]
    gm_par = [_row(w["gm_ln_g"]), _row(w["gm_ln_b"]), w["gm_ws"].reshape(4 * GM_CHUNK, GM_CHUNK), w["gm_bs"].T]
    c_out = _rw(f_gmlp, [(z, 2 * MIX_W, Z_GM // (2 * MIX_W))], gm_par, [(MIX_W, BF16)], name=nm("gmlp"))[0]
    pc = _conv_fwd(z, Z_POOL // LANES, MIX_W, None, POOL_K, pool=True, name=nm("pool"))
    p_par = [w["pool_w"].reshape(4 * LANES, LANES), _row(w["pool_scale"])]
    p_out = _rw(f_poolpost, [(pc, MIX_W, 0)], p_par, [(MIX_W, BF16)], name=nm("pool_post"))[0]
    branches = [a_out, o_out, c_out, p_out]
    proj = [_mm(br, w["w_branch"][n], name=nm(f"branch{n}")) for n, br in enumerate(branches)]
    merged = _rw(f_merge, [(z, N_BRANCH * D_MODEL, 0)] + [(p, D_MODEL, 0) for p in proj], [], [(D_MODEL, BF16)],
                 name=nm("merge"), tr=128)[0]
    x1 = _mm(merged, w["w_out"], add=x, name=nm("out_proj"))
    sv.update(h=h, z=z, a_glu=a_glu, ac=ac, qc=qc, qn=qn, kn=kn, v=v, bb=bb, gb=gb, o=o, states=states, pc=pc,
              branches=branches, proj=proj, merged=merged, x1=x1)
    h2 = _rw(f_norm, [(x1, D_MODEL, 0)], [_row(w["norm_xa"])], [(D_MODEL, BF16)], name=nm("norm_xa"))[0]
    mn = _rw(f_norm, [(mem, D_MODEL, 0)], [_row(w["norm_mem"])], [(D_MODEL, BF16)], name=nm("norm_mem"))[0]
    kv = _mm(mn, w["xa_wkv"], name=nm("xa_kv"))
    q = _mm(h2, w["xa_wq"], name=nm("xa_q"))
    att = _rw(f_attn, [(q, D_MODEL, 0)], [kv], [(D_MODEL, BF16)], name=nm("xa_attn"))[0]
    x2 = _mm(att, w["xa_wo"], add=x1, name=nm("xa_o"))
    sv.update(h2=h2, mn=mn, kv=kv, q=q, att=att, x2=x2)
    h3 = _rw(f_norm, [(x2, D_MODEL, 0)], [_row(w["norm_mlp"])], [(D_MODEL, BF16)], name=nm("norm_mlp"))[0]
    u = _mm(h3, w["mlp_w1"], name=nm("mlp_up"))
    act = _rw(f_relu2, [(u, FFN_W, 0)], [], [(FFN_W, BF16)], name=nm("relu2"))[0]
    x3 = _mm(act, w["mlp_w2"], add=x2, name=nm("mlp_down"))
    sv.update(h3=h3, u=u, act=act)
    return x3, sv


def _layer_bwd(dx, mem, w, sv, l, after_attention):
    nm = lambda s: f"{s}_bwd_l{l}"
    s = dx.shape[0]
    g = {}
    dact = _mm(dx, w["mlp_w2"], tb=True, name=nm("mlp_down_dx"))
    g["mlp_w2"] = _mm(sv["act"], dx, ta=True, out_dtype=BF16, name=nm("mlp_down_dw"))
    du = _rw_bwd(f_relu2, [(sv["u"], FFN_W, 0)], [], [(dact, FFN_W, 0)], row_grads=[(0, BF16)], name=nm("relu2"))[0]
    g["mlp_w1"] = _mm(sv["h3"], du, ta=True, out_dtype=BF16, name=nm("mlp_up_dw"))
    dh3 = _mm(du, w["mlp_w1"], tb=True, name=nm("mlp_up_dx"))
    dx2, g["norm_mlp"] = _rw_bwd(f_norm, [(sv["x2"], D_MODEL, 0)], [_row(w["norm_mlp"])], [(dh3, D_MODEL, 0)],
                                 row_grads=[(0, F32)], param_grads=[0], add={0: (dx, D_MODEL, 0)}, name=nm("norm_mlp"))
    datt = _mm(dx2, w["xa_wo"], tb=True, name=nm("xa_o_dx"))
    g["xa_wo"] = _mm(sv["att"], dx2, ta=True, out_dtype=BF16, name=nm("xa_o_dw"))
    dq, dkv = _rw_bwd(f_attn, [(sv["q"], D_MODEL, 0)], [sv["kv"]], [(datt, D_MODEL, 0)], row_grads=[(0, BF16)],
                      param_grads=[0], name=nm("xa_attn"))
    g["xa_wq"] = _mm(sv["h2"], dq, ta=True, out_dtype=BF16, name=nm("xa_q_dw"))
    dh2 = _mm(dq, w["xa_wq"], tb=True, name=nm("xa_q_dx"))
    g["xa_wkv"] = _mm(sv["mn"], dkv, ta=True, out_dtype=BF16, name=nm("xa_kv_dw"))
    dmn = _mm(dkv, w["xa_wkv"], tb=True, name=nm("xa_kv_dx"))
    g["norm_mem"] = _rw_bwd(f_norm, [(mem, D_MODEL, 0)], [_row(w["norm_mem"])], [(dmn, D_MODEL, 0)], row_grads=[],
                            param_grads=[0], name=nm("norm_mem"))[0]
    dx1, g["norm_xa"] = _rw_bwd(f_norm, [(sv["x1"], D_MODEL, 0)], [_row(w["norm_xa"])], [(dh2, D_MODEL, 0)],
                                row_grads=[(0, F32)], param_grads=[0], add={0: (dx2, D_MODEL, 0)}, name=nm("norm_xa"))
    dx1 = dx1 + after_attention(g)
    z = sv["z"]
    dmerged = _mm(dx1, w["w_out"], tb=True, name=nm("out_proj_dx"))
    g["w_out"] = _mm(sv["merged"], dx1, ta=True, out_dtype=BF16, name=nm("out_proj_dw"))
    mg = _rw_bwd(f_merge, [(z, N_BRANCH * D_MODEL, 0)] + [(p, D_MODEL, 0) for p in sv["proj"]], [],
                 [(dmerged, D_MODEL, 0)], row_grads=[(i, BF16) for i in range(5)], name=nm("merge"), tr=128)
    dgate, dproj = mg[0], mg[1:]
    g["w_branch"] = jnp.stack([_mm(br, dp, ta=True, out_dtype=BF16, name=nm(f"branch{n}_dw"))
                               for n, (br, dp) in enumerate(zip(sv["branches"], dproj))])
    dbr = [_mm(dp, w["w_branch"][n], tb=True, name=nm(f"branch{n}_dx")) for n, dp in enumerate(dproj)]
    p_par = [w["pool_w"].reshape(4 * LANES, LANES), _row(w["pool_scale"])]
    dpc, dpw, g["pool_scale"] = _rw_bwd(f_poolpost, [(sv["pc"], MIX_W, 0)], p_par, [(dbr[3], MIX_W, 0)],
                                        row_grads=[(0, F32)], param_grads=[0, 1], name=nm("pool_post"))
    g["pool_w"] = dpw.reshape(4, LANES, LANES)
    dpool = _conv_bwd(None, 0, MIX_W, None, POOL_K, dpc, pool=True, name=nm("pool"))
    gm_par = [_row(w["gm_ln_g"]), _row(w["gm_ln_b"]), w["gm_ws"].reshape(4 * GM_CHUNK, GM_CHUNK), w["gm_bs"].T]
    dgm, g["gm_ln_g"], g["gm_ln_b"], dws, dbst = _rw_bwd(
        f_gmlp, [(z, 2 * MIX_W, Z_GM // (2 * MIX_W))], gm_par, [(dbr[2], MIX_W, 0)], row_grads=[(0, BF16)],
        param_grads=[0, 1, 2, 3], name=nm("gmlp"))
    g["gm_ws"] = dws.reshape(4, GM_CHUNK, GM_CHUNK)
    g["gm_bs"] = dbst.T
    do, ddg, g["dn_norm_g"] = _rw_bwd(f_dnout, [(sv["o"], MIX_W, 0), (z, MIX_W, Z_DG // MIX_W)], [_row(w["dn_norm_g"])],
                                      [(dbr[1], MIX_W, 0)], row_grads=[(0, F32), (1, BF16)], param_grads=[0],
                                      name=nm("dn_out"))
    dqn, dkn, dv, dbb, dgb = _dn_bwd(sv["qn"], sv["kn"], sv["v"], sv["bb"], sv["gb"], sv["states"], do, name=nm("dn_scan"))
    dn_par = [_lane_pad(w["dn_a_log"]), _lane_pad(w["dn_dt_bias"])]
    dqc, dbd, dal, ddt = _rw_bwd(
        f_dnprep, [(sv["qc"], 3 * MIX_W, 0), (z, LANES, Z_BD // LANES)], dn_par,
        [(t, MIX_W, 0) for t in (dqn, dkn, dv, dbb, dgb)], row_grads=[(0, F32), (1, BF16)], param_grads=[0, 1],
        name=nm("dn_prep"))
    g["dn_a_log"], g["dn_dt_bias"] = dal[0, :HEADS], ddt[0, :HEADS]
    dqkv, g["dn_conv_w"] = _conv_bwd(z, Z_QKV // LANES, 3 * MIX_W, w["dn_conv_w"], DN_CONV_K, dqc, name=nm("conv_dn"))
    a_par = [_row(w["conv_a_b"]), _row(w["ln_a_g"]), _row(w["ln_a_b"])]
    dac, g["conv_a_b"], g["ln_a_g"], g["ln_a_b"] = _rw_bwd(
        f_lnsilu, [(sv["ac"], MIX_W, 0)], a_par, [(dbr[0], MIX_W, 0)], row_grads=[(0, F32)], param_grads=[0, 1, 2],
        name=nm("ln_a"))
    dglu, g["conv_a_w"] = _conv_bwd(sv["a_glu"], 0, MIX_W, w["conv_a_w"], CONV_K, dac, dx_dtype=F32, name=nm("conv_a"))
    da_in = _rw_bwd(f_glu, [(z, 2 * MIX_W, Z_A // (2 * MIX_W))], [], [(dglu, MIX_W, 0)], row_grads=[(0, BF16)],
                    name=nm("glu"))[0]
    dz = jnp.concatenate([dgate, da_in, dgm, dqkv, ddg, dpool, dbd, jnp.zeros((s, Z_W - Z_BD - LANES), BF16)], axis=1)
    g["w_in"] = _ungroup_w_in(_mm(sv["h"], dz, ta=True, out_dtype=BF16, name=nm("in_proj_dw")))
    dh = _mm(dz, w["w_in_z"], tb=True, name=nm("in_proj_dx"))
    dx0, g["norm_mix"] = _rw_bwd(f_norm, [(sv["x0"], D_MODEL, 0)], [_row(w["norm_mix"])], [(dh, D_MODEL, 0)],
                                 row_grads=[(0, F32)], param_grads=[0], add={0: (dx1, D_MODEL, 0)}, name=nm("norm_mix"))
    for n in ("norm_mlp", "norm_xa", "norm_mem", "norm_mix", "pool_scale", "gm_ln_g", "gm_ln_b", "dn_norm_g",
              "conv_a_b", "ln_a_g", "ln_a_b"):
        g[n] = g[n].reshape(-1)
    return dx0, g


def _shard_slice(a, axis, j):
    n = a.shape[axis] // 4
    return lax.slice_in_dim(a, j * n, (j + 1) * n, axis=axis)


def kernel(x, mem, norm_mix, w_in, conv_a_w, conv_a_b, ln_a_g, ln_a_b, dn_conv_w, dn_a_log, dn_dt_bias, dn_norm_g, gm_ln_g, gm_ln_b, gm_ws, gm_bs, pool_w, pool_scale, w_branch, w_out, norm_xa, norm_mem, xa_wq, xa_wkv, xa_wo, norm_mlp, mlp_w1, mlp_w2, norm_f, loss_target, m_norm_mix, m_w_in, m_conv_a_w, m_conv_a_b, m_ln_a_g, m_ln_a_b, m_dn_conv_w, m_dn_a_log, m_dn_dt_bias, m_dn_norm_g, m_gm_ln_g, m_gm_ln_b, m_gm_ws, m_gm_bs, m_pool_w, m_pool_scale, m_w_branch, m_w_out, m_norm_xa, m_norm_mem, m_xa_wq, m_xa_wkv, m_xa_wo, m_norm_mlp, m_mlp_w1, m_mlp_w2, m_norm_f, v_norm_mix, v_w_in, v_conv_a_w, v_conv_a_b, v_ln_a_g, v_ln_a_b, v_dn_conv_w, v_dn_a_log, v_dn_dt_bias, v_dn_norm_g, v_gm_ln_g, v_gm_ln_b, v_gm_ws, v_gm_bs, v_pool_w, v_pool_scale, v_w_branch, v_w_out, v_norm_xa, v_norm_mem, v_xa_wq, v_xa_wkv, v_xa_wo, v_norm_mlp, v_mlp_w1, v_mlp_w2, v_norm_f):
    given = dict(locals())
    wts = {n: given[n] for n in WEIGHTS}
    depth = norm_mix.shape[0]
    x = x[0]
    mem = mem[0]
    tgt = loss_target[0]

    me = 2 * lax.axis_index("x") + lax.axis_index("y")
    sharded = BIG + CONVS
    shard_axis = dict(BIG_AXIS, conv_a_w=1, dn_conv_w=1)

    def shards(l):
        return [wts[n][l].astype(BF16) for n in BIG] + [wts[n][l] for n in CONVS]

    def assemble(l, waited):
        mine, landed = waited
        w = {}
        for n, own, got in zip(sharded, mine, landed):
            got = lax.dynamic_update_index_in_dim(got, own[None], me, 0)
            w[n] = jnp.concatenate([got[j] for j in range(4)], axis=shard_axis[n])
        w["w_in_z"] = _regroup_w_in(w.pop("w_in"))
        for n in SMALL:
            if n != "norm_f" and n not in CONVS:
                w[n] = wts[n][l]
        return w

    saved, layer_w = [], []
    gather = _xy_start(shards(0), broadcast=True, name="gather_start_l0")
    after = gather[4]
    for l in range(depth):
        waited = _xy_wait(gather, after, broadcast=True, name=f"gather_wait_l{l}")
        if l + 1 < depth:
            waited, nxt = lax.optimization_barrier((waited, shards(l + 1)))
            gather = _xy_start(nxt, broadcast=True, name=f"gather_start_l{l + 1}")
            x = x + gather[4][0, 0]
        w = assemble(l, waited)
        x, sv = _layer_fwd(x, mem, w, l)
        after = x
        saved.append(sv)
        layer_w.append(w)
    dx, g_norm_f, loss_rows = _rw_bwd(
        f_loss, [(x, D_MODEL, 0), (tgt, D_MODEL, 0)], [_row(norm_f)], [(jnp.ones((x.shape[0], 1), F32), 1, 0)],
        row_grads=[(0, F32)], param_grads=[0], primal=[(0, 1, F32)], name="loss_head")
    loss = lax.psum(jnp.sum(loss_rows), ("x", "y", "c"))

    early = ("mlp_w1", "mlp_w2", "xa_wq", "xa_wkv", "xa_wo")
    late = ("w_in", "w_branch", "w_out")
    grads, reduces = [None] * depth, []

    def start_reduce(g, names, tag):
        blocks = [jnp.stack([_shard_slice(g[n], BIG_AXIS[n], j) for j in range(4)]).astype(BF16) for n in names]
        reduces.append((tag, names, _xy_start(blocks, broadcast=False, name=f"reduce_start_{tag}")))
        return reduces[-1][2][4][0, 0]

    for l in reversed(range(depth)):
        dx, grads[l] = _layer_bwd(dx, mem, layer_w[l], saved[l], l,
                                  functools.partial(start_reduce, names=early, tag=f"l{l}a"))
        saved[l] = None
        dx = dx + start_reduce(grads[l], late, f"l{l}b")
    grad_x = dx[None]

    parts, keys = [], []
    for tag, names, started_reduce in reduces:
        mine, landed = _xy_wait(started_reduce, dx, broadcast=False, name=f"reduce_wait_{tag}")
        for n, own, got in zip(names, mine, landed):
            got = lax.dynamic_update_index_in_dim(got, lax.dynamic_index_in_dim(own, me, 0), me, 0)
            g3 = got.reshape(4, -1, got.shape[-1])
            parts.append(_ew(f_sum4, [(g3, j) for j in range(4)], [F32], name=f"reduce_sum4_{tag}_{n}")[0])
            keys.append((n, int(tag[1:-1])))
    others = _c_swap(parts, name="reduce_big_c")
    sums = {key: _ew(f_add2, [p, o], [F32], name=f"reduce_add2_{key[0]}_l{key[1]}")[0]
            for key, p, o in zip(keys, parts, others)}
    gw = {n: jnp.stack([sums[n, l].reshape(wts[n].shape[1:]) for l in range(depth)]) for n in BIG}

    small_names = [n for n in SMALL if n != "norm_f"]
    small_shapes = [(depth,) + (wts[n].shape[1:2] + (4 * wts[n].shape[2],) if n in CONVS else wts[n].shape[1:])
                    for n in small_names]
    small_buf = _flat_rows([jnp.stack([grads[l][n] for l in range(depth)]) for n in small_names] + [g_norm_f], F32,
                           row_multiple=1024)
    small_sum = _all_sum([small_buf], broadcast=True, name="reduce_small")[0]
    small_parts = _unflat(small_sum, small_shapes + [norm_f.shape])
    me = 2 * lax.axis_index("x") + lax.axis_index("y")
    for n, t in zip(small_names + ["norm_f"], small_parts):
        if n in CONVS:
            width = wts[n].shape[2]
            t = lax.dynamic_slice_in_dim(t, me * width, width, axis=2)
        gw[n] = t

    deltas, new_m, new_v = {}, {}, {}
    for n in WEIGHTS:
        deltas[n], new_m[n], new_v[n] = _adamw(wts[n], gw[n], given["m_" + n], given["v_" + n], name=f"adamw_{n}")
    return (loss, grad_x, *[gw[n] for n in WEIGHTS], *[deltas[n] for n in WEIGHTS], *[new_m[n] for n in WEIGHTS],
            *[new_v[n] for n in WEIGHTS])
```

```python
import functools

import jax
import jax.numpy as jnp
from jax import lax
from jax.experimental import pallas as pl
from jax.experimental.pallas import tpu as pltpu

F32 = jnp.float32
BF16 = jnp.bfloat16
HI = lax.Precision.HIGHEST

D_MODEL = 1024
MIX_W = 512
N_BRANCH = 4
HEADS = 4
HEAD_DIM = 128
CONV_K = 31
DN_CONV_K = 4
DN_CHUNK = 64
GM_CHUNK = 128
POOL_K = 16
XA_HEADS = 4
XA_HEAD_DIM = 256
FFN_W = 4096
IN_W = 8712
Z_W = 9216
LANES = 128
VMEM_LIMIT = 56 * 1024 * 1024

ADAM_LR, ADAM_B1, ADAM_B2, ADAM_EPS, ADAM_WD, ADAM_STEP = 0.001, 0.9, 0.999, 1e-08, 0.01, 10

Z_GATE, Z_A, Z_GM, Z_QKV, Z_DG, Z_POOL, Z_BD = 0, 4096, 5120, 6144, 7680, 8192, 8704
_Z_SEGS = ((4616, 8712), (0, 1024), (3080, 4104), (1024, 2560), (2560, 3072), (4104, 4616), (3072, 3080))

BIG = ("w_in", "w_branch", "w_out", "xa_wq", "xa_wkv", "xa_wo", "mlp_w1", "mlp_w2")
BIG_AXIS = {"w_in": 1, "w_branch": 2, "w_out": 0, "xa_wq": 0, "xa_wkv": 1, "xa_wo": 0, "mlp_w1": 1, "mlp_w2": 0}
CONVS = ("conv_a_w", "dn_conv_w")
WEIGHTS = ("norm_mix", "w_in", "conv_a_w", "conv_a_b", "ln_a_g", "ln_a_b", "dn_conv_w", "dn_a_log", "dn_dt_bias",
           "dn_norm_g", "gm_ln_g", "gm_ln_b", "gm_ws", "gm_bs", "pool_w", "pool_scale", "w_branch", "w_out",
           "norm_xa", "norm_mem", "xa_wq", "xa_wkv", "xa_wo", "norm_mlp", "mlp_w1", "mlp_w2", "norm_f")
SMALL = tuple(n for n in WEIGHTS if n not in BIG)


def _params(sem=None):
    return pltpu.CompilerParams(vmem_limit_bytes=VMEM_LIMIT, dimension_semantics=sem)


def _pick(n, cands):
    for c in cands:
        if n % c == 0:
            return c
    return n


def _make_dots(prec):
    def raw(a, b, ca, cb):
        if prec is None:
            a, b = a.astype(BF16), b.astype(BF16)
        return lax.dot_general(a, b, (((ca,), (cb,)), ((), ())), precision=prec, preferred_element_type=F32)

    nn = jax.custom_vjp(lambda a, b: raw(a, b, 1, 0))
    nt = jax.custom_vjp(lambda a, b: raw(a, b, 1, 1))
    tn = jax.custom_vjp(lambda a, b: raw(a, b, 0, 0))
    nn.defvjp(lambda a, b: (raw(a, b, 1, 0), (a, b)), lambda r, g: (raw(g, r[1], 1, 1), raw(r[0], g, 0, 0)))
    nt.defvjp(lambda a, b: (raw(a, b, 1, 1), (a, b)), lambda r, g: (raw(g, r[1], 1, 0), raw(g, r[0], 0, 0)))
    tn.defvjp(lambda a, b: (raw(a, b, 0, 0), (a, b)), lambda r, g: (raw(r[1], g, 1, 1), raw(r[0], g, 1, 0)))
    return nn, nt, tn


_bnn, _bnt, _btn = _make_dots(None)
_hnn, _hnt, _htn = _make_dots(HI)
_mnn, _mnt, _mtn = _make_dots(lax.Precision.HIGH)


def _col(x, j):
    lane = lax.broadcasted_iota(jnp.int32, x.shape, 1)
    return jnp.sum(jnp.where(lane == j, x, 0.0), axis=-1, keepdims=True)


def _rms(x, g, eps=1e-6):
    return x * lax.rsqrt(jnp.mean(x * x, axis=-1, keepdims=True) + eps) * g


def _ln(x, g, b, eps=1e-5):
    xc = x - jnp.mean(x, axis=-1, keepdims=True)
    var = jnp.mean(xc * xc, axis=-1, keepdims=True)
    return xc * lax.rsqrt(var + eps) * g + b


_sigmoid = jax.nn.sigmoid


def _silu(x):
    return x * _sigmoid(x)


def _softplus(x):
    return jnp.maximum(x, 0.0) + jnp.log1p(jnp.exp(-jnp.abs(x)))


def _gelu(x):
    return 0.5 * x * (1.0 + lax.erf(x * 0.7071067811865476))


def f_norm(x, g):
    return (_rms(x, g),)


def f_glu(a_in):
    return (a_in[:, :MIX_W] * _sigmoid(a_in[:, MIX_W:]),)


def f_lnsilu(ac, cb, g, b):
    return (_silu(_ln(ac + cb, g, b)),)


def f_dnprep(qc, bd, a_log, dt_bias):
    t = qc.shape[0]
    qkv = _silu(qc)
    qs, ks, bs, gs = [], [], [], []
    for h in range(HEADS):
        q = qkv[:, h * HEAD_DIM:(h + 1) * HEAD_DIM]
        k = qkv[:, MIX_W + h * HEAD_DIM:MIX_W + (h + 1) * HEAD_DIM]
        qs.append(q * lax.rsqrt(jnp.sum(q * q, axis=-1, keepdims=True) + 1e-6) * (HEAD_DIM ** -0.5))
        ks.append(k * lax.rsqrt(jnp.sum(k * k, axis=-1, keepdims=True) + 1e-6))
        beta = _sigmoid(_col(bd, h))
        g = -jnp.exp(_col(a_log, h)) * _softplus(_col(bd, HEADS + h) + _col(dt_bias, h))
        bs.append(jnp.broadcast_to(beta, (t, HEAD_DIM)))
        gs.append(jnp.broadcast_to(g, (t, HEAD_DIM)))
    cat = lambda xs: jnp.concatenate(xs, axis=1)
    return cat(qs), cat(ks), qkv[:, 2 * MIX_W:], cat(bs), cat(gs)


def f_dnout(o, dgate, g):
    outs = []
    for h in range(HEADS):
        sl = slice(h * HEAD_DIM, (h + 1) * HEAD_DIM)
        outs.append(_rms(o[:, sl], g) * _silu(dgate[:, sl]))
    return (jnp.concatenate(outs, axis=1),)


def f_gmlp(gm_in, lg, lb, ws, bst):
    t = gm_in.shape[0]
    ge = _gelu(gm_in)
    u, vg = ge[:, :MIX_W], _ln(ge[:, MIX_W:], lg, lb)
    ri = lax.broadcasted_iota(jnp.int32, (GM_CHUNK, GM_CHUNK), 0)
    ci = lax.broadcasted_iota(jnp.int32, (GM_CHUNK, GM_CHUNK), 1)
    chunks = []
    for n in range(t // GM_CHUNK):
        vc = vg[n * GM_CHUNK:(n + 1) * GM_CHUNK]
        cols = []
        for g in range(4):
            w = jnp.where(ri >= ci, ws[g * GM_CHUNK:(g + 1) * GM_CHUNK], 0.0)
            cols.append(_bnn(w, vc[:, g * LANES:(g + 1) * LANES]) + _col(bst, g))
        chunks.append(jnp.concatenate(cols, axis=1))
    mixed = chunks[0] if len(chunks) == 1 else jnp.concatenate(chunks, axis=0)
    return (u * mixed,)


def f_poolpost(pc, pw, ps):
    cols = [_bnn(pc[:, g * LANES:(g + 1) * LANES], pw[g * LANES:(g + 1) * LANES]) for g in range(4)]
    return (jnp.concatenate(cols, axis=1) * ps,)


def f_merge(gate, p0, p1, p2, p3):
    m = None
    for n, p in enumerate((p0, p1, p2, p3)):
        t = _sigmoid(gate[:, n * D_MODEL:(n + 1) * D_MODEL]) * p
        m = t if m is None else m + t
    return (m,)


def f_attn(q, kv):
    outs = []
    for h in range(XA_HEADS):
        sl = slice(h * XA_HEAD_DIM, (h + 1) * XA_HEAD_DIM)
        s = _bnt(q[:, sl], kv[:, sl]) * (XA_HEAD_DIM ** -0.5)
        s = s - lax.stop_gradient(jnp.max(s, axis=-1, keepdims=True))
        p = jnp.exp(s)
        p = p / jnp.sum(p, axis=-1, keepdims=True)
        outs.append(_bnn(p, kv[:, D_MODEL + h * XA_HEAD_DIM:D_MODEL + (h + 1) * XA_HEAD_DIM]))
    return (jnp.concatenate(outs, axis=1),)


def f_loss(x, t, g):
    e = _rms(x, g) - t
    return (0.5 * jnp.mean(e * e, axis=-1, keepdims=True),)


def _row_spec(tr, width, cb):
    return pl.BlockSpec((tr, width), functools.partial(lambda i, cb: (i, cb), cb=cb))


def _full_spec(shape):
    return pl.BlockSpec(shape, lambda i: (0,) * len(shape))


def _rw(f, rows, params, outs, *, name, tr=256):
    s = rows[0][0].shape[0]
    tr = min(tr, s)
    nr, npar = len(rows), len(params)

    def body(*refs):
        rv = [r[...].astype(F32) for r in refs[:nr]]
        pv = [p[...] for p in refs[nr:nr + npar]]
        res = f(*rv, *pv)
        for o_ref, r in zip(refs[nr + npar:], res):
            o_ref[...] = r.astype(o_ref.dtype)

    return pl.pallas_call(
        body, name=name, grid=(s // tr,),
        in_specs=[_row_spec(tr, w, cb) for _, w, cb in rows] + [_full_spec(p.shape) for p in params],
        out_specs=[_row_spec(tr, w, 0) for w, _ in outs],
        out_shape=[jax.ShapeDtypeStruct((s, w), dt) for w, dt in outs],
        compiler_params=_params(("arbitrary",)),
    )(*[a for a, _, _ in rows], *params)


def _rw_bwd(f, rows, params, cts, *, row_grads, param_grads=(), add=None, primal=(), name, tr=256):
    s = rows[0][0].shape[0]
    tr = min(tr, s)
    nr, npar, nct = len(rows), len(params), len(cts)
    add = add or {}
    add_keys = list(add)

    def body(*refs):
        it = iter(refs)
        row_refs = [next(it) for _ in range(nr)]
        par_refs = [next(it) for _ in range(npar)]
        ct_refs = [next(it) for _ in range(nct)]
        add_refs = {k: next(it) for k in add_keys}
        rg_refs = [next(it) for _ in row_grads]
        pg_refs = [next(it) for _ in param_grads]
        pr_refs = [next(it) for _ in primal]
        rv = [r[...].astype(F32) for r in row_refs]
        pv = [p[...] for p in par_refs]
        out, vjp = jax.vjp(f, *rv, *pv)
        grads = vjp(tuple(c[...].astype(F32) for c in ct_refs))
        for (idx, _), ref in zip(row_grads, rg_refs):
            g = grads[idx]
            if idx in add_refs:
                g = g + add_refs[idx][...].astype(F32)
            ref[...] = g.astype(ref.dtype)

        @pl.when(pl.program_id(0) == 0)
        def _():
            for ref in pg_refs:
                ref[...] = jnp.zeros_like(ref)

        for idx, ref in zip(param_grads, pg_refs):
            ref[...] += grads[nr + idx]
        for (idx, _, _), ref in zip(primal, pr_refs):
            ref[...] = out[idx].astype(ref.dtype)

    in_arrays = [a for a, _, _ in rows] + list(params) + [a for a, _, _ in cts] + [add[k][0] for k in add_keys]
    in_specs = ([_row_spec(tr, w, cb) for _, w, cb in rows] + [_full_spec(p.shape) for p in params]
                + [_row_spec(tr, w, cb) for _, w, cb in cts] + [_row_spec(tr, add[k][1], add[k][2]) for k in add_keys])
    out_specs = ([_row_spec(tr, rows[idx][1], 0) for idx, _ in row_grads]
                 + [_full_spec(params[idx].shape) for idx in param_grads]
                 + [_row_spec(tr, w, 0) for _, w, _ in primal])
    out_shape = ([jax.ShapeDtypeStruct((s, rows[idx][1]), dt) for idx, dt in row_grads]
                 + [jax.ShapeDtypeStruct(params[idx].shape, F32) for idx in param_grads]
                 + [jax.ShapeDtypeStruct((s, w), dt) for _, w, dt in primal])
    return pl.pallas_call(
        body, name=name, grid=(s // tr,), in_specs=in_specs, out_specs=out_specs, out_shape=out_shape,
        compiler_params=_params(("arbitrary",)),
    )(*in_arrays)


def _mm(a, b, *, ta=False, tb=False, out_dtype=F32, add=None, post=None, extra=(), out_dtypes=None, name):
    m, k = (a.shape[1], a.shape[0]) if ta else a.shape
    n = b.shape[0] if tb else b.shape[1]
    tm = _pick(m, (1024, 512, 256, 128))
    tn = _pick(n, (1024, 512, 256, 128))
    tk = _pick(k, (1024, 512, 256, 128))
    nk = k // tk
    dims = (((0 if ta else 1,), (1 if tb else 0,)), ((), ()))
    if post is None:
        extra = [add] if add is not None else []
        post = (lambda r, e: (r + e,)) if add is not None else (lambda r: (r,))
    dtypes = out_dtypes or [out_dtype]
    n_ex, n_out = len(extra), len(dtypes)

    def body(*refs):
        a_ref, b_ref = refs[:2]
        ex_refs = refs[2:2 + n_ex]
        o_refs = refs[2 + n_ex:2 + n_ex + n_out]
        part = lax.dot_general(a_ref[...].astype(BF16), b_ref[...].astype(BF16), dims, preferred_element_type=F32)

        def finish(r):
            for o_ref, val in zip(o_refs, post(r, *[e[...] for e in ex_refs])):
                o_ref[...] = val.astype(o_ref.dtype)

        if nk == 1:
            finish(part)
            return
        acc = refs[-1]
        kk = pl.program_id(2)

        @pl.when(kk == 0)
        def _():
            acc[...] = part

        @pl.when((kk > 0) & (kk < nk - 1))
        def _():
            acc[...] += part

        @pl.when(kk == nk - 1)
        def _():
            finish(acc[...] + part)

    a_spec = (pl.BlockSpec((tk, tm), lambda i, j, kk: (kk, i)) if ta else pl.BlockSpec((tm, tk), lambda i, j, kk: (i, kk)))
    b_spec = (pl.BlockSpec((tn, tk), lambda i, j, kk: (j, kk)) if tb else pl.BlockSpec((tk, tn), lambda i, j, kk: (kk, j)))
    o_spec = pl.BlockSpec((tm, tn), lambda i, j, kk: (i, j))
    res = pl.pallas_call(
        body, name=name, grid=(m // tm, n // tn, nk), in_specs=[a_spec, b_spec] + [o_spec] * n_ex,
        out_specs=[o_spec] * n_out, out_shape=[jax.ShapeDtypeStruct((m, n), dt) for dt in dtypes],
        scratch_shapes=[pltpu.VMEM((tm, tn), F32)] if nk > 1 else [],
        compiler_params=_params(("arbitrary", "arbitrary", "arbitrary")),
    )(a, b, *extra)
    return res if out_dtypes else res[0]


CONV_TT = 1024
CONV_SUB = 256


def _halo(k):
    return -(-(k - 1) // 8) * 8


def _pool_count(row0, c, rows):
    win = lax.shift_left(jnp.int32(2), c)
    t = row0 + lax.broadcasted_iota(jnp.int32, (rows, LANES), 0)
    return win, jnp.minimum(t + 1, win).astype(F32)


def _conv_fwd(x, xcb0, channels, w, k, *, pool=False, name):
    s = x.shape[0]
    tt = min(CONV_TT, s)
    sub = min(CONV_SUB, tt)
    nt = s // tt
    halo = _halo(k)

    def body(*refs):
        if pool:
            xc_ref, xp_ref, o_ref, xs = refs
        else:
            xc_ref, xp_ref, w_ref, o_ref, xs = refs
        c, i = pl.program_id(0), pl.program_id(1)
        xs[0:halo, :] = jnp.where(i > 0, xp_ref[...], 0.0)
        xs[halo:halo + tt, :] = xc_ref[...]
        for r0 in range(0, tt, sub):
            acc = jnp.zeros((sub, LANES), F32)
            if pool:
                win, cnt = _pool_count(i * tt + r0, c, sub)
                for j in range(k):
                    acc = acc + jnp.where(j >= k - win, xs[pl.ds(r0 + halo - (k - 1) + j, sub), :], 0.0)
                o_ref[r0:r0 + sub, :] = acc / cnt - xc_ref[r0:r0 + sub, :]
            else:
                for j in range(k):
                    acc = acc + w_ref[j:j + 1, :] * xs[pl.ds(r0 + halo - (k - 1) + j, sub), :]
                o_ref[r0:r0 + sub, :] = acc

    per = tt // halo
    in_specs = [pl.BlockSpec((tt, LANES), lambda c, i: (i, xcb0 + c)),
                pl.BlockSpec((halo, LANES), lambda c, i: (jnp.maximum(i * per - 1, 0), xcb0 + c))]
    args = [x, x]
    if not pool:
        in_specs.append(pl.BlockSpec((k, LANES), lambda c, i: (0, c)))
        args.append(w)
    return pl.pallas_call(
        body, name=name, grid=(channels // LANES, nt), in_specs=in_specs,
        out_specs=pl.BlockSpec((tt, LANES), lambda c, i: (i, c)),
        out_shape=jax.ShapeDtypeStruct((s, channels), F32),
        scratch_shapes=[pltpu.VMEM((halo + tt, LANES), F32)],
        compiler_params=_params(("arbitrary", "arbitrary")),
    )(*args)


def _conv_bwd(x, xcb0, channels, w, k, dy, *, pool=False, dx_dtype=BF16, name):
    s = dy.shape[0]
    tt = min(CONV_TT, s)
    sub = min(CONV_SUB, tt)
    nt = s // tt
    halo = _halo(k)

    def body(*refs):
        if pool:
            dyc_ref, dyn_ref, dx_ref, ys = refs
        else:
            xc_ref, xp_ref, dyc_ref, dyn_ref, w_ref, dx_ref, dw_ref, xs, ys = refs
        c, i = pl.program_id(0), pl.program_id(1)
        dyn = jnp.where(i < nt - 1, dyn_ref[...], 0.0)
        if pool:
            win, cnt = _pool_count(i * tt, c, tt)
            ys[0:tt, :] = dyc_ref[...] / cnt
            ys[tt:tt + halo, :] = dyn / win.astype(F32)
            for r0 in range(0, tt, sub):
                acc = jnp.zeros((sub, LANES), F32)
                for j in range(k):
                    acc = acc + jnp.where(j >= k - win, ys[pl.ds(r0 + (k - 1) - j, sub), :], 0.0)
                dx_ref[r0:r0 + sub, :] = (acc - dyc_ref[r0:r0 + sub, :]).astype(dx_ref.dtype)
            return
        ys[0:tt, :] = dyc_ref[...]
        ys[tt:tt + halo, :] = dyn
        xs[0:halo, :] = jnp.where(i > 0, xp_ref[...], 0.0)
        xs[halo:halo + tt, :] = xc_ref[...]

        @pl.when(i == 0)
        def _():
            dw_ref[...] = jnp.zeros_like(dw_ref)

        for r0 in range(0, tt, sub):
            acc = jnp.zeros((sub, LANES), F32)
            for j in range(k):
                acc = acc + w_ref[j:j + 1, :] * ys[pl.ds(r0 + (k - 1) - j, sub), :]
            dx_ref[r0:r0 + sub, :] = acc.astype(dx_ref.dtype)
            dyc = dyc_ref[r0:r0 + sub, :]
            for j in range(k):
                dw_ref[j:j + 1, :] += jnp.sum(dyc * xs[pl.ds(r0 + halo - (k - 1) + j, sub), :], axis=0, keepdims=True)

    per = tt // halo
    cur = lambda cb0: pl.BlockSpec((tt, LANES), lambda c, i: (i, cb0 + c))
    dy_specs = [cur(0), pl.BlockSpec((halo, LANES), lambda c, i: (jnp.minimum((i + 1) * per, s // halo - 1), c))]
    dx_spec = pl.BlockSpec((tt, LANES), lambda c, i: (i, c))
    dx_shape = jax.ShapeDtypeStruct((s, channels), dx_dtype)
    if pool:
        return pl.pallas_call(
            body, name=name, grid=(channels // LANES, nt), in_specs=dy_specs, out_specs=dx_spec, out_shape=dx_shape,
            scratch_shapes=[pltpu.VMEM((tt + halo, LANES), F32)],
            compiler_params=_params(("arbitrary", "arbitrary")),
        )(dy, dy)
    in_specs = [cur(xcb0), pl.BlockSpec((halo, LANES), lambda c, i: (jnp.maximum(i * per - 1, 0), xcb0 + c))] + dy_specs
    in_specs.append(pl.BlockSpec((k, LANES), lambda c, i: (0, c)))
    return pl.pallas_call(
        body, name=name, grid=(channels // LANES, nt), in_specs=in_specs,
        out_specs=[dx_spec, pl.BlockSpec((k, LANES), lambda c, i: (0, c))],
        out_shape=[dx_shape, jax.ShapeDtypeStruct((k, channels), F32)],
        scratch_shapes=[pltpu.VMEM((halo + tt, LANES), F32), pltpu.VMEM((tt + halo, LANES), F32)],
        compiler_params=_params(("arbitrary", "arbitrary")),
    )(x, x, dy, dy, w)


DN_STEP_ROWS = 256


def _dn_step(q, k, v, bb, gb, state):
    c = DN_CHUNK
    r = q.shape[0]
    ri = lax.broadcasted_iota(jnp.int32, (r, r), 0)
    ci = lax.broadcasted_iota(jnp.int32, (r, r), 1)
    same = jnp.bitwise_or(ri, c - 1) == jnp.bitwise_or(ci, c - 1)
    causal = same & (ri >= ci)
    strict = same & (ri > ci)
    eye = jnp.where(ri == ci, 1.0, 0.0)
    gam = _hnn(jnp.where(causal, 1.0, 0.0), gb)
    gam_i = jnp.concatenate([gam] * (r // LANES), axis=1)
    gam_j = gam_i.T
    decay = jnp.where(causal, jnp.exp(jnp.where(causal, gam_i - gam_j, 0.0)), 0.0)
    row = lax.broadcasted_iota(jnp.int32, gam.shape, 0)
    lasts = [jnp.sum(jnp.where(row == n * c + c - 1, gam, 0.0), axis=0, keepdims=True) for n in range(r // c)]
    g_last = jnp.zeros_like(gam)
    for n, gl in enumerate(lasts):
        g_last = jnp.where(jnp.bitwise_or(row, c - 1) == n * c + c - 1, gl, g_last)
    kb = k * bb
    b = -jnp.where(strict, _mnt(kb, k) * decay, 0.0)
    inv = eye + b
    bp = b
    for _ in range(5):
        bp = _mnn(bp, bp)
        inv = inv + _mnn(inv, bp)
    eg = jnp.exp(gam)
    u = _mnn(inv, v * bb)
    w = _mnn(inv, kb * eg)
    attn = _mnt(q, k) * decay
    q_dec = q * eg
    k_dec = k * jnp.exp(g_last - gam)
    v_news, o_inter = [], []
    for n in range(r // c):
        s = slice(n * c, (n + 1) * c)
        v_new = u[s] - _mnn(w[s], state)
        o_inter.append(_mnn(q_dec[s], state))
        state = state * jnp.exp(lasts[n]) + _mtn(k_dec[s], v_new)
        v_news.append(v_new)
    o = jnp.concatenate(o_inter, axis=0) + _mnn(attn, jnp.concatenate(v_news, axis=0))
    return o, state


def _dn_specs(steps, reverse):
    if reverse:
        tile = pl.BlockSpec((DN_STEP_ROWS, HEAD_DIM), lambda h, i: (steps - 1 - i, h))
        st = pl.BlockSpec((1, 1, HEAD_DIM, HEAD_DIM), lambda h, i: (steps - 1 - i, h, 0, 0))
    else:
        tile = pl.BlockSpec((DN_STEP_ROWS, HEAD_DIM), lambda h, i: (i, h))
        st = pl.BlockSpec((1, 1, HEAD_DIM, HEAD_DIM), lambda h, i: (i, h, 0, 0))
    return tile, st


def _dn_fwd(q, k, v, bb, gb, *, name):
    s = q.shape[0]
    steps = s // DN_STEP_ROWS

    def body(q_ref, k_ref, v_ref, b_ref, g_ref, o_ref, st_ref, state):
        @pl.when(pl.program_id(1) == 0)
        def _():
            state[...] = jnp.zeros_like(state)

        st = state[...]
        st_ref[0, 0] = st
        o, st = _dn_step(q_ref[...], k_ref[...], v_ref[...], b_ref[...], g_ref[...], st)
        o_ref[...] = o
        state[...] = st

    tile, stspec = _dn_specs(steps, False)
    return pl.pallas_call(
        body, name=name, grid=(HEADS, steps), in_specs=[tile] * 5, out_specs=[tile, stspec],
        out_shape=[jax.ShapeDtypeStruct((s, MIX_W), F32),
                   jax.ShapeDtypeStruct((steps, HEADS, HEAD_DIM, HEAD_DIM), F32)],
        scratch_shapes=[pltpu.VMEM((HEAD_DIM, HEAD_DIM), F32)],
        compiler_params=_params(("arbitrary", "arbitrary")),
    )(q, k, v, bb, gb)


def _dn_bwd(q, k, v, bb, gb, states, do, *, name):
    s = q.shape[0]
    steps = s // DN_STEP_ROWS

    def body(q_ref, k_ref, v_ref, b_ref, g_ref, st_ref, do_ref, dq_ref, dk_ref, dv_ref, db_ref, dg_ref, dstate):
        @pl.when(pl.program_id(1) == 0)
        def _():
            dstate[...] = jnp.zeros_like(dstate)

        _, vjp = jax.vjp(_dn_step, q_ref[...], k_ref[...], v_ref[...], b_ref[...], g_ref[...], st_ref[0, 0])
        dq, dk, dv, db, dg, dst = vjp((do_ref[...], dstate[...]))
        dq_ref[...] = dq
        dk_ref[...] = dk
        dv_ref[...] = dv
        db_ref[...] = db
        dg_ref[...] = dg
        dstate[...] = dst

    tile, stspec = _dn_specs(steps, True)
    return pl.pallas_call(
        body, name=name, grid=(HEADS, steps), in_specs=[tile] * 5 + [stspec, tile], out_specs=[tile] * 5,
        out_shape=[jax.ShapeDtypeStruct((s, MIX_W), F32)] * 5,
        scratch_shapes=[pltpu.VMEM((HEAD_DIM, HEAD_DIM), F32)],
        compiler_params=_params(("arbitrary", "arbitrary")),
    )(q, k, v, bb, gb, states, do)


EW_TILE_BYTES = 2 << 20


def _ew(f, ins, out_dtypes, *, name):
    r, c = ins[0][0].shape[1:] if isinstance(ins[0], tuple) else ins[0].shape
    row_bytes = -(-c // LANES) * LANES * 4
    tr = r
    if r * row_bytes > EW_TILE_BYTES:
        tr = next((t for t in (4096, 2048, 1024, 512, 256, 128, 64, 32, 16, 8)
                   if r % t == 0 and t * row_bytes <= EW_TILE_BYTES), r)
    n_in = len(ins)

    def body(*refs):
        res = f(*[x[...] for x in refs[:n_in]])
        for o_ref, v in zip(refs[n_in:], res):
            o_ref[...] = v.astype(o_ref.dtype)

    in_specs, args = [], []
    for x in ins:
        if isinstance(x, tuple):
            in_specs.append(pl.BlockSpec((None, tr, c), functools.partial(lambda i, j: (j, i, 0), j=x[1])))
            args.append(x[0])
        else:
            in_specs.append(pl.BlockSpec((tr, c), lambda i: (i, 0)))
            args.append(x)
    return pl.pallas_call(
        body, name=name, grid=(r // tr,), in_specs=in_specs,
        out_specs=[pl.BlockSpec((tr, c), lambda i: (i, 0)) for _ in out_dtypes],
        out_shape=[jax.ShapeDtypeStruct((r, c), dt) for dt in out_dtypes],
        compiler_params=_params(("arbitrary",)),
    )(*args)


def f_adamw(w, g, m, v):
    m = ADAM_B1 * m + (1.0 - ADAM_B1) * g
    v = ADAM_B2 * v + (1.0 - ADAM_B2) * (g * g)
    m_hat = m / (1.0 - ADAM_B1 ** ADAM_STEP)
    v_hat = v / (1.0 - ADAM_B2 ** ADAM_STEP)
    return -ADAM_LR * (m_hat / (jnp.sqrt(v_hat) + ADAM_EPS) + ADAM_WD * w), m, v


def f_sum4(a, b, c, d):
    return (((a.astype(F32) + b.astype(F32)) + c.astype(F32)) + d.astype(F32),)


def f_add2(a, b):
    return (a + b,)


def _adamw(w, g, m, v, *, name):
    shape = w.shape
    two = (1, shape[0]) if w.ndim == 1 else (-1, shape[-1])
    outs = _ew(f_adamw, [t.reshape(two) for t in (w, g, m, v)], [F32, F32, F32], name=name)
    return [o.reshape(shape) for o in outs]


_ANY = pl.BlockSpec(memory_space=pl.ANY)


def _xy_exchange(srcs, *, broadcast, name):
    nt = len(srcs)

    def body(*refs):
        src_refs, out_refs = refs[:nt], refs[nt:2 * nt]
        send_sems, recv_sems, local_sems = refs[2 * nt:]
        x, y, c = lax.axis_index("x"), lax.axis_index("y"), lax.axis_index("c")
        me = 2 * x + y
        peers = [(1 - x, y), (x, 1 - y), (1 - x, 1 - y)]

        def block(t, j):
            return src_refs[t] if broadcast else src_refs[t].at[j]

        def copy(t, n, px, py, src_blk, dst_blk):
            return pltpu.make_async_remote_copy(
                src_ref=src_blk, dst_ref=dst_blk, send_sem=send_sems.at[3 * t + n], recv_sem=recv_sems.at[3 * t + n],
                device_id=(px, py, c), device_id_type=pl.DeviceIdType.MESH)

        mine = [pltpu.make_async_copy(block(t, me), out_refs[t].at[me], local_sems.at[t]) for t in range(nt)]
        sends = [copy(t, n, px, py, block(t, 2 * px + py), out_refs[t].at[me])
                 for t in range(nt) for n, (px, py) in enumerate(peers)]
        for cp in mine + sends:
            cp.start()
        for cp in sends:
            cp.wait_send()
        for t in range(nt):
            for n, (px, py) in enumerate(peers):
                copy(t, n, px, py, block(t, me), out_refs[t].at[2 * px + py]).wait_recv()
        for cp in mine:
            cp.wait()

    shapes = [(4,) + tuple(s.shape) if broadcast else tuple(s.shape) for s in srcs]
    return pl.pallas_call(
        body, name=name, in_specs=[_ANY] * nt, out_specs=[_ANY] * nt,
        out_shape=[jax.ShapeDtypeStruct(shp, s.dtype) for shp, s in zip(shapes, srcs)],
        scratch_shapes=[pltpu.SemaphoreType.DMA((3 * nt,)), pltpu.SemaphoreType.DMA((3 * nt,)),
                        pltpu.SemaphoreType.DMA((nt,))],
        compiler_params=pltpu.CompilerParams(has_side_effects=True),
    )(*srcs)


def _c_swap(srcs, *, name):
    nt = len(srcs)

    def body(*refs):
        src_refs, out_refs = refs[:nt], refs[nt:2 * nt]
        send_sems, recv_sems = refs[2 * nt:]
        sibling = (lax.axis_index("x"), lax.axis_index("y"), 1 - lax.axis_index("c"))
        cps = [pltpu.make_async_remote_copy(src_ref=src_refs[t], dst_ref=out_refs[t], send_sem=send_sems.at[t],
                                            recv_sem=recv_sems.at[t], device_id=sibling,
                                            device_id_type=pl.DeviceIdType.MESH) for t in range(nt)]
        for cp in cps:
            cp.start()
        for cp in cps:
            cp.wait()

    return pl.pallas_call(
        body, name=name, in_specs=[_ANY] * nt, out_specs=[_ANY] * nt,
        out_shape=[jax.ShapeDtypeStruct(s.shape, s.dtype) for s in srcs],
        scratch_shapes=[pltpu.SemaphoreType.DMA((nt,)), pltpu.SemaphoreType.DMA((nt,))],
        compiler_params=pltpu.CompilerParams(has_side_effects=True),
    )(*srcs)


_HBM = pl.BlockSpec(memory_space=pltpu.HBM)
_SEM = pl.BlockSpec(memory_space=pltpu.SEMAPHORE)
_DATAFLOW = pltpu.SideEffectType.DATAFLOW_SIDE_EFFECTING


def _xy_peers():
    x, y, c = lax.axis_index("x"), lax.axis_index("y"), lax.axis_index("c")
    return 2 * x + y, c, [(1 - x, y), (x, 1 - y), (1 - x, 1 - y)]


def _xy_start(srcs, *, broadcast, name):
    nt = len(srcs)
    lands = [lax.empty((4,) + tuple(s.shape) if broadcast else tuple(s.shape), s.dtype) for s in srcs]

    def body(*refs):
        src_refs, land_refs = refs[:nt], refs[nt:2 * nt]
        send_sems, recv_sems = refs[2 * nt], refs[2 * nt + 1]
        token = refs[-1]
        me, c, peers = _xy_peers()
        for t in range(nt):
            for n, (px, py) in enumerate(peers):
                src_blk = src_refs[t] if broadcast else src_refs[t].at[2 * px + py]
                pltpu.make_async_remote_copy(
                    src_ref=src_blk, dst_ref=land_refs[t].at[me], send_sem=send_sems.at[3 * t + n],
                    recv_sem=recv_sems.at[3 * t + n], device_id=(px, py, c),
                    device_id_type=pl.DeviceIdType.MESH).start()
        token[...] = jnp.zeros_like(token)

    res = pl.pallas_call(
        body, name=name, in_specs=[_HBM] * (2 * nt),
        out_specs=[_SEM, _SEM] + [_HBM] * (2 * nt) + [pl.BlockSpec(memory_space=pltpu.VMEM)],
        out_shape=[pltpu.SemaphoreType.DMA((3 * nt,)), pltpu.SemaphoreType.DMA((3 * nt,))]
        + [pltpu.HBM(a.shape, a.dtype) for a in list(srcs) + lands] + [jax.ShapeDtypeStruct((8, LANES), F32)],
        input_output_aliases={i: 2 + i for i in range(2 * nt)},
        compiler_params=pltpu.CompilerParams(has_side_effects=_DATAFLOW),
    )(*[pltpu.with_memory_space_constraint(a, pltpu.HBM) for a in list(srcs) + lands])
    return res[0], res[1], res[2:2 + nt], res[2 + nt:2 + 2 * nt], res[-1]


def _xy_wait(started, after, *, broadcast, name):
    send_sems, recv_sems, srcs, lands, _ = started
    nt = len(srcs)

    def body(*refs):
        src_refs, land_refs = refs[:nt], refs[nt:2 * nt]
        send_sems, recv_sems = refs[2 * nt], refs[2 * nt + 1]
        me, c, peers = _xy_peers()
        for t in range(nt):
            for n, (px, py) in enumerate(peers):
                src_blk = src_refs[t] if broadcast else src_refs[t].at[me]
                cp = pltpu.make_async_remote_copy(
                    src_ref=src_blk, dst_ref=land_refs[t].at[2 * px + py], send_sem=send_sems.at[3 * t + n],
                    recv_sem=recv_sems.at[3 * t + n], device_id=(px, py, c), device_id_type=pl.DeviceIdType.MESH)
                cp.wait_send()
                cp.wait_recv()

    res = pl.pallas_call(
        body, name=name, in_specs=[_HBM] * (2 * nt) + [_SEM, _SEM, _ANY], out_specs=[_HBM] * (2 * nt),
        out_shape=[pltpu.HBM(a.shape, a.dtype) for a in list(srcs) + list(lands)],
        input_output_aliases={i: i for i in range(2 * nt)},
        compiler_params=pltpu.CompilerParams(has_side_effects=_DATAFLOW),
    )(*srcs, *lands, send_sems, recv_sems, after)
    return res[:nt], res[nt:]


def _all_sum(srcs, *, broadcast, name):
    got = _xy_exchange(srcs, broadcast=broadcast, name=name + "_xy")
    parts = []
    for t, gt in enumerate(got):
        g3 = gt.reshape(4, -1, gt.shape[-1])
        parts.append(_ew(f_sum4, [(g3, j) for j in range(4)], [F32], name=f"{name}_sum4_{t}")[0])
    others = _c_swap(parts, name=name + "_c")
    return [_ew(f_add2, [p, o], [F32], name=f"{name}_add2_{t}")[0].reshape(gt.shape[1:])
            for t, (p, o, gt) in enumerate(zip(parts, others, got))]


def _flat_rows(parts, dtype, row_multiple=8):
    flat = jnp.concatenate([p.reshape(-1).astype(dtype) for p in parts])
    pad = (-flat.shape[0]) % (row_multiple * LANES)
    if pad:
        flat = jnp.concatenate([flat, jnp.zeros((pad,), dtype)])
    return flat.reshape(-1, LANES)


def _unflat(buf, shapes):
    flat = buf.reshape(-1)
    out, off = [], 0
    for shp in shapes:
        n = 1
        for d in shp:
            n *= d
        out.append(flat[off:off + n].reshape(shp))
        off += n
    return out


def _regroup_w_in(w):
    cols = [w[:, a:b] for a, b in _Z_SEGS] + [jnp.zeros((w.shape[0], Z_W - IN_W), w.dtype)]
    return jnp.concatenate(cols, axis=1)


def _ungroup_w_in(wz):
    starts, off = {}, 0
    for a, b in _Z_SEGS:
        starts[a] = (off, off + b - a)
        off += b - a
    return jnp.concatenate([wz[:, starts[a][0]:starts[a][1]] for a in sorted(starts)], axis=1)


def _row(vec):
    return vec.reshape(1, -1)


def _lane_pad(vec):
    return jnp.zeros((1, LANES), F32).at[0, :vec.shape[0]].set(vec)


def _layer_fwd(x, mem, w, l):
    nm = lambda s: f"{s}_l{l}"
    sv = {"x0": x}
    h = _rw(f_norm, [(x, D_MODEL, 0)], [_row(w["norm_mix"])], [(D_MODEL, BF16)], name=nm("norm_mix"))[0]
    z = _mm(h, w["w_in_z"], name=nm("in_proj"))
    a_glu = _rw(f_glu, [(z, 2 * MIX_W, Z_A // (2 * MIX_W))], [], [(MIX_W, F32)], name=nm("glu"))[0]
    ac = _conv_fwd(a_glu, 0, MIX_W, w["conv_a_w"], CONV_K, name=nm("conv_a"))
    a_par = [_row(w["conv_a_b"]), _row(w["ln_a_g"]), _row(w["ln_a_b"])]
    a_out = _rw(f_lnsilu, [(ac, MIX_W, 0)], a_par, [(MIX_W, BF16)], name=nm("ln_a"))[0]
    qc = _conv_fwd(z, Z_QKV // LANES, 3 * MIX_W, w["dn_conv_w"], DN_CONV_K, name=nm("conv_dn"))
    dn_par = [_lane_pad(w["dn_a_log"]), _lane_pad(w["dn_dt_bias"])]
    qn, kn, v, bb, gb = _rw(f_dnprep, [(qc, 3 * MIX_W, 0), (z, LANES, Z_BD // LANES)], dn_par,
                            [(MIX_W, F32)] * 5, name=nm("dn_prep"))
    o, states = _dn_fwd(qn, kn, v, bb, gb, name=nm("dn_scan"))
    o_out = _rw(f_dnout, [(o, MIX_W, 0), (z, MIX_W, Z_DG // MIX_W)], [_row(w["dn_norm_g"])], [(MIX_W, BF16)],
                name=nm("dn_out"))[0]
    gm_par = [_row(w["gm_ln_g"]), _row(w["gm_ln_b"]), w["gm_ws"].reshape(4 * GM_CHUNK, GM_CHUNK), w["gm_bs"].T]
    c_out = _rw(f_gmlp, [(z, 2 * MIX_W, Z_GM // (2 * MIX_W))], gm_par, [(MIX_W, BF16)], name=nm("gmlp"))[0]
    pc = _conv_fwd(z, Z_POOL // LANES, MIX_W, None, POOL_K, pool=True, name=nm("pool"))
    p_par = [w["pool_w"].reshape(4 * LANES, LANES), _row(w["pool_scale"])]
    p_out = _rw(f_poolpost, [(pc, MIX_W, 0)], p_par, [(MIX_W, BF16)], name=nm("pool_post"))[0]
    branches = [a_out, o_out, c_out, p_out]
    proj = [_mm(br, w["w_branch"][n], name=nm(f"branch{n}")) for n, br in enumerate(branches)]
    merged = _rw(f_merge, [(z, N_BRANCH * D_MODEL, 0)] + [(p, D_MODEL, 0) for p in proj], [], [(D_MODEL, BF16)],
                 name=nm("merge"), tr=128)[0]
    x1 = _mm(merged, w["w_out"], add=x, name=nm("out_proj"))
    sv.update(h=h, z=z, a_glu=a_glu, ac=ac, qc=qc, qn=qn, kn=kn, v=v, bb=bb, gb=gb, o=o, states=states, pc=pc,
              branches=branches, proj=proj, merged=merged, x1=x1)
    h2 = _rw(f_norm, [(x1, D_MODEL, 0)], [_row(w["norm_xa"])], [(D_MODEL, BF16)], name=nm("norm_xa"))[0]
    mn = _rw(f_norm, [(mem, D_MODEL, 0)], [_row(w["norm_mem"])], [(D_MODEL, BF16)], name=nm("norm_mem"))[0]
    kv = _mm(mn, w["xa_wkv"], name=nm("xa_kv"))
    q = _mm(h2, w["xa_wq"], name=nm("xa_q"))
    att = _rw(f_attn, [(q, D_MODEL, 0)], [kv], [(D_MODEL, BF16)], name=nm("xa_attn"))[0]
    x2 = _mm(att, w["xa_wo"], add=x1, name=nm("xa_o"))
    sv.update(h2=h2, mn=mn, kv=kv, q=q, att=att, x2=x2)
    h3 = _rw(f_norm, [(x2, D_MODEL, 0)], [_row(w["norm_mlp"])], [(D_MODEL, BF16)], name=nm("norm_mlp"))[0]
    u, act = _mm(h3, w["mlp_w1"], post=lambda r: (r, jnp.square(jnp.maximum(r, 0.0))), out_dtypes=[F32, BF16],
                 name=nm("mlp_up"))
    x3 = _mm(act, w["mlp_w2"], add=x2, name=nm("mlp_down"))
    sv.update(h3=h3, u=u, act=act)
    return x3, sv


def _layer_bwd(dx, mem, w, sv, l, after_attention, at_end):
    nm = lambda s: f"{s}_bwd_l{l}"
    s = dx.shape[0]
    g = {}
    du = _mm(dx, w["mlp_w2"], tb=True, extra=[sv["u"]], post=lambda r, u: (r * (2.0 * jnp.maximum(u, 0.0)),),
             out_dtypes=[BF16], name=nm("mlp_down_dx"))[0]
    g["mlp_w2"] = _mm(sv["act"], dx, ta=True, out_dtype=BF16, name=nm("mlp_down_dw"))
    g["mlp_w1"] = _mm(sv["h3"], du, ta=True, out_dtype=BF16, name=nm("mlp_up_dw"))
    dh3 = _mm(du, w["mlp_w1"], tb=True, name=nm("mlp_up_dx"))
    dx2, g["norm_mlp"] = _rw_bwd(f_norm, [(sv["x2"], D_MODEL, 0)], [_row(w["norm_mlp"])], [(dh3, D_MODEL, 0)],
                                 row_grads=[(0, F32)], param_grads=[0], add={0: (dx, D_MODEL, 0)}, name=nm("norm_mlp"))
    datt = _mm(dx2, w["xa_wo"], tb=True, name=nm("xa_o_dx"))
    g["xa_wo"] = _mm(sv["att"], dx2, ta=True, out_dtype=BF16, name=nm("xa_o_dw"))
    dq, dkv = _rw_bwd(f_attn, [(sv["q"], D_MODEL, 0)], [sv["kv"]], [(datt, D_MODEL, 0)], row_grads=[(0, BF16)],
                      param_grads=[0], name=nm("xa_attn"))
    g["xa_wq"] = _mm(sv["h2"], dq, ta=True, out_dtype=BF16, name=nm("xa_q_dw"))
    dh2 = _mm(dq, w["xa_wq"], tb=True, name=nm("xa_q_dx"))
    g["xa_wkv"] = _mm(sv["mn"], dkv, ta=True, out_dtype=BF16, name=nm("xa_kv_dw"))
    dmn = _mm(dkv, w["xa_wkv"], tb=True, name=nm("xa_kv_dx"))
    g["norm_mem"] = _rw_bwd(f_norm, [(mem, D_MODEL, 0)], [_row(w["norm_mem"])], [(dmn, D_MODEL, 0)], row_grads=[],
                            param_grads=[0], name=nm("norm_mem"))[0]
    gain = _row(w["norm_xa"] + after_attention(g))
    dx1, g["norm_xa"] = _rw_bwd(f_norm, [(sv["x1"], D_MODEL, 0)], [gain], [(dh2, D_MODEL, 0)],
                                row_grads=[(0, F32)], param_grads=[0], add={0: (dx2, D_MODEL, 0)}, name=nm("norm_xa"))
    z = sv["z"]
    dmerged = _mm(dx1, w["w_out"], tb=True, name=nm("out_proj_dx"))
    g["w_out"] = _mm(sv["merged"], dx1, ta=True, out_dtype=BF16, name=nm("out_proj_dw"))
    mg = _rw_bwd(f_merge, [(z, N_BRANCH * D_MODEL, 0)] + [(p, D_MODEL, 0) for p in sv["proj"]], [],
                 [(dmerged, D_MODEL, 0)], row_grads=[(i, BF16) for i in range(5)], name=nm("merge"), tr=128)
    dgate, dproj = mg[0], mg[1:]
    g["w_branch"] = jnp.stack([_mm(br, dp, ta=True, out_dtype=BF16, name=nm(f"branch{n}_dw"))
                               for n, (br, dp) in enumerate(zip(sv["branches"], dproj))])
    dbr = [_mm(dp, w["w_branch"][n], tb=True, name=nm(f"branch{n}_dx")) for n, dp in enumerate(dproj)]
    p_par = [w["pool_w"].reshape(4 * LANES, LANES), _row(w["pool_scale"])]
    dpc, dpw, g["pool_scale"] = _rw_bwd(f_poolpost, [(sv["pc"], MIX_W, 0)], p_par, [(dbr[3], MIX_W, 0)],
                                        row_grads=[(0, F32)], param_grads=[0, 1], name=nm("pool_post"))
    g["pool_w"] = dpw.reshape(4, LANES, LANES)
    dpool = _conv_bwd(None, 0, MIX_W, None, POOL_K, dpc, pool=True, name=nm("pool"))
    gm_par = [_row(w["gm_ln_g"]), _row(w["gm_ln_b"]), w["gm_ws"].reshape(4 * GM_CHUNK, GM_CHUNK), w["gm_bs"].T]
    dgm, g["gm_ln_g"], g["gm_ln_b"], dws, dbst = _rw_bwd(
        f_gmlp, [(z, 2 * MIX_W, Z_GM // (2 * MIX_W))], gm_par, [(dbr[2], MIX_W, 0)], row_grads=[(0, BF16)],
        param_grads=[0, 1, 2, 3], name=nm("gmlp"))
    g["gm_ws"] = dws.reshape(4, GM_CHUNK, GM_CHUNK)
    g["gm_bs"] = dbst.T
    do, ddg, g["dn_norm_g"] = _rw_bwd(f_dnout, [(sv["o"], MIX_W, 0), (z, MIX_W, Z_DG // MIX_W)], [_row(w["dn_norm_g"])],
                                      [(dbr[1], MIX_W, 0)], row_grads=[(0, F32), (1, BF16)], param_grads=[0],
                                      name=nm("dn_out"))
    dqn, dkn, dv, dbb, dgb = _dn_bwd(sv["qn"], sv["kn"], sv["v"], sv["bb"], sv["gb"], sv["states"], do, name=nm("dn_scan"))
    dn_par = [_lane_pad(w["dn_a_log"]), _lane_pad(w["dn_dt_bias"])]
    dqc, dbd, dal, ddt = _rw_bwd(
        f_dnprep, [(sv["qc"], 3 * MIX_W, 0), (z, LANES, Z_BD // LANES)], dn_par,
        [(t, MIX_W, 0) for t in (dqn, dkn, dv, dbb, dgb)], row_grads=[(0, F32), (1, BF16)], param_grads=[0, 1],
        name=nm("dn_prep"))
    g["dn_a_log"], g["dn_dt_bias"] = dal[0, :HEADS], ddt[0, :HEADS]
    dqkv, g["dn_conv_w"] = _conv_bwd(z, Z_QKV // LANES, 3 * MIX_W, w["dn_conv_w"], DN_CONV_K, dqc, name=nm("conv_dn"))
    a_par = [_row(w["conv_a_b"]), _row(w["ln_a_g"]), _row(w["ln_a_b"])]
    dac, g["conv_a_b"], g["ln_a_g"], g["ln_a_b"] = _rw_bwd(
        f_lnsilu, [(sv["ac"], MIX_W, 0)], a_par, [(dbr[0], MIX_W, 0)], row_grads=[(0, F32)], param_grads=[0, 1, 2],
        name=nm("ln_a"))
    dglu, g["conv_a_w"] = _conv_bwd(sv["a_glu"], 0, MIX_W, w["conv_a_w"], CONV_K, dac, dx_dtype=F32, name=nm("conv_a"))
    da_in = _rw_bwd(f_glu, [(z, 2 * MIX_W, Z_A // (2 * MIX_W))], [], [(dglu, MIX_W, 0)], row_grads=[(0, BF16)],
                    name=nm("glu"))[0]
    dz = jnp.concatenate([dgate, da_in, dgm, dqkv, ddg, dpool, dbd, jnp.zeros((s, Z_W - Z_BD - LANES), BF16)], axis=1)
    g["w_in"] = _ungroup_w_in(_mm(sv["h"], dz, ta=True, out_dtype=BF16, name=nm("in_proj_dw")))
    dh = _mm(dz, w["w_in_z"], tb=True, name=nm("in_proj_dx"))
    gain = _row(w["norm_mix"] + at_end(g))
    dx0, g["norm_mix"] = _rw_bwd(f_norm, [(sv["x0"], D_MODEL, 0)], [gain], [(dh, D_MODEL, 0)],
                                 row_grads=[(0, F32)], param_grads=[0], add={0: (dx1, D_MODEL, 0)}, name=nm("norm_mix"))
    for n in ("norm_mlp", "norm_xa", "norm_mem", "norm_mix", "pool_scale", "gm_ln_g", "gm_ln_b", "dn_norm_g",
              "conv_a_b", "ln_a_g", "ln_a_b"):
        g[n] = g[n].reshape(-1)
    return dx0, g


def _shard_slice(a, axis, j):
    n = a.shape[axis] // 4
    return lax.slice_in_dim(a, j * n, (j + 1) * n, axis=axis)


def kernel(x, mem, norm_mix, w_in, conv_a_w, conv_a_b, ln_a_g, ln_a_b, dn_conv_w, dn_a_log, dn_dt_bias, dn_norm_g, gm_ln_g, gm_ln_b, gm_ws, gm_bs, pool_w, pool_scale, w_branch, w_out, norm_xa, norm_mem, xa_wq, xa_wkv, xa_wo, norm_mlp, mlp_w1, mlp_w2, norm_f, loss_target, m_norm_mix, m_w_in, m_conv_a_w, m_conv_a_b, m_ln_a_g, m_ln_a_b, m_dn_conv_w, m_dn_a_log, m_dn_dt_bias, m_dn_norm_g, m_gm_ln_g, m_gm_ln_b, m_gm_ws, m_gm_bs, m_pool_w, m_pool_scale, m_w_branch, m_w_out, m_norm_xa, m_norm_mem, m_xa_wq, m_xa_wkv, m_xa_wo, m_norm_mlp, m_mlp_w1, m_mlp_w2, m_norm_f, v_norm_mix, v_w_in, v_conv_a_w, v_conv_a_b, v_ln_a_g, v_ln_a_b, v_dn_conv_w, v_dn_a_log, v_dn_dt_bias, v_dn_norm_g, v_gm_ln_g, v_gm_ln_b, v_gm_ws, v_gm_bs, v_pool_w, v_pool_scale, v_w_branch, v_w_out, v_norm_xa, v_norm_mem, v_xa_wq, v_xa_wkv, v_xa_wo, v_norm_mlp, v_mlp_w1, v_mlp_w2, v_norm_f):
    given = dict(locals())
    wts = {n: given[n] for n in WEIGHTS}
    depth = norm_mix.shape[0]
    x = x[0]
    mem = mem[0]
    tgt = loss_target[0]

    me = 2 * lax.axis_index("x") + lax.axis_index("y")
    sharded = BIG + CONVS
    shard_axis = dict(BIG_AXIS, conv_a_w=1, dn_conv_w=1)

    def shards(l):
        return [wts[n][l].astype(BF16) for n in BIG] + [wts[n][l] for n in CONVS]

    def assemble(l, waited):
        mine, landed = waited
        w = {}
        for n, own, got in zip(sharded, mine, landed):
            got = lax.dynamic_update_index_in_dim(got, own[None], me, 0)
            w[n] = jnp.concatenate([got[j] for j in range(4)], axis=shard_axis[n])
        w["w_in_z"] = _regroup_w_in(w.pop("w_in"))
        for n in SMALL:
            if n != "norm_f" and n not in CONVS:
                w[n] = wts[n][l]
        return w

    saved, layer_w = [], []
    gather = _xy_start(shards(0), broadcast=True, name="gather_start_l0")
    after = gather[4]
    for l in range(depth):
        waited = _xy_wait(gather, after, broadcast=True, name=f"gather_wait_l{l}")
        w = assemble(l, waited)
        if l + 1 < depth:
            waited, nxt = lax.optimization_barrier((waited, shards(l + 1)))
            gather = _xy_start(nxt, broadcast=True, name=f"gather_start_l{l + 1}")
            w["norm_mix"] = w["norm_mix"] + gather[4][0, 0]
        x, sv = _layer_fwd(x, mem, w, l)
        after = x
        saved.append(sv)
        layer_w.append(w)
    dx, g_norm_f, loss_rows = _rw_bwd(
        f_loss, [(x, D_MODEL, 0), (tgt, D_MODEL, 0)], [_row(norm_f)], [(jnp.ones((x.shape[0], 1), F32), 1, 0)],
        row_grads=[(0, F32)], param_grads=[0], primal=[(0, 1, F32)], name="loss_head")
    loss = lax.psum(jnp.sum(loss_rows), ("x", "y", "c"))

    early = ("mlp_w1", "mlp_w2", "xa_wq", "xa_wkv", "xa_wo")
    late = ("w_in", "w_branch", "w_out")
    grads, reduces = [None] * depth, []

    def start_reduce(g, names, tag):
        blocks = [jnp.stack([_shard_slice(g[n], BIG_AXIS[n], j) for j in range(4)]).astype(BF16) for n in names]
        reduces.append((tag, names, _xy_start(blocks, broadcast=False, name=f"reduce_start_{tag}")))
        return reduces[-1][2][4][0, 0]

    for l in reversed(range(depth)):
        dx, grads[l] = _layer_bwd(dx, mem, layer_w[l], saved[l], l,
                                  functools.partial(start_reduce, names=early, tag=f"l{l}a"),
                                  functools.partial(start_reduce, names=late, tag=f"l{l}b"))
        saved[l] = None
    grad_x = dx[None]

    parts, keys = [], []
    for tag, names, started_reduce in reduces:
        mine, landed = _xy_wait(started_reduce, dx, broadcast=False, name=f"reduce_wait_{tag}")
        for n, own, got in zip(names, mine, landed):
            got = lax.dynamic_update_index_in_dim(got, lax.dynamic_index_in_dim(own, me, 0), me, 0)
            g3 = got.reshape(4, -1, got.shape[-1])
            parts.append(_ew(f_sum4, [(g3, j) for j in range(4)], [F32], name=f"reduce_sum4_{tag}_{n}")[0])
            keys.append((n, int(tag[1:-1])))
    others = _c_swap(parts, name="reduce_big_c")
    sums = {key: _ew(f_add2, [p, o], [F32], name=f"reduce_add2_{key[0]}_l{key[1]}")[0]
            for key, p, o in zip(keys, parts, others)}
    gw = {n: jnp.stack([sums[n, l].reshape(wts[n].shape[1:]) for l in range(depth)]) for n in BIG}

    small_names = [n for n in SMALL if n != "norm_f"]
    small_shapes = [(depth,) + (wts[n].shape[1:2] + (4 * wts[n].shape[2],) if n in CONVS else wts[n].shape[1:])
                    for n in small_names]
    small_buf = _flat_rows([jnp.stack([grads[l][n] for l in range(depth)]) for n in small_names] + [g_norm_f], F32,
                           row_multiple=1024)
    small_sum = _all_sum([small_buf], broadcast=True, name="reduce_small")[0]
    small_parts = _unflat(small_sum, small_shapes + [norm_f.shape])
    me = 2 * lax.axis_index("x") + lax.axis_index("y")
    for n, t in zip(small_names + ["norm_f"], small_parts):
        if n in CONVS:
            width = wts[n].shape[2]
            t = lax.dynamic_slice_in_dim(t, me * width, width, axis=2)
        gw[n] = t

    deltas, new_m, new_v = {}, {}, {}
    for n in WEIGHTS:
        deltas[n], new_m[n], new_v[n] = _adamw(wts[n], gw[n], given["m_" + n], given["v_" + n], name=f"adamw_{n}")
    return (loss, grad_x, *[gw[n] for n in WEIGHTS], *[deltas[n] for n in WEIGHTS], *[new_m[n] for n in WEIGHTS],
            *[new_v[n] for n in WEIGHTS])
```

```python
import functools

import jax
import jax.numpy as jnp
from jax import lax
from jax.experimental import pallas as pl
from jax.experimental.pallas import tpu as pltpu

F32 = jnp.float32
BF16 = jnp.bfloat16
HI = lax.Precision.HIGHEST

D_MODEL = 1024
MIX_W = 512
N_BRANCH = 4
HEADS = 4
HEAD_DIM = 128
CONV_K = 31
DN_CONV_K = 4
DN_CHUNK = 64
GM_CHUNK = 128
POOL_K = 16
XA_HEADS = 4
XA_HEAD_DIM = 256
FFN_W = 4096
IN_W = 8712
Z_W = 9216
LANES = 128
VMEM_LIMIT = 56 * 1024 * 1024

ADAM_LR, ADAM_B1, ADAM_B2, ADAM_EPS, ADAM_WD, ADAM_STEP = 0.001, 0.9, 0.999, 1e-08, 0.01, 10

Z_GATE, Z_A, Z_GM, Z_QKV, Z_DG, Z_POOL, Z_BD = 0, 4096, 5120, 6144, 7680, 8192, 8704
_Z_SEGS = ((4616, 8712), (0, 1024), (3080, 4104), (1024, 2560), (2560, 3072), (4104, 4616), (3072, 3080))

BIG = ("w_in", "w_branch", "w_out", "xa_wq", "xa_wkv", "xa_wo", "mlp_w1", "mlp_w2")
BIG_AXIS = {"w_in": 1, "w_branch": 2, "w_out": 0, "xa_wq": 0, "xa_wkv": 1, "xa_wo": 0, "mlp_w1": 1, "mlp_w2": 0}
CONVS = ("conv_a_w", "dn_conv_w")
WEIGHTS = ("norm_mix", "w_in", "conv_a_w", "conv_a_b", "ln_a_g", "ln_a_b", "dn_conv_w", "dn_a_log", "dn_dt_bias",
           "dn_norm_g", "gm_ln_g", "gm_ln_b", "gm_ws", "gm_bs", "pool_w", "pool_scale", "w_branch", "w_out",
           "norm_xa", "norm_mem", "xa_wq", "xa_wkv", "xa_wo", "norm_mlp", "mlp_w1", "mlp_w2", "norm_f")
SMALL = tuple(n for n in WEIGHTS if n not in BIG)


def _params(sem=None):
    return pltpu.CompilerParams(vmem_limit_bytes=VMEM_LIMIT, dimension_semantics=sem)


def _pick(n, cands):
    for c in cands:
        if n % c == 0:
            return c
    return n


def _make_dots(prec):
    def raw(a, b, ca, cb):
        if prec is None:
            a, b = a.astype(BF16), b.astype(BF16)
        return lax.dot_general(a, b, (((ca,), (cb,)), ((), ())), precision=prec, preferred_element_type=F32)

    nn = jax.custom_vjp(lambda a, b: raw(a, b, 1, 0))
    nt = jax.custom_vjp(lambda a, b: raw(a, b, 1, 1))
    tn = jax.custom_vjp(lambda a, b: raw(a, b, 0, 0))
    nn.defvjp(lambda a, b: (raw(a, b, 1, 0), (a, b)), lambda r, g: (raw(g, r[1], 1, 1), raw(r[0], g, 0, 0)))
    nt.defvjp(lambda a, b: (raw(a, b, 1, 1), (a, b)), lambda r, g: (raw(g, r[1], 1, 0), raw(g, r[0], 0, 0)))
    tn.defvjp(lambda a, b: (raw(a, b, 0, 0), (a, b)), lambda r, g: (raw(r[1], g, 1, 1), raw(r[0], g, 1, 0)))
    return nn, nt, tn


_bnn, _bnt, _btn = _make_dots(None)
_hnn, _hnt, _htn = _make_dots(HI)
_mnn, _mnt, _mtn = _make_dots(lax.Precision.HIGH)


def _col(x, j):
    lane = lax.broadcasted_iota(jnp.int32, x.shape, 1)
    return jnp.sum(jnp.where(lane == j, x, 0.0), axis=-1, keepdims=True)


def _rms(x, g, eps=1e-6):
    return x * lax.rsqrt(jnp.mean(x * x, axis=-1, keepdims=True) + eps) * g


def _ln(x, g, b, eps=1e-5):
    xc = x - jnp.mean(x, axis=-1, keepdims=True)
    var = jnp.mean(xc * xc, axis=-1, keepdims=True)
    return xc * lax.rsqrt(var + eps) * g + b


_sigmoid = jax.nn.sigmoid


def _silu(x):
    return x * _sigmoid(x)


def _softplus(x):
    return jnp.maximum(x, 0.0) + jnp.log1p(jnp.exp(-jnp.abs(x)))


def _gelu(x):
    return 0.5 * x * (1.0 + lax.erf(x * 0.7071067811865476))


def f_norm(x, g):
    return (_rms(x, g),)


def f_glu(a_in):
    return (a_in[:, :MIX_W] * _sigmoid(a_in[:, MIX_W:]),)


def f_lnsilu(ac, cb, g, b):
    return (_silu(_ln(ac + cb, g, b)),)


def f_dnprep(qc, bd, a_log, dt_bias):
    t = qc.shape[0]
    qkv = _silu(qc)
    qs, ks, bs, gs = [], [], [], []
    for h in range(HEADS):
        q = qkv[:, h * HEAD_DIM:(h + 1) * HEAD_DIM]
        k = qkv[:, MIX_W + h * HEAD_DIM:MIX_W + (h + 1) * HEAD_DIM]
        qs.append(q * lax.rsqrt(jnp.sum(q * q, axis=-1, keepdims=True) + 1e-6) * (HEAD_DIM ** -0.5))
        ks.append(k * lax.rsqrt(jnp.sum(k * k, axis=-1, keepdims=True) + 1e-6))
        beta = _sigmoid(_col(bd, h))
        g = -jnp.exp(_col(a_log, h)) * _softplus(_col(bd, HEADS + h) + _col(dt_bias, h))
        bs.append(jnp.broadcast_to(beta, (t, HEAD_DIM)))
        gs.append(jnp.broadcast_to(g, (t, HEAD_DIM)))
    cat = lambda xs: jnp.concatenate(xs, axis=1)
    return cat(qs), cat(ks), qkv[:, 2 * MIX_W:], cat(bs), cat(gs)


def f_dnout(o, dgate, g):
    outs = []
    for h in range(HEADS):
        sl = slice(h * HEAD_DIM, (h + 1) * HEAD_DIM)
        outs.append(_rms(o[:, sl], g) * _silu(dgate[:, sl]))
    return (jnp.concatenate(outs, axis=1),)


def f_gmlp(gm_in, lg, lb, ws, bst):
    t = gm_in.shape[0]
    ge = _gelu(gm_in)
    u, vg = ge[:, :MIX_W], _ln(ge[:, MIX_W:], lg, lb)
    ri = lax.broadcasted_iota(jnp.int32, (GM_CHUNK, GM_CHUNK), 0)
    ci = lax.broadcasted_iota(jnp.int32, (GM_CHUNK, GM_CHUNK), 1)
    chunks = []
    for n in range(t // GM_CHUNK):
        vc = vg[n * GM_CHUNK:(n + 1) * GM_CHUNK]
        cols = []
        for g in range(4):
            w = jnp.where(ri >= ci, ws[g * GM_CHUNK:(g + 1) * GM_CHUNK], 0.0)
            cols.append(_bnn(w, vc[:, g * LANES:(g + 1) * LANES]) + _col(bst, g))
        chunks.append(jnp.concatenate(cols, axis=1))
    mixed = chunks[0] if len(chunks) == 1 else jnp.concatenate(chunks, axis=0)
    return (u * mixed,)


def f_poolpost(pc, pw, ps):
    cols = [_bnn(pc[:, g * LANES:(g + 1) * LANES], pw[g * LANES:(g + 1) * LANES]) for g in range(4)]
    return (jnp.concatenate(cols, axis=1) * ps,)


def f_merge(gate, p0, p1, p2, p3):
    m = None
    for n, p in enumerate((p0, p1, p2, p3)):
        t = _sigmoid(gate[:, n * D_MODEL:(n + 1) * D_MODEL]) * p
        m = t if m is None else m + t
    return (m,)


def f_attn(q, kv):
    outs = []
    for h in range(XA_HEADS):
        sl = slice(h * XA_HEAD_DIM, (h + 1) * XA_HEAD_DIM)
        s = _bnt(q[:, sl], kv[:, sl]) * (XA_HEAD_DIM ** -0.5)
        s = s - lax.stop_gradient(jnp.max(s, axis=-1, keepdims=True))
        p = jnp.exp(s)
        p = p / jnp.sum(p, axis=-1, keepdims=True)
        outs.append(_bnn(p, kv[:, D_MODEL + h * XA_HEAD_DIM:D_MODEL + (h + 1) * XA_HEAD_DIM]))
    return (jnp.concatenate(outs, axis=1),)


def f_loss(x, t, g):
    e = _rms(x, g) - t
    return (0.5 * jnp.mean(e * e, axis=-1, keepdims=True),)


def _row_spec(tr, width, cb):
    return pl.BlockSpec((tr, width), functools.partial(lambda i, cb: (i, cb), cb=cb))


def _full_spec(shape):
    return pl.BlockSpec(shape, lambda i: (0,) * len(shape))


def _rw(f, rows, params, outs, *, name, tr=256):
    s = rows[0][0].shape[0]
    tr = min(tr, s)
    nr, npar = len(rows), len(params)

    def body(*refs):
        rv = [r[...].astype(F32) for r in refs[:nr]]
        pv = [p[...] for p in refs[nr:nr + npar]]
        res = f(*rv, *pv)
        for o_ref, r in zip(refs[nr + npar:], res):
            o_ref[...] = r.astype(o_ref.dtype)

    return pl.pallas_call(
        body, name=name, grid=(s // tr,),
        in_specs=[_row_spec(tr, w, cb) for _, w, cb in rows] + [_full_spec(p.shape) for p in params],
        out_specs=[_row_spec(tr, w, 0) for w, _ in outs],
        out_shape=[jax.ShapeDtypeStruct((s, w), dt) for w, dt in outs],
        compiler_params=_params(("arbitrary",)),
    )(*[a for a, _, _ in rows], *params)


def _rw_bwd(f, rows, params, cts, *, row_grads, param_grads=(), add=None, primal=(), name, tr=256):
    s = rows[0][0].shape[0]
    tr = min(tr, s)
    nr, npar, nct = len(rows), len(params), len(cts)
    add = add or {}
    add_keys = list(add)

    def body(*refs):
        it = iter(refs)
        row_refs = [next(it) for _ in range(nr)]
        par_refs = [next(it) for _ in range(npar)]
        ct_refs = [next(it) for _ in range(nct)]
        add_refs = {k: next(it) for k in add_keys}
        rg_refs = [next(it) for _ in row_grads]
        pg_refs = [next(it) for _ in param_grads]
        pr_refs = [next(it) for _ in primal]
        rv = [r[...].astype(F32) for r in row_refs]
        pv = [p[...] for p in par_refs]
        out, vjp = jax.vjp(f, *rv, *pv)
        grads = vjp(tuple(c[...].astype(F32) for c in ct_refs))
        for (idx, _), ref in zip(row_grads, rg_refs):
            g = grads[idx]
            if idx in add_refs:
                g = g + add_refs[idx][...].astype(F32)
            ref[...] = g.astype(ref.dtype)

        @pl.when(pl.program_id(0) == 0)
        def _():
            for ref in pg_refs:
                ref[...] = jnp.zeros_like(ref)

        for idx, ref in zip(param_grads, pg_refs):
            ref[...] += grads[nr + idx]
        for (idx, _, _), ref in zip(primal, pr_refs):
            ref[...] = out[idx].astype(ref.dtype)

    in_arrays = [a for a, _, _ in rows] + list(params) + [a for a, _, _ in cts] + [add[k][0] for k in add_keys]
    in_specs = ([_row_spec(tr, w, cb) for _, w, cb in rows] + [_full_spec(p.shape) for p in params]
                + [_row_spec(tr, w, cb) for _, w, cb in cts] + [_row_spec(tr, add[k][1], add[k][2]) for k in add_keys])
    out_specs = ([_row_spec(tr, rows[idx][1], 0) for idx, _ in row_grads]
                 + [_full_spec(params[idx].shape) for idx in param_grads]
                 + [_row_spec(tr, w, 0) for _, w, _ in primal])
    out_shape = ([jax.ShapeDtypeStruct((s, rows[idx][1]), dt) for idx, dt in row_grads]
                 + [jax.ShapeDtypeStruct(params[idx].shape, F32) for idx in param_grads]
                 + [jax.ShapeDtypeStruct((s, w), dt) for _, w, dt in primal])
    return pl.pallas_call(
        body, name=name, grid=(s // tr,), in_specs=in_specs, out_specs=out_specs, out_shape=out_shape,
        compiler_params=_params(("arbitrary",)),
    )(*in_arrays)


def _mm(a, b, *, ta=False, tb=False, out_dtype=F32, add=None, post=None, extra=(), out_dtypes=None, name):
    m, k = (a.shape[1], a.shape[0]) if ta else a.shape
    n = b.shape[0] if tb else b.shape[1]
    tm = _pick(m, (1024, 512, 256, 128))
    tn = _pick(n, (1024, 512, 256, 128))
    tk = _pick(k, (1024, 512, 256, 128))
    nk = k // tk
    dims = (((0 if ta else 1,), (1 if tb else 0,)), ((), ()))
    if post is None:
        extra = [add] if add is not None else []
        post = (lambda r, e: (r + e,)) if add is not None else (lambda r: (r,))
    dtypes = out_dtypes or [out_dtype]
    n_ex, n_out = len(extra), len(dtypes)

    def body(*refs):
        a_ref, b_ref = refs[:2]
        ex_refs = refs[2:2 + n_ex]
        o_refs = refs[2 + n_ex:2 + n_ex + n_out]
        part = lax.dot_general(a_ref[...].astype(BF16), b_ref[...].astype(BF16), dims, preferred_element_type=F32)

        def finish(r):
            for o_ref, val in zip(o_refs, post(r, *[e[...] for e in ex_refs])):
                o_ref[...] = val.astype(o_ref.dtype)

        if nk == 1:
            finish(part)
            return
        acc = refs[-1]
        kk = pl.program_id(2)

        @pl.when(kk == 0)
        def _():
            acc[...] = part

        @pl.when((kk > 0) & (kk < nk - 1))
        def _():
            acc[...] += part

        @pl.when(kk == nk - 1)
        def _():
            finish(acc[...] + part)

    a_spec = (pl.BlockSpec((tk, tm), lambda i, j, kk: (kk, i)) if ta else pl.BlockSpec((tm, tk), lambda i, j, kk: (i, kk)))
    b_spec = (pl.BlockSpec((tn, tk), lambda i, j, kk: (j, kk)) if tb else pl.BlockSpec((tk, tn), lambda i, j, kk: (kk, j)))
    o_spec = pl.BlockSpec((tm, tn), lambda i, j, kk: (i, j))
    res = pl.pallas_call(
        body, name=name, grid=(m // tm, n // tn, nk), in_specs=[a_spec, b_spec] + [o_spec] * n_ex,
        out_specs=[o_spec] * n_out, out_shape=[jax.ShapeDtypeStruct((m, n), dt) for dt in dtypes],
        scratch_shapes=[pltpu.VMEM((tm, tn), F32)] if nk > 1 else [],
        compiler_params=_params(("arbitrary", "arbitrary", "arbitrary")),
    )(a, b, *extra)
    return res if out_dtypes else res[0]


CONV_TT = 1024
CONV_SUB = 256


def _halo(k):
    return -(-(k - 1) // 8) * 8


def _pool_count(row0, c, rows):
    win = lax.shift_left(jnp.int32(2), c)
    t = row0 + lax.broadcasted_iota(jnp.int32, (rows, LANES), 0)
    return win, jnp.minimum(t + 1, win).astype(F32)


def _conv_fwd(x, xcb0, channels, w, k, *, pool=False, name):
    s = x.shape[0]
    tt = min(CONV_TT, s)
    sub = min(CONV_SUB, tt)
    nt = s // tt
    halo = _halo(k)

    def body(*refs):
        if pool:
            xc_ref, xp_ref, o_ref, xs = refs
        else:
            xc_ref, xp_ref, w_ref, o_ref, xs = refs
        c, i = pl.program_id(0), pl.program_id(1)
        xs[0:halo, :] = jnp.where(i > 0, xp_ref[...], 0.0)
        xs[halo:halo + tt, :] = xc_ref[...]
        for r0 in range(0, tt, sub):
            acc = jnp.zeros((sub, LANES), F32)
            if pool:
                win, cnt = _pool_count(i * tt + r0, c, sub)
                for j in range(k):
                    acc = acc + jnp.where(j >= k - win, xs[pl.ds(r0 + halo - (k - 1) + j, sub), :], 0.0)
                o_ref[r0:r0 + sub, :] = acc / cnt - xc_ref[r0:r0 + sub, :]
            else:
                for j in range(k):
                    acc = acc + w_ref[j:j + 1, :] * xs[pl.ds(r0 + halo - (k - 1) + j, sub), :]
                o_ref[r0:r0 + sub, :] = acc

    per = tt // halo
    in_specs = [pl.BlockSpec((tt, LANES), lambda c, i: (i, xcb0 + c)),
                pl.BlockSpec((halo, LANES), lambda c, i: (jnp.maximum(i * per - 1, 0), xcb0 + c))]
    args = [x, x]
    if not pool:
        in_specs.append(pl.BlockSpec((k, LANES), lambda c, i: (0, c)))
        args.append(w)
    return pl.pallas_call(
        body, name=name, grid=(channels // LANES, nt), in_specs=in_specs,
        out_specs=pl.BlockSpec((tt, LANES), lambda c, i: (i, c)),
        out_shape=jax.ShapeDtypeStruct((s, channels), F32),
        scratch_shapes=[pltpu.VMEM((halo + tt, LANES), F32)],
        compiler_params=_params(("arbitrary", "arbitrary")),
    )(*args)


def _conv_bwd(x, xcb0, channels, w, k, dy, *, pool=False, dx_dtype=BF16, name):
    s = dy.shape[0]
    tt = min(CONV_TT, s)
    sub = min(CONV_SUB, tt)
    nt = s // tt
    halo = _halo(k)

    def body(*refs):
        if pool:
            dyc_ref, dyn_ref, dx_ref, ys = refs
        else:
            xc_ref, xp_ref, dyc_ref, dyn_ref, w_ref, dx_ref, dw_ref, xs, ys = refs
        c, i = pl.program_id(0), pl.program_id(1)
        dyn = jnp.where(i < nt - 1, dyn_ref[...], 0.0)
        if pool:
            win, cnt = _pool_count(i * tt, c, tt)
            ys[0:tt, :] = dyc_ref[...] / cnt
            ys[tt:tt + halo, :] = dyn / win.astype(F32)
            for r0 in range(0, tt, sub):
                acc = jnp.zeros((sub, LANES), F32)
                for j in range(k):
                    acc = acc + jnp.where(j >= k - win, ys[pl.ds(r0 + (k - 1) - j, sub), :], 0.0)
                dx_ref[r0:r0 + sub, :] = (acc - dyc_ref[r0:r0 + sub, :]).astype(dx_ref.dtype)
            return
        ys[0:tt, :] = dyc_ref[...]
        ys[tt:tt + halo, :] = dyn
        xs[0:halo, :] = jnp.where(i > 0, xp_ref[...], 0.0)
        xs[halo:halo + tt, :] = xc_ref[...]

        @pl.when(i == 0)
        def _():
            dw_ref[...] = jnp.zeros_like(dw_ref)

        for r0 in range(0, tt, sub):
            acc = jnp.zeros((sub, LANES), F32)
            for j in range(k):
                acc = acc + w_ref[j:j + 1, :] * ys[pl.ds(r0 + (k - 1) - j, sub), :]
            dx_ref[r0:r0 + sub, :] = acc.astype(dx_ref.dtype)
            dyc = dyc_ref[r0:r0 + sub, :]
            for j in range(k):
                dw_ref[j:j + 1, :] += jnp.sum(dyc * xs[pl.ds(r0 + halo - (k - 1) + j, sub), :], axis=0, keepdims=True)

    per = tt // halo
    cur = lambda cb0: pl.BlockSpec((tt, LANES), lambda c, i: (i, cb0 + c))
    dy_specs = [cur(0), pl.BlockSpec((halo, LANES), lambda c, i: (jnp.minimum((i + 1) * per, s // halo - 1), c))]
    dx_spec = pl.BlockSpec((tt, LANES), lambda c, i: (i, c))
    dx_shape = jax.ShapeDtypeStruct((s, channels), dx_dtype)
    if pool:
        return pl.pallas_call(
            body, name=name, grid=(channels // LANES, nt), in_specs=dy_specs, out_specs=dx_spec, out_shape=dx_shape,
            scratch_shapes=[pltpu.VMEM((tt + halo, LANES), F32)],
            compiler_params=_params(("arbitrary", "arbitrary")),
        )(dy, dy)
    in_specs = [cur(xcb0), pl.BlockSpec((halo, LANES), lambda c, i: (jnp.maximum(i * per - 1, 0), xcb0 + c))] + dy_specs
    in_specs.append(pl.BlockSpec((k, LANES), lambda c, i: (0, c)))
    return pl.pallas_call(
        body, name=name, grid=(channels // LANES, nt), in_specs=in_specs,
        out_specs=[dx_spec, pl.BlockSpec((k, LANES), lambda c, i: (0, c))],
        out_shape=[dx_shape, jax.ShapeDtypeStruct((k, channels), F32)],
        scratch_shapes=[pltpu.VMEM((halo + tt, LANES), F32), pltpu.VMEM((tt + halo, LANES), F32)],
        compiler_params=_params(("arbitrary", "arbitrary")),
    )(x, x, dy, dy, w)


DN_STEP_ROWS = 256


def _dn_step(q, k, v, bb, gb, state):
    c = DN_CHUNK
    r = q.shape[0]
    ri = lax.broadcasted_iota(jnp.int32, (r, r), 0)
    ci = lax.broadcasted_iota(jnp.int32, (r, r), 1)
    same = jnp.bitwise_or(ri, c - 1) == jnp.bitwise_or(ci, c - 1)
    causal = same & (ri >= ci)
    strict = same & (ri > ci)
    eye = jnp.where(ri == ci, 1.0, 0.0)
    gam = _hnn(jnp.where(causal, 1.0, 0.0), gb)
    gam_i = jnp.concatenate([gam] * (r // LANES), axis=1)
    gam_j = gam_i.T
    decay = jnp.where(causal, jnp.exp(jnp.where(causal, gam_i - gam_j, 0.0)), 0.0)
    row = lax.broadcasted_iota(jnp.int32, gam.shape, 0)
    lasts = [jnp.sum(jnp.where(row == n * c + c - 1, gam, 0.0), axis=0, keepdims=True) for n in range(r // c)]
    g_last = jnp.zeros_like(gam)
    for n, gl in enumerate(lasts):
        g_last = jnp.where(jnp.bitwise_or(row, c - 1) == n * c + c - 1, gl, g_last)
    kb = k * bb
    b = -jnp.where(strict, _mnt(kb, k) * decay, 0.0)
    inv = eye + b
    bp = b
    for _ in range(5):
        bp = _mnn(bp, bp)
        inv = inv + _mnn(inv, bp)
    eg = jnp.exp(gam)
    u = _mnn(inv, v * bb)
    w = _mnn(inv, kb * eg)
    attn = _bnt(q, k) * decay
    q_dec = q * eg
    k_dec = k * jnp.exp(g_last - gam)
    v_news, o_inter = [], []
    for n in range(r // c):
        s = slice(n * c, (n + 1) * c)
        v_new = u[s] - _mnn(w[s], state)
        o_inter.append(_bnn(q_dec[s], state))
        state = state * jnp.exp(lasts[n]) + _mtn(k_dec[s], v_new)
        v_news.append(v_new)
    o = jnp.concatenate(o_inter, axis=0) + _bnn(attn, jnp.concatenate(v_news, axis=0))
    return o, state


def _dn_specs(steps, reverse):
    if reverse:
        tile = pl.BlockSpec((DN_STEP_ROWS, HEAD_DIM), lambda h, i: (steps - 1 - i, h))
        st = pl.BlockSpec((1, 1, HEAD_DIM, HEAD_DIM), lambda h, i: (steps - 1 - i, h, 0, 0))
    else:
        tile = pl.BlockSpec((DN_STEP_ROWS, HEAD_DIM), lambda h, i: (i, h))
        st = pl.BlockSpec((1, 1, HEAD_DIM, HEAD_DIM), lambda h, i: (i, h, 0, 0))
    return tile, st


def _dn_fwd(q, k, v, bb, gb, *, name):
    s = q.shape[0]
    steps = s // DN_STEP_ROWS

    def body(q_ref, k_ref, v_ref, b_ref, g_ref, o_ref, st_ref, state):
        @pl.when(pl.program_id(1) == 0)
        def _():
            state[...] = jnp.zeros_like(state)

        st = state[...]
        st_ref[0, 0] = st
        o, st = _dn_step(q_ref[...], k_ref[...], v_ref[...], b_ref[...], g_ref[...], st)
        o_ref[...] = o
        state[...] = st

    tile, stspec = _dn_specs(steps, False)
    return pl.pallas_call(
        body, name=name, grid=(HEADS, steps), in_specs=[tile] * 5, out_specs=[tile, stspec],
        out_shape=[jax.ShapeDtypeStruct((s, MIX_W), F32),
                   jax.ShapeDtypeStruct((steps, HEADS, HEAD_DIM, HEAD_DIM), F32)],
        scratch_shapes=[pltpu.VMEM((HEAD_DIM, HEAD_DIM), F32)],
        compiler_params=_params(("arbitrary", "arbitrary")),
    )(q, k, v, bb, gb)


def _dn_bwd(q, k, v, bb, gb, states, do, *, name):
    s = q.shape[0]
    steps = s // DN_STEP_ROWS

    def body(q_ref, k_ref, v_ref, b_ref, g_ref, st_ref, do_ref, dq_ref, dk_ref, dv_ref, db_ref, dg_ref, dstate):
        @pl.when(pl.program_id(1) == 0)
        def _():
            dstate[...] = jnp.zeros_like(dstate)

        _, vjp = jax.vjp(_dn_step, q_ref[...], k_ref[...], v_ref[...], b_ref[...], g_ref[...], st_ref[0, 0])
        dq, dk, dv, db, dg, dst = vjp((do_ref[...], dstate[...]))
        dq_ref[...] = dq
        dk_ref[...] = dk
        dv_ref[...] = dv
        db_ref[...] = db
        dg_ref[...] = dg
        dstate[...] = dst

    tile, stspec = _dn_specs(steps, True)
    return pl.pallas_call(
        body, name=name, grid=(HEADS, steps), in_specs=[tile] * 5 + [stspec, tile], out_specs=[tile] * 5,
        out_shape=[jax.ShapeDtypeStruct((s, MIX_W), F32)] * 5,
        scratch_shapes=[pltpu.VMEM((HEAD_DIM, HEAD_DIM), F32)],
        compiler_params=_params(("arbitrary", "arbitrary")),
    )(q, k, v, bb, gb, states, do)


EW_TILE_BYTES = 2 << 20


def _ew(f, ins, out_dtypes, *, name):
    r, c = ins[0][0].shape[1:] if isinstance(ins[0], tuple) else ins[0].shape
    row_bytes = -(-c // LANES) * LANES * 4
    tr = r
    if r * row_bytes > EW_TILE_BYTES:
        tr = next((t for t in (4096, 2048, 1024, 512, 256, 128, 64, 32, 16, 8)
                   if r % t == 0 and t * row_bytes <= EW_TILE_BYTES), r)
    n_in = len(ins)

    def body(*refs):
        res = f(*[x[...] for x in refs[:n_in]])
        for o_ref, v in zip(refs[n_in:], res):
            o_ref[...] = v.astype(o_ref.dtype)

    in_specs, args = [], []
    for x in ins:
        if isinstance(x, tuple):
            in_specs.append(pl.BlockSpec((None, tr, c), functools.partial(lambda i, j: (j, i, 0), j=x[1])))
            args.append(x[0])
        else:
            in_specs.append(pl.BlockSpec((tr, c), lambda i: (i, 0)))
            args.append(x)
    return pl.pallas_call(
        body, name=name, grid=(r // tr,), in_specs=in_specs,
        out_specs=[pl.BlockSpec((tr, c), lambda i: (i, 0)) for _ in out_dtypes],
        out_shape=[jax.ShapeDtypeStruct((r, c), dt) for dt in out_dtypes],
        compiler_params=_params(("arbitrary",)),
    )(*args)


def f_adamw(w, g, m, v):
    m = ADAM_B1 * m + (1.0 - ADAM_B1) * g
    v = ADAM_B2 * v + (1.0 - ADAM_B2) * (g * g)
    m_hat = m / (1.0 - ADAM_B1 ** ADAM_STEP)
    v_hat = v / (1.0 - ADAM_B2 ** ADAM_STEP)
    return -ADAM_LR * (m_hat / (jnp.sqrt(v_hat) + ADAM_EPS) + ADAM_WD * w), m, v


def f_sum4(a, b, c, d):
    return (((a.astype(F32) + b.astype(F32)) + c.astype(F32)) + d.astype(F32),)


def f_add2(a, b):
    return (a + b,)


def _adamw(w, g, m, v, *, name):
    shape = w.shape
    two = (1, shape[0]) if w.ndim == 1 else (-1, shape[-1])
    outs = _ew(f_adamw, [t.reshape(two) for t in (w, g, m, v)], [F32, F32, F32], name=name)
    return [o.reshape(shape) for o in outs]


_ANY = pl.BlockSpec(memory_space=pl.ANY)


def _xy_exchange(srcs, *, broadcast, name):
    nt = len(srcs)

    def body(*refs):
        src_refs, out_refs = refs[:nt], refs[nt:2 * nt]
        send_sems, recv_sems, local_sems = refs[2 * nt:]
        x, y, c = lax.axis_index("x"), lax.axis_index("y"), lax.axis_index("c")
        me = 2 * x + y
        peers = [(1 - x, y), (x, 1 - y), (1 - x, 1 - y)]

        def block(t, j):
            return src_refs[t] if broadcast else src_refs[t].at[j]

        def copy(t, n, px, py, src_blk, dst_blk):
            return pltpu.make_async_remote_copy(
                src_ref=src_blk, dst_ref=dst_blk, send_sem=send_sems.at[3 * t + n], recv_sem=recv_sems.at[3 * t + n],
                device_id=(px, py, c), device_id_type=pl.DeviceIdType.MESH)

        mine = [pltpu.make_async_copy(block(t, me), out_refs[t].at[me], local_sems.at[t]) for t in range(nt)]
        sends = [copy(t, n, px, py, block(t, 2 * px + py), out_refs[t].at[me])
                 for t in range(nt) for n, (px, py) in enumerate(peers)]
        for cp in mine + sends:
            cp.start()
        for cp in sends:
            cp.wait_send()
        for t in range(nt):
            for n, (px, py) in enumerate(peers):
                copy(t, n, px, py, block(t, me), out_refs[t].at[2 * px + py]).wait_recv()
        for cp in mine:
            cp.wait()

    shapes = [(4,) + tuple(s.shape) if broadcast else tuple(s.shape) for s in srcs]
    return pl.pallas_call(
        body, name=name, in_specs=[_ANY] * nt, out_specs=[_ANY] * nt,
        out_shape=[jax.ShapeDtypeStruct(shp, s.dtype) for shp, s in zip(shapes, srcs)],
        scratch_shapes=[pltpu.SemaphoreType.DMA((3 * nt,)), pltpu.SemaphoreType.DMA((3 * nt,)),
                        pltpu.SemaphoreType.DMA((nt,))],
        compiler_params=pltpu.CompilerParams(has_side_effects=True),
    )(*srcs)


def _c_swap(srcs, *, name):
    nt = len(srcs)

    def body(*refs):
        src_refs, out_refs = refs[:nt], refs[nt:2 * nt]
        send_sems, recv_sems = refs[2 * nt:]
        sibling = (lax.axis_index("x"), lax.axis_index("y"), 1 - lax.axis_index("c"))
        cps = [pltpu.make_async_remote_copy(src_ref=src_refs[t], dst_ref=out_refs[t], send_sem=send_sems.at[t],
                                            recv_sem=recv_sems.at[t], device_id=sibling,
                                            device_id_type=pl.DeviceIdType.MESH) for t in range(nt)]
        for cp in cps:
            cp.start()
        for cp in cps:
            cp.wait()

    return pl.pallas_call(
        body, name=name, in_specs=[_ANY] * nt, out_specs=[_ANY] * nt,
        out_shape=[jax.ShapeDtypeStruct(s.shape, s.dtype) for s in srcs],
        scratch_shapes=[pltpu.SemaphoreType.DMA((nt,)), pltpu.SemaphoreType.DMA((nt,))],
        compiler_params=pltpu.CompilerParams(has_side_effects=True),
    )(*srcs)


_HBM = pl.BlockSpec(memory_space=pltpu.HBM)
_SEM = pl.BlockSpec(memory_space=pltpu.SEMAPHORE)
_DATAFLOW = pltpu.SideEffectType.DATAFLOW_SIDE_EFFECTING


def _xy_peers():
    x, y, c = lax.axis_index("x"), lax.axis_index("y"), lax.axis_index("c")
    return 2 * x + y, c, [(1 - x, y), (x, 1 - y), (1 - x, 1 - y)]


def _xy_start(srcs, *, broadcast, name):
    nt = len(srcs)
    lands = [lax.empty((4,) + tuple(s.shape) if broadcast else tuple(s.shape), s.dtype) for s in srcs]

    def body(*refs):
        src_refs, land_refs = refs[:nt], refs[nt:2 * nt]
        send_sems, recv_sems = refs[2 * nt], refs[2 * nt + 1]
        token = refs[-1]
        me, c, peers = _xy_peers()
        for t in range(nt):
            for n, (px, py) in enumerate(peers):
                src_blk = src_refs[t] if broadcast else src_refs[t].at[2 * px + py]
                pltpu.make_async_remote_copy(
                    src_ref=src_blk, dst_ref=land_refs[t].at[me], send_sem=send_sems.at[3 * t + n],
                    recv_sem=recv_sems.at[3 * t + n], device_id=(px, py, c),
                    device_id_type=pl.DeviceIdType.MESH).start()
        token[...] = jnp.zeros_like(token)

    res = pl.pallas_call(
        body, name=name, in_specs=[_HBM] * (2 * nt),
        out_specs=[_SEM, _SEM] + [_HBM] * (2 * nt) + [pl.BlockSpec(memory_space=pltpu.VMEM)],
        out_shape=[pltpu.SemaphoreType.DMA((3 * nt,)), pltpu.SemaphoreType.DMA((3 * nt,))]
        + [pltpu.HBM(a.shape, a.dtype) for a in list(srcs) + lands] + [jax.ShapeDtypeStruct((8, LANES), F32)],
        input_output_aliases={i: 2 + i for i in range(2 * nt)},
        compiler_params=pltpu.CompilerParams(has_side_effects=_DATAFLOW),
    )(*[pltpu.with_memory_space_constraint(a, pltpu.HBM) for a in list(srcs) + lands])
    return res[0], res[1], res[2:2 + nt], res[2 + nt:2 + 2 * nt], res[-1]


def _xy_wait(started, after, *, broadcast, name):
    send_sems, recv_sems, srcs, lands, _ = started
    nt = len(srcs)

    def body(*refs):
        src_refs, land_refs = refs[:nt], refs[nt:2 * nt]
        send_sems, recv_sems = refs[2 * nt], refs[2 * nt + 1]
        me, c, peers = _xy_peers()
        for t in range(nt):
            for n, (px, py) in enumerate(peers):
                src_blk = src_refs[t] if broadcast else src_refs[t].at[me]
                cp = pltpu.make_async_remote_copy(
                    src_ref=src_blk, dst_ref=land_refs[t].at[2 * px + py], send_sem=send_sems.at[3 * t + n],
                    recv_sem=recv_sems.at[3 * t + n], device_id=(px, py, c), device_id_type=pl.DeviceIdType.MESH)
                cp.wait_send()
                cp.wait_recv()

    res = pl.pallas_call(
        body, name=name, in_specs=[_HBM] * (2 * nt) + [_SEM, _SEM, _ANY], out_specs=[_HBM] * (2 * nt),
        out_shape=[pltpu.HBM(a.shape, a.dtype) for a in list(srcs) + list(lands)],
        input_output_aliases={i: i for i in range(2 * nt)},
        compiler_params=pltpu.CompilerParams(has_side_effects=_DATAFLOW),
    )(*srcs, *lands, send_sems, recv_sems, after)
    return res[:nt], res[nt:]


def _all_sum(srcs, *, broadcast, name):
    got = _xy_exchange(srcs, broadcast=broadcast, name=name + "_xy")
    parts = []
    for t, gt in enumerate(got):
        g3 = gt.reshape(4, -1, gt.shape[-1])
        parts.append(_ew(f_sum4, [(g3, j) for j in range(4)], [F32], name=f"{name}_sum4_{t}")[0])
    others = _c_swap(parts, name=name + "_c")
    return [_ew(f_add2, [p, o], [F32], name=f"{name}_add2_{t}")[0].reshape(gt.shape[1:])
            for t, (p, o, gt) in enumerate(zip(parts, others, got))]


def _flat_rows(parts, dtype, row_multiple=8):
    flat = jnp.concatenate([p.reshape(-1).astype(dtype) for p in parts])
    pad = (-flat.shape[0]) % (row_multiple * LANES)
    if pad:
        flat = jnp.concatenate([flat, jnp.zeros((pad,), dtype)])
    return flat.reshape(-1, LANES)


def _unflat(buf, shapes):
    flat = buf.reshape(-1)
    out, off = [], 0
    for shp in shapes:
        n = 1
        for d in shp:
            n *= d
        out.append(flat[off:off + n].reshape(shp))
        off += n
    return out


def _regroup_w_in(w):
    cols = [w[:, a:b] for a, b in _Z_SEGS] + [jnp.zeros((w.shape[0], Z_W - IN_W), w.dtype)]
    return jnp.concatenate(cols, axis=1)


def _ungroup_w_in(wz):
    starts, off = {}, 0
    for a, b in _Z_SEGS:
        starts[a] = (off, off + b - a)
        off += b - a
    return jnp.concatenate([wz[:, starts[a][0]:starts[a][1]] for a in sorted(starts)], axis=1)


def _row(vec):
    return vec.reshape(1, -1)


def _lane_pad(vec):
    return jnp.zeros((1, LANES), F32).at[0, :vec.shape[0]].set(vec)


def _layer_fwd(x, mem, w, l):
    nm = lambda s: f"{s}_l{l}"
    sv = {"x0": x}
    h = _rw(f_norm, [(x, D_MODEL, 0)], [_row(w["norm_mix"])], [(D_MODEL, BF16)], name=nm("norm_mix"))[0]
    z = _mm(h, w["w_in_z"], name=nm("in_proj"))
    a_glu = _rw(f_glu, [(z, 2 * MIX_W, Z_A // (2 * MIX_W))], [], [(MIX_W, F32)], name=nm("glu"))[0]
    ac = _conv_fwd(a_glu, 0, MIX_W, w["conv_a_w"], CONV_K, name=nm("conv_a"))
    a_par = [_row(w["conv_a_b"]), _row(w["ln_a_g"]), _row(w["ln_a_b"])]
    a_out = _rw(f_lnsilu, [(ac, MIX_W, 0)], a_par, [(MIX_W, BF16)], name=nm("ln_a"))[0]
    qc = _conv_fwd(z, Z_QKV // LANES, 3 * MIX_W, w["dn_conv_w"], DN_CONV_K, name=nm("conv_dn"))
    dn_par = [_lane_pad(w["dn_a_log"]), _lane_pad(w["dn_dt_bias"])]
    qn, kn, v, bb, gb = _rw(f_dnprep, [(qc, 3 * MIX_W, 0), (z, LANES, Z_BD // LANES)], dn_par,
                            [(MIX_W, F32)] * 5, name=nm("dn_prep"))
    o, states = _dn_fwd(qn, kn, v, bb, gb, name=nm("dn_scan"))
    o_out = _rw(f_dnout, [(o, MIX_W, 0), (z, MIX_W, Z_DG // MIX_W)], [_row(w["dn_norm_g"])], [(MIX_W, BF16)],
                name=nm("dn_out"))[0]
    gm_par = [_row(w["gm_ln_g"]), _row(w["gm_ln_b"]), w["gm_ws"].reshape(4 * GM_CHUNK, GM_CHUNK), w["gm_bs"].T]
    c_out = _rw(f_gmlp, [(z, 2 * MIX_W, Z_GM // (2 * MIX_W))], gm_par, [(MIX_W, BF16)], name=nm("gmlp"))[0]
    pc = _conv_fwd(z, Z_POOL // LANES, MIX_W, None, POOL_K, pool=True, name=nm("pool"))
    p_par = [w["pool_w"].reshape(4 * LANES, LANES), _row(w["pool_scale"])]
    p_out = _rw(f_poolpost, [(pc, MIX_W, 0)], p_par, [(MIX_W, BF16)], name=nm("pool_post"))[0]
    branches = [a_out, o_out, c_out, p_out]
    proj = [_mm(br, w["w_branch"][n], name=nm(f"branch{n}")) for n, br in enumerate(branches)]
    merged = _rw(f_merge, [(z, N_BRANCH * D_MODEL, 0)] + [(p, D_MODEL, 0) for p in proj], [], [(D_MODEL, BF16)],
                 name=nm("merge"), tr=128)[0]
    x1 = _mm(merged, w["w_out"], add=x, name=nm("out_proj"))
    sv.update(h=h, z=z, a_glu=a_glu, ac=ac, qc=qc, qn=qn, kn=kn, v=v, bb=bb, gb=gb, o=o, states=states, pc=pc,
              branches=branches, proj=proj, merged=merged, x1=x1)
    h2 = _rw(f_norm, [(x1, D_MODEL, 0)], [_row(w["norm_xa"])], [(D_MODEL, BF16)], name=nm("norm_xa"))[0]
    mn = _rw(f_norm, [(mem, D_MODEL, 0)], [_row(w["norm_mem"])], [(D_MODEL, BF16)], name=nm("norm_mem"))[0]
    kv = _mm(mn, w["xa_wkv"], name=nm("xa_kv"))
    q = _mm(h2, w["xa_wq"], name=nm("xa_q"))
    att = _rw(f_attn, [(q, D_MODEL, 0)], [kv], [(D_MODEL, BF16)], name=nm("xa_attn"))[0]
    x2 = _mm(att, w["xa_wo"], add=x1, name=nm("xa_o"))
    sv.update(h2=h2, mn=mn, kv=kv, q=q, att=att, x2=x2)
    h3 = _rw(f_norm, [(x2, D_MODEL, 0)], [_row(w["norm_mlp"])], [(D_MODEL, BF16)], name=nm("norm_mlp"))[0]
    u, act = _mm(h3, w["mlp_w1"], post=lambda r: (r, jnp.square(jnp.maximum(r, 0.0))), out_dtypes=[F32, BF16],
                 name=nm("mlp_up"))
    x3 = _mm(act, w["mlp_w2"], add=x2, name=nm("mlp_down"))
    sv.update(h3=h3, u=u, act=act)
    return x3, sv


def _layer_bwd(dx, mem, w, sv, l, after_attention, at_end):
    nm = lambda s: f"{s}_bwd_l{l}"
    s = dx.shape[0]
    g = {}
    du = _mm(dx, w["mlp_w2"], tb=True, extra=[sv["u"]], post=lambda r, u: (r * (2.0 * jnp.maximum(u, 0.0)),),
             out_dtypes=[BF16], name=nm("mlp_down_dx"))[0]
    g["mlp_w2"] = _mm(sv["act"], dx, ta=True, out_dtype=BF16, name=nm("mlp_down_dw"))
    g["mlp_w1"] = _mm(sv["h3"], du, ta=True, out_dtype=BF16, name=nm("mlp_up_dw"))
    dh3 = _mm(du, w["mlp_w1"], tb=True, name=nm("mlp_up_dx"))
    dx2, g["norm_mlp"] = _rw_bwd(f_norm, [(sv["x2"], D_MODEL, 0)], [_row(w["norm_mlp"])], [(dh3, D_MODEL, 0)],
                                 row_grads=[(0, F32)], param_grads=[0], add={0: (dx, D_MODEL, 0)}, name=nm("norm_mlp"))
    datt = _mm(dx2, w["xa_wo"], tb=True, name=nm("xa_o_dx"))
    g["xa_wo"] = _mm(sv["att"], dx2, ta=True, out_dtype=BF16, name=nm("xa_o_dw"))
    dq, dkv = _rw_bwd(f_attn, [(sv["q"], D_MODEL, 0)], [sv["kv"]], [(datt, D_MODEL, 0)], row_grads=[(0, BF16)],
                      param_grads=[0], name=nm("xa_attn"))
    g["xa_wq"] = _mm(sv["h2"], dq, ta=True, out_dtype=BF16, name=nm("xa_q_dw"))
    dh2 = _mm(dq, w["xa_wq"], tb=True, name=nm("xa_q_dx"))
    g["xa_wkv"] = _mm(sv["mn"], dkv, ta=True, out_dtype=BF16, name=nm("xa_kv_dw"))
    dmn = _mm(dkv, w["xa_wkv"], tb=True, name=nm("xa_kv_dx"))
    g["norm_mem"] = _rw_bwd(f_norm, [(mem, D_MODEL, 0)], [_row(w["norm_mem"])], [(dmn, D_MODEL, 0)], row_grads=[],
                            param_grads=[0], name=nm("norm_mem"))[0]
    gain = _row(w["norm_xa"] + after_attention(g))
    dx1, g["norm_xa"] = _rw_bwd(f_norm, [(sv["x1"], D_MODEL, 0)], [gain], [(dh2, D_MODEL, 0)],
                                row_grads=[(0, F32)], param_grads=[0], add={0: (dx2, D_MODEL, 0)}, name=nm("norm_xa"))
    z = sv["z"]
    dmerged = _mm(dx1, w["w_out"], tb=True, name=nm("out_proj_dx"))
    g["w_out"] = _mm(sv["merged"], dx1, ta=True, out_dtype=BF16, name=nm("out_proj_dw"))
    mg = _rw_bwd(f_merge, [(z, N_BRANCH * D_MODEL, 0)] + [(p, D_MODEL, 0) for p in sv["proj"]], [],
                 [(dmerged, D_MODEL, 0)], row_grads=[(i, BF16) for i in range(5)], name=nm("merge"), tr=128)
    dgate, dproj = mg[0], mg[1:]
    g["w_branch"] = jnp.stack([_mm(br, dp, ta=True, out_dtype=BF16, name=nm(f"branch{n}_dw"))
                               for n, (br, dp) in enumerate(zip(sv["branches"], dproj))])
    dbr = [_mm(dp, w["w_branch"][n], tb=True, name=nm(f"branch{n}_dx")) for n, dp in enumerate(dproj)]
    p_par = [w["pool_w"].reshape(4 * LANES, LANES), _row(w["pool_scale"])]
    dpc, dpw, g["pool_scale"] = _rw_bwd(f_poolpost, [(sv["pc"], MIX_W, 0)], p_par, [(dbr[3], MIX_W, 0)],
                                        row_grads=[(0, F32)], param_grads=[0, 1], name=nm("pool_post"))
    g["pool_w"] = dpw.reshape(4, LANES, LANES)
    dpool = _conv_bwd(None, 0, MIX_W, None, POOL_K, dpc, pool=True, name=nm("pool"))
    gm_par = [_row(w["gm_ln_g"]), _row(w["gm_ln_b"]), w["gm_ws"].reshape(4 * GM_CHUNK, GM_CHUNK), w["gm_bs"].T]
    dgm, g["gm_ln_g"], g["gm_ln_b"], dws, dbst = _rw_bwd(
        f_gmlp, [(z, 2 * MIX_W, Z_GM // (2 * MIX_W))], gm_par, [(dbr[2], MIX_W, 0)], row_grads=[(0, BF16)],
        param_grads=[0, 1, 2, 3], name=nm("gmlp"))
    g["gm_ws"] = dws.reshape(4, GM_CHUNK, GM_CHUNK)
    g["gm_bs"] = dbst.T
    do, ddg, g["dn_norm_g"] = _rw_bwd(f_dnout, [(sv["o"], MIX_W, 0), (z, MIX_W, Z_DG // MIX_W)], [_row(w["dn_norm_g"])],
                                      [(dbr[1], MIX_W, 0)], row_grads=[(0, F32), (1, BF16)], param_grads=[0],
                                      name=nm("dn_out"))
    dqn, dkn, dv, dbb, dgb = _dn_bwd(sv["qn"], sv["kn"], sv["v"], sv["bb"], sv["gb"], sv["states"], do, name=nm("dn_scan"))
    dn_par = [_lane_pad(w["dn_a_log"]), _lane_pad(w["dn_dt_bias"])]
    dqc, dbd, dal, ddt = _rw_bwd(
        f_dnprep, [(sv["qc"], 3 * MIX_W, 0), (z, LANES, Z_BD // LANES)], dn_par,
        [(t, MIX_W, 0) for t in (dqn, dkn, dv, dbb, dgb)], row_grads=[(0, F32), (1, BF16)], param_grads=[0, 1],
        name=nm("dn_prep"))
    g["dn_a_log"], g["dn_dt_bias"] = dal[0, :HEADS], ddt[0, :HEADS]
    dqkv, g["dn_conv_w"] = _conv_bwd(z, Z_QKV // LANES, 3 * MIX_W, w["dn_conv_w"], DN_CONV_K, dqc, name=nm("conv_dn"))
    a_par = [_row(w["conv_a_b"]), _row(w["ln_a_g"]), _row(w["ln_a_b"])]
    dac, g["conv_a_b"], g["ln_a_g"], g["ln_a_b"] = _rw_bwd(
        f_lnsilu, [(sv["ac"], MIX_W, 0)], a_par, [(dbr[0], MIX_W, 0)], row_grads=[(0, F32)], param_grads=[0, 1, 2],
        name=nm("ln_a"))
    dglu, g["conv_a_w"] = _conv_bwd(sv["a_glu"], 0, MIX_W, w["conv_a_w"], CONV_K, dac, dx_dtype=F32, name=nm("conv_a"))
    da_in = _rw_bwd(f_glu, [(z, 2 * MIX_W, Z_A // (2 * MIX_W))], [], [(dglu, MIX_W, 0)], row_grads=[(0, BF16)],
                    name=nm("glu"))[0]
    dz = jnp.concatenate([dgate, da_in, dgm, dqkv, ddg, dpool, dbd, jnp.zeros((s, Z_W - Z_BD - LANES), BF16)], axis=1)
    g["w_in"] = _ungroup_w_in(_mm(sv["h"], dz, ta=True, out_dtype=BF16, name=nm("in_proj_dw")))
    dh = _mm(dz, w["w_in_z"], tb=True, name=nm("in_proj_dx"))
    gain = _row(w["norm_mix"] + at_end(g))
    dx0, g["norm_mix"] = _rw_bwd(f_norm, [(sv["x0"], D_MODEL, 0)], [gain], [(dh, D_MODEL, 0)],
                                 row_grads=[(0, F32)], param_grads=[0], add={0: (dx1, D_MODEL, 0)}, name=nm("norm_mix"))
    for n in ("norm_mlp", "norm_xa", "norm_mem", "norm_mix", "pool_scale", "gm_ln_g", "gm_ln_b", "dn_norm_g",
              "conv_a_b", "ln_a_g", "ln_a_b"):
        g[n] = g[n].reshape(-1)
    return dx0, g


def _shard_slice(a, axis, j):
    n = a.shape[axis] // 4
    return lax.slice_in_dim(a, j * n, (j + 1) * n, axis=axis)


def kernel(x, mem, norm_mix, w_in, conv_a_w, conv_a_b, ln_a_g, ln_a_b, dn_conv_w, dn_a_log, dn_dt_bias, dn_norm_g, gm_ln_g, gm_ln_b, gm_ws, gm_bs, pool_w, pool_scale, w_branch, w_out, norm_xa, norm_mem, xa_wq, xa_wkv, xa_wo, norm_mlp, mlp_w1, mlp_w2, norm_f, loss_target, m_norm_mix, m_w_in, m_conv_a_w, m_conv_a_b, m_ln_a_g, m_ln_a_b, m_dn_conv_w, m_dn_a_log, m_dn_dt_bias, m_dn_norm_g, m_gm_ln_g, m_gm_ln_b, m_gm_ws, m_gm_bs, m_pool_w, m_pool_scale, m_w_branch, m_w_out, m_norm_xa, m_norm_mem, m_xa_wq, m_xa_wkv, m_xa_wo, m_norm_mlp, m_mlp_w1, m_mlp_w2, m_norm_f, v_norm_mix, v_w_in, v_conv_a_w, v_conv_a_b, v_ln_a_g, v_ln_a_b, v_dn_conv_w, v_dn_a_log, v_dn_dt_bias, v_dn_norm_g, v_gm_ln_g, v_gm_ln_b, v_gm_ws, v_gm_bs, v_pool_w, v_pool_scale, v_w_branch, v_w_out, v_norm_xa, v_norm_mem, v_xa_wq, v_xa_wkv, v_xa_wo, v_norm_mlp, v_mlp_w1, v_mlp_w2, v_norm_f):
    given = dict(locals())
    wts = {n: given[n] for n in WEIGHTS}
    depth = norm_mix.shape[0]
    x = x[0]
    mem = mem[0]
    tgt = loss_target[0]

    me = 2 * lax.axis_index("x") + lax.axis_index("y")
    sharded = BIG + CONVS
    shard_axis = dict(BIG_AXIS, conv_a_w=1, dn_conv_w=1)

    def shards(l):
        return [wts[n][l].astype(BF16) for n in BIG] + [wts[n][l] for n in CONVS]

    def assemble(l, waited):
        mine, landed = waited
        w = {}
        for n, own, got in zip(sharded, mine, landed):
            got = lax.dynamic_update_index_in_dim(got, own[None], me, 0)
            w[n] = jnp.concatenate([got[j] for j in range(4)], axis=shard_axis[n])
        w["w_in_z"] = _regroup_w_in(w.pop("w_in"))
        for n in SMALL:
            if n != "norm_f" and n not in CONVS:
                w[n] = wts[n][l]
        return w

    saved, layer_w = [], []
    gather = _xy_start(shards(0), broadcast=True, name="gather_start_l0")
    after = gather[4]
    for l in range(depth):
        waited = _xy_wait(gather, after, broadcast=True, name=f"gather_wait_l{l}")
        w = assemble(l, waited)
        if l + 1 < depth:
            waited, nxt = lax.optimization_barrier((waited, shards(l + 1)))
            gather = _xy_start(nxt, broadcast=True, name=f"gather_start_l{l + 1}")
            w["norm_mix"] = w["norm_mix"] + gather[4][0, 0]
        x, sv = _layer_fwd(x, mem, w, l)
        after = x
        saved.append(sv)
        layer_w.append(w)
    dx, g_norm_f, loss_rows = _rw_bwd(
        f_loss, [(x, D_MODEL, 0), (tgt, D_MODEL, 0)], [_row(norm_f)], [(jnp.ones((x.shape[0], 1), F32), 1, 0)],
        row_grads=[(0, F32)], param_grads=[0], primal=[(0, 1, F32)], name="loss_head")
    loss = lax.psum(jnp.sum(loss_rows), ("x", "y", "c"))

    early = ("mlp_w1", "mlp_w2", "xa_wq", "xa_wkv", "xa_wo")
    late = ("w_in", "w_branch", "w_out")
    grads, reduces = [None] * depth, []

    def start_reduce(g, names, tag):
        blocks = [jnp.stack([_shard_slice(g[n], BIG_AXIS[n], j) for j in range(4)]).astype(BF16) for n in names]
        reduces.append((tag, names, _xy_start(blocks, broadcast=False, name=f"reduce_start_{tag}")))
        return reduces[-1][2][4][0, 0]

    for l in reversed(range(depth)):
        dx, grads[l] = _layer_bwd(dx, mem, layer_w[l], saved[l], l,
                                  functools.partial(start_reduce, names=early, tag=f"l{l}a"),
                                  functools.partial(start_reduce, names=late, tag=f"l{l}b"))
        saved[l] = None
    grad_x = dx[None]

    parts, keys = [], []
    for tag, names, started_reduce in reduces:
        mine, landed = _xy_wait(started_reduce, dx, broadcast=False, name=f"reduce_wait_{tag}")
        for n, own, got in zip(names, mine, landed):
            got = lax.dynamic_update_index_in_dim(got, lax.dynamic_index_in_dim(own, me, 0), me, 0)
            g3 = got.reshape(4, -1, got.shape[-1])
            parts.append(_ew(f_sum4, [(g3, j) for j in range(4)], [F32], name=f"reduce_sum4_{tag}_{n}")[0])
            keys.append((n, int(tag[1:-1])))
    others = _c_swap(parts, name="reduce_big_c")
    sums = {key: _ew(f_add2, [p, o], [F32], name=f"reduce_add2_{key[0]}_l{key[1]}")[0]
            for key, p, o in zip(keys, parts, others)}
    gw = {n: jnp.stack([sums[n, l].reshape(wts[n].shape[1:]) for l in range(depth)]) for n in BIG}

    small_names = [n for n in SMALL if n != "norm_f"]
    small_shapes = [(depth,) + (wts[n].shape[1:2] + (4 * wts[n].shape[2],) if n in CONVS else wts[n].shape[1:])
                    for n in small_names]
    small_buf = _flat_rows([jnp.stack([grads[l][n] for l in range(depth)]) for n in small_names] + [g_norm_f], F32,
                           row_multiple=1024)
    small_sum = _all_sum([small_buf], broadcast=True, name="reduce_small")[0]
    small_parts = _unflat(small_sum, small_shapes + [norm_f.shape])
    me = 2 * lax.axis_index("x") + lax.axis_index("y")
    for n, t in zip(small_names + ["norm_f"], small_parts):
        if n in CONVS:
            width = wts[n].shape[2]
            t = lax.dynamic_slice_in_dim(t, me * width, width, axis=2)
        gw[n] = t

    deltas, new_m, new_v = {}, {}, {}
    for n in WEIGHTS:
        deltas[n], new_m[n], new_v[n] = _adamw(wts[n], gw[n], given["m_" + n], given["v_" + n], name=f"adamw_{n}")
    return (loss, grad_x, *[gw[n] for n in WEIGHTS], *[deltas[n] for n in WEIGHTS], *[new_m[n] for n in WEIGHTS],
            *[new_v[n] for n in WEIGHTS])
```

```python
import functools

import jax
import jax.numpy as jnp
from jax import lax
from jax.experimental import pallas as pl
from jax.experimental.pallas import tpu as pltpu

F32 = jnp.float32
BF16 = jnp.bfloat16
HI = lax.Precision.HIGHEST

D_MODEL = 1024
MIX_W = 512
N_BRANCH = 4
HEADS = 4
HEAD_DIM = 128
CONV_K = 31
DN_CONV_K = 4
DN_CHUNK = 64
GM_CHUNK = 128
POOL_K = 16
XA_HEADS = 4
XA_HEAD_DIM = 256
FFN_W = 4096
IN_W = 8712
Z_W = 9216
LANES = 128
VMEM_LIMIT = 56 * 1024 * 1024

ADAM_LR, ADAM_B1, ADAM_B2, ADAM_EPS, ADAM_WD, ADAM_STEP = 0.001, 0.9, 0.999, 1e-08, 0.01, 10

Z_GATE, Z_A, Z_GM, Z_QKV, Z_DG, Z_POOL, Z_BD = 0, 4096, 5120, 6144, 7680, 8192, 8704
_Z_SEGS = ((4616, 8712), (0, 1024), (3080, 4104), (1024, 2560), (2560, 3072), (4104, 4616), (3072, 3080))

BIG = ("w_in", "w_branch", "w_out", "xa_wq", "xa_wkv", "xa_wo", "mlp_w1", "mlp_w2")
BIG_AXIS = {"w_in": 1, "w_branch": 2, "w_out": 0, "xa_wq": 0, "xa_wkv": 1, "xa_wo": 0, "mlp_w1": 1, "mlp_w2": 0}
CONVS = ("conv_a_w", "dn_conv_w")
WEIGHTS = ("norm_mix", "w_in", "conv_a_w", "conv_a_b", "ln_a_g", "ln_a_b", "dn_conv_w", "dn_a_log", "dn_dt_bias",
           "dn_norm_g", "gm_ln_g", "gm_ln_b", "gm_ws", "gm_bs", "pool_w", "pool_scale", "w_branch", "w_out",
           "norm_xa", "norm_mem", "xa_wq", "xa_wkv", "xa_wo", "norm_mlp", "mlp_w1", "mlp_w2", "norm_f")
SMALL = tuple(n for n in WEIGHTS if n not in BIG)


def _params(sem=None):
    return pltpu.CompilerParams(vmem_limit_bytes=VMEM_LIMIT, dimension_semantics=sem)


def _pick(n, cands):
    for c in cands:
        if n % c == 0:
            return c
    return n


def _make_dots(prec):
    def raw(a, b, ca, cb):
        if prec is None:
            a, b = a.astype(BF16), b.astype(BF16)
        return lax.dot_general(a, b, (((ca,), (cb,)), ((), ())), precision=prec, preferred_element_type=F32)

    nn = jax.custom_vjp(lambda a, b: raw(a, b, 1, 0))
    nt = jax.custom_vjp(lambda a, b: raw(a, b, 1, 1))
    tn = jax.custom_vjp(lambda a, b: raw(a, b, 0, 0))
    nn.defvjp(lambda a, b: (raw(a, b, 1, 0), (a, b)), lambda r, g: (raw(g, r[1], 1, 1), raw(r[0], g, 0, 0)))
    nt.defvjp(lambda a, b: (raw(a, b, 1, 1), (a, b)), lambda r, g: (raw(g, r[1], 1, 0), raw(g, r[0], 0, 0)))
    tn.defvjp(lambda a, b: (raw(a, b, 0, 0), (a, b)), lambda r, g: (raw(r[1], g, 1, 1), raw(r[0], g, 1, 0)))
    return nn, nt, tn


_bnn, _bnt, _btn = _make_dots(None)
_hnn, _hnt, _htn = _make_dots(HI)
_mnn, _mnt, _mtn = _make_dots(lax.Precision.HIGH)


def _col(x, j):
    lane = lax.broadcasted_iota(jnp.int32, x.shape, 1)
    return jnp.sum(jnp.where(lane == j, x, 0.0), axis=-1, keepdims=True)


def _rms(x, g, eps=1e-6):
    return x * lax.rsqrt(jnp.mean(x * x, axis=-1, keepdims=True) + eps) * g


def _ln(x, g, b, eps=1e-5):
    xc = x - jnp.mean(x, axis=-1, keepdims=True)
    var = jnp.mean(xc * xc, axis=-1, keepdims=True)
    return xc * lax.rsqrt(var + eps) * g + b


_sigmoid = jax.nn.sigmoid


def _silu(x):
    return x * _sigmoid(x)


def _softplus(x):
    return jnp.maximum(x, 0.0) + jnp.log1p(jnp.exp(-jnp.abs(x)))


def _gelu(x):
    return 0.5 * x * (1.0 + lax.erf(x * 0.7071067811865476))


def f_norm(x, g):
    return (_rms(x, g),)


def f_glu(a_in):
    return (a_in[:, :MIX_W] * _sigmoid(a_in[:, MIX_W:]),)


def f_lnsilu(ac, cb, g, b):
    return (_silu(_ln(ac + cb, g, b)),)


def f_dnprep(qc, bd, a_log, dt_bias):
    t = qc.shape[0]
    qkv = _silu(qc)
    qs, ks, bs, gs = [], [], [], []
    for h in range(HEADS):
        q = qkv[:, h * HEAD_DIM:(h + 1) * HEAD_DIM]
        k = qkv[:, MIX_W + h * HEAD_DIM:MIX_W + (h + 1) * HEAD_DIM]
        qs.append(q * lax.rsqrt(jnp.sum(q * q, axis=-1, keepdims=True) + 1e-6) * (HEAD_DIM ** -0.5))
        ks.append(k * lax.rsqrt(jnp.sum(k * k, axis=-1, keepdims=True) + 1e-6))
        beta = _sigmoid(_col(bd, h))
        g = -jnp.exp(_col(a_log, h)) * _softplus(_col(bd, HEADS + h) + _col(dt_bias, h))
        bs.append(jnp.broadcast_to(beta, (t, HEAD_DIM)))
        gs.append(jnp.broadcast_to(g, (t, HEAD_DIM)))
    cat = lambda xs: jnp.concatenate(xs, axis=1)
    return cat(qs), cat(ks), qkv[:, 2 * MIX_W:], cat(bs), cat(gs)


def f_dnout(o, dgate, g):
    outs = []
    for h in range(HEADS):
        sl = slice(h * HEAD_DIM, (h + 1) * HEAD_DIM)
        outs.append(_rms(o[:, sl], g) * _silu(dgate[:, sl]))
    return (jnp.concatenate(outs, axis=1),)


def f_gmlp(gm_in, lg, lb, ws, bst):
    t = gm_in.shape[0]
    ge = _gelu(gm_in)
    u, vg = ge[:, :MIX_W], _ln(ge[:, MIX_W:], lg, lb)
    ri = lax.broadcasted_iota(jnp.int32, (GM_CHUNK, GM_CHUNK), 0)
    ci = lax.broadcasted_iota(jnp.int32, (GM_CHUNK, GM_CHUNK), 1)
    chunks = []
    for n in range(t // GM_CHUNK):
        vc = vg[n * GM_CHUNK:(n + 1) * GM_CHUNK]
        cols = []
        for g in range(4):
            w = jnp.where(ri >= ci, ws[g * GM_CHUNK:(g + 1) * GM_CHUNK], 0.0)
            cols.append(_bnn(w, vc[:, g * LANES:(g + 1) * LANES]) + _col(bst, g))
        chunks.append(jnp.concatenate(cols, axis=1))
    mixed = chunks[0] if len(chunks) == 1 else jnp.concatenate(chunks, axis=0)
    return (u * mixed,)


def f_poolpost(pc, pw, ps):
    cols = [_bnn(pc[:, g * LANES:(g + 1) * LANES], pw[g * LANES:(g + 1) * LANES]) for g in range(4)]
    return (jnp.concatenate(cols, axis=1) * ps,)


def f_merge(gate, p0, p1, p2, p3):
    m = None
    for n, p in enumerate((p0, p1, p2, p3)):
        t = _sigmoid(gate[:, n * D_MODEL:(n + 1) * D_MODEL]) * p
        m = t if m is None else m + t
    return (m,)


def f_attn(q, kv):
    outs = []
    for h in range(XA_HEADS):
        sl = slice(h * XA_HEAD_DIM, (h + 1) * XA_HEAD_DIM)
        s = _bnt(q[:, sl], kv[:, sl]) * (XA_HEAD_DIM ** -0.5)
        s = s - lax.stop_gradient(jnp.max(s, axis=-1, keepdims=True))
        p = jnp.exp(s)
        p = p / jnp.sum(p, axis=-1, keepdims=True)
        outs.append(_bnn(p, kv[:, D_MODEL + h * XA_HEAD_DIM:D_MODEL + (h + 1) * XA_HEAD_DIM]))
    return (jnp.concatenate(outs, axis=1),)


def f_loss(x, t, g):
    e = _rms(x, g) - t
    return (0.5 * jnp.mean(e * e, axis=-1, keepdims=True),)


def _row_spec(tr, width, cb):
    return pl.BlockSpec((tr, width), functools.partial(lambda i, cb: (i, cb), cb=cb))


def _full_spec(shape):
    return pl.BlockSpec(shape, lambda i: (0,) * len(shape))


def _rw(f, rows, params, outs, *, name, tr=256):
    s = rows[0][0].shape[0]
    tr = min(tr, s)
    nr, npar = len(rows), len(params)

    def body(*refs):
        rv = [r[...].astype(F32) for r in refs[:nr]]
        pv = [p[...] for p in refs[nr:nr + npar]]
        res = f(*rv, *pv)
        for o_ref, r in zip(refs[nr + npar:], res):
            o_ref[...] = r.astype(o_ref.dtype)

    return pl.pallas_call(
        body, name=name, grid=(s // tr,),
        in_specs=[_row_spec(tr, w, cb) for _, w, cb in rows] + [_full_spec(p.shape) for p in params],
        out_specs=[_row_spec(tr, w, 0) for w, _ in outs],
        out_shape=[jax.ShapeDtypeStruct((s, w), dt) for w, dt in outs],
        compiler_params=_params(("arbitrary",)),
    )(*[a for a, _, _ in rows], *params)


def _rw_bwd(f, rows, params, cts, *, row_grads, param_grads=(), add=None, primal=(), name, tr=256):
    s = rows[0][0].shape[0]
    tr = min(tr, s)
    nr, npar, nct = len(rows), len(params), len(cts)
    add = add or {}
    add_keys = list(add)

    def body(*refs):
        it = iter(refs)
        row_refs = [next(it) for _ in range(nr)]
        par_refs = [next(it) for _ in range(npar)]
        ct_refs = [next(it) for _ in range(nct)]
        add_refs = {k: next(it) for k in add_keys}
        rg_refs = [next(it) for _ in row_grads]
        pg_refs = [next(it) for _ in param_grads]
        pr_refs = [next(it) for _ in primal]
        rv = [r[...].astype(F32) for r in row_refs]
        pv = [p[...] for p in par_refs]
        out, vjp = jax.vjp(f, *rv, *pv)
        grads = vjp(tuple(c[...].astype(F32) for c in ct_refs))
        for (idx, _), ref in zip(row_grads, rg_refs):
            g = grads[idx]
            if idx in add_refs:
                g = g + add_refs[idx][...].astype(F32)
            ref[...] = g.astype(ref.dtype)

        @pl.when(pl.program_id(0) == 0)
        def _():
            for ref in pg_refs:
                ref[...] = jnp.zeros_like(ref)

        for idx, ref in zip(param_grads, pg_refs):
            ref[...] += grads[nr + idx]
        for (idx, _, _), ref in zip(primal, pr_refs):
            ref[...] = out[idx].astype(ref.dtype)

    in_arrays = [a for a, _, _ in rows] + list(params) + [a for a, _, _ in cts] + [add[k][0] for k in add_keys]
    in_specs = ([_row_spec(tr, w, cb) for _, w, cb in rows] + [_full_spec(p.shape) for p in params]
                + [_row_spec(tr, w, cb) for _, w, cb in cts] + [_row_spec(tr, add[k][1], add[k][2]) for k in add_keys])
    out_specs = ([_row_spec(tr, rows[idx][1], 0) for idx, _ in row_grads]
                 + [_full_spec(params[idx].shape) for idx in param_grads]
                 + [_row_spec(tr, w, 0) for _, w, _ in primal])
    out_shape = ([jax.ShapeDtypeStruct((s, rows[idx][1]), dt) for idx, dt in row_grads]
                 + [jax.ShapeDtypeStruct(params[idx].shape, F32) for idx in param_grads]
                 + [jax.ShapeDtypeStruct((s, w), dt) for _, w, dt in primal])
    return pl.pallas_call(
        body, name=name, grid=(s // tr,), in_specs=in_specs, out_specs=out_specs, out_shape=out_shape,
        compiler_params=_params(("arbitrary",)),
    )(*in_arrays)


def _mm(a, b, *, ta=False, tb=False, out_dtype=F32, add=None, post=None, extra=(), out_dtypes=None, name):
    m, k = (a.shape[1], a.shape[0]) if ta else a.shape
    n = b.shape[0] if tb else b.shape[1]
    tm = _pick(m, (1024, 512, 256, 128))
    tn = _pick(n, (1024, 512, 256, 128))
    tk = _pick(k, (1024, 512, 256, 128))
    nk = k // tk
    dims = (((0 if ta else 1,), (1 if tb else 0,)), ((), ()))
    if post is None:
        extra = [add] if add is not None else []
        post = (lambda r, e: (r + e,)) if add is not None else (lambda r: (r,))
    dtypes = out_dtypes or [out_dtype]
    n_ex, n_out = len(extra), len(dtypes)

    def body(*refs):
        a_ref, b_ref = refs[:2]
        ex_refs = refs[2:2 + n_ex]
        o_refs = refs[2 + n_ex:2 + n_ex + n_out]
        part = lax.dot_general(a_ref[...].astype(BF16), b_ref[...].astype(BF16), dims, preferred_element_type=F32)

        def finish(r):
            for o_ref, val in zip(o_refs, post(r, *[e[...] for e in ex_refs])):
                o_ref[...] = val.astype(o_ref.dtype)

        if nk == 1:
            finish(part)
            return
        acc = refs[-1]
        kk = pl.program_id(2)

        @pl.when(kk == 0)
        def _():
            acc[...] = part

        @pl.when((kk > 0) & (kk < nk - 1))
        def _():
            acc[...] += part

        @pl.when(kk == nk - 1)
        def _():
            finish(acc[...] + part)

    a_spec = (pl.BlockSpec((tk, tm), lambda i, j, kk: (kk, i)) if ta else pl.BlockSpec((tm, tk), lambda i, j, kk: (i, kk)))
    b_spec = (pl.BlockSpec((tn, tk), lambda i, j, kk: (j, kk)) if tb else pl.BlockSpec((tk, tn), lambda i, j, kk: (kk, j)))
    o_spec = pl.BlockSpec((tm, tn), lambda i, j, kk: (i, j))
    res = pl.pallas_call(
        body, name=name, grid=(m // tm, n // tn, nk), in_specs=[a_spec, b_spec] + [o_spec] * n_ex,
        out_specs=[o_spec] * n_out, out_shape=[jax.ShapeDtypeStruct((m, n), dt) for dt in dtypes],
        scratch_shapes=[pltpu.VMEM((tm, tn), F32)] if nk > 1 else [],
        compiler_params=_params(("arbitrary", "arbitrary", "arbitrary")),
    )(a, b, *extra)
    return res if out_dtypes else res[0]


CONV_TT = 1024
CONV_SUB = 256


def _halo(k):
    return -(-(k - 1) // 8) * 8


def _pool_count(row0, c, rows):
    win = lax.shift_left(jnp.int32(2), c)
    t = row0 + lax.broadcasted_iota(jnp.int32, (rows, LANES), 0)
    return win, jnp.minimum(t + 1, win).astype(F32)


def _conv_fwd(x, xcb0, channels, w, k, *, pool=False, name):
    s = x.shape[0]
    tt = min(CONV_TT, s)
    sub = min(CONV_SUB, tt)
    nt = s // tt
    halo = _halo(k)

    def body(*refs):
        if pool:
            xc_ref, xp_ref, o_ref, xs = refs
        else:
            xc_ref, xp_ref, w_ref, o_ref, xs = refs
        c, i = pl.program_id(0), pl.program_id(1)
        xs[0:halo, :] = jnp.where(i > 0, xp_ref[...], 0.0)
        xs[halo:halo + tt, :] = xc_ref[...]
        for r0 in range(0, tt, sub):
            acc = jnp.zeros((sub, LANES), F32)
            if pool:
                win, cnt = _pool_count(i * tt + r0, c, sub)
                for j in range(k):
                    acc = acc + jnp.where(j >= k - win, xs[pl.ds(r0 + halo - (k - 1) + j, sub), :], 0.0)
                o_ref[r0:r0 + sub, :] = acc / cnt - xc_ref[r0:r0 + sub, :]
            else:
                for j in range(k):
                    acc = acc + w_ref[j:j + 1, :] * xs[pl.ds(r0 + halo - (k - 1) + j, sub), :]
                o_ref[r0:r0 + sub, :] = acc

    per = tt // halo
    in_specs = [pl.BlockSpec((tt, LANES), lambda c, i: (i, xcb0 + c)),
                pl.BlockSpec((halo, LANES), lambda c, i: (jnp.maximum(i * per - 1, 0), xcb0 + c))]
    args = [x, x]
    if not pool:
        in_specs.append(pl.BlockSpec((k, LANES), lambda c, i: (0, c)))
        args.append(w)
    return pl.pallas_call(
        body, name=name, grid=(channels // LANES, nt), in_specs=in_specs,
        out_specs=pl.BlockSpec((tt, LANES), lambda c, i: (i, c)),
        out_shape=jax.ShapeDtypeStruct((s, channels), F32),
        scratch_shapes=[pltpu.VMEM((halo + tt, LANES), F32)],
        compiler_params=_params(("arbitrary", "arbitrary")),
    )(*args)


def _conv_bwd(x, xcb0, channels, w, k, dy, *, pool=False, dx_dtype=BF16, name):
    s = dy.shape[0]
    tt = min(CONV_TT, s)
    sub = min(CONV_SUB, tt)
    nt = s // tt
    halo = _halo(k)

    def body(*refs):
        if pool:
            dyc_ref, dyn_ref, dx_ref, ys = refs
        else:
            xc_ref, xp_ref, dyc_ref, dyn_ref, w_ref, dx_ref, dw_ref, xs, ys = refs
        c, i = pl.program_id(0), pl.program_id(1)
        dyn = jnp.where(i < nt - 1, dyn_ref[...], 0.0)
        if pool:
            win, cnt = _pool_count(i * tt, c, tt)
            ys[0:tt, :] = dyc_ref[...] / cnt
            ys[tt:tt + halo, :] = dyn / win.astype(F32)
            for r0 in range(0, tt, sub):
                acc = jnp.zeros((sub, LANES), F32)
                for j in range(k):
                    acc = acc + jnp.where(j >= k - win, ys[pl.ds(r0 + (k - 1) - j, sub), :], 0.0)
                dx_ref[r0:r0 + sub, :] = (acc - dyc_ref[r0:r0 + sub, :]).astype(dx_ref.dtype)
            return
        ys[0:tt, :] = dyc_ref[...]
        ys[tt:tt + halo, :] = dyn
        xs[0:halo, :] = jnp.where(i > 0, xp_ref[...], 0.0)
        xs[halo:halo + tt, :] = xc_ref[...]

        @pl.when(i == 0)
        def _():
            dw_ref[...] = jnp.zeros_like(dw_ref)

        for r0 in range(0, tt, sub):
            acc = jnp.zeros((sub, LANES), F32)
            for j in range(k):
                acc = acc + w_ref[j:j + 1, :] * ys[pl.ds(r0 + (k - 1) - j, sub), :]
            dx_ref[r0:r0 + sub, :] = acc.astype(dx_ref.dtype)
            dyc = dyc_ref[r0:r0 + sub, :]
            for j in range(k):
                dw_ref[j:j + 1, :] += jnp.sum(dyc * xs[pl.ds(r0 + halo - (k - 1) + j, sub), :], axis=0, keepdims=True)

    per = tt // halo
    cur = lambda cb0: pl.BlockSpec((tt, LANES), lambda c, i: (i, cb0 + c))
    dy_specs = [cur(0), pl.BlockSpec((halo, LANES), lambda c, i: (jnp.minimum((i + 1) * per, s // halo - 1), c))]
    dx_spec = pl.BlockSpec((tt, LANES), lambda c, i: (i, c))
    dx_shape = jax.ShapeDtypeStruct((s, channels), dx_dtype)
    if pool:
        return pl.pallas_call(
            body, name=name, grid=(channels // LANES, nt), in_specs=dy_specs, out_specs=dx_spec, out_shape=dx_shape,
            scratch_shapes=[pltpu.VMEM((tt + halo, LANES), F32)],
            compiler_params=_params(("arbitrary", "arbitrary")),
        )(dy, dy)
    in_specs = [cur(xcb0), pl.BlockSpec((halo, LANES), lambda c, i: (jnp.maximum(i * per - 1, 0), xcb0 + c))] + dy_specs
    in_specs.append(pl.BlockSpec((k, LANES), lambda c, i: (0, c)))
    return pl.pallas_call(
        body, name=name, grid=(channels // LANES, nt), in_specs=in_specs,
        out_specs=[dx_spec, pl.BlockSpec((k, LANES), lambda c, i: (0, c))],
        out_shape=[dx_shape, jax.ShapeDtypeStruct((k, channels), F32)],
        scratch_shapes=[pltpu.VMEM((halo + tt, LANES), F32), pltpu.VMEM((tt + halo, LANES), F32)],
        compiler_params=_params(("arbitrary", "arbitrary")),
    )(x, x, dy, dy, w)


DN_STEP_ROWS = 256


@jax.custom_vjp
def _chunk_inverse(b):
    r = b.shape[0]
    eye = jnp.where(lax.broadcasted_iota(jnp.int32, (r, r), 0) == lax.broadcasted_iota(jnp.int32, (r, r), 1), 1.0, 0.0)
    inv = eye + b
    bp = b
    for _ in range(5):
        bp = _mnn(bp, bp)
        inv = inv + _mnn(inv, bp)
    return inv


def _chunk_inverse_fwd(b):
    inv = _chunk_inverse(b)
    return inv, inv


def _chunk_inverse_bwd(inv, g):
    return (_mtn(inv, _mnt(g, inv)),)


_chunk_inverse.defvjp(_chunk_inverse_fwd, _chunk_inverse_bwd)


def _dn_step(q, k, v, bb, gb, state):
    c = DN_CHUNK
    r = q.shape[0]
    ri = lax.broadcasted_iota(jnp.int32, (r, r), 0)
    ci = lax.broadcasted_iota(jnp.int32, (r, r), 1)
    same = jnp.bitwise_or(ri, c - 1) == jnp.bitwise_or(ci, c - 1)
    causal = same & (ri >= ci)
    strict = same & (ri > ci)
    eye = jnp.where(ri == ci, 1.0, 0.0)
    gam = _hnn(jnp.where(causal, 1.0, 0.0), gb)
    gam_i = jnp.concatenate([gam] * (r // LANES), axis=1)
    gam_j = gam_i.T
    decay = jnp.where(causal, jnp.exp(jnp.where(causal, gam_i - gam_j, 0.0)), 0.0)
    row = lax.broadcasted_iota(jnp.int32, gam.shape, 0)
    lasts = [jnp.sum(jnp.where(row == n * c + c - 1, gam, 0.0), axis=0, keepdims=True) for n in range(r // c)]
    g_last = jnp.zeros_like(gam)
    for n, gl in enumerate(lasts):
        g_last = jnp.where(jnp.bitwise_or(row, c - 1) == n * c + c - 1, gl, g_last)
    kb = k * bb
    b = -jnp.where(strict, _mnt(kb, k) * decay, 0.0)
    inv = _chunk_inverse(b)
    eg = jnp.exp(gam)
    u = _mnn(inv, v * bb)
    w = _mnn(inv, kb * eg)
    attn = _bnt(q, k) * decay
    q_dec = q * eg
    k_dec = k * jnp.exp(g_last - gam)
    v_news, o_inter = [], []
    for n in range(r // c):
        s = slice(n * c, (n + 1) * c)
        v_new = u[s] - _mnn(w[s], state)
        o_inter.append(_bnn(q_dec[s], state))
        state = state * jnp.exp(lasts[n]) + _mtn(k_dec[s], v_new)
        v_news.append(v_new)
    o = jnp.concatenate(o_inter, axis=0) + _bnn(attn, jnp.concatenate(v_news, axis=0))
    return o, state


def _dn_specs(steps, reverse):
    if reverse:
        tile = pl.BlockSpec((DN_STEP_ROWS, HEAD_DIM), lambda h, i: (steps - 1 - i, h))
        st = pl.BlockSpec((1, 1, HEAD_DIM, HEAD_DIM), lambda h, i: (steps - 1 - i, h, 0, 0))
    else:
        tile = pl.BlockSpec((DN_STEP_ROWS, HEAD_DIM), lambda h, i: (i, h))
        st = pl.BlockSpec((1, 1, HEAD_DIM, HEAD_DIM), lambda h, i: (i, h, 0, 0))
    return tile, st


def _dn_fwd(q, k, v, bb, gb, *, name):
    s = q.shape[0]
    steps = s // DN_STEP_ROWS

    def body(q_ref, k_ref, v_ref, b_ref, g_ref, o_ref, st_ref, state):
        @pl.when(pl.program_id(1) == 0)
        def _():
            state[...] = jnp.zeros_like(state)

        st = state[...]
        st_ref[0, 0] = st
        o, st = _dn_step(q_ref[...], k_ref[...], v_ref[...], b_ref[...], g_ref[...], st)
        o_ref[...] = o
        state[...] = st

    tile, stspec = _dn_specs(steps, False)
    return pl.pallas_call(
        body, name=name, grid=(HEADS, steps), in_specs=[tile] * 5, out_specs=[tile, stspec],
        out_shape=[jax.ShapeDtypeStruct((s, MIX_W), F32),
                   jax.ShapeDtypeStruct((steps, HEADS, HEAD_DIM, HEAD_DIM), F32)],
        scratch_shapes=[pltpu.VMEM((HEAD_DIM, HEAD_DIM), F32)],
        compiler_params=_params(("arbitrary", "arbitrary")),
    )(q, k, v, bb, gb)


def _dn_bwd(q, k, v, bb, gb, states, do, *, name):
    s = q.shape[0]
    steps = s // DN_STEP_ROWS

    def body(q_ref, k_ref, v_ref, b_ref, g_ref, st_ref, do_ref, dq_ref, dk_ref, dv_ref, db_ref, dg_ref, dstate):
        @pl.when(pl.program_id(1) == 0)
        def _():
            dstate[...] = jnp.zeros_like(dstate)

        _, vjp = jax.vjp(_dn_step, q_ref[...], k_ref[...], v_ref[...], b_ref[...], g_ref[...], st_ref[0, 0])
        dq, dk, dv, db, dg, dst = vjp((do_ref[...], dstate[...]))
        dq_ref[...] = dq
        dk_ref[...] = dk
        dv_ref[...] = dv
        db_ref[...] = db
        dg_ref[...] = dg
        dstate[...] = dst

    tile, stspec = _dn_specs(steps, True)
    return pl.pallas_call(
        body, name=name, grid=(HEADS, steps), in_specs=[tile] * 5 + [stspec, tile], out_specs=[tile] * 5,
        out_shape=[jax.ShapeDtypeStruct((s, MIX_W), F32)] * 5,
        scratch_shapes=[pltpu.VMEM((HEAD_DIM, HEAD_DIM), F32)],
        compiler_params=_params(("arbitrary", "arbitrary")),
    )(q, k, v, bb, gb, states, do)


EW_TILE_BYTES = 2 << 20


def _ew(f, ins, out_dtypes, *, name):
    r, c = ins[0][0].shape[1:] if isinstance(ins[0], tuple) else ins[0].shape
    row_bytes = -(-c // LANES) * LANES * 4
    tr = r
    if r * row_bytes > EW_TILE_BYTES:
        tr = next((t for t in (4096, 2048, 1024, 512, 256, 128, 64, 32, 16, 8)
                   if r % t == 0 and t * row_bytes <= EW_TILE_BYTES), r)
    n_in = len(ins)

    def body(*refs):
        res = f(*[x[...] for x in refs[:n_in]])
        for o_ref, v in zip(refs[n_in:], res):
            o_ref[...] = v.astype(o_ref.dtype)

    in_specs, args = [], []
    for x in ins:
        if isinstance(x, tuple):
            in_specs.append(pl.BlockSpec((None, tr, c), functools.partial(lambda i, j: (j, i, 0), j=x[1])))
            args.append(x[0])
        else:
            in_specs.append(pl.BlockSpec((tr, c), lambda i: (i, 0)))
            args.append(x)
    return pl.pallas_call(
        body, name=name, grid=(r // tr,), in_specs=in_specs,
        out_specs=[pl.BlockSpec((tr, c), lambda i: (i, 0)) for _ in out_dtypes],
        out_shape=[jax.ShapeDtypeStruct((r, c), dt) for dt in out_dtypes],
        compiler_params=_params(("arbitrary",)),
    )(*args)


def f_adamw(w, g, m, v):
    m = ADAM_B1 * m + (1.0 - ADAM_B1) * g
    v = ADAM_B2 * v + (1.0 - ADAM_B2) * (g * g)
    m_hat = m / (1.0 - ADAM_B1 ** ADAM_STEP)
    v_hat = v / (1.0 - ADAM_B2 ** ADAM_STEP)
    return -ADAM_LR * (m_hat / (jnp.sqrt(v_hat) + ADAM_EPS) + ADAM_WD * w), m, v


def f_sum4(a, b, c, d):
    return (((a.astype(F32) + b.astype(F32)) + c.astype(F32)) + d.astype(F32),)


def f_add2(a, b):
    return (a + b,)


def _adamw(w, g, m, v, *, name):
    shape = w.shape
    two = (1, shape[0]) if w.ndim == 1 else (-1, shape[-1])
    outs = _ew(f_adamw, [t.reshape(two) for t in (w, g, m, v)], [F32, F32, F32], name=name)
    return [o.reshape(shape) for o in outs]


_ANY = pl.BlockSpec(memory_space=pl.ANY)


def _xy_exchange(srcs, *, broadcast, name):
    nt = len(srcs)

    def body(*refs):
        src_refs, out_refs = refs[:nt], refs[nt:2 * nt]
        send_sems, recv_sems, local_sems = refs[2 * nt:]
        x, y, c = lax.axis_index("x"), lax.axis_index("y"), lax.axis_index("c")
        me = 2 * x + y
        peers = [(1 - x, y), (x, 1 - y), (1 - x, 1 - y)]

        def block(t, j):
            return src_refs[t] if broadcast else src_refs[t].at[j]

        def copy(t, n, px, py, src_blk, dst_blk):
            return pltpu.make_async_remote_copy(
                src_ref=src_blk, dst_ref=dst_blk, send_sem=send_sems.at[3 * t + n], recv_sem=recv_sems.at[3 * t + n],
                device_id=(px, py, c), device_id_type=pl.DeviceIdType.MESH)

        mine = [pltpu.make_async_copy(block(t, me), out_refs[t].at[me], local_sems.at[t]) for t in range(nt)]
        sends = [copy(t, n, px, py, block(t, 2 * px + py), out_refs[t].at[me])
                 for t in range(nt) for n, (px, py) in enumerate(peers)]
        for cp in mine + sends:
            cp.start()
        for cp in sends:
            cp.wait_send()
        for t in range(nt):
            for n, (px, py) in enumerate(peers):
                copy(t, n, px, py, block(t, me), out_refs[t].at[2 * px + py]).wait_recv()
        for cp in mine:
            cp.wait()

    shapes = [(4,) + tuple(s.shape) if broadcast else tuple(s.shape) for s in srcs]
    return pl.pallas_call(
        body, name=name, in_specs=[_ANY] * nt, out_specs=[_ANY] * nt,
        out_shape=[jax.ShapeDtypeStruct(shp, s.dtype) for shp, s in zip(shapes, srcs)],
        scratch_shapes=[pltpu.SemaphoreType.DMA((3 * nt,)), pltpu.SemaphoreType.DMA((3 * nt,)),
                        pltpu.SemaphoreType.DMA((nt,))],
        compiler_params=pltpu.CompilerParams(has_side_effects=True),
    )(*srcs)


def _c_swap(srcs, *, name):
    nt = len(srcs)

    def body(*refs):
        src_refs, out_refs = refs[:nt], refs[nt:2 * nt]
        send_sems, recv_sems = refs[2 * nt:]
        sibling = (lax.axis_index("x"), lax.axis_index("y"), 1 - lax.axis_index("c"))
        cps = [pltpu.make_async_remote_copy(src_ref=src_refs[t], dst_ref=out_refs[t], send_sem=send_sems.at[t],
                                            recv_sem=recv_sems.at[t], device_id=sibling,
                                            device_id_type=pl.DeviceIdType.MESH) for t in range(nt)]
        for cp in cps:
            cp.start()
        for cp in cps:
            cp.wait()

    return pl.pallas_call(
        body, name=name, in_specs=[_ANY] * nt, out_specs=[_ANY] * nt,
        out_shape=[jax.ShapeDtypeStruct(s.shape, s.dtype) for s in srcs],
        scratch_shapes=[pltpu.SemaphoreType.DMA((nt,)), pltpu.SemaphoreType.DMA((nt,))],
        compiler_params=pltpu.CompilerParams(has_side_effects=True),
    )(*srcs)


_HBM = pl.BlockSpec(memory_space=pltpu.HBM)
_SEM = pl.BlockSpec(memory_space=pltpu.SEMAPHORE)
_DATAFLOW = pltpu.SideEffectType.DATAFLOW_SIDE_EFFECTING


def _xy_peers():
    x, y, c = lax.axis_index("x"), lax.axis_index("y"), lax.axis_index("c")
    return 2 * x + y, c, [(1 - x, y), (x, 1 - y), (1 - x, 1 - y)]


def _xy_start(srcs, *, broadcast, name):
    nt = len(srcs)
    lands = [lax.empty((4,) + tuple(s.shape) if broadcast else tuple(s.shape), s.dtype) for s in srcs]

    def body(*refs):
        src_refs, land_refs = refs[:nt], refs[nt:2 * nt]
        send_sems, recv_sems = refs[2 * nt], refs[2 * nt + 1]
        token = refs[-1]
        me, c, peers = _xy_peers()
        for t in range(nt):
            for n, (px, py) in enumerate(peers):
                src_blk = src_refs[t] if broadcast else src_refs[t].at[2 * px + py]
                pltpu.make_async_remote_copy(
                    src_ref=src_blk, dst_ref=land_refs[t].at[me], send_sem=send_sems.at[3 * t + n],
                    recv_sem=recv_sems.at[3 * t + n], device_id=(px, py, c),
                    device_id_type=pl.DeviceIdType.MESH).start()
        token[...] = jnp.zeros_like(token)

    res = pl.pallas_call(
        body, name=name, in_specs=[_HBM] * (2 * nt),
        out_specs=[_SEM, _SEM] + [_HBM] * (2 * nt) + [pl.BlockSpec(memory_space=pltpu.VMEM)],
        out_shape=[pltpu.SemaphoreType.DMA((3 * nt,)), pltpu.SemaphoreType.DMA((3 * nt,))]
        + [pltpu.HBM(a.shape, a.dtype) for a in list(srcs) + lands] + [jax.ShapeDtypeStruct((8, LANES), F32)],
        input_output_aliases={i: 2 + i for i in range(2 * nt)},
        compiler_params=pltpu.CompilerParams(has_side_effects=_DATAFLOW),
    )(*[pltpu.with_memory_space_constraint(a, pltpu.HBM) for a in list(srcs) + lands])
    return res[0], res[1], res[2:2 + nt], res[2 + nt:2 + 2 * nt], res[-1]


def _xy_wait(started, after, *, broadcast, name):
    send_sems, recv_sems, srcs, lands, _ = started
    nt = len(srcs)

    def body(*refs):
        src_refs, land_refs = refs[:nt], refs[nt:2 * nt]
        send_sems, recv_sems = refs[2 * nt], refs[2 * nt + 1]
        me, c, peers = _xy_peers()
        for t in range(nt):
            for n, (px, py) in enumerate(peers):
                src_blk = src_refs[t] if broadcast else src_refs[t].at[me]
                cp = pltpu.make_async_remote_copy(
                    src_ref=src_blk, dst_ref=land_refs[t].at[2 * px + py], send_sem=send_sems.at[3 * t + n],
                    recv_sem=recv_sems.at[3 * t + n], device_id=(px, py, c), device_id_type=pl.DeviceIdType.MESH)
                cp.wait_send()
                cp.wait_recv()

    res = pl.pallas_call(
        body, name=name, in_specs=[_HBM] * (2 * nt) + [_SEM, _SEM, _ANY], out_specs=[_HBM] * (2 * nt),
        out_shape=[pltpu.HBM(a.shape, a.dtype) for a in list(srcs) + list(lands)],
        input_output_aliases={i: i for i in range(2 * nt)},
        compiler_params=pltpu.CompilerParams(has_side_effects=_DATAFLOW),
    )(*srcs, *lands, send_sems, recv_sems, after)
    return res[:nt], res[nt:]


def _all_sum(srcs, *, broadcast, name):
    got = _xy_exchange(srcs, broadcast=broadcast, name=name + "_xy")
    parts = []
    for t, gt in enumerate(got):
        g3 = gt.reshape(4, -1, gt.shape[-1])
        parts.append(_ew(f_sum4, [(g3, j) for j in range(4)], [F32], name=f"{name}_sum4_{t}")[0])
    others = _c_swap(parts, name=name + "_c")
    return [_ew(f_add2, [p, o], [F32], name=f"{name}_add2_{t}")[0].reshape(gt.shape[1:])
            for t, (p, o, gt) in enumerate(zip(parts, others, got))]


def _flat_rows(parts, dtype, row_multiple=8):
    flat = jnp.concatenate([p.reshape(-1).astype(dtype) for p in parts])
    pad = (-flat.shape[0]) % (row_multiple * LANES)
    if pad:
        flat = jnp.concatenate([flat, jnp.zeros((pad,), dtype)])
    return flat.reshape(-1, LANES)


def _unflat(buf, shapes):
    flat = buf.reshape(-1)
    out, off = [], 0
    for shp in shapes:
        n = 1
        for d in shp:
            n *= d
        out.append(flat[off:off + n].reshape(shp))
        off += n
    return out


def _regroup_w_in(w):
    cols = [w[:, a:b] for a, b in _Z_SEGS] + [jnp.zeros((w.shape[0], Z_W - IN_W), w.dtype)]
    return jnp.concatenate(cols, axis=1)


def _ungroup_w_in(wz):
    starts, off = {}, 0
    for a, b in _Z_SEGS:
        starts[a] = (off, off + b - a)
        off += b - a
    return jnp.concatenate([wz[:, starts[a][0]:starts[a][1]] for a in sorted(starts)], axis=1)


def _row(vec):
    return vec.reshape(1, -1)


def _lane_pad(vec):
    return jnp.zeros((1, LANES), F32).at[0, :vec.shape[0]].set(vec)


def _layer_fwd(x, mem, w, l):
    nm = lambda s: f"{s}_l{l}"
    sv = {"x0": x}
    h = _rw(f_norm, [(x, D_MODEL, 0)], [_row(w["norm_mix"])], [(D_MODEL, BF16)], name=nm("norm_mix"))[0]
    z = _mm(h, w["w_in_z"], name=nm("in_proj"))
    a_glu = _rw(f_glu, [(z, 2 * MIX_W, Z_A // (2 * MIX_W))], [], [(MIX_W, F32)], name=nm("glu"))[0]
    ac = _conv_fwd(a_glu, 0, MIX_W, w["conv_a_w"], CONV_K, name=nm("conv_a"))
    a_par = [_row(w["conv_a_b"]), _row(w["ln_a_g"]), _row(w["ln_a_b"])]
    a_out = _rw(f_lnsilu, [(ac, MIX_W, 0)], a_par, [(MIX_W, BF16)], name=nm("ln_a"))[0]
    qc = _conv_fwd(z, Z_QKV // LANES, 3 * MIX_W, w["dn_conv_w"], DN_CONV_K, name=nm("conv_dn"))
    dn_par = [_lane_pad(w["dn_a_log"]), _lane_pad(w["dn_dt_bias"])]
    qn, kn, v, bb, gb = _rw(f_dnprep, [(qc, 3 * MIX_W, 0), (z, LANES, Z_BD // LANES)], dn_par,
                            [(MIX_W, F32)] * 5, name=nm("dn_prep"))
    o, states = _dn_fwd(qn, kn, v, bb, gb, name=nm("dn_scan"))
    o_out = _rw(f_dnout, [(o, MIX_W, 0), (z, MIX_W, Z_DG // MIX_W)], [_row(w["dn_norm_g"])], [(MIX_W, BF16)],
                name=nm("dn_out"))[0]
    gm_par = [_row(w["gm_ln_g"]), _row(w["gm_ln_b"]), w["gm_ws"].reshape(4 * GM_CHUNK, GM_CHUNK), w["gm_bs"].T]
    c_out = _rw(f_gmlp, [(z, 2 * MIX_W, Z_GM // (2 * MIX_W))], gm_par, [(MIX_W, BF16)], name=nm("gmlp"))[0]
    pc = _conv_fwd(z, Z_POOL // LANES, MIX_W, None, POOL_K, pool=True, name=nm("pool"))
    p_par = [w["pool_w"].reshape(4 * LANES, LANES), _row(w["pool_scale"])]
    p_out = _rw(f_poolpost, [(pc, MIX_W, 0)], p_par, [(MIX_W, BF16)], name=nm("pool_post"))[0]
    branches = [a_out, o_out, c_out, p_out]
    proj = [_mm(br, w["w_branch"][n], name=nm(f"branch{n}")) for n, br in enumerate(branches)]
    merged = _rw(f_merge, [(z, N_BRANCH * D_MODEL, 0)] + [(p, D_MODEL, 0) for p in proj], [], [(D_MODEL, BF16)],
                 name=nm("merge"), tr=128)[0]
    x1 = _mm(merged, w["w_out"], add=x, name=nm("out_proj"))
    sv.update(h=h, z=z, a_glu=a_glu, ac=ac, qc=qc, qn=qn, kn=kn, v=v, bb=bb, gb=gb, o=o, states=states, pc=pc,
              branches=branches, proj=proj, merged=merged, x1=x1)
    h2 = _rw(f_norm, [(x1, D_MODEL, 0)], [_row(w["norm_xa"])], [(D_MODEL, BF16)], name=nm("norm_xa"))[0]
    mn = _rw(f_norm, [(mem, D_MODEL, 0)], [_row(w["norm_mem"])], [(D_MODEL, BF16)], name=nm("norm_mem"))[0]
    kv = _mm(mn, w["xa_wkv"], name=nm("xa_kv"))
    q = _mm(h2, w["xa_wq"], name=nm("xa_q"))
    att = _rw(f_attn, [(q, D_MODEL, 0)], [kv], [(D_MODEL, BF16)], name=nm("xa_attn"))[0]
    x2 = _mm(att, w["xa_wo"], add=x1, name=nm("xa_o"))
    sv.update(h2=h2, mn=mn, kv=kv, q=q, att=att, x2=x2)
    h3 = _rw(f_norm, [(x2, D_MODEL, 0)], [_row(w["norm_mlp"])], [(D_MODEL, BF16)], name=nm("norm_mlp"))[0]
    u, act = _mm(h3, w["mlp_w1"], post=lambda r: (r, jnp.square(jnp.maximum(r, 0.0))), out_dtypes=[F32, BF16],
                 name=nm("mlp_up"))
    x3 = _mm(act, w["mlp_w2"], add=x2, name=nm("mlp_down"))
    sv.update(h3=h3, u=u, act=act)
    return x3, sv


def _layer_bwd(dx, mem, w, sv, l, after_attention, at_end):
    nm = lambda s: f"{s}_bwd_l{l}"
    s = dx.shape[0]
    g = {}
    du = _mm(dx, w["mlp_w2"], tb=True, extra=[sv["u"]], post=lambda r, u: (r * (2.0 * jnp.maximum(u, 0.0)),),
             out_dtypes=[BF16], name=nm("mlp_down_dx"))[0]
    g["mlp_w2"] = _mm(sv["act"], dx, ta=True, out_dtype=BF16, name=nm("mlp_down_dw"))
    g["mlp_w1"] = _mm(sv["h3"], du, ta=True, out_dtype=BF16, name=nm("mlp_up_dw"))
    dh3 = _mm(du, w["mlp_w1"], tb=True, name=nm("mlp_up_dx"))
    dx2, g["norm_mlp"] = _rw_bwd(f_norm, [(sv["x2"], D_MODEL, 0)], [_row(w["norm_mlp"])], [(dh3, D_MODEL, 0)],
                                 row_grads=[(0, F32)], param_grads=[0], add={0: (dx, D_MODEL, 0)}, name=nm("norm_mlp"))
    datt = _mm(dx2, w["xa_wo"], tb=True, name=nm("xa_o_dx"))
    g["xa_wo"] = _mm(sv["att"], dx2, ta=True, out_dtype=BF16, name=nm("xa_o_dw"))
    dq, dkv = _rw_bwd(f_attn, [(sv["q"], D_MODEL, 0)], [sv["kv"]], [(datt, D_MODEL, 0)], row_grads=[(0, BF16)],
                      param_grads=[0], name=nm("xa_attn"))
    g["xa_wq"] = _mm(sv["h2"], dq, ta=True, out_dtype=BF16, name=nm("xa_q_dw"))
    dh2 = _mm(dq, w["xa_wq"], tb=True, name=nm("xa_q_dx"))
    g["xa_wkv"] = _mm(sv["mn"], dkv, ta=True, out_dtype=BF16, name=nm("xa_kv_dw"))
    dmn = _mm(dkv, w["xa_wkv"], tb=True, name=nm("xa_kv_dx"))
    g["norm_mem"] = _rw_bwd(f_norm, [(mem, D_MODEL, 0)], [_row(w["norm_mem"])], [(dmn, D_MODEL, 0)], row_grads=[],
                            param_grads=[0], name=nm("norm_mem"))[0]
    gain = _row(w["norm_xa"] + after_attention(g))
    dx1, g["norm_xa"] = _rw_bwd(f_norm, [(sv["x1"], D_MODEL, 0)], [gain], [(dh2, D_MODEL, 0)],
                                row_grads=[(0, F32)], param_grads=[0], add={0: (dx2, D_MODEL, 0)}, name=nm("norm_xa"))
    z = sv["z"]
    dmerged = _mm(dx1, w["w_out"], tb=True, name=nm("out_proj_dx"))
    g["w_out"] = _mm(sv["merged"], dx1, ta=True, out_dtype=BF16, name=nm("out_proj_dw"))
    mg = _rw_bwd(f_merge, [(z, N_BRANCH * D_MODEL, 0)] + [(p, D_MODEL, 0) for p in sv["proj"]], [],
                 [(dmerged, D_MODEL, 0)], row_grads=[(i, BF16) for i in range(5)], name=nm("merge"), tr=128)
    dgate, dproj = mg[0], mg[1:]
    g["w_branch"] = jnp.stack([_mm(br, dp, ta=True, out_dtype=BF16, name=nm(f"branch{n}_dw"))
                               for n, (br, dp) in enumerate(zip(sv["branches"], dproj))])
    dbr = [_mm(dp, w["w_branch"][n], tb=True, name=nm(f"branch{n}_dx")) for n, dp in enumerate(dproj)]
    p_par = [w["pool_w"].reshape(4 * LANES, LANES), _row(w["pool_scale"])]
    dpc, dpw, g["pool_scale"] = _rw_bwd(f_poolpost, [(sv["pc"], MIX_W, 0)], p_par, [(dbr[3], MIX_W, 0)],
                                        row_grads=[(0, F32)], param_grads=[0, 1], name=nm("pool_post"))
    g["pool_w"] = dpw.reshape(4, LANES, LANES)
    dpool = _conv_bwd(None, 0, MIX_W, None, POOL_K, dpc, pool=True, name=nm("pool"))
    gm_par = [_row(w["gm_ln_g"]), _row(w["gm_ln_b"]), w["gm_ws"].reshape(4 * GM_CHUNK, GM_CHUNK), w["gm_bs"].T]
    dgm, g["gm_ln_g"], g["gm_ln_b"], dws, dbst = _rw_bwd(
        f_gmlp, [(z, 2 * MIX_W, Z_GM // (2 * MIX_W))], gm_par, [(dbr[2], MIX_W, 0)], row_grads=[(0, BF16)],
        param_grads=[0, 1, 2, 3], name=nm("gmlp"))
    g["gm_ws"] = dws.reshape(4, GM_CHUNK, GM_CHUNK)
    g["gm_bs"] = dbst.T
    do, ddg, g["dn_norm_g"] = _rw_bwd(f_dnout, [(sv["o"], MIX_W, 0), (z, MIX_W, Z_DG // MIX_W)], [_row(w["dn_norm_g"])],
                                      [(dbr[1], MIX_W, 0)], row_grads=[(0, F32), (1, BF16)], param_grads=[0],
                                      name=nm("dn_out"))
    dqn, dkn, dv, dbb, dgb = _dn_bwd(sv["qn"], sv["kn"], sv["v"], sv["bb"], sv["gb"], sv["states"], do, name=nm("dn_scan"))
    dn_par = [_lane_pad(w["dn_a_log"]), _lane_pad(w["dn_dt_bias"])]
    dqc, dbd, dal, ddt = _rw_bwd(
        f_dnprep, [(sv["qc"], 3 * MIX_W, 0), (z, LANES, Z_BD // LANES)], dn_par,
        [(t, MIX_W, 0) for t in (dqn, dkn, dv, dbb, dgb)], row_grads=[(0, F32), (1, BF16)], param_grads=[0, 1],
        name=nm("dn_prep"))
    g["dn_a_log"], g["dn_dt_bias"] = dal[0, :HEADS], ddt[0, :HEADS]
    dqkv, g["dn_conv_w"] = _conv_bwd(z, Z_QKV // LANES, 3 * MIX_W, w["dn_conv_w"], DN_CONV_K, dqc, name=nm("conv_dn"))
    a_par = [_row(w["conv_a_b"]), _row(w["ln_a_g"]), _row(w["ln_a_b"])]
    dac, g["conv_a_b"], g["ln_a_g"], g["ln_a_b"] = _rw_bwd(
        f_lnsilu, [(sv["ac"], MIX_W, 0)], a_par, [(dbr[0], MIX_W, 0)], row_grads=[(0, F32)], param_grads=[0, 1, 2],
        name=nm("ln_a"))
    dglu, g["conv_a_w"] = _conv_bwd(sv["a_glu"], 0, MIX_W, w["conv_a_w"], CONV_K, dac, dx_dtype=F32, name=nm("conv_a"))
    da_in = _rw_bwd(f_glu, [(z, 2 * MIX_W, Z_A // (2 * MIX_W))], [], [(dglu, MIX_W, 0)], row_grads=[(0, BF16)],
                    name=nm("glu"))[0]
    dz = jnp.concatenate([dgate, da_in, dgm, dqkv, ddg, dpool, dbd, jnp.zeros((s, Z_W - Z_BD - LANES), BF16)], axis=1)
    g["w_in"] = _ungroup_w_in(_mm(sv["h"], dz, ta=True, out_dtype=BF16, name=nm("in_proj_dw")))
    dh = _mm(dz, w["w_in_z"], tb=True, name=nm("in_proj_dx"))
    gain = _row(w["norm_mix"] + at_end(g))
    dx0, g["norm_mix"] = _rw_bwd(f_norm, [(sv["x0"], D_MODEL, 0)], [gain], [(dh, D_MODEL, 0)],
                                 row_grads=[(0, F32)], param_grads=[0], add={0: (dx1, D_MODEL, 0)}, name=nm("norm_mix"))
    for n in ("norm_mlp", "norm_xa", "norm_mem", "norm_mix", "pool_scale", "gm_ln_g", "gm_ln_b", "dn_norm_g",
              "conv_a_b", "ln_a_g", "ln_a_b"):
        g[n] = g[n].reshape(-1)
    return dx0, g


def _shard_slice(a, axis, j):
    n = a.shape[axis] // 4
    return lax.slice_in_dim(a, j * n, (j + 1) * n, axis=axis)


def kernel(x, mem, norm_mix, w_in, conv_a_w, conv_a_b, ln_a_g, ln_a_b, dn_conv_w, dn_a_log, dn_dt_bias, dn_norm_g, gm_ln_g, gm_ln_b, gm_ws, gm_bs, pool_w, pool_scale, w_branch, w_out, norm_xa, norm_mem, xa_wq, xa_wkv, xa_wo, norm_mlp, mlp_w1, mlp_w2, norm_f, loss_target, m_norm_mix, m_w_in, m_conv_a_w, m_conv_a_b, m_ln_a_g, m_ln_a_b, m_dn_conv_w, m_dn_a_log, m_dn_dt_bias, m_dn_norm_g, m_gm_ln_g, m_gm_ln_b, m_gm_ws, m_gm_bs, m_pool_w, m_pool_scale, m_w_branch, m_w_out, m_norm_xa, m_norm_mem, m_xa_wq, m_xa_wkv, m_xa_wo, m_norm_mlp, m_mlp_w1, m_mlp_w2, m_norm_f, v_norm_mix, v_w_in, v_conv_a_w, v_conv_a_b, v_ln_a_g, v_ln_a_b, v_dn_conv_w, v_dn_a_log, v_dn_dt_bias, v_dn_norm_g, v_gm_ln_g, v_gm_ln_b, v_gm_ws, v_gm_bs, v_pool_w, v_pool_scale, v_w_branch, v_w_out, v_norm_xa, v_norm_mem, v_xa_wq, v_xa_wkv, v_xa_wo, v_norm_mlp, v_mlp_w1, v_mlp_w2, v_norm_f):
    given = dict(locals())
    wts = {n: given[n] for n in WEIGHTS}
    depth = norm_mix.shape[0]
    x = x[0]
    mem = mem[0]
    tgt = loss_target[0]

    me = 2 * lax.axis_index("x") + lax.axis_index("y")
    sharded = BIG + CONVS
    shard_axis = dict(BIG_AXIS, conv_a_w=1, dn_conv_w=1)

    def shards(l):
        return [wts[n][l].astype(BF16) for n in BIG] + [wts[n][l] for n in CONVS]

    def assemble(l, waited):
        mine, landed = waited
        w = {}
        for n, own, got in zip(sharded, mine, landed):
            got = lax.dynamic_update_index_in_dim(got, own[None], me, 0)
            w[n] = jnp.concatenate([got[j] for j in range(4)], axis=shard_axis[n])
        w["w_in_z"] = _regroup_w_in(w.pop("w_in"))
        for n in SMALL:
            if n != "norm_f" and n not in CONVS:
                w[n] = wts[n][l]
        return w

    saved, layer_w = [], []
    gather = _xy_start(shards(0), broadcast=True, name="gather_start_l0")
    after = gather[4]
    for l in range(depth):
        waited = _xy_wait(gather, after, broadcast=True, name=f"gather_wait_l{l}")
        w = assemble(l, waited)
        if l + 1 < depth:
            waited, nxt = lax.optimization_barrier((waited, shards(l + 1)))
            gather = _xy_start(nxt, broadcast=True, name=f"gather_start_l{l + 1}")
            w["norm_mix"] = w["norm_mix"] + gather[4][0, 0]
        x, sv = _layer_fwd(x, mem, w, l)
        after = x
        saved.append(sv)
        layer_w.append(w)
    dx, g_norm_f, loss_rows = _rw_bwd(
        f_loss, [(x, D_MODEL, 0), (tgt, D_MODEL, 0)], [_row(norm_f)], [(jnp.ones((x.shape[0], 1), F32), 1, 0)],
        row_grads=[(0, F32)], param_grads=[0], primal=[(0, 1, F32)], name="loss_head")
    loss = lax.psum(jnp.sum(loss_rows), ("x", "y", "c"))

    early = ("mlp_w1", "mlp_w2", "xa_wq", "xa_wkv", "xa_wo")
    late = ("w_in", "w_branch", "w_out")
    grads, reduces = [None] * depth, []

    def start_reduce(g, names, tag):
        blocks = [jnp.stack([_shard_slice(g[n], BIG_AXIS[n], j) for j in range(4)]).astype(BF16) for n in names]
        reduces.append((tag, names, _xy_start(blocks, broadcast=False, name=f"reduce_start_{tag}")))
        return reduces[-1][2][4][0, 0]

    for l in reversed(range(depth)):
        dx, grads[l] = _layer_bwd(dx, mem, layer_w[l], saved[l], l,
                                  functools.partial(start_reduce, names=early, tag=f"l{l}a"),
                                  functools.partial(start_reduce, names=late, tag=f"l{l}b"))
        saved[l] = None
    grad_x = dx[None]

    parts, keys = [], []
    for tag, names, started_reduce in reduces:
        mine, landed = _xy_wait(started_reduce, dx, broadcast=False, name=f"reduce_wait_{tag}")
        for n, own, got in zip(names, mine, landed):
            got = lax.dynamic_update_index_in_dim(got, lax.dynamic_index_in_dim(own, me, 0), me, 0)
            g3 = got.reshape(4, -1, got.shape[-1])
            parts.append(_ew(f_sum4, [(g3, j) for j in range(4)], [F32], name=f"reduce_sum4_{tag}_{n}")[0])
            keys.append((n, int(tag[1:-1])))
    others = _c_swap(parts, name="reduce_big_c")
    sums = {key: _ew(f_add2, [p, o], [F32], name=f"reduce_add2_{key[0]}_l{key[1]}")[0]
            for key, p, o in zip(keys, parts, others)}
    gw = {n: jnp.stack([sums[n, l].reshape(wts[n].shape[1:]) for l in range(depth)]) for n in BIG}

    small_names = [n for n in SMALL if n != "norm_f"]
    small_shapes = [(depth,) + (wts[n].shape[1:2] + (4 * wts[n].shape[2],) if n in CONVS else wts[n].shape[1:])
                    for n in small_names]
    small_buf = _flat_rows([jnp.stack([grads[l][n] for l in range(depth)]) for n in small_names] + [g_norm_f], F32,
                           row_multiple=1024)
    small_sum = _all_sum([small_buf], broadcast=True, name="reduce_small")[0]
    small_parts = _unflat(small_sum, small_shapes + [norm_f.shape])
    me = 2 * lax.axis_index("x") + lax.axis_index("y")
    for n, t in zip(small_names + ["norm_f"], small_parts):
        if n in CONVS:
            width = wts[n].shape[2]
            t = lax.dynamic_slice_in_dim(t, me * width, width, axis=2)
        gw[n] = t

    deltas, new_m, new_v = {}, {}, {}
    for n in WEIGHTS:
        deltas[n], new_m[n], new_v[n] = _adamw(wts[n], gw[n], given["m_" + n], given["v_" + n], name=f"adamw_{n}")
    return (loss, grad_x, *[gw[n] for n in WEIGHTS], *[deltas[n] for n in WEIGHTS], *[new_m[n] for n in WEIGHTS],
            *[new_v[n] for n in WEIGHTS])
```

```python
import functools

import jax
import jax.numpy as jnp
from jax import lax
from jax.experimental import pallas as pl
from jax.experimental.pallas import tpu as pltpu

F32 = jnp.float32
BF16 = jnp.bfloat16
HI = lax.Precision.HIGHEST

D_MODEL = 1024
MIX_W = 512
N_BRANCH = 4
HEADS = 4
HEAD_DIM = 128
CONV_K = 31
DN_CONV_K = 4
DN_CHUNK = 64
GM_CHUNK = 128
POOL_K = 16
XA_HEADS = 4
XA_HEAD_DIM = 256
FFN_W = 4096
IN_W = 8712
Z_W = 9216
LANES = 128
VMEM_LIMIT = 56 * 1024 * 1024

ADAM_LR, ADAM_B1, ADAM_B2, ADAM_EPS, ADAM_WD, ADAM_STEP = 0.001, 0.9, 0.999, 1e-08, 0.01, 10

Z_GATE, Z_A, Z_GM, Z_QKV, Z_DG, Z_POOL, Z_BD = 0, 4096, 5120, 6144, 7680, 8192, 8704
_Z_SEGS = ((4616, 8712), (0, 1024), (3080, 4104), (1024, 2560), (2560, 3072), (4104, 4616), (3072, 3080))

BIG = ("w_in", "w_branch", "w_out", "xa_wq", "xa_wkv", "xa_wo", "mlp_w1", "mlp_w2")
BIG_AXIS = {"w_in": 1, "w_branch": 2, "w_out": 0, "xa_wq": 0, "xa_wkv": 1, "xa_wo": 0, "mlp_w1": 1, "mlp_w2": 0}
CONVS = ("conv_a_w", "dn_conv_w")
WEIGHTS = ("norm_mix", "w_in", "conv_a_w", "conv_a_b", "ln_a_g", "ln_a_b", "dn_conv_w", "dn_a_log", "dn_dt_bias",
           "dn_norm_g", "gm_ln_g", "gm_ln_b", "gm_ws", "gm_bs", "pool_w", "pool_scale", "w_branch", "w_out",
           "norm_xa", "norm_mem", "xa_wq", "xa_wkv", "xa_wo", "norm_mlp", "mlp_w1", "mlp_w2", "norm_f")
SMALL = tuple(n for n in WEIGHTS if n not in BIG)


def _params(sem=None):
    return pltpu.CompilerParams(vmem_limit_bytes=VMEM_LIMIT, dimension_semantics=sem)


def _pick(n, cands):
    for c in cands:
        if n % c == 0:
            return c
    return n


def _make_dots(prec):
    def raw(a, b, ca, cb):
        if prec is None:
            a, b = a.astype(BF16), b.astype(BF16)
        return lax.dot_general(a, b, (((ca,), (cb,)), ((), ())), precision=prec, preferred_element_type=F32)

    nn = jax.custom_vjp(lambda a, b: raw(a, b, 1, 0))
    nt = jax.custom_vjp(lambda a, b: raw(a, b, 1, 1))
    tn = jax.custom_vjp(lambda a, b: raw(a, b, 0, 0))
    nn.defvjp(lambda a, b: (raw(a, b, 1, 0), (a, b)), lambda r, g: (raw(g, r[1], 1, 1), raw(r[0], g, 0, 0)))
    nt.defvjp(lambda a, b: (raw(a, b, 1, 1), (a, b)), lambda r, g: (raw(g, r[1], 1, 0), raw(g, r[0], 0, 0)))
    tn.defvjp(lambda a, b: (raw(a, b, 0, 0), (a, b)), lambda r, g: (raw(r[1], g, 1, 1), raw(r[0], g, 1, 0)))
    return nn, nt, tn


_bnn, _bnt, _btn = _make_dots(None)
_hnn, _hnt, _htn = _make_dots(HI)
_mnn, _mnt, _mtn = _make_dots(lax.Precision.HIGH)


def _col(x, j):
    lane = lax.broadcasted_iota(jnp.int32, x.shape, 1)
    return jnp.sum(jnp.where(lane == j, x, 0.0), axis=-1, keepdims=True)


def _rms(x, g, eps=1e-6):
    return x * lax.rsqrt(jnp.mean(x * x, axis=-1, keepdims=True) + eps) * g


def _ln(x, g, b, eps=1e-5):
    xc = x - jnp.mean(x, axis=-1, keepdims=True)
    var = jnp.mean(xc * xc, axis=-1, keepdims=True)
    return xc * lax.rsqrt(var + eps) * g + b


_sigmoid = jax.nn.sigmoid


def _silu(x):
    return x * _sigmoid(x)


def _softplus(x):
    return jnp.maximum(x, 0.0) + jnp.log1p(jnp.exp(-jnp.abs(x)))


def _gelu(x):
    return 0.5 * x * (1.0 + lax.erf(x * 0.7071067811865476))


def f_norm(x, g):
    return (_rms(x, g),)


def f_glu(a_in):
    return (a_in[:, :MIX_W] * _sigmoid(a_in[:, MIX_W:]),)


def f_lnsilu(ac, cb, g, b):
    return (_silu(_ln(ac + cb, g, b)),)


def f_dnprep(qc, bd, a_log, dt_bias):
    t = qc.shape[0]
    qkv = _silu(qc)
    qs, ks, bs, gs = [], [], [], []
    for h in range(HEADS):
        q = qkv[:, h * HEAD_DIM:(h + 1) * HEAD_DIM]
        k = qkv[:, MIX_W + h * HEAD_DIM:MIX_W + (h + 1) * HEAD_DIM]
        qs.append(q * lax.rsqrt(jnp.sum(q * q, axis=-1, keepdims=True) + 1e-6) * (HEAD_DIM ** -0.5))
        ks.append(k * lax.rsqrt(jnp.sum(k * k, axis=-1, keepdims=True) + 1e-6))
        beta = _sigmoid(_col(bd, h))
        g = -jnp.exp(_col(a_log, h)) * _softplus(_col(bd, HEADS + h) + _col(dt_bias, h))
        bs.append(jnp.broadcast_to(beta, (t, HEAD_DIM)))
        gs.append(jnp.broadcast_to(g, (t, HEAD_DIM)))
    cat = lambda xs: jnp.concatenate(xs, axis=1)
    return cat(qs), cat(ks), qkv[:, 2 * MIX_W:], cat(bs), cat(gs)


def f_dnout(o, dgate, g):
    outs = []
    for h in range(HEADS):
        sl = slice(h * HEAD_DIM, (h + 1) * HEAD_DIM)
        outs.append(_rms(o[:, sl], g) * _silu(dgate[:, sl]))
    return (jnp.concatenate(outs, axis=1),)


def f_gmlp(gm_in, lg, lb, ws, bst):
    t = gm_in.shape[0]
    ge = _gelu(gm_in)
    u, vg = ge[:, :MIX_W], _ln(ge[:, MIX_W:], lg, lb)
    ri = lax.broadcasted_iota(jnp.int32, (GM_CHUNK, GM_CHUNK), 0)
    ci = lax.broadcasted_iota(jnp.int32, (GM_CHUNK, GM_CHUNK), 1)
    chunks = []
    for n in range(t // GM_CHUNK):
        vc = vg[n * GM_CHUNK:(n + 1) * GM_CHUNK]
        cols = []
        for g in range(4):
            w = jnp.where(ri >= ci, ws[g * GM_CHUNK:(g + 1) * GM_CHUNK], 0.0)
            cols.append(_bnn(w, vc[:, g * LANES:(g + 1) * LANES]) + _col(bst, g))
        chunks.append(jnp.concatenate(cols, axis=1))
    mixed = chunks[0] if len(chunks) == 1 else jnp.concatenate(chunks, axis=0)
    return (u * mixed,)


def f_poolpost(pc, pw, ps):
    cols = [_bnn(pc[:, g * LANES:(g + 1) * LANES], pw[g * LANES:(g + 1) * LANES]) for g in range(4)]
    return (jnp.concatenate(cols, axis=1) * ps,)


def f_merge(gate, p0, p1, p2, p3):
    m = None
    for n, p in enumerate((p0, p1, p2, p3)):
        t = _sigmoid(gate[:, n * D_MODEL:(n + 1) * D_MODEL]) * p
        m = t if m is None else m + t
    return (m,)


def f_attn(q, kv):
    outs = []
    for h in range(XA_HEADS):
        sl = slice(h * XA_HEAD_DIM, (h + 1) * XA_HEAD_DIM)
        s = _bnt(q[:, sl], kv[:, sl]) * (XA_HEAD_DIM ** -0.5)
        s = s - lax.stop_gradient(jnp.max(s, axis=-1, keepdims=True))
        p = jnp.exp(s)
        p = p / jnp.sum(p, axis=-1, keepdims=True)
        outs.append(_bnn(p, kv[:, D_MODEL + h * XA_HEAD_DIM:D_MODEL + (h + 1) * XA_HEAD_DIM]))
    return (jnp.concatenate(outs, axis=1),)


def f_loss(x, t, g):
    e = _rms(x, g) - t
    return (0.5 * jnp.mean(e * e, axis=-1, keepdims=True),)


def _row_spec(tr, width, cb):
    return pl.BlockSpec((tr, width), functools.partial(lambda i, cb: (i, cb), cb=cb))


def _full_spec(shape):
    return pl.BlockSpec(shape, lambda i: (0,) * len(shape))


def _rw(f, rows, params, outs, *, name, tr=256):
    s = rows[0][0].shape[0]
    tr = min(tr, s)
    nr, npar = len(rows), len(params)

    def body(*refs):
        rv = [r[...].astype(F32) for r in refs[:nr]]
        pv = [p[...] for p in refs[nr:nr + npar]]
        res = f(*rv, *pv)
        for o_ref, r in zip(refs[nr + npar:], res):
            o_ref[...] = r.astype(o_ref.dtype)

    return pl.pallas_call(
        body, name=name, grid=(s // tr,),
        in_specs=[_row_spec(tr, w, cb) for _, w, cb in rows] + [_full_spec(p.shape) for p in params],
        out_specs=[_row_spec(tr, w, 0) for w, _ in outs],
        out_shape=[jax.ShapeDtypeStruct((s, w), dt) for w, dt in outs],
        compiler_params=_params(("arbitrary",)),
    )(*[a for a, _, _ in rows], *params)


def _rw_bwd(f, rows, params, cts, *, row_grads, param_grads=(), add=None, primal=(), name, tr=256):
    s = rows[0][0].shape[0]
    tr = min(tr, s)
    nr, npar, nct = len(rows), len(params), len(cts)
    add = add or {}
    add_keys = list(add)

    def body(*refs):
        it = iter(refs)
        row_refs = [next(it) for _ in range(nr)]
        par_refs = [next(it) for _ in range(npar)]
        ct_refs = [next(it) for _ in range(nct)]
        add_refs = {k: next(it) for k in add_keys}
        rg_refs = [next(it) for _ in row_grads]
        pg_refs = [next(it) for _ in param_grads]
        pr_refs = [next(it) for _ in primal]
        rv = [r[...].astype(F32) for r in row_refs]
        pv = [p[...] for p in par_refs]
        out, vjp = jax.vjp(f, *rv, *pv)
        grads = vjp(tuple(c[...].astype(F32) for c in ct_refs))
        for (idx, _), ref in zip(row_grads, rg_refs):
            g = grads[idx]
            if idx in add_refs:
                g = g + add_refs[idx][...].astype(F32)
            ref[...] = g.astype(ref.dtype)

        @pl.when(pl.program_id(0) == 0)
        def _():
            for ref in pg_refs:
                ref[...] = jnp.zeros_like(ref)

        for idx, ref in zip(param_grads, pg_refs):
            ref[...] += grads[nr + idx]
        for (idx, _, _), ref in zip(primal, pr_refs):
            ref[...] = out[idx].astype(ref.dtype)

    in_arrays = [a for a, _, _ in rows] + list(params) + [a for a, _, _ in cts] + [add[k][0] for k in add_keys]
    in_specs = ([_row_spec(tr, w, cb) for _, w, cb in rows] + [_full_spec(p.shape) for p in params]
                + [_row_spec(tr, w, cb) for _, w, cb in cts] + [_row_spec(tr, add[k][1], add[k][2]) for k in add_keys])
    out_specs = ([_row_spec(tr, rows[idx][1], 0) for idx, _ in row_grads]
                 + [_full_spec(params[idx].shape) for idx in param_grads]
                 + [_row_spec(tr, w, 0) for _, w, _ in primal])
    out_shape = ([jax.ShapeDtypeStruct((s, rows[idx][1]), dt) for idx, dt in row_grads]
                 + [jax.ShapeDtypeStruct(params[idx].shape, F32) for idx in param_grads]
                 + [jax.ShapeDtypeStruct((s, w), dt) for _, w, dt in primal])
    return pl.pallas_call(
        body, name=name, grid=(s // tr,), in_specs=in_specs, out_specs=out_specs, out_shape=out_shape,
        compiler_params=_params(("arbitrary",)),
    )(*in_arrays)


def _mm(a, b, *, ta=False, tb=False, out_dtype=F32, add=None, post=None, extra=(), out_dtypes=None, name):
    m, k = (a.shape[1], a.shape[0]) if ta else a.shape
    n = b.shape[0] if tb else b.shape[1]
    tm = _pick(m, (1024, 512, 256, 128))
    tn = _pick(n, (1024, 512, 256, 128))
    tk = _pick(k, (1024, 512, 256, 128))
    nk = k // tk
    dims = (((0 if ta else 1,), (1 if tb else 0,)), ((), ()))
    if post is None:
        extra = [add] if add is not None else []
        post = (lambda r, e: (r + e,)) if add is not None else (lambda r: (r,))
    dtypes = out_dtypes or [out_dtype]
    n_ex, n_out = len(extra), len(dtypes)

    def body(*refs):
        a_ref, b_ref = refs[:2]
        ex_refs = refs[2:2 + n_ex]
        o_refs = refs[2 + n_ex:2 + n_ex + n_out]
        part = lax.dot_general(a_ref[...].astype(BF16), b_ref[...].astype(BF16), dims, preferred_element_type=F32)

        def finish(r):
            for o_ref, val in zip(o_refs, post(r, *[e[...] for e in ex_refs])):
                o_ref[...] = val.astype(o_ref.dtype)

        if nk == 1:
            finish(part)
            return
        acc = refs[-1]
        kk = pl.program_id(2)

        @pl.when(kk == 0)
        def _():
            acc[...] = part

        @pl.when((kk > 0) & (kk < nk - 1))
        def _():
            acc[...] += part

        @pl.when(kk == nk - 1)
        def _():
            finish(acc[...] + part)

    a_spec = (pl.BlockSpec((tk, tm), lambda i, j, kk: (kk, i)) if ta else pl.BlockSpec((tm, tk), lambda i, j, kk: (i, kk)))
    b_spec = (pl.BlockSpec((tn, tk), lambda i, j, kk: (j, kk)) if tb else pl.BlockSpec((tk, tn), lambda i, j, kk: (kk, j)))
    o_spec = pl.BlockSpec((tm, tn), lambda i, j, kk: (i, j))
    res = pl.pallas_call(
        body, name=name, grid=(m // tm, n // tn, nk), in_specs=[a_spec, b_spec] + [o_spec] * n_ex,
        out_specs=[o_spec] * n_out, out_shape=[jax.ShapeDtypeStruct((m, n), dt) for dt in dtypes],
        scratch_shapes=[pltpu.VMEM((tm, tn), F32)] if nk > 1 else [],
        compiler_params=_params(("arbitrary", "arbitrary", "arbitrary")),
    )(a, b, *extra)
    return res if out_dtypes else res[0]


CONV_TT = 1024
CONV_SUB = 256


def _halo(k):
    return -(-(k - 1) // 8) * 8


def _pool_count(row0, c, rows):
    win = lax.shift_left(jnp.int32(2), c)
    t = row0 + lax.broadcasted_iota(jnp.int32, (rows, LANES), 0)
    return win, jnp.minimum(t + 1, win).astype(F32)


def _conv_fwd(x, xcb0, channels, w, k, *, pool=False, name):
    s = x.shape[0]
    tt = min(CONV_TT, s)
    sub = min(CONV_SUB, tt)
    nt = s // tt
    halo = _halo(k)

    def body(*refs):
        if pool:
            xc_ref, xp_ref, o_ref, xs = refs
        else:
            xc_ref, xp_ref, w_ref, o_ref, xs = refs
        c, i = pl.program_id(0), pl.program_id(1)
        xs[0:halo, :] = jnp.where(i > 0, xp_ref[...], 0.0)
        xs[halo:halo + tt, :] = xc_ref[...]
        for r0 in range(0, tt, sub):
            acc = jnp.zeros((sub, LANES), F32)
            if pool:
                win, cnt = _pool_count(i * tt + r0, c, sub)
                for j in range(k):
                    acc = acc + jnp.where(j >= k - win, xs[pl.ds(r0 + halo - (k - 1) + j, sub), :], 0.0)
                o_ref[r0:r0 + sub, :] = acc / cnt - xc_ref[r0:r0 + sub, :]
            else:
                for j in range(k):
                    acc = acc + w_ref[j:j + 1, :] * xs[pl.ds(r0 + halo - (k - 1) + j, sub), :]
                o_ref[r0:r0 + sub, :] = acc

    per = tt // halo
    in_specs = [pl.BlockSpec((tt, LANES), lambda c, i: (i, xcb0 + c)),
                pl.BlockSpec((halo, LANES), lambda c, i: (jnp.maximum(i * per - 1, 0), xcb0 + c))]
    args = [x, x]
    if not pool:
        in_specs.append(pl.BlockSpec((k, LANES), lambda c, i: (0, c)))
        args.append(w)
    return pl.pallas_call(
        body, name=name, grid=(channels // LANES, nt), in_specs=in_specs,
        out_specs=pl.BlockSpec((tt, LANES), lambda c, i: (i, c)),
        out_shape=jax.ShapeDtypeStruct((s, channels), F32),
        scratch_shapes=[pltpu.VMEM((halo + tt, LANES), F32)],
        compiler_params=_params(("arbitrary", "arbitrary")),
    )(*args)


def _conv_bwd(x, xcb0, channels, w, k, dy, *, pool=False, dx_dtype=BF16, name):
    s = dy.shape[0]
    tt = min(CONV_TT, s)
    sub = min(CONV_SUB, tt)
    nt = s // tt
    halo = _halo(k)

    def body(*refs):
        if pool:
            dyc_ref, dyn_ref, dx_ref, ys = refs
        else:
            xc_ref, xp_ref, dyc_ref, dyn_ref, w_ref, dx_ref, dw_ref, xs, ys = refs
        c, i = pl.program_id(0), pl.program_id(1)
        dyn = jnp.where(i < nt - 1, dyn_ref[...], 0.0)
        if pool:
            win, cnt = _pool_count(i * tt, c, tt)
            ys[0:tt, :] = dyc_ref[...] / cnt
            ys[tt:tt + halo, :] = dyn / win.astype(F32)
            for r0 in range(0, tt, sub):
                acc = jnp.zeros((sub, LANES), F32)
                for j in range(k):
                    acc = acc + jnp.where(j >= k - win, ys[pl.ds(r0 + (k - 1) - j, sub), :], 0.0)
                dx_ref[r0:r0 + sub, :] = (acc - dyc_ref[r0:r0 + sub, :]).astype(dx_ref.dtype)
            return
        ys[0:tt, :] = dyc_ref[...]
        ys[tt:tt + halo, :] = dyn
        xs[0:halo, :] = jnp.where(i > 0, xp_ref[...], 0.0)
        xs[halo:halo + tt, :] = xc_ref[...]

        @pl.when(i == 0)
        def _():
            dw_ref[...] = jnp.zeros_like(dw_ref)

        for r0 in range(0, tt, sub):
            acc = jnp.zeros((sub, LANES), F32)
            for j in range(k):
                acc = acc + w_ref[j:j + 1, :] * ys[pl.ds(r0 + (k - 1) - j, sub), :]
            dx_ref[r0:r0 + sub, :] = acc.astype(dx_ref.dtype)
            dyc = dyc_ref[r0:r0 + sub, :]
            for j in range(k):
                dw_ref[j:j + 1, :] += jnp.sum(dyc * xs[pl.ds(r0 + halo - (k - 1) + j, sub), :], axis=0, keepdims=True)

    per = tt // halo
    cur = lambda cb0: pl.BlockSpec((tt, LANES), lambda c, i: (i, cb0 + c))
    dy_specs = [cur(0), pl.BlockSpec((halo, LANES), lambda c, i: (jnp.minimum((i + 1) * per, s // halo - 1), c))]
    dx_spec = pl.BlockSpec((tt, LANES), lambda c, i: (i, c))
    dx_shape = jax.ShapeDtypeStruct((s, channels), dx_dtype)
    if pool:
        return pl.pallas_call(
            body, name=name, grid=(channels // LANES, nt), in_specs=dy_specs, out_specs=dx_spec, out_shape=dx_shape,
            scratch_shapes=[pltpu.VMEM((tt + halo, LANES), F32)],
            compiler_params=_params(("arbitrary", "arbitrary")),
        )(dy, dy)
    in_specs = [cur(xcb0), pl.BlockSpec((halo, LANES), lambda c, i: (jnp.maximum(i * per - 1, 0), xcb0 + c))] + dy_specs
    in_specs.append(pl.BlockSpec((k, LANES), lambda c, i: (0, c)))
    return pl.pallas_call(
        body, name=name, grid=(channels // LANES, nt), in_specs=in_specs,
        out_specs=[dx_spec, pl.BlockSpec((k, LANES), lambda c, i: (0, c))],
        out_shape=[dx_shape, jax.ShapeDtypeStruct((k, channels), F32)],
        scratch_shapes=[pltpu.VMEM((halo + tt, LANES), F32), pltpu.VMEM((tt + halo, LANES), F32)],
        compiler_params=_params(("arbitrary", "arbitrary")),
    )(x, x, dy, dy, w)


DN_STEP_ROWS = 256


@jax.custom_vjp
def _chunk_inverse(b):
    r = b.shape[0]
    eye = jnp.where(lax.broadcasted_iota(jnp.int32, (r, r), 0) == lax.broadcasted_iota(jnp.int32, (r, r), 1), 1.0, 0.0)
    inv = eye + b
    bp = b
    for _ in range(5):
        bp = _mnn(bp, bp)
        inv = inv + _mnn(inv, bp)
    return inv


def _chunk_inverse_fwd(b):
    inv = _chunk_inverse(b)
    return inv, inv


def _chunk_inverse_bwd(inv, g):
    return (_mtn(inv, _mnt(g, inv)),)


_chunk_inverse.defvjp(_chunk_inverse_fwd, _chunk_inverse_bwd)


@jax.custom_vjp
def _saved_inverse(b, inv):
    return inv


_saved_inverse.defvjp(lambda b, inv: (inv, inv), lambda inv, g: (_mtn(inv, _mnt(g, inv)), jnp.zeros_like(inv)))


def _dn_step(q, k, v, bb, gb, state, saved=None, keep=None):
    c = DN_CHUNK
    r = q.shape[0]
    ri = lax.broadcasted_iota(jnp.int32, (r, r), 0)
    ci = lax.broadcasted_iota(jnp.int32, (r, r), 1)
    same = jnp.bitwise_or(ri, c - 1) == jnp.bitwise_or(ci, c - 1)
    causal = same & (ri >= ci)
    strict = same & (ri > ci)
    eye = jnp.where(ri == ci, 1.0, 0.0)
    gam = _hnn(jnp.where(causal, 1.0, 0.0), gb)
    gam_i = jnp.concatenate([gam] * (r // LANES), axis=1)
    gam_j = gam_i.T
    decay = jnp.where(causal, jnp.exp(jnp.where(causal, gam_i - gam_j, 0.0)), 0.0)
    row = lax.broadcasted_iota(jnp.int32, gam.shape, 0)
    lasts = [jnp.sum(jnp.where(row == n * c + c - 1, gam, 0.0), axis=0, keepdims=True) for n in range(r // c)]
    g_last = jnp.zeros_like(gam)
    for n, gl in enumerate(lasts):
        g_last = jnp.where(jnp.bitwise_or(row, c - 1) == n * c + c - 1, gl, g_last)
    kb = k * bb
    b = -jnp.where(strict, _mnt(kb, k) * decay, 0.0)
    inv = _chunk_inverse(b) if saved is None else _saved_inverse(b, saved)
    if keep is not None:
        keep.append(inv)
    eg = jnp.exp(gam)
    u = _mnn(inv, v * bb)
    w = _mnn(inv, kb * eg)
    attn = _bnt(q, k) * decay
    q_dec = q * eg
    k_dec = k * jnp.exp(g_last - gam)
    v_news, o_inter = [], []
    for n in range(r // c):
        s = slice(n * c, (n + 1) * c)
        v_new = u[s] - _mnn(w[s], state)
        o_inter.append(_bnn(q_dec[s], state))
        state = state * jnp.exp(lasts[n]) + _mtn(k_dec[s], v_new)
        v_news.append(v_new)
    o = jnp.concatenate(o_inter, axis=0) + _bnn(attn, jnp.concatenate(v_news, axis=0))
    return o, state


def _dn_specs(steps, reverse):
    if reverse:
        tile = pl.BlockSpec((DN_STEP_ROWS, HEAD_DIM), lambda h, i: (steps - 1 - i, h))
        st = pl.BlockSpec((1, 1, HEAD_DIM, HEAD_DIM), lambda h, i: (steps - 1 - i, h, 0, 0))
    else:
        tile = pl.BlockSpec((DN_STEP_ROWS, HEAD_DIM), lambda h, i: (i, h))
        st = pl.BlockSpec((1, 1, HEAD_DIM, HEAD_DIM), lambda h, i: (i, h, 0, 0))
    return tile, st


def _dn_fwd(q, k, v, bb, gb, *, name):
    s = q.shape[0]
    steps = s // DN_STEP_ROWS

    def body(q_ref, k_ref, v_ref, b_ref, g_ref, o_ref, st_ref, inv_ref, state):
        @pl.when(pl.program_id(1) == 0)
        def _():
            state[...] = jnp.zeros_like(state)

        st = state[...]
        st_ref[0, 0] = st
        keep = []
        o, st = _dn_step(q_ref[...], k_ref[...], v_ref[...], b_ref[...], g_ref[...], st, keep=keep)
        o_ref[...] = o
        inv_ref[0, 0] = keep[0]
        state[...] = st

    tile, stspec = _dn_specs(steps, False)
    invspec = pl.BlockSpec((1, 1, DN_STEP_ROWS, DN_STEP_ROWS), lambda h, i: (i, h, 0, 0))
    return pl.pallas_call(
        body, name=name, grid=(HEADS, steps), in_specs=[tile] * 5, out_specs=[tile, stspec, invspec],
        out_shape=[jax.ShapeDtypeStruct((s, MIX_W), F32),
                   jax.ShapeDtypeStruct((steps, HEADS, HEAD_DIM, HEAD_DIM), F32),
                   jax.ShapeDtypeStruct((steps, HEADS, DN_STEP_ROWS, DN_STEP_ROWS), F32)],
        scratch_shapes=[pltpu.VMEM((HEAD_DIM, HEAD_DIM), F32)],
        compiler_params=_params(("arbitrary", "arbitrary")),
    )(q, k, v, bb, gb)


def _dn_bwd(q, k, v, bb, gb, states, invs, do, *, name):
    s = q.shape[0]
    steps = s // DN_STEP_ROWS

    def body(q_ref, k_ref, v_ref, b_ref, g_ref, st_ref, inv_ref, do_ref, dq_ref, dk_ref, dv_ref, db_ref, dg_ref,
             dstate):
        @pl.when(pl.program_id(1) == 0)
        def _():
            dstate[...] = jnp.zeros_like(dstate)

        step = functools.partial(_dn_step, saved=inv_ref[0, 0])
        _, vjp = jax.vjp(step, q_ref[...], k_ref[...], v_ref[...], b_ref[...], g_ref[...], st_ref[0, 0])
        dq, dk, dv, db, dg, dst = vjp((do_ref[...], dstate[...]))
        dq_ref[...] = dq
        dk_ref[...] = dk
        dv_ref[...] = dv
        db_ref[...] = db
        dg_ref[...] = dg
        dstate[...] = dst

    tile, stspec = _dn_specs(steps, True)
    invspec = pl.BlockSpec((1, 1, DN_STEP_ROWS, DN_STEP_ROWS), lambda h, i: (steps - 1 - i, h, 0, 0))
    return pl.pallas_call(
        body, name=name, grid=(HEADS, steps), in_specs=[tile] * 5 + [stspec, invspec, tile], out_specs=[tile] * 5,
        out_shape=[jax.ShapeDtypeStruct((s, MIX_W), F32)] * 5,
        scratch_shapes=[pltpu.VMEM((HEAD_DIM, HEAD_DIM), F32)],
        compiler_params=_params(("arbitrary", "arbitrary")),
    )(q, k, v, bb, gb, states, invs, do)


EW_TILE_BYTES = 2 << 20


def _ew(f, ins, out_dtypes, *, name):
    r, c = ins[0][0].shape[1:] if isinstance(ins[0], tuple) else ins[0].shape
    row_bytes = -(-c // LANES) * LANES * 4
    tr = r
    if r * row_bytes > EW_TILE_BYTES:
        tr = next((t for t in (4096, 2048, 1024, 512, 256, 128, 64, 32, 16, 8)
                   if r % t == 0 and t * row_bytes <= EW_TILE_BYTES), r)
    n_in = len(ins)

    def body(*refs):
        res = f(*[x[...] for x in refs[:n_in]])
        for o_ref, v in zip(refs[n_in:], res):
            o_ref[...] = v.astype(o_ref.dtype)

    in_specs, args = [], []
    for x in ins:
        if isinstance(x, tuple):
            in_specs.append(pl.BlockSpec((None, tr, c), functools.partial(lambda i, j: (j, i, 0), j=x[1])))
            args.append(x[0])
        else:
            in_specs.append(pl.BlockSpec((tr, c), lambda i: (i, 0)))
            args.append(x)
    return pl.pallas_call(
        body, name=name, grid=(r // tr,), in_specs=in_specs,
        out_specs=[pl.BlockSpec((tr, c), lambda i: (i, 0)) for _ in out_dtypes],
        out_shape=[jax.ShapeDtypeStruct((r, c), dt) for dt in out_dtypes],
        compiler_params=_params(("arbitrary",)),
    )(*args)


def f_adamw(w, g, m, v):
    m = ADAM_B1 * m + (1.0 - ADAM_B1) * g
    v = ADAM_B2 * v + (1.0 - ADAM_B2) * (g * g)
    m_hat = m / (1.0 - ADAM_B1 ** ADAM_STEP)
    v_hat = v / (1.0 - ADAM_B2 ** ADAM_STEP)
    return -ADAM_LR * (m_hat / (jnp.sqrt(v_hat) + ADAM_EPS) + ADAM_WD * w), m, v


def f_sum4(a, b, c, d):
    return (((a.astype(F32) + b.astype(F32)) + c.astype(F32)) + d.astype(F32),)


def f_add2(a, b):
    return (a + b,)


def _adamw(w, g, m, v, *, name):
    shape = w.shape
    two = (1, shape[0]) if w.ndim == 1 else (-1, shape[-1])
    outs = _ew(f_adamw, [t.reshape(two) for t in (w, g, m, v)], [F32, F32, F32], name=name)
    return [o.reshape(shape) for o in outs]


_ANY = pl.BlockSpec(memory_space=pl.ANY)


def _xy_exchange(srcs, *, broadcast, name):
    nt = len(srcs)

    def body(*refs):
        src_refs, out_refs = refs[:nt], refs[nt:2 * nt]
        send_sems, recv_sems, local_sems = refs[2 * nt:]
        x, y, c = lax.axis_index("x"), lax.axis_index("y"), lax.axis_index("c")
        me = 2 * x + y
        peers = [(1 - x, y), (x, 1 - y), (1 - x, 1 - y)]

        def block(t, j):
            return src_refs[t] if broadcast else src_refs[t].at[j]

        def copy(t, n, px, py, src_blk, dst_blk):
            return pltpu.make_async_remote_copy(
                src_ref=src_blk, dst_ref=dst_blk, send_sem=send_sems.at[3 * t + n], recv_sem=recv_sems.at[3 * t + n],
                device_id=(px, py, c), device_id_type=pl.DeviceIdType.MESH)

        mine = [pltpu.make_async_copy(block(t, me), out_refs[t].at[me], local_sems.at[t]) for t in range(nt)]
        sends = [copy(t, n, px, py, block(t, 2 * px + py), out_refs[t].at[me])
                 for t in range(nt) for n, (px, py) in enumerate(peers)]
        for cp in mine + sends:
            cp.start()
        for cp in sends:
            cp.wait_send()
        for t in range(nt):
            for n, (px, py) in enumerate(peers):
                copy(t, n, px, py, block(t, me), out_refs[t].at[2 * px + py]).wait_recv()
        for cp in mine:
            cp.wait()

    shapes = [(4,) + tuple(s.shape) if broadcast else tuple(s.shape) for s in srcs]
    return pl.pallas_call(
        body, name=name, in_specs=[_ANY] * nt, out_specs=[_ANY] * nt,
        out_shape=[jax.ShapeDtypeStruct(shp, s.dtype) for shp, s in zip(shapes, srcs)],
        scratch_shapes=[pltpu.SemaphoreType.DMA((3 * nt,)), pltpu.SemaphoreType.DMA((3 * nt,)),
                        pltpu.SemaphoreType.DMA((nt,))],
        compiler_params=pltpu.CompilerParams(has_side_effects=True),
    )(*srcs)


def _c_swap(srcs, *, name):
    nt = len(srcs)

    def body(*refs):
        src_refs, out_refs = refs[:nt], refs[nt:2 * nt]
        send_sems, recv_sems = refs[2 * nt:]
        sibling = (lax.axis_index("x"), lax.axis_index("y"), 1 - lax.axis_index("c"))
        cps = [pltpu.make_async_remote_copy(src_ref=src_refs[t], dst_ref=out_refs[t], send_sem=send_sems.at[t],
                                            recv_sem=recv_sems.at[t], device_id=sibling,
                                            device_id_type=pl.DeviceIdType.MESH) for t in range(nt)]
        for cp in cps:
            cp.start()
        for cp in cps:
            cp.wait()

    return pl.pallas_call(
        body, name=name, in_specs=[_ANY] * nt, out_specs=[_ANY] * nt,
        out_shape=[jax.ShapeDtypeStruct(s.shape, s.dtype) for s in srcs],
        scratch_shapes=[pltpu.SemaphoreType.DMA((nt,)), pltpu.SemaphoreType.DMA((nt,))],
        compiler_params=pltpu.CompilerParams(has_side_effects=True),
    )(*srcs)


_HBM = pl.BlockSpec(memory_space=pltpu.HBM)
_SEM = pl.BlockSpec(memory_space=pltpu.SEMAPHORE)
_DATAFLOW = pltpu.SideEffectType.DATAFLOW_SIDE_EFFECTING


def _xy_peers():
    x, y, c = lax.axis_index("x"), lax.axis_index("y"), lax.axis_index("c")
    return 2 * x + y, c, [(1 - x, y), (x, 1 - y), (1 - x, 1 - y)]


def _xy_start(srcs, *, broadcast, name):
    nt = len(srcs)
    lands = [lax.empty((4,) + tuple(s.shape) if broadcast else tuple(s.shape), s.dtype) for s in srcs]

    def body(*refs):
        src_refs, land_refs = refs[:nt], refs[nt:2 * nt]
        send_sems, recv_sems = refs[2 * nt], refs[2 * nt + 1]
        token = refs[-1]
        me, c, peers = _xy_peers()
        for t in range(nt):
            for n, (px, py) in enumerate(peers):
                src_blk = src_refs[t] if broadcast else src_refs[t].at[2 * px + py]
                pltpu.make_async_remote_copy(
                    src_ref=src_blk, dst_ref=land_refs[t].at[me], send_sem=send_sems.at[3 * t + n],
                    recv_sem=recv_sems.at[3 * t + n], device_id=(px, py, c),
                    device_id_type=pl.DeviceIdType.MESH).start()
        token[...] = jnp.zeros_like(token)

    res = pl.pallas_call(
        body, name=name, in_specs=[_HBM] * (2 * nt),
        out_specs=[_SEM, _SEM] + [_HBM] * (2 * nt) + [pl.BlockSpec(memory_space=pltpu.VMEM)],
        out_shape=[pltpu.SemaphoreType.DMA((3 * nt,)), pltpu.SemaphoreType.DMA((3 * nt,))]
        + [pltpu.HBM(a.shape, a.dtype) for a in list(srcs) + lands] + [jax.ShapeDtypeStruct((8, LANES), F32)],
        input_output_aliases={i: 2 + i for i in range(2 * nt)},
        compiler_params=pltpu.CompilerParams(has_side_effects=_DATAFLOW),
    )(*[pltpu.with_memory_space_constraint(a, pltpu.HBM) for a in list(srcs) + lands])
    return res[0], res[1], res[2:2 + nt], res[2 + nt:2 + 2 * nt], res[-1]


def _xy_wait(started, after, *, broadcast, name):
    send_sems, recv_sems, srcs, lands, _ = started
    nt = len(srcs)

    def body(*refs):
        src_refs, land_refs = refs[:nt], refs[nt:2 * nt]
        send_sems, recv_sems = refs[2 * nt], refs[2 * nt + 1]
        me, c, peers = _xy_peers()
        for t in range(nt):
            for n, (px, py) in enumerate(peers):
                src_blk = src_refs[t] if broadcast else src_refs[t].at[me]
                cp = pltpu.make_async_remote_copy(
                    src_ref=src_blk, dst_ref=land_refs[t].at[2 * px + py], send_sem=send_sems.at[3 * t + n],
                    recv_sem=recv_sems.at[3 * t + n], device_id=(px, py, c), device_id_type=pl.DeviceIdType.MESH)
                cp.wait_send()
                cp.wait_recv()

    res = pl.pallas_call(
        body, name=name, in_specs=[_HBM] * (2 * nt) + [_SEM, _SEM, _ANY], out_specs=[_HBM] * (2 * nt),
        out_shape=[pltpu.HBM(a.shape, a.dtype) for a in list(srcs) + list(lands)],
        input_output_aliases={i: i for i in range(2 * nt)},
        compiler_params=pltpu.CompilerParams(has_side_effects=_DATAFLOW),
    )(*srcs, *lands, send_sems, recv_sems, after)
    return res[:nt], res[nt:]


def _all_sum(srcs, *, broadcast, name):
    got = _xy_exchange(srcs, broadcast=broadcast, name=name + "_xy")
    parts = []
    for t, gt in enumerate(got):
        g3 = gt.reshape(4, -1, gt.shape[-1])
        parts.append(_ew(f_sum4, [(g3, j) for j in range(4)], [F32], name=f"{name}_sum4_{t}")[0])
    others = _c_swap(parts, name=name + "_c")
    return [_ew(f_add2, [p, o], [F32], name=f"{name}_add2_{t}")[0].reshape(gt.shape[1:])
            for t, (p, o, gt) in enumerate(zip(parts, others, got))]


def _flat_rows(parts, dtype, row_multiple=8):
    flat = jnp.concatenate([p.reshape(-1).astype(dtype) for p in parts])
    pad = (-flat.shape[0]) % (row_multiple * LANES)
    if pad:
        flat = jnp.concatenate([flat, jnp.zeros((pad,), dtype)])
    return flat.reshape(-1, LANES)


def _unflat(buf, shapes):
    flat = buf.reshape(-1)
    out, off = [], 0
    for shp in shapes:
        n = 1
        for d in shp:
            n *= d
        out.append(flat[off:off + n].reshape(shp))
        off += n
    return out


def _regroup_w_in(w):
    cols = [w[:, a:b] for a, b in _Z_SEGS] + [jnp.zeros((w.shape[0], Z_W - IN_W), w.dtype)]
    return jnp.concatenate(cols, axis=1)


def _ungroup_w_in(wz):
    starts, off = {}, 0
    for a, b in _Z_SEGS:
        starts[a] = (off, off + b - a)
        off += b - a
    return jnp.concatenate([wz[:, starts[a][0]:starts[a][1]] for a in sorted(starts)], axis=1)


def _row(vec):
    return vec.reshape(1, -1)


def _lane_pad(vec):
    return jnp.zeros((1, LANES), F32).at[0, :vec.shape[0]].set(vec)


def _layer_fwd(x, mem, w, l):
    nm = lambda s: f"{s}_l{l}"
    sv = {"x0": x}
    h = _rw(f_norm, [(x, D_MODEL, 0)], [_row(w["norm_mix"])], [(D_MODEL, BF16)], name=nm("norm_mix"))[0]
    z = _mm(h, w["w_in_z"], name=nm("in_proj"))
    a_glu = _rw(f_glu, [(z, 2 * MIX_W, Z_A // (2 * MIX_W))], [], [(MIX_W, F32)], name=nm("glu"))[0]
    ac = _conv_fwd(a_glu, 0, MIX_W, w["conv_a_w"], CONV_K, name=nm("conv_a"))
    a_par = [_row(w["conv_a_b"]), _row(w["ln_a_g"]), _row(w["ln_a_b"])]
    a_out = _rw(f_lnsilu, [(ac, MIX_W, 0)], a_par, [(MIX_W, BF16)], name=nm("ln_a"))[0]
    qc = _conv_fwd(z, Z_QKV // LANES, 3 * MIX_W, w["dn_conv_w"], DN_CONV_K, name=nm("conv_dn"))
    dn_par = [_lane_pad(w["dn_a_log"]), _lane_pad(w["dn_dt_bias"])]
    qn, kn, v, bb, gb = _rw(f_dnprep, [(qc, 3 * MIX_W, 0), (z, LANES, Z_BD // LANES)], dn_par,
                            [(MIX_W, F32)] * 5, name=nm("dn_prep"))
    o, states, invs = _dn_fwd(qn, kn, v, bb, gb, name=nm("dn_scan"))
    sv["invs"] = invs
    o_out = _rw(f_dnout, [(o, MIX_W, 0), (z, MIX_W, Z_DG // MIX_W)], [_row(w["dn_norm_g"])], [(MIX_W, BF16)],
                name=nm("dn_out"))[0]
    gm_par = [_row(w["gm_ln_g"]), _row(w["gm_ln_b"]), w["gm_ws"].reshape(4 * GM_CHUNK, GM_CHUNK), w["gm_bs"].T]
    c_out = _rw(f_gmlp, [(z, 2 * MIX_W, Z_GM // (2 * MIX_W))], gm_par, [(MIX_W, BF16)], name=nm("gmlp"))[0]
    pc = _conv_fwd(z, Z_POOL // LANES, MIX_W, None, POOL_K, pool=True, name=nm("pool"))
    p_par = [w["pool_w"].reshape(4 * LANES, LANES), _row(w["pool_scale"])]
    p_out = _rw(f_poolpost, [(pc, MIX_W, 0)], p_par, [(MIX_W, BF16)], name=nm("pool_post"))[0]
    branches = [a_out, o_out, c_out, p_out]
    proj = [_mm(br, w["w_branch"][n], name=nm(f"branch{n}")) for n, br in enumerate(branches)]
    merged = _rw(f_merge, [(z, N_BRANCH * D_MODEL, 0)] + [(p, D_MODEL, 0) for p in proj], [], [(D_MODEL, BF16)],
                 name=nm("merge"), tr=128)[0]
    x1 = _mm(merged, w["w_out"], add=x, name=nm("out_proj"))
    sv.update(h=h, z=z, a_glu=a_glu, ac=ac, qc=qc, qn=qn, kn=kn, v=v, bb=bb, gb=gb, o=o, states=states, pc=pc,
              branches=branches, proj=proj, merged=merged, x1=x1)
    h2 = _rw(f_norm, [(x1, D_MODEL, 0)], [_row(w["norm_xa"])], [(D_MODEL, BF16)], name=nm("norm_xa"))[0]
    mn = _rw(f_norm, [(mem, D_MODEL, 0)], [_row(w["norm_mem"])], [(D_MODEL, BF16)], name=nm("norm_mem"))[0]
    kv = _mm(mn, w["xa_wkv"], name=nm("xa_kv"))
    q = _mm(h2, w["xa_wq"], name=nm("xa_q"))
    att = _rw(f_attn, [(q, D_MODEL, 0)], [kv], [(D_MODEL, BF16)], name=nm("xa_attn"))[0]
    x2 = _mm(att, w["xa_wo"], add=x1, name=nm("xa_o"))
    sv.update(h2=h2, mn=mn, kv=kv, q=q, att=att, x2=x2)
    h3 = _rw(f_norm, [(x2, D_MODEL, 0)], [_row(w["norm_mlp"])], [(D_MODEL, BF16)], name=nm("norm_mlp"))[0]
    u, act = _mm(h3, w["mlp_w1"], post=lambda r: (r, jnp.square(jnp.maximum(r, 0.0))), out_dtypes=[F32, BF16],
                 name=nm("mlp_up"))
    x3 = _mm(act, w["mlp_w2"], add=x2, name=nm("mlp_down"))
    sv.update(h3=h3, u=u, act=act)
    return x3, sv


def _layer_bwd(dx, mem, w, sv, l, after_attention, at_end):
    nm = lambda s: f"{s}_bwd_l{l}"
    s = dx.shape[0]
    g = {}
    du = _mm(dx, w["mlp_w2"], tb=True, extra=[sv["u"]], post=lambda r, u: (r * (2.0 * jnp.maximum(u, 0.0)),),
             out_dtypes=[BF16], name=nm("mlp_down_dx"))[0]
    g["mlp_w2"] = _mm(sv["act"], dx, ta=True, out_dtype=BF16, name=nm("mlp_down_dw"))
    g["mlp_w1"] = _mm(sv["h3"], du, ta=True, out_dtype=BF16, name=nm("mlp_up_dw"))
    dh3 = _mm(du, w["mlp_w1"], tb=True, name=nm("mlp_up_dx"))
    dx2, g["norm_mlp"] = _rw_bwd(f_norm, [(sv["x2"], D_MODEL, 0)], [_row(w["norm_mlp"])], [(dh3, D_MODEL, 0)],
                                 row_grads=[(0, F32)], param_grads=[0], add={0: (dx, D_MODEL, 0)}, name=nm("norm_mlp"))
    datt = _mm(dx2, w["xa_wo"], tb=True, name=nm("xa_o_dx"))
    g["xa_wo"] = _mm(sv["att"], dx2, ta=True, out_dtype=BF16, name=nm("xa_o_dw"))
    dq, dkv = _rw_bwd(f_attn, [(sv["q"], D_MODEL, 0)], [sv["kv"]], [(datt, D_MODEL, 0)], row_grads=[(0, BF16)],
                      param_grads=[0], name=nm("xa_attn"))
    g["xa_wq"] = _mm(sv["h2"], dq, ta=True, out_dtype=BF16, name=nm("xa_q_dw"))
    dh2 = _mm(dq, w["xa_wq"], tb=True, name=nm("xa_q_dx"))
    g["xa_wkv"] = _mm(sv["mn"], dkv, ta=True, out_dtype=BF16, name=nm("xa_kv_dw"))
    dmn = _mm(dkv, w["xa_wkv"], tb=True, name=nm("xa_kv_dx"))
    g["norm_mem"] = _rw_bwd(f_norm, [(mem, D_MODEL, 0)], [_row(w["norm_mem"])], [(dmn, D_MODEL, 0)], row_grads=[],
                            param_grads=[0], name=nm("norm_mem"))[0]
    gain = _row(w["norm_xa"] + after_attention(g))
    dx1, g["norm_xa"] = _rw_bwd(f_norm, [(sv["x1"], D_MODEL, 0)], [gain], [(dh2, D_MODEL, 0)],
                                row_grads=[(0, F32)], param_grads=[0], add={0: (dx2, D_MODEL, 0)}, name=nm("norm_xa"))
    z = sv["z"]
    dmerged = _mm(dx1, w["w_out"], tb=True, name=nm("out_proj_dx"))
    g["w_out"] = _mm(sv["merged"], dx1, ta=True, out_dtype=BF16, name=nm("out_proj_dw"))
    mg = _rw_bwd(f_merge, [(z, N_BRANCH * D_MODEL, 0)] + [(p, D_MODEL, 0) for p in sv["proj"]], [],
                 [(dmerged, D_MODEL, 0)], row_grads=[(i, BF16) for i in range(5)], name=nm("merge"), tr=128)
    dgate, dproj = mg[0], mg[1:]
    g["w_branch"] = jnp.stack([_mm(br, dp, ta=True, out_dtype=BF16, name=nm(f"branch{n}_dw"))
                               for n, (br, dp) in enumerate(zip(sv["branches"], dproj))])
    dbr = [_mm(dp, w["w_branch"][n], tb=True, name=nm(f"branch{n}_dx")) for n, dp in enumerate(dproj)]
    p_par = [w["pool_w"].reshape(4 * LANES, LANES), _row(w["pool_scale"])]
    dpc, dpw, g["pool_scale"] = _rw_bwd(f_poolpost, [(sv["pc"], MIX_W, 0)], p_par, [(dbr[3], MIX_W, 0)],
                                        row_grads=[(0, F32)], param_grads=[0, 1], name=nm("pool_post"))
    g["pool_w"] = dpw.reshape(4, LANES, LANES)
    dpool = _conv_bwd(None, 0, MIX_W, None, POOL_K, dpc, pool=True, name=nm("pool"))
    gm_par = [_row(w["gm_ln_g"]), _row(w["gm_ln_b"]), w["gm_ws"].reshape(4 * GM_CHUNK, GM_CHUNK), w["gm_bs"].T]
    dgm, g["gm_ln_g"], g["gm_ln_b"], dws, dbst = _rw_bwd(
        f_gmlp, [(z, 2 * MIX_W, Z_GM // (2 * MIX_W))], gm_par, [(dbr[2], MIX_W, 0)], row_grads=[(0, BF16)],
        param_grads=[0, 1, 2, 3], name=nm("gmlp"))
    g["gm_ws"] = dws.reshape(4, GM_CHUNK, GM_CHUNK)
    g["gm_bs"] = dbst.T
    do, ddg, g["dn_norm_g"] = _rw_bwd(f_dnout, [(sv["o"], MIX_W, 0), (z, MIX_W, Z_DG // MIX_W)], [_row(w["dn_norm_g"])],
                                      [(dbr[1], MIX_W, 0)], row_grads=[(0, F32), (1, BF16)], param_grads=[0],
                                      name=nm("dn_out"))
    dqn, dkn, dv, dbb, dgb = _dn_bwd(sv["qn"], sv["kn"], sv["v"], sv["bb"], sv["gb"], sv["states"], sv["invs"], do,
                                     name=nm("dn_scan"))
    dn_par = [_lane_pad(w["dn_a_log"]), _lane_pad(w["dn_dt_bias"])]
    dqc, dbd, dal, ddt = _rw_bwd(
        f_dnprep, [(sv["qc"], 3 * MIX_W, 0), (z, LANES, Z_BD // LANES)], dn_par,
        [(t, MIX_W, 0) for t in (dqn, dkn, dv, dbb, dgb)], row_grads=[(0, F32), (1, BF16)], param_grads=[0, 1],
        name=nm("dn_prep"))
    g["dn_a_log"], g["dn_dt_bias"] = dal[0, :HEADS], ddt[0, :HEADS]
    dqkv, g["dn_conv_w"] = _conv_bwd(z, Z_QKV // LANES, 3 * MIX_W, w["dn_conv_w"], DN_CONV_K, dqc, name=nm("conv_dn"))
    a_par = [_row(w["conv_a_b"]), _row(w["ln_a_g"]), _row(w["ln_a_b"])]
    dac, g["conv_a_b"], g["ln_a_g"], g["ln_a_b"] = _rw_bwd(
        f_lnsilu, [(sv["ac"], MIX_W, 0)], a_par, [(dbr[0], MIX_W, 0)], row_grads=[(0, F32)], param_grads=[0, 1, 2],
        name=nm("ln_a"))
    dglu, g["conv_a_w"] = _conv_bwd(sv["a_glu"], 0, MIX_W, w["conv_a_w"], CONV_K, dac, dx_dtype=F32, name=nm("conv_a"))
    da_in = _rw_bwd(f_glu, [(z, 2 * MIX_W, Z_A // (2 * MIX_W))], [], [(dglu, MIX_W, 0)], row_grads=[(0, BF16)],
                    name=nm("glu"))[0]
    dz = jnp.concatenate([dgate, da_in, dgm, dqkv, ddg, dpool, dbd, jnp.zeros((s, Z_W - Z_BD - LANES), BF16)], axis=1)
    g["w_in"] = _ungroup_w_in(_mm(sv["h"], dz, ta=True, out_dtype=BF16, name=nm("in_proj_dw")))
    dh = _mm(dz, w["w_in_z"], tb=True, name=nm("in_proj_dx"))
    gain = _row(w["norm_mix"] + at_end(g))
    dx0, g["norm_mix"] = _rw_bwd(f_norm, [(sv["x0"], D_MODEL, 0)], [gain], [(dh, D_MODEL, 0)],
                                 row_grads=[(0, F32)], param_grads=[0], add={0: (dx1, D_MODEL, 0)}, name=nm("norm_mix"))
    for n in ("norm_mlp", "norm_xa", "norm_mem", "norm_mix", "pool_scale", "gm_ln_g", "gm_ln_b", "dn_norm_g",
              "conv_a_b", "ln_a_g", "ln_a_b"):
        g[n] = g[n].reshape(-1)
    return dx0, g


def _shard_slice(a, axis, j):
    n = a.shape[axis] // 4
    return lax.slice_in_dim(a, j * n, (j + 1) * n, axis=axis)


def kernel(x, mem, norm_mix, w_in, conv_a_w, conv_a_b, ln_a_g, ln_a_b, dn_conv_w, dn_a_log, dn_dt_bias, dn_norm_g, gm_ln_g, gm_ln_b, gm_ws, gm_bs, pool_w, pool_scale, w_branch, w_out, norm_xa, norm_mem, xa_wq, xa_wkv, xa_wo, norm_mlp, mlp_w1, mlp_w2, norm_f, loss_target, m_norm_mix, m_w_in, m_conv_a_w, m_conv_a_b, m_ln_a_g, m_ln_a_b, m_dn_conv_w, m_dn_a_log, m_dn_dt_bias, m_dn_norm_g, m_gm_ln_g, m_gm_ln_b, m_gm_ws, m_gm_bs, m_pool_w, m_pool_scale, m_w_branch, m_w_out, m_norm_xa, m_norm_mem, m_xa_wq, m_xa_wkv, m_xa_wo, m_norm_mlp, m_mlp_w1, m_mlp_w2, m_norm_f, v_norm_mix, v_w_in, v_conv_a_w, v_conv_a_b, v_ln_a_g, v_ln_a_b, v_dn_conv_w, v_dn_a_log, v_dn_dt_bias, v_dn_norm_g, v_gm_ln_g, v_gm_ln_b, v_gm_ws, v_gm_bs, v_pool_w, v_pool_scale, v_w_branch, v_w_out, v_norm_xa, v_norm_mem, v_xa_wq, v_xa_wkv, v_xa_wo, v_norm_mlp, v_mlp_w1, v_mlp_w2, v_norm_f):
    given = dict(locals())
    wts = {n: given[n] for n in WEIGHTS}
    depth = norm_mix.shape[0]
    x = x[0]
    mem = mem[0]
    tgt = loss_target[0]

    me = 2 * lax.axis_index("x") + lax.axis_index("y")
    sharded = BIG + CONVS
    shard_axis = dict(BIG_AXIS, conv_a_w=1, dn_conv_w=1)

    def shards(l):
        return [wts[n][l].astype(BF16) for n in BIG] + [wts[n][l] for n in CONVS]

    def assemble(l, waited):
        mine, landed = waited
        w = {}
        for n, own, got in zip(sharded, mine, landed):
            got = lax.dynamic_update_index_in_dim(got, own[None], me, 0)
            w[n] = jnp.concatenate([got[j] for j in range(4)], axis=shard_axis[n])
        w["w_in_z"] = _regroup_w_in(w.pop("w_in"))
        for n in SMALL:
            if n != "norm_f" and n not in CONVS:
                w[n] = wts[n][l]
        return w

    saved, layer_w = [], []
    gather = _xy_start(shards(0), broadcast=True, name="gather_start_l0")
    after = gather[4]
    for l in range(depth):
        waited = _xy_wait(gather, after, broadcast=True, name=f"gather_wait_l{l}")
        w = assemble(l, waited)
        if l + 1 < depth:
            waited, nxt = lax.optimization_barrier((waited, shards(l + 1)))
            gather = _xy_start(nxt, broadcast=True, name=f"gather_start_l{l + 1}")
            w["norm_mix"] = w["norm_mix"] + gather[4][0, 0]
        x, sv = _layer_fwd(x, mem, w, l)
        after = x
        saved.append(sv)
        layer_w.append(w)
    dx, g_norm_f, loss_rows = _rw_bwd(
        f_loss, [(x, D_MODEL, 0), (tgt, D_MODEL, 0)], [_row(norm_f)], [(jnp.ones((x.shape[0], 1), F32), 1, 0)],
        row_grads=[(0, F32)], param_grads=[0], primal=[(0, 1, F32)], name="loss_head")
    loss = lax.psum(jnp.sum(loss_rows), ("x", "y", "c"))

    early = ("mlp_w1", "mlp_w2", "xa_wq", "xa_wkv", "xa_wo")
    late = ("w_in", "w_branch", "w_out")
    grads, reduces = [None] * depth, []

    def start_reduce(g, names, tag):
        blocks = [jnp.stack([_shard_slice(g[n], BIG_AXIS[n], j) for j in range(4)]).astype(BF16) for n in names]
        reduces.append((tag, names, _xy_start(blocks, broadcast=False, name=f"reduce_start_{tag}")))
        return reduces[-1][2][4][0, 0]

    for l in reversed(range(depth)):
        dx, grads[l] = _layer_bwd(dx, mem, layer_w[l], saved[l], l,
                                  functools.partial(start_reduce, names=early, tag=f"l{l}a"),
                                  functools.partial(start_reduce, names=late, tag=f"l{l}b"))
        saved[l] = None
    grad_x = dx[None]

    parts, keys = [], []
    for tag, names, started_reduce in reduces:
        mine, landed = _xy_wait(started_reduce, dx, broadcast=False, name=f"reduce_wait_{tag}")
        for n, own, got in zip(names, mine, landed):
            got = lax.dynamic_update_index_in_dim(got, lax.dynamic_index_in_dim(own, me, 0), me, 0)
            g3 = got.reshape(4, -1, got.shape[-1])
            parts.append(_ew(f_sum4, [(g3, j) for j in range(4)], [F32], name=f"reduce_sum4_{tag}_{n}")[0])
            keys.append((n, int(tag[1:-1])))
    others = _c_swap(parts, name="reduce_big_c")
    sums = {key: _ew(f_add2, [p, o], [F32], name=f"reduce_add2_{key[0]}_l{key[1]}")[0]
            for key, p, o in zip(keys, parts, others)}
    gw = {n: jnp.stack([sums[n, l].reshape(wts[n].shape[1:]) for l in range(depth)]) for n in BIG}

    small_names = [n for n in SMALL if n != "norm_f"]
    small_shapes = [(depth,) + (wts[n].shape[1:2] + (4 * wts[n].shape[2],) if n in CONVS else wts[n].shape[1:])
                    for n in small_names]
    small_buf = _flat_rows([jnp.stack([grads[l][n] for l in range(depth)]) for n in small_names] + [g_norm_f], F32,
                           row_multiple=1024)
    small_sum = _all_sum([small_buf], broadcast=True, name="reduce_small")[0]
    small_parts = _unflat(small_sum, small_shapes + [norm_f.shape])
    me = 2 * lax.axis_index("x") + lax.axis_index("y")
    for n, t in zip(small_names + ["norm_f"], small_parts):
        if n in CONVS:
            width = wts[n].shape[2]
            t = lax.dynamic_slice_in_dim(t, me * width, width, axis=2)
        gw[n] = t

    deltas, new_m, new_v = {}, {}, {}
    for n in WEIGHTS:
        deltas[n], new_m[n], new_v[n] = _adamw(wts[n], gw[n], given["m_" + n], given["v_" + n], name=f"adamw_{n}")
    return (loss, grad_x, *[gw[n] for n in WEIGHTS], *[deltas[n] for n in WEIGHTS], *[new_m[n] for n in WEIGHTS],
            *[new_v[n] for n in WEIGHTS])
```
